```python
import math
import jax, jax.numpy as jnp
from jax import lax
import numpy as np

D_MODEL = 1024
BATCH = 2
SEQ = 16384
DEPTH = 2
DEC_BATCH = 2
DEC_SEQ = 8192
PAST_LEN = 128

GRID_W = 64
HEAD_DIM = 64
ROPE_THETA = 10000.0
EPS = 1e-6
NEG = -1e30
NA_HEADS = 8
NA_KH = 8
NA_KW = 16
NA_QB = NA_KW
NA_KSPAN = 2 * NA_KW
DIL_HEADS = 8
DIL_PATTERNS = ((128, 1), (512, 4), (2048, 16))
N_DIL = len(DIL_PATTERNS)
C_Q_HEADS = 16
C_KV_HEADS = 4
C_QBLOCK = 128
N_EXPERTS = 32
TOP_K = 4
D_FF = D_MODEL
SWIGLU_LIMIT = 7.0
SWIGLU_ALPHA = 1.702
MOE_BLOCK = 256

N_EVEN = (DEPTH + 1) // 2
N_ODD = DEPTH // 2
NA_IN = 3 * NA_HEADS * HEAD_DIM
AB_IN = 3 * HEAD_DIM * (NA_HEADS + N_DIL * DIL_HEADS)
AB_OUT = HEAD_DIM * (NA_HEADS + DIL_HEADS)
C_IN = HEAD_DIM * (C_Q_HEADS + 2 * C_KV_HEADS)
C_OUT = HEAD_DIM * C_Q_HEADS

kernel_name = "hybrid_na_dilated_gqa_moe_encoder"


def rms_norm(x, g):
    xf = x.astype(jnp.float32)
    y = xf * lax.rsqrt(jnp.mean(xf * xf, axis=-1, keepdims=True) + EPS)
    return (y * g.astype(jnp.float32)).astype(x.dtype)


def rope_angles(pos, dim):
    inv = ROPE_THETA ** (-jnp.arange(0, dim, 2, dtype=jnp.float32) / dim)
    ang = pos.astype(jnp.float32)[:, None] * inv[None, :]
    ang = jnp.concatenate([ang, ang], axis=-1)
    return jnp.cos(ang), jnp.sin(ang)


def apply_rope(x, cos, sin):
    d = x.shape[-1]
    xf = x.astype(jnp.float32)
    rot = jnp.concatenate([-xf[..., d // 2:], xf[..., :d // 2]], axis=-1)
    return (xf * cos[None, :, None, :] + rot * sin[None, :, None, :]).astype(x.dtype)


def axial_rope(x, S):
    half = HEAD_DIM // 2
    t = jnp.arange(S)
    cr, sr = rope_angles(t // GRID_W, half)
    cc, sc = rope_angles(t % GRID_W, half)
    return jnp.concatenate([apply_rope(x[..., :half], cr, sr), apply_rope(x[..., half:], cc, sc)], axis=-1)


def neighborhood_attention(q, k, v, rpb):
    B, S, H, Dh = q.shape
    rows = S // GRID_W
    kh = min(NA_KH, rows)
    n_cb = GRID_W // NA_QB
    r = jnp.arange(rows)
    r0 = jnp.clip(r - kh // 2, 0, rows - kh)
    row_idx = r0[:, None] + jnp.arange(kh)[None, :]
    m = jnp.arange(n_cb)
    s0 = jnp.clip(m * NA_QB - NA_KW // 2, 0, GRID_W - NA_KSPAN)
    col_idx = s0[:, None] + jnp.arange(NA_KSPAN)[None, :]
    qcol = m[:, None] * NA_QB + jnp.arange(NA_QB)[None, :]
    c0 = jnp.clip(qcol - NA_KW // 2, 0, GRID_W - NA_KW)
    kc = col_idx[:, None, :]
    col_ok = (kc >= c0[..., None]) & (kc < c0[..., None] + NA_KW)
    nk = kh * NA_KSPAN
    mask = jnp.broadcast_to(col_ok[:, :, None, :], (n_cb, NA_QB, kh, NA_KSPAN)).reshape(n_cb, NA_QB, nk)
    dr = row_idx - r[:, None]
    dc = jnp.clip(kc - qcol[..., None], -(NA_KW - 1), NA_KW - 1)
    bidx = (dr[:, None, None, :, None] + NA_KH - 1) * (2 * NA_KW - 1) + (dc[None, :, :, None, :] + NA_KW - 1)
    bias = jnp.take(rpb.reshape(H, -1), bidx.reshape(rows, n_cb, NA_QB, nk), axis=1)
    bias = bias.astype(jnp.float32).transpose(1, 2, 0, 3, 4)[None]
    ri = row_idx[:, None, :, None]
    ci = col_idx[None, :, None, :]
    kg = k.reshape(B, rows, GRID_W, H, Dh)[:, ri, ci].reshape(B, rows, n_cb, nk, H, Dh)
    vg = v.reshape(B, rows, GRID_W, H, Dh)[:, ri, ci].reshape(B, rows, n_cb, nk, H, Dh)
    qg = q.reshape(B, rows, n_cb, NA_QB, H, Dh)
    s = jnp.einsum('brmqhd,brmkhd->brmhqk', qg.astype(jnp.float32), kg.astype(jnp.float32)) * (HEAD_DIM ** -0.5)
    s = jnp.where(mask[None, None, :, None], s + bias, NEG)
    p = jax.nn.softmax(s, axis=-1)
    o = jnp.einsum('brmhqk,brmkhd->brmqhd', p, vg.astype(jnp.float32))
    return o.reshape(B, S, H, Dh)


def dilated_attention(q, k, v, window, dilation):
    B, S, H, Dh = q.shape
    side = (window // 2) // dilation
    L = S // dilation
    nb = -(-L // side)
    Lp = nb * side

    def split(x):
        return x.reshape(B, L, dilation, H, Dh).transpose(0, 2, 1, 3, 4)

    qs = jnp.pad(split(q), ((0, 0), (0, 0), (0, Lp - L), (0, 0), (0, 0))).reshape(B, dilation, nb, side, H, Dh)
    kpad = jnp.pad(split(k), ((0, 0), (0, 0), (side, Lp - L + side), (0, 0), (0, 0)))
    vpad = jnp.pad(split(v), ((0, 0), (0, 0), (side, Lp - L + side), (0, 0), (0, 0)))
    kidx = jnp.arange(nb)[:, None] * side + jnp.arange(3 * side)[None, :]
    kb = kpad[:, :, kidx]
    vb = vpad[:, :, kidx]
    ql = jnp.arange(nb)[:, None] * side + jnp.arange(side)[None, :]
    kl = kidx - side
    rel = kl[:, None, :] - ql[:, :, None]
    mask = (jnp.abs(rel) <= side) & (kl[:, None, :] >= 0) & (kl[:, None, :] < L)
    s = jnp.einsum('bgnqhd,bgnkhd->bgnhqk', qs.astype(jnp.float32), kb.astype(jnp.float32)) * (HEAD_DIM ** -0.5)
    s = jnp.where(mask[None, None, :, None], s, NEG)
    lse = jax.nn.logsumexp(s, axis=-1)
    p = jnp.exp(s - lse[..., None])
    o = jnp.einsum('bgnhqk,bgnkhd->bgnqhd', p, vb.astype(jnp.float32))
    o = o.reshape(B, dilation, Lp, H, Dh)[:, :, :L].transpose(0, 2, 1, 3, 4).reshape(B, S, H, Dh)
    lse = lse.transpose(0, 1, 2, 4, 3).reshape(B, dilation, Lp, H)[:, :, :L].transpose(0, 2, 1, 3).reshape(B, S, H)
    return o, lse


def mixer_ab(h, w_in, w_out, g_qn_a, g_kn_a, rpb, g_qn_b, g_kn_b):
    B, S, _ = h.shape
    proj = h @ w_in
    pa = proj[..., :NA_IN].reshape(B, S, 3, NA_HEADS, HEAD_DIM)
    qa = rms_norm(pa[:, :, 0], g_qn_a)
    ka = rms_norm(pa[:, :, 1], g_kn_a)
    oa = neighborhood_attention(qa, ka, pa[:, :, 2], rpb)
    pb = proj[..., NA_IN:].reshape(B, S, 3, N_DIL, DIL_HEADS, HEAD_DIM)
    cos, sin = rope_angles(jnp.arange(S), HEAD_DIM)
    outs, lses = [], []
    for g, (window, dilation) in enumerate(DIL_PATTERNS):
        qg = apply_rope(rms_norm(pb[:, :, 0, g], g_qn_b[g]), cos, sin)
        kg = apply_rope(rms_norm(pb[:, :, 1, g], g_kn_b[g]), cos, sin)
        o, lse = dilated_attention(qg, kg, pb[:, :, 2, g], window, dilation)
        outs.append(o)
        lses.append(lse)
    wts = jax.nn.softmax(jnp.stack(lses, axis=0), axis=0)
    ob = jnp.sum(wts[..., None] * jnp.stack(outs, axis=0), axis=0)
    cat = jnp.concatenate([oa.reshape(B, S, -1), ob.reshape(B, S, -1)], axis=-1).astype(h.dtype)
    return cat @ w_out


def mixer_c(h, w_in, w_out, g_qn, g_kn):
    B, S, _ = h.shape
    proj = h @ w_in
    nq = C_Q_HEADS * HEAD_DIM
    nkv = C_KV_HEADS * HEAD_DIM
    q = proj[..., :nq].reshape(B, S, C_Q_HEADS, HEAD_DIM)
    k = proj[..., nq:nq + nkv].reshape(B, S, C_KV_HEADS, HEAD_DIM)
    v = proj[..., nq + nkv:].reshape(B, S, C_KV_HEADS, HEAD_DIM)
    q = axial_rope(rms_norm(q, g_qn), S)
    k = axial_rope(rms_norm(k, g_kn), S)
    G = C_Q_HEADS // C_KV_HEADS
    nqb = S // C_QBLOCK
    qb = q.reshape(B, nqb, C_QBLOCK, C_KV_HEADS, G, HEAD_DIM).transpose(1, 0, 2, 3, 4, 5).astype(jnp.float32)
    kf = k.astype(jnp.float32)
    vf = v.astype(jnp.float32)

    def block(qblk):
        s = jnp.einsum('bqkgd,bskd->bkgqs', qblk, kf) * (HEAD_DIM ** -0.5)
        p = jax.nn.softmax(s, axis=-1)
        return jnp.einsum('bkgqs,bskd->bqkgd', p, vf)

    o = lax.map(block, qb)
    o = o.transpose(1, 0, 2, 3, 4, 5).reshape(B, S, C_OUT).astype(h.dtype)
    return o @ w_out


def moe(h, w_router, b_router, w_gu, b_gu, w_down, b_down):
    B, S, D = h.shape
    T = B * S
    xt = h.reshape(T, D)
    logits = (xt @ w_router + b_router).astype(jnp.float32)
    top_v, top_i = lax.top_k(logits, TOP_K)
    gates = jax.nn.softmax(top_v, axis=-1)
    M = T * TOP_K
    flat_e = top_i.reshape(M)
    flat_t = jnp.repeat(jnp.arange(T, dtype=jnp.int32), TOP_K)
    flat_g = gates.reshape(M)
    order = jnp.argsort(flat_e)
    se = flat_e[order]
    counts = jnp.bincount(flat_e, length=N_EXPERTS)
    pcounts = ((counts + MOE_BLOCK - 1) // MOE_BLOCK) * MOE_BLOCK
    pend = jnp.cumsum(pcounts)
    pstart = pend - pcounts
    ustart = jnp.cumsum(counts) - counts
    dest = pstart[se] + jnp.arange(M) - ustart[se]
    nblk = -(-M // MOE_BLOCK) + N_EXPERTS
    P = nblk * MOE_BLOCK
    tok = jnp.full((P,), T, dtype=jnp.int32).at[dest].set(flat_t[order])
    gbuf = jnp.zeros((P,), jnp.float32).at[dest].set(flat_g[order])
    blk_e = jnp.minimum(jnp.searchsorted(pend, jnp.arange(nblk) * MOE_BLOCK, side='right'), N_EXPERTS - 1)
    xpad = jnp.concatenate([xt, jnp.zeros((1, D), xt.dtype)], axis=0)
    xb = xpad[tok].reshape(nblk, MOE_BLOCK, D)

    def expert_block(args):
        xblk, e = args
        gu = xblk @ w_gu[e] + b_gu[e]
        glu = jnp.minimum(gu[:, :D_FF], SWIGLU_LIMIT)
        lin = jnp.clip(gu[:, D_FF:], -SWIGLU_LIMIT, SWIGLU_LIMIT)
        act = glu * jax.nn.sigmoid(SWIGLU_ALPHA * glu) * (lin + 1)
        return act @ w_down[e] + b_down[e]

    yb = lax.map(expert_block, (xb, blk_e)).reshape(P, D)
    y = jnp.zeros((T + 1, D), jnp.float32).at[tok].add(yb.astype(jnp.float32) * gbuf[:, None])
    return y[:T].reshape(B, S, D).astype(h.dtype)


def trunk(x, c, w_ada, b_ada, g_norm_mix, g_norm_ffn, w_in_ab, w_out_ab, g_qn_a, g_kn_a, rpb_a,
          g_qn_b, g_kn_b, w_in_c, w_out_c, g_qn_c, g_kn_c, w_router, b_router, w_gate_up, b_gate_up,
          w_down, b_down):
    for l in range(DEPTH):
        mod = jax.nn.silu(c) @ w_ada[l] + b_ada[l]
        sh1, sc1, g1, sh2, sc2, g2 = jnp.split(mod[:, None, :], 6, axis=-1)
        h = rms_norm(x, g_norm_mix[l]) * (1 + sc1) + sh1
        if l % 2 == 0:
            i = l // 2
            m = mixer_ab(h, w_in_ab[i], w_out_ab[i], g_qn_a[i], g_kn_a[i], rpb_a[i], g_qn_b[i], g_kn_b[i])
        else:
            i = l // 2
            m = mixer_c(h, w_in_c[i], w_out_c[i], g_qn_c[i], g_kn_c[i])
        x = x + g1 * m
        h = rms_norm(x, g_norm_ffn[l]) * (1 + sc2) + sh2
        x = x + g2 * moe(h, w_router[l], b_router[l], w_gate_up[l], b_gate_up[l], w_down[l], b_down[l])
    return x


def setup_inputs(seed: int = 0) -> dict:
    key = jax.random.key(seed)
    ks = jax.random.split(key, 32)
    D = D_MODEL

    def nrm(k, shape, scale):
        return jax.random.normal(k, shape, jnp.float32) * scale

    def gain(k, shape):
        return 1.0 + 0.1 * jax.random.normal(k, shape, jnp.float32)

    return {
        "x_prompt": nrm(ks[0], (BATCH, SEQ, D), 1.0),
        "x_sample": nrm(ks[1], (DEC_BATCH, DEC_SEQ, D), 1.0),
        "c_prompt": nrm(ks[2], (BATCH, D), 1.0),
        "c_sample": nrm(ks[3], (DEC_BATCH, D), 1.0),
        "w_ada": nrm(ks[4], (DEPTH, D, 6 * D), D ** -0.5),
        "b_ada": nrm(ks[5], (DEPTH, 6 * D), 0.02),
        "g_norm_mix": gain(ks[6], (DEPTH, D)),
        "g_norm_ffn": gain(ks[7], (DEPTH, D)),
        "w_in_ab": nrm(ks[8], (N_EVEN, D, AB_IN), D ** -0.5),
        "w_out_ab": nrm(ks[9], (N_EVEN, AB_OUT, D), AB_OUT ** -0.5),
        "g_qn_a": gain(ks[10], (N_EVEN, HEAD_DIM)),
        "g_kn_a": gain(ks[11], (N_EVEN, HEAD_DIM)),
        "rpb_a": nrm(ks[12], (N_EVEN, NA_HEADS, 2 * NA_KH - 1, 2 * NA_KW - 1), 0.1),
        "g_qn_b": gain(ks[13], (N_EVEN, N_DIL, HEAD_DIM)),
        "g_kn_b": gain(ks[14], (N_EVEN, N_DIL, HEAD_DIM)),
        "w_in_c": nrm(ks[15], (N_ODD, D, C_IN), D ** -0.5),
        "w_out_c": nrm(ks[16], (N_ODD, C_OUT, D), C_OUT ** -0.5),
        "g_qn_c": gain(ks[17], (N_ODD, HEAD_DIM)),
        "g_kn_c": gain(ks[18], (N_ODD, HEAD_DIM)),
        "w_router": nrm(ks[19], (DEPTH, D, N_EXPERTS), D ** -0.5),
        "b_router": nrm(ks[20], (DEPTH, N_EXPERTS), 0.01),
        "w_gate_up": nrm(ks[21], (DEPTH, N_EXPERTS, D, 2 * D_FF), D ** -0.5),
        "b_gate_up": nrm(ks[22], (DEPTH, N_EXPERTS, 2 * D_FF), 0.02),
        "w_down": nrm(ks[23], (DEPTH, N_EXPERTS, D_FF, D), D_FF ** -0.5),
        "b_down": nrm(ks[24], (DEPTH, N_EXPERTS, D), 0.02),
    }


def reference(x_prompt, x_sample, c_prompt, c_sample, w_ada, b_ada, g_norm_mix, g_norm_ffn, w_in_ab,
              w_out_ab, g_qn_a, g_kn_a, rpb_a, g_qn_b, g_kn_b, w_in_c, w_out_c, g_qn_c, g_kn_c,
              w_router, b_router, w_gate_up, b_gate_up, w_down, b_down):
    y_prompt = trunk(x_prompt, c_prompt, w_ada, b_ada, g_norm_mix, g_norm_ffn, w_in_ab, w_out_ab,
                     g_qn_a, g_kn_a, rpb_a, g_qn_b, g_kn_b, w_in_c, w_out_c, g_qn_c, g_kn_c,
                     w_router, b_router, w_gate_up, b_gate_up, w_down, b_down)
    y_sample = trunk(x_sample, c_sample, w_ada, b_ada, g_norm_mix, g_norm_ffn, w_in_ab, w_out_ab,
                     g_qn_a, g_kn_a, rpb_a, g_qn_b, g_kn_b, w_in_c, w_out_c, g_qn_c, g_kn_c,
                     w_router, b_router, w_gate_up, b_gate_up, w_down, b_down)
    return (y_prompt, y_sample)
```

```python
import functools
import math
from typing import NamedTuple

import jax
import jax.numpy as jnp
from jax import lax
from jax.experimental import pallas as pl
from jax.experimental.pallas import tpu as pltpu

F32 = jnp.float32
BF16 = jnp.bfloat16
HIGHEST = lax.Precision.HIGHEST

GRID_W = 64
HEAD_DIM = 64
ROPE_THETA = 10000.0
EPS = 1e-6
NEG = -1e30
NA_HEADS = 8
NA_KH = 8
NA_KW = 16
DIL_HEADS = 8
DIL_PATTERNS = ((128, 1), (512, 4), (2048, 16))
N_DIL = len(DIL_PATTERNS)
C_Q_HEADS = 16
C_KV_HEADS = 4
N_EXPERTS = 32
TOP_K = 4
SWIGLU_LIMIT = 7.0
SWIGLU_ALPHA = 1.702
Q_SCALE = HEAD_DIM ** -0.5

V7X_VMEM_BYTES = 64 * 1024 * 1024
VMEM_LIMIT = V7X_VMEM_BYTES - 8 * 1024 * 1024

TOKEN_TILE = 1024
NA_BLOCK = 8 * GRID_W
DIL_QBLOCK = 128
DIL_HALO = 128
FLASH_TQ = 256
FLASH_TK = 512
MOE_BLOCK = 512


class Geom(NamedTuple):
    b_p: int
    s_p: int
    b_d: int
    s_d: int

    @property
    def n_p(self):
        return self.b_p * self.s_p

    @property
    def n_tok(self):
        return self.n_p + self.b_d * self.s_d

    @property
    def n_seq(self):
        return self.b_p + self.b_d

    def groups(self):
        return ((0, self.b_p, self.s_p), (self.n_p, self.b_d, self.s_d))

    def seq_of_token(self, t0):
        return jnp.where(t0 < self.n_p, t0 // self.s_p,
                         self.b_p + (t0 - self.n_p) // self.s_d)


def _params(semantics, vmem=None):
    return pltpu.CompilerParams(dimension_semantics=semantics,
                                vmem_limit_bytes=vmem)


def _ada_kernel(c_ref, w_ref, b_ref, o_ref):
    c = c_ref[...]
    a = c / (1.0 + jnp.exp(-c))
    o_ref[...] = jnp.dot(a, w_ref[...], precision=HIGHEST,
                         preferred_element_type=F32) + b_ref[...]


def ada_modulation(c_all, w_ada, b_ada):
    depth, d, d6 = w_ada.shape
    n_seq = c_all.shape[0]
    rows = 8
    c_pad = jnp.zeros((rows, d), F32).at[:n_seq].set(c_all)
    out = pl.pallas_call(
        _ada_kernel,
        grid=(depth, d6 // d),
        in_specs=[
            pl.BlockSpec((rows, d), lambda l, j: (0, 0)),
            pl.BlockSpec((None, d, d), lambda l, j: (l, 0, j)),
            pl.BlockSpec((None, 1, d), lambda l, j: (l, 0, j)),
        ],
        out_specs=pl.BlockSpec((None, rows, d), lambda l, j: (l, 0, j)),
        out_shape=jax.ShapeDtypeStruct((depth, rows, d6), F32),
        compiler_params=_params(("arbitrary", "arbitrary")),
        name="ada_modulation",
    )(c_pad, w_ada, b_ada.reshape(depth, 1, d6))
    return out[:, :n_seq].reshape(depth, n_seq, 6, d)


def _modulated_norm(x, g, shift, scale):
    r = lax.rsqrt(jnp.mean(x * x, axis=-1, keepdims=True) + EPS)
    return (x * r * g) * (1.0 + scale) + shift


def _norm_proj_kernel(x_ref, mod_ref, g_ref, w_ref, o_ref, h_ref):
    @pl.when(pl.program_id(1) == 0)
    def _():
        h = _modulated_norm(x_ref[...], g_ref[...], mod_ref[0:1, :], mod_ref[1:2, :])
        h_ref[...] = h.astype(BF16)

    o_ref[...] = jnp.dot(h_ref[...], w_ref[...],
                         preferred_element_type=F32).astype(o_ref.dtype)


def norm_proj(x, mod, g_norm, w_bf16, layer, geom, tn):
    n_tok, d = x.shape
    n_out = w_bf16.shape[1]
    tm = TOKEN_TILE
    return pl.pallas_call(
        _norm_proj_kernel,
        grid=(n_tok // tm, n_out // tn),
        in_specs=[
            pl.BlockSpec((tm, d), lambda i, j: (i, 0)),
            pl.BlockSpec((None, None, 6, d),
                         lambda i, j: (layer, geom.seq_of_token(i * tm), 0, 0)),
            pl.BlockSpec((None, 1, d), lambda i, j: (layer, 0, 0)),
            pl.BlockSpec((d, tn), lambda i, j: (0, j)),
        ],
        out_specs=pl.BlockSpec((tm, tn), lambda i, j: (i, j)),
        out_shape=jax.ShapeDtypeStruct((n_tok, n_out), BF16),
        scratch_shapes=[pltpu.VMEM((tm, d), BF16)],
        compiler_params=_params(("parallel", "arbitrary"), VMEM_LIMIT),
        name="norm_proj",
    )(x, mod, g_norm.reshape(-1, 1, d), w_bf16)


def _prep_kernel(x_ref, g_ref, *rest, rope):
    o_ref = rest[-1]
    hb, ts, hd = x_ref.shape
    x = x_ref[...].astype(F32)
    r = lax.rsqrt(jnp.mean(x * x, axis=-1, keepdims=True) + EPS)
    y = x * r * g_ref[...]
    if rope:
        cos_ref, sin_ref, p_ref = rest[:3]
        yr = jnp.dot(y.reshape(hb * ts, hd).astype(BF16), p_ref[...],
                     preferred_element_type=F32).reshape(hb, ts, hd)
        y = y * cos_ref[...] + yr * sin_ref[...]
    o_ref[...] = y.astype(o_ref.dtype)


def qk_prep(x, gains, tables=None):
    a_n, r_n, s_n, hd = x.shape
    hb = math.gcd(r_n, 8)
    ts = min(s_n, 2048)
    rope = tables is not None
    in_specs = [
        pl.BlockSpec((None, hb, ts, hd), lambda a, r, s: (a, r, s, 0)),
        pl.BlockSpec((None, 1, hd), lambda a, r, s: (a, 0, 0)),
    ]
    args = [x, gains.reshape(a_n, 1, hd).astype(F32)]
    if rope:
        in_specs += [
            pl.BlockSpec((ts, hd), lambda a, r, s: (s, 0)),
            pl.BlockSpec((ts, hd), lambda a, r, s: (s, 0)),
            pl.BlockSpec((hd, hd), lambda a, r, s: (0, 0)),
        ]
        args += list(tables)
    return pl.pallas_call(
        functools.partial(_prep_kernel, rope=rope),
        grid=(a_n, r_n // hb, s_n // ts),
        in_specs=in_specs,
        out_specs=pl.BlockSpec((None, hb, ts, hd), lambda a, r, s: (a, r, s, 0)),
        out_shape=jax.ShapeDtypeStruct(x.shape, BF16),
        compiler_params=_params(("parallel", "parallel", "parallel")),
        name="qk_prep",
    )(*args)


def _rope_angles(pos, dim):
    inv = ROPE_THETA ** (-jnp.arange(0, dim, 2, dtype=F32) / dim)
    ang = pos.astype(F32)[:, None] * inv[None, :]
    ang = jnp.concatenate([ang, ang], axis=-1)
    return jnp.cos(ang), jnp.sin(ang)


def _rotate_half_matrix(dim, n_blocks):
    half = dim // 2
    i = jnp.arange(dim * n_blocks)
    blk, w = i // dim, i % dim
    src = blk * dim + jnp.where(w < half, w + half, w - half)
    sign = jnp.where(w < half, -1.0, 1.0)
    p = jnp.zeros((dim * n_blocks, dim * n_blocks), F32).at[src, i].set(sign)
    return p.astype(BF16)


def _na_bias_table(rpb):
    n_h = rpb.shape[0]
    qc = jnp.arange(GRID_W)[:, None]
    kc = jnp.arange(GRID_W)[None, :]
    c0 = jnp.clip(qc - NA_KW // 2, 0, GRID_W - NA_KW)
    ok = (kc >= c0) & (kc < c0 + NA_KW)
    dc = jnp.clip(kc - qc, -(NA_KW - 1), NA_KW - 1) + NA_KW - 1
    d0 = jnp.arange(NA_KH)[:, None]
    ik = jnp.arange(NA_KH)[None, :]
    dr = ik - d0 + NA_KH - 1
    bias = rpb[:, dr[:, :, None, None], dc[None, None, :, :]]
    bias = jnp.where(ok[None, None, None], bias, NEG)
    bias = bias.transpose(1, 0, 3, 2, 4).reshape(NA_KH, n_h, GRID_W, NA_KH * GRID_W)
    return bias.astype(BF16)


def _na_kernel(q_ref, kp_ref, kc_ref, kn_ref, vp_ref, vc_ref, vn_ref, tbl_ref,
               o_ref, ks_ref, vs_ref, *, geom):
    blk = NA_BLOCK
    j = pl.program_id(0)
    t0 = j * blk
    in_p = t0 < geom.n_p
    seq_blk0 = jnp.where(in_p, (t0 // geom.s_p) * (geom.s_p // blk),
                         geom.n_p // blk + ((t0 - geom.n_p) // geom.s_d) * (geom.s_d // blk))
    rows = jnp.where(in_p, geom.s_p // GRID_W, geom.s_d // GRID_W)
    jl = j - seq_blk0
    ks_ref[:, 0:blk, :] = kp_ref[...]
    ks_ref[:, blk:2 * blk, :] = kc_ref[...]
    ks_ref[:, 2 * blk:3 * blk, :] = kn_ref[...]
    vs_ref[:, 0:blk, :] = vp_ref[...]
    vs_ref[:, blk:2 * blk, :] = vc_ref[...]
    vs_ref[:, 2 * blk:3 * blk, :] = vn_ref[...]
    n_keys = NA_KH * GRID_W

    def body(i, carry):
        r = jl * NA_KH + i
        r0 = jnp.clip(r - NA_KH // 2, 0, rows - NA_KH)
        off = pl.multiple_of((r0 - (jl - 1) * NA_KH) * GRID_W, GRID_W)
        d0 = r - r0
        qoff = pl.multiple_of(i * GRID_W, GRID_W)
        for h in range(NA_HEADS):
            q = q_ref[h, pl.ds(qoff, GRID_W), :]
            k = ks_ref[h, pl.ds(off, n_keys), :]
            v = vs_ref[h, pl.ds(off, n_keys), :]
            s = lax.dot_general(q, k, (((1,), (1,)), ((), ())), preferred_element_type=F32)
            s = s + tbl_ref[d0, h].astype(F32)
            m = jnp.max(s, axis=-1, keepdims=True)
            p = jnp.exp(s - m)
            l = jnp.sum(p, axis=-1, keepdims=True)
            o = jnp.dot(p.astype(BF16), v, preferred_element_type=F32) / l
            o_ref[h, pl.ds(qoff, GRID_W), :] = o.astype(o_ref.dtype)
        return carry

    lax.fori_loop(0, NA_KH, body, 0)


def neighborhood_attention(q, k, v, tbl, geom):
    n_h, n_tok, hd = q.shape
    blk = NA_BLOCK
    nb = n_tok // blk
    cur = pl.BlockSpec((n_h, blk, hd), lambda j: (0, j, 0))
    prev = pl.BlockSpec((n_h, blk, hd), lambda j: (0, jnp.maximum(j - 1, 0), 0))
    nxt = pl.BlockSpec((n_h, blk, hd), lambda j: (0, jnp.minimum(j + 1, nb - 1), 0))
    return pl.pallas_call(
        functools.partial(_na_kernel, geom=geom),
        grid=(nb,),
        in_specs=[cur, prev, cur, nxt, prev, cur, nxt,
                  pl.BlockSpec(tbl.shape, lambda j: (0, 0, 0, 0))],
        out_specs=cur,
        out_shape=jax.ShapeDtypeStruct(q.shape, BF16),
        scratch_shapes=[pltpu.VMEM((n_h, 3 * blk, hd), BF16),
                        pltpu.VMEM((n_h, 3 * blk, hd), BF16)],
        compiler_params=_params(("parallel",), VMEM_LIMIT),
        name="neighborhood_attention",
    )(q, k, k, k, v, v, v, tbl)


def _dil_kernel(q_ref, kp_ref, kc_ref, kn_ref, vp_ref, vc_ref, vn_ref,
                o_ref, lse_ref, ks_ref, vs_ref, *, cls_len, side, chunk):
    c = pl.program_id(1)
    hl = DIL_HALO
    ks_ref[0:hl, :] = kp_ref[...]
    ks_ref[hl:hl + chunk, :] = kc_ref[...]
    ks_ref[hl + chunk:, :] = kn_ref[...]
    vs_ref[0:hl, :] = vp_ref[...]
    vs_ref[hl:hl + chunk, :] = vc_ref[...]
    vs_ref[hl + chunk:, :] = vn_ref[...]
    qb = DIL_QBLOCK
    kb = qb + 2 * side
    base = c * chunk
    hd = q_ref.shape[-1]

    def body(i, carry):
        qoff = pl.multiple_of(i * qb, qb)
        koff = pl.multiple_of(i * qb + hl - side, side)
        q = q_ref[pl.ds(qoff, qb), :]
        k = ks_ref[pl.ds(koff, kb), :]
        v = vs_ref[pl.ds(koff, kb), :]
        s = lax.dot_general(q, k, (((1,), (1,)), ((), ())), preferred_element_type=F32)
        pq = base + i * qb + lax.broadcasted_iota(jnp.int32, (qb, kb), 0)
        pk = base + i * qb - side + lax.broadcasted_iota(jnp.int32, (qb, kb), 1)
        same = (pk // cls_len) == (pq // cls_len)
        near = jnp.abs(pk - pq) <= side
        s = jnp.where(near, jnp.where(same, s, NEG), NEG)
        m = jnp.max(s, axis=-1, keepdims=True)
        p = jnp.exp(s - m)
        l = jnp.sum(p, axis=-1, keepdims=True)
        o = jnp.dot(p.astype(BF16), v, preferred_element_type=F32) / l
        o_ref[pl.ds(qoff, qb), :] = o.astype(o_ref.dtype)
        lse_ref[pl.ds(qoff, qb), :] = jnp.broadcast_to(m + jnp.log(l), (qb, hd))
        return carry

    lax.fori_loop(0, chunk // qb, body, 0)


def dilated_attention(q, k, v, cls_len, side):
    r_n, s_n, hd = q.shape
    assert side == DIL_QBLOCK // 2 and cls_len % DIL_QBLOCK == 0
    chunk = min(s_n, 2048)
    nc = s_n // chunk
    per = chunk // DIL_HALO
    n_halo = s_n // DIL_HALO
    cur = pl.BlockSpec((None, chunk, hd), lambda r, c: (r, c, 0))
    prev = pl.BlockSpec((None, DIL_HALO, hd), lambda r, c: (r, jnp.maximum(c * per - 1, 0), 0))
    nxt = pl.BlockSpec((None, DIL_HALO, hd),
                       lambda r, c: (r, jnp.minimum((c + 1) * per, n_halo - 1), 0))
    return pl.pallas_call(
        functools.partial(_dil_kernel, cls_len=cls_len, side=side, chunk=chunk),
        grid=(r_n, nc),
        in_specs=[cur, prev, cur, nxt, prev, cur, nxt],
        out_specs=[cur, cur],
        out_shape=[jax.ShapeDtypeStruct(q.shape, BF16),
                   jax.ShapeDtypeStruct(q.shape, F32)],
        scratch_shapes=[pltpu.VMEM((chunk + 2 * DIL_HALO, hd), BF16),
                        pltpu.VMEM((chunk + 2 * DIL_HALO, hd), BF16)],
        compiler_params=_params(("parallel", "parallel")),
        name="dilated_attention",
    )(q, k, k, k, v, v, v)


def _outproj_ab_kernel(x_ref, mod_ref, oa_ref, od_ref, lse_ref, w_ref, o_ref):
    half = oa_ref.shape[-1]
    l0, l1, l2 = lse_ref[0], lse_ref[1], lse_ref[2]
    m = jnp.maximum(jnp.maximum(l0, l1), l2)
    e0, e1, e2 = jnp.exp(l0 - m), jnp.exp(l1 - m), jnp.exp(l2 - m)
    ob = (e0 * od_ref[0].astype(F32) + e1 * od_ref[1].astype(F32)
          + e2 * od_ref[2].astype(F32)) / (e0 + e1 + e2)
    res = jnp.dot(oa_ref[...], w_ref[0:half, :], preferred_element_type=F32)
    res = res + jnp.dot(ob.astype(BF16), w_ref[half:, :], preferred_element_type=F32)
    o_ref[...] = x_ref[...] + mod_ref[2:3, :] * res


def outproj_ab(x, mod, oa, od, lse, w_bf16, layer, geom):
    n_tok, d = x.shape
    half = oa.shape[-1]
    tm = TOKEN_TILE // 2
    return pl.pallas_call(
        _outproj_ab_kernel,
        grid=(n_tok // tm,),
        in_specs=[
            pl.BlockSpec((tm, d), lambda i: (i, 0)),
            pl.BlockSpec((None, None, 6, d), lambda i: (layer, geom.seq_of_token(i * tm), 0, 0)),
            pl.BlockSpec((tm, half), lambda i: (i, 0)),
            pl.BlockSpec((N_DIL, tm, half), lambda i: (0, i, 0)),
            pl.BlockSpec((N_DIL, tm, half), lambda i: (0, i, 0)),
            pl.BlockSpec(w_bf16.shape, lambda i: (0, 0)),
        ],
        out_specs=pl.BlockSpec((tm, d), lambda i: (i, 0)),
        out_shape=jax.ShapeDtypeStruct(x.shape, F32),
        compiler_params=_params(("parallel",), VMEM_LIMIT),
        name="outproj_ab",
    )(x, mod, oa, od, lse, w_bf16)


def _outproj_kernel(x_ref, mod_ref, o_in_ref, w_ref, o_ref):
    res = jnp.dot(o_in_ref[...], w_ref[...], preferred_element_type=F32)
    o_ref[...] = x_ref[...] + mod_ref[2:3, :] * res


def outproj(x, mod, o_in, w_bf16, layer, geom):
    n_tok, d = x.shape
    tm = TOKEN_TILE
    return pl.pallas_call(
        _outproj_kernel,
        grid=(n_tok // tm,),
        in_specs=[
            pl.BlockSpec((tm, d), lambda i: (i, 0)),
            pl.BlockSpec((None, None, 6, d), lambda i: (layer, geom.seq_of_token(i * tm), 0, 0)),
            pl.BlockSpec((tm, o_in.shape[1]), lambda i: (i, 0)),
            pl.BlockSpec(w_bf16.shape, lambda i: (0, 0)),
        ],
        out_specs=pl.BlockSpec((tm, d), lambda i: (i, 0)),
        out_shape=jax.ShapeDtypeStruct(x.shape, F32),
        compiler_params=_params(("parallel",), VMEM_LIMIT),
        name="outproj",
    )(x, mod, o_in, w_bf16)


def _flash_kernel(qt_ref, k_ref, vt_ref, o_ref):
    n_g, hd, tq = qt_ref.shape
    n_kt = k_ref.shape[0]
    for g in range(n_g):
        qt = qt_ref[g]

        def body(j, carry):
            m, l, acc = carry
            s = jnp.dot(k_ref[j], qt, preferred_element_type=F32)
            m_new = jnp.maximum(m, jnp.max(s, axis=0, keepdims=True))
            alpha = jnp.exp(m - m_new)
            p = jnp.exp(s - m_new)
            l = alpha * l + jnp.sum(p, axis=0, keepdims=True)
            acc = alpha * acc + jnp.dot(vt_ref[j], p.astype(BF16), preferred_element_type=F32)
            return m_new, l, acc

        init = (jnp.full((1, tq), -jnp.inf, F32), jnp.zeros((1, tq), F32),
                jnp.zeros((hd, tq), F32))
        m, l, acc = lax.fori_loop(0, n_kt, body, init)
        o_ref[g] = (acc / l).astype(o_ref.dtype)


def flash_attention(qt, k_tiles, vt_tiles, tok0, n_b, s_n):
    n_h, hd, _ = qt.shape
    n_kv = k_tiles.shape[0]
    n_g = n_h // n_kv
    tq, tk = FLASH_TQ, FLASH_TK
    nq = s_n // tq
    kt_per = s_n // tk
    return pl.pallas_call(
        _flash_kernel,
        grid=(n_b, n_kv, nq),
        in_specs=[
            pl.BlockSpec((n_g, hd, tq), lambda b, h, i: (h, 0, (tok0 + b * s_n) // tq + i)),
            pl.BlockSpec((None, kt_per, tk, hd), lambda b, h, i: (h, (tok0 + b * s_n) // s_n, 0, 0)),
            pl.BlockSpec((None, kt_per, hd, tk), lambda b, h, i: (h, (tok0 + b * s_n) // s_n, 0, 0)),
        ],
        out_specs=pl.BlockSpec((n_g, hd, tq), lambda b, h, i: (h, 0, b * nq + i)),
        out_shape=jax.ShapeDtypeStruct((n_h, hd, n_b * s_n), BF16),
        compiler_params=_params(("parallel", "parallel", "arbitrary"), VMEM_LIMIT),
        name="flash_attention",
    )(qt, k_tiles, vt_tiles)


def _router_kernel(x_ref, mod_ref, g_ref, wr_ref, br_ref, h_ref, idx_ref, gate_ref):
    h = _modulated_norm(x_ref[...], g_ref[...], mod_ref[3:4, :], mod_ref[4:5, :])
    h_ref[...] = h.astype(BF16)
    logits = jnp.dot(h, wr_ref[...], precision=HIGHEST,
                     preferred_element_type=F32) + br_ref[...]
    tm, n_e = logits.shape
    iota = lax.broadcasted_iota(jnp.int32, (tm, n_e), 1)
    vals, ids = [], []
    cur = logits
    for _ in range(TOP_K):
        m = jnp.max(cur, axis=-1, keepdims=True)
        am = jnp.min(jnp.where(cur == m, iota, n_e), axis=-1, keepdims=True)
        vals.append(m)
        ids.append(am)
        cur = jnp.where(iota == am, -jnp.inf, cur)
    es = [jnp.exp(v - vals[0]) for v in vals]
    den = es[0] + es[1] + es[2] + es[3]
    lane = lax.broadcasted_iota(jnp.int32, idx_ref.shape, 1)
    idx_out = jnp.zeros(idx_ref.shape, jnp.int32)
    gate_out = jnp.zeros(gate_ref.shape, F32)
    for kk in range(TOP_K):
        idx_out = jnp.where(lane == kk, ids[kk], idx_out)
        gate_out = jnp.where(lane == kk, es[kk] / den, gate_out)
    idx_ref[...] = idx_out
    gate_ref[...] = gate_out


def router(x, mod, g_norm, w_router, b_router, layer, geom):
    n_tok, d = x.shape
    tm = TOKEN_TILE // 2
    return pl.pallas_call(
        _router_kernel,
        grid=(n_tok // tm,),
        in_specs=[
            pl.BlockSpec((tm, d), lambda i: (i, 0)),
            pl.BlockSpec((None, None, 6, d), lambda i: (layer, geom.seq_of_token(i * tm), 0, 0)),
            pl.BlockSpec((None, 1, d), lambda i: (layer, 0, 0)),
            pl.BlockSpec((None, d, N_EXPERTS), lambda i: (layer, 0, 0)),
            pl.BlockSpec((None, 1, N_EXPERTS), lambda i: (layer, 0, 0)),
        ],
        out_specs=[
            pl.BlockSpec((tm, d), lambda i: (i, 0)),
            pl.BlockSpec((tm, 8), lambda i: (i, 0)),
            pl.BlockSpec((tm, 8), lambda i: (i, 0)),
        ],
        out_shape=[
            jax.ShapeDtypeStruct((n_tok, d), BF16),
            jax.ShapeDtypeStruct((n_tok, 8), jnp.int32),
            jax.ShapeDtypeStruct((n_tok, 8), F32),
        ],
        compiler_params=_params(("parallel",), VMEM_LIMIT),
        name="router",
    )(x, mod, g_norm.reshape(-1, 1, d), w_router, b_router.reshape(-1, 1, N_EXPERTS))


def _moe_kernel(be_ref, nu_ref, x_ref, wgu_ref, bgu_ref, wd_ref, bd_ref, o_ref,
                wgu_s, wd_s):
    i = pl.program_id(0)
    e = be_ref[i]
    e_prev = be_ref[jnp.maximum(i - 1, 0)]

    @pl.when((i == 0) | (e != e_prev))
    def _():
        wgu_s[...] = wgu_ref[...].astype(BF16)
        wd_s[...] = wd_ref[...].astype(BF16)

    @pl.when(i < nu_ref[0])
    def _():
        d_ff = wd_s.shape[0]
        gu = jnp.dot(x_ref[...], wgu_s[...], preferred_element_type=F32) + bgu_ref[...]
        glu = jnp.minimum(gu[:, :d_ff], SWIGLU_LIMIT)
        lin = jnp.clip(gu[:, d_ff:], -SWIGLU_LIMIT, SWIGLU_LIMIT)
        act = glu / (1.0 + jnp.exp(-SWIGLU_ALPHA * glu)) * (lin + 1.0)
        y = jnp.dot(act.astype(BF16), wd_s[...], preferred_element_type=F32) + bd_ref[...]
        o_ref[...] = y.astype(o_ref.dtype)

    @pl.when(i >= nu_ref[0])
    def _():
        o_ref[...] = jnp.zeros(o_ref.shape, o_ref.dtype)


def moe_experts(xs, blk_e, n_used, w_gu, b_gu, w_down, b_down, layer):
    n_rows, d = xs.shape
    d_ff = w_down.shape[2]
    nblk = n_rows // MOE_BLOCK
    grid_spec = pltpu.PrefetchScalarGridSpec(
        num_scalar_prefetch=2,
        grid=(nblk,),
        in_specs=[
            pl.BlockSpec((MOE_BLOCK, d), lambda i, be, nu: (jnp.minimum(i, nu[0] - 1), 0)),
            pl.BlockSpec((None, None, d, 2 * d_ff), lambda i, be, nu: (layer, be[i], 0, 0)),
            pl.BlockSpec((None, None, 1, 2 * d_ff), lambda i, be, nu: (layer, be[i], 0, 0)),
            pl.BlockSpec((None, None, d_ff, d), lambda i, be, nu: (layer, be[i], 0, 0)),
            pl.BlockSpec((None, None, 1, d), lambda i, be, nu: (layer, be[i], 0, 0)),
        ],
        out_specs=pl.BlockSpec((MOE_BLOCK, d), lambda i, be, nu: (i, 0)),
        scratch_shapes=[pltpu.VMEM((d, 2 * d_ff), BF16), pltpu.VMEM((d_ff, d), BF16)],
    )
    depth, n_e = w_gu.shape[:2]
    return pl.pallas_call(
        _moe_kernel,
        grid_spec=grid_spec,
        out_shape=jax.ShapeDtypeStruct((n_rows, d), BF16),
        compiler_params=_params(("arbitrary",), VMEM_LIMIT),
        name="moe_experts",
    )(blk_e, n_used, xs, w_gu, b_gu.reshape(depth, n_e, 1, 2 * d_ff), w_down,
      b_down.reshape(depth, n_e, 1, d))


def _combine_kernel(x_ref, mod_ref, y_ref, gate_ref, o_ref):
    g = gate_ref[...]
    acc = g[:, 0:1] * y_ref[0].astype(F32)
    for kk in range(1, TOP_K):
        acc = acc + g[:, kk:kk + 1] * y_ref[kk].astype(F32)
    o_ref[...] = x_ref[...] + mod_ref[5:6, :] * acc


def moe_combine(x, mod, yk, gates, layer, geom):
    n_tok, d = x.shape
    tm = TOKEN_TILE // 2
    return pl.pallas_call(
        _combine_kernel,
        grid=(n_tok // tm,),
        in_specs=[
            pl.BlockSpec((tm, d), lambda i: (i, 0)),
            pl.BlockSpec((None, None, 6, d), lambda i: (layer, geom.seq_of_token(i * tm), 0, 0)),
            pl.BlockSpec((TOP_K, tm, d), lambda i: (0, i, 0)),
            pl.BlockSpec((tm, 8), lambda i: (i, 0)),
        ],
        out_specs=pl.BlockSpec((tm, d), lambda i: (i, 0)),
        out_shape=jax.ShapeDtypeStruct(x.shape, F32),
        compiler_params=_params(("parallel",), VMEM_LIMIT),
        name="moe_combine",
    )(x, mod, yk, gates)


def moe_layer(x, mod, g_norm, w_router, b_router, w_gu, b_gu, w_down, b_down, layer, geom):
    n_tok, d = x.shape
    h, idx8, gates8 = router(x, mod, g_norm, w_router, b_router, layer, geom)
    idx = idx8[:, :TOP_K]
    chosen = jnp.sum(idx[:, :, None] == jnp.arange(N_EXPERTS, dtype=jnp.int32)[None, None, :],
                     axis=1, dtype=jnp.int32)
    incl = jnp.cumsum(chosen, axis=0)
    rank = jnp.take_along_axis(incl - chosen, idx, axis=1)
    counts = incl[-1]
    pcounts = ((counts + MOE_BLOCK - 1) // MOE_BLOCK) * MOE_BLOCK
    pend = jnp.cumsum(pcounts)
    pstart = pend - pcounts
    dest = pstart[idx] + rank
    nblk = (n_tok * TOP_K) // MOE_BLOCK + N_EXPERTS
    n_rows = nblk * MOE_BLOCK
    tok_ids = jnp.broadcast_to(jnp.arange(n_tok, dtype=jnp.int32)[:, None], dest.shape)
    tok = jnp.zeros((n_rows,), jnp.int32).at[dest.reshape(-1)].set(tok_ids.reshape(-1))
    n_used = (pend[-1] // MOE_BLOCK).astype(jnp.int32).reshape(1)
    blk_start = jnp.arange(nblk, dtype=jnp.int32) * MOE_BLOCK
    blk_e = jnp.searchsorted(pend, jnp.minimum(blk_start, pend[-1] - 1), side='right')
    blk_e = jnp.minimum(blk_e, N_EXPERTS - 1).astype(jnp.int32)
    xs = jnp.take(h, tok, axis=0)
    yb = moe_experts(xs, blk_e, n_used, w_gu, b_gu, w_down, b_down, layer)
    yk = jnp.take(yb, dest.T.reshape(-1), axis=0).reshape(TOP_K, n_tok, d)
    return moe_combine(x, mod, yk, gates8, layer, geom)


def _seq_positions(geom):
    return jnp.concatenate([jnp.arange(s, dtype=jnp.int32)
                            for _, n_b, s in geom.groups() for _ in range(n_b)])


def mixer_ab_layer(x, mod, g_norm, w_in, w_out, g_qn_a, g_kn_a, rpb, g_qn_b, g_kn_b, layer, geom):
    n_tok, d = x.shape
    hd = HEAD_DIM
    proj = norm_proj(x, mod, g_norm, w_in.astype(BF16), layer, geom, tn=1536)
    na_in = 3 * NA_HEADS * hd
    pa = proj[:, :na_in].reshape(n_tok, 3, NA_HEADS, hd).transpose(1, 2, 0, 3)
    qk = qk_prep(pa[:2], jnp.stack([g_qn_a * Q_SCALE, g_kn_a]))
    oa = neighborhood_attention(qk[0], qk[1], pa[2], _na_bias_table(rpb), geom)
    oa = oa.transpose(1, 0, 2).reshape(n_tok, NA_HEADS * hd)

    pb = proj[:, na_in:].reshape(n_tok, 3, N_DIL, DIL_HEADS, hd)
    perm = _rotate_half_matrix(hd, 1)
    od_all, lse_all = [], []
    for gi, (window, dil) in enumerate(DIL_PATTERNS):
        side = (window // 2) // dil
        od_g, lse_g = [], []
        for tok0, n_b, s_n in geom.groups():
            cls_len = s_n // dil
            xg = pb[tok0:tok0 + n_b * s_n, :, gi].reshape(n_b, cls_len, dil, 3, DIL_HEADS, hd)
            xg = xg.transpose(3, 0, 4, 2, 1, 5).reshape(3, n_b * DIL_HEADS, s_n, hd)
            pos = (jnp.arange(cls_len)[None, :] * dil + jnp.arange(dil)[:, None]).reshape(-1)
            cos, sin = _rope_angles(pos, hd)
            qk = qk_prep(xg[:2], jnp.stack([g_qn_b[gi] * Q_SCALE, g_kn_b[gi]]), (cos, sin, perm))
            o, lse = dilated_attention(qk[0], qk[1], xg[2], cls_len, side)

            def back(a):
                a = a.reshape(n_b, DIL_HEADS, dil, cls_len, hd).transpose(0, 3, 2, 1, 4)
                return a.reshape(n_b * s_n, DIL_HEADS * hd)

            od_g.append(back(o))
            lse_g.append(back(lse))
        od_all.append(jnp.concatenate(od_g, axis=0))
        lse_all.append(jnp.concatenate(lse_g, axis=0))
    od = jnp.stack(od_all)
    lse = jnp.stack(lse_all)
    return outproj_ab(x, mod, oa, od, lse, w_out.astype(BF16), layer, geom)


def mixer_c_layer(x, mod, g_norm, w_in, w_out, g_qn, g_kn, layer, geom):
    n_tok, d = x.shape
    hd = HEAD_DIM
    proj = norm_proj(x, mod, g_norm, w_in.astype(BF16), layer, geom, tn=w_in.shape[1])
    nq = C_Q_HEADS * hd
    nkv = C_KV_HEADS * hd
    q = proj[:, :nq].reshape(n_tok, C_Q_HEADS, hd).transpose(1, 0, 2)[None]
    k = proj[:, nq:nq + nkv].reshape(n_tok, C_KV_HEADS, hd).transpose(1, 0, 2)[None]
    v = proj[:, nq + nkv:].reshape(n_tok, C_KV_HEADS, hd)
    pos = _seq_positions(geom)
    half = hd // 2
    cr, sr = _rope_angles(pos // GRID_W, half)
    cc, sc = _rope_angles(pos % GRID_W, half)
    tables = (jnp.concatenate([cr, cc], axis=-1), jnp.concatenate([sr, sc], axis=-1),
              _rotate_half_matrix(half, 2))
    q = qk_prep(q, (g_qn * Q_SCALE)[None], tables)[0]
    k = qk_prep(k, g_kn[None], tables)[0]
    tk = FLASH_TK
    qt = q.transpose(0, 2, 1)
    k_tiles = k.reshape(C_KV_HEADS, n_tok // tk, tk, hd)
    vt_tiles = v.reshape(n_tok // tk, tk, C_KV_HEADS, hd).transpose(2, 0, 3, 1)
    ot = jnp.concatenate([flash_attention(qt, k_tiles, vt_tiles, tok0, n_b, s_n)
                          for tok0, n_b, s_n in geom.groups()], axis=-1)
    o = ot.transpose(2, 0, 1).reshape(n_tok, C_Q_HEADS * hd)
    return outproj(x, mod, o, w_out.astype(BF16), layer, geom)


def kernel(x_prompt, x_sample, c_prompt, c_sample, w_ada, b_ada, g_norm_mix, g_norm_ffn, w_in_ab, w_out_ab, g_qn_a, g_kn_a, rpb_a, g_qn_b, g_kn_b, w_in_c, w_out_c, g_qn_c, g_kn_c, w_router, b_router, w_gate_up, b_gate_up, w_down, b_down):
    b_p, s_p, d = x_prompt.shape
    b_d, s_d, _ = x_sample.shape
    geom = Geom(b_p, s_p, b_d, s_d)
    depth = w_ada.shape[0]
    x = jnp.concatenate([x_prompt.reshape(-1, d), x_sample.reshape(-1, d)], axis=0)
    mod = ada_modulation(jnp.concatenate([c_prompt, c_sample], axis=0), w_ada, b_ada)
    for layer in range(depth):
        i = layer // 2
        if layer % 2 == 0:
            x = mixer_ab_layer(x, mod, g_norm_mix, w_in_ab[i], w_out_ab[i], g_qn_a[i], g_kn_a[i],
                               rpb_a[i], g_qn_b[i], g_kn_b[i], layer, geom)
        else:
            x = mixer_c_layer(x, mod, g_norm_mix, w_in_c[i], w_out_c[i], g_qn_c[i], g_kn_c[i],
                              layer, geom)
        x = moe_layer(x, mod, g_norm_ffn, w_router, b_router, w_gate_up, b_gate_up, w_down,
                      b_down, layer, geom)
    y_prompt = x[:geom.n_p].reshape(b_p, s_p, d)
    y_sample = x[geom.n_p:].reshape(b_d, s_d, d)
    return (y_prompt, y_sample)
```

```python
import functools
import math
from typing import NamedTuple

import jax
import jax.numpy as jnp
from jax import lax
from jax.experimental import pallas as pl
from jax.experimental.pallas import tpu as pltpu

F32 = jnp.float32
BF16 = jnp.bfloat16
HIGHEST = lax.Precision.HIGHEST

GRID_W = 64
HEAD_DIM = 64
ROPE_THETA = 10000.0
EPS = 1e-6
NEG = -1e30
NA_HEADS = 8
NA_KH = 8
NA_KW = 16
DIL_HEADS = 8
DIL_PATTERNS = ((128, 1), (512, 4), (2048, 16))
N_DIL = len(DIL_PATTERNS)
C_Q_HEADS = 16
C_KV_HEADS = 4
N_EXPERTS = 32
TOP_K = 4
SWIGLU_LIMIT = 7.0
SWIGLU_ALPHA = 1.702
Q_SCALE = HEAD_DIM ** -0.5
LOG2_E = math.log2(math.e)
BF16_SUBLANES = 16

V7X_VMEM_BYTES = 64 * 1024 * 1024
VMEM_LIMIT = V7X_VMEM_BYTES - 8 * 1024 * 1024

TOKEN_TILE = 1024
NA_BLOCK = 8 * GRID_W
DIL_QBLOCK = 128
DIL_HALO = 128
FLASH_TQ = 256
FLASH_TK = 512
MOE_BLOCK = 512


class Geom(NamedTuple):
    b_p: int
    s_p: int
    b_d: int
    s_d: int

    @property
    def n_p(self):
        return self.b_p * self.s_p

    @property
    def n_tok(self):
        return self.n_p + self.b_d * self.s_d

    @property
    def n_seq(self):
        return self.b_p + self.b_d

    def groups(self):
        return ((0, self.b_p, self.s_p), (self.n_p, self.b_d, self.s_d))

    def seq_of_token(self, t0):
        return jnp.where(t0 < self.n_p, t0 // self.s_p,
                         self.b_p + (t0 - self.n_p) // self.s_d)


def _params(semantics, vmem=None):
    return pltpu.CompilerParams(dimension_semantics=semantics,
                                vmem_limit_bytes=vmem)


def _ada_kernel(c_ref, w_ref, b_ref, o_ref):
    c = c_ref[...]
    a = c / (1.0 + jnp.exp(-c))
    o_ref[...] = jnp.dot(a, w_ref[...], precision=HIGHEST,
                         preferred_element_type=F32) + b_ref[...]


def ada_modulation(c_all, w_ada, b_ada):
    depth, d, d6 = w_ada.shape
    n_seq = c_all.shape[0]
    rows = 8
    c_pad = jnp.zeros((rows, d), F32).at[:n_seq].set(c_all)
    out = pl.pallas_call(
        _ada_kernel,
        grid=(depth, d6 // d),
        in_specs=[
            pl.BlockSpec((rows, d), lambda l, j: (0, 0)),
            pl.BlockSpec((None, d, d), lambda l, j: (l, 0, j)),
            pl.BlockSpec((None, 1, d), lambda l, j: (l, 0, j)),
        ],
        out_specs=pl.BlockSpec((None, rows, d), lambda l, j: (l, 0, j)),
        out_shape=jax.ShapeDtypeStruct((depth, rows, d6), F32),
        compiler_params=_params(("arbitrary", "arbitrary")),
        name="ada_modulation",
    )(c_pad, w_ada, b_ada.reshape(depth, 1, d6))
    return out[:, :n_seq].reshape(depth, n_seq, 6, d)


def _modulated_norm(x, g, shift, scale):
    r = lax.rsqrt(jnp.mean(x * x, axis=-1, keepdims=True) + EPS)
    return (x * r * g) * (1.0 + scale) + shift


def _norm_proj_kernel(x_ref, mod_ref, g_ref, w_ref, o_ref, h_ref):
    @pl.when(pl.program_id(1) == 0)
    def _():
        h = _modulated_norm(x_ref[...], g_ref[...], mod_ref[0:1, :], mod_ref[1:2, :])
        h_ref[...] = h.astype(BF16)

    o_ref[...] = jnp.dot(h_ref[...], w_ref[...],
                         preferred_element_type=F32).astype(o_ref.dtype)


def norm_proj(x, mod, g_norm, w_bf16, layer, geom, tn):
    n_tok, d = x.shape
    n_out = w_bf16.shape[1]
    tm = TOKEN_TILE
    return pl.pallas_call(
        _norm_proj_kernel,
        grid=(n_tok // tm, n_out // tn),
        in_specs=[
            pl.BlockSpec((tm, d), lambda i, j: (i, 0)),
            pl.BlockSpec((None, None, 6, d),
                         lambda i, j: (layer, geom.seq_of_token(i * tm), 0, 0)),
            pl.BlockSpec((None, 1, d), lambda i, j: (layer, 0, 0)),
            pl.BlockSpec((d, tn), lambda i, j: (0, j)),
        ],
        out_specs=pl.BlockSpec((tm, tn), lambda i, j: (i, j)),
        out_shape=jax.ShapeDtypeStruct((n_tok, n_out), BF16),
        scratch_shapes=[pltpu.VMEM((tm, d), BF16)],
        compiler_params=_params(("parallel", "arbitrary"), VMEM_LIMIT),
        name="norm_proj",
    )(x, mod, g_norm.reshape(-1, 1, d), w_bf16)


def _prep_kernel(x_ref, g_ref, *rest, rope):
    o_ref = rest[-1]
    hb, ts, hd = x_ref.shape
    x = x_ref[...].astype(F32)
    r = lax.rsqrt(jnp.mean(x * x, axis=-1, keepdims=True) + EPS)
    y = x * r * g_ref[...]
    if rope:
        cos_ref, sin_ref, p_ref = rest[:3]
        yr = jnp.dot(y.reshape(hb * ts, hd).astype(BF16), p_ref[...],
                     preferred_element_type=F32).reshape(hb, ts, hd)
        y = y * cos_ref[...] + yr * sin_ref[...]
    o_ref[...] = y.astype(o_ref.dtype)


def qk_prep(x, gains, tables=None):
    a_n, r_n, s_n, hd = x.shape
    hb = math.gcd(r_n, 8)
    ts = min(s_n, 2048)
    rope = tables is not None
    in_specs = [
        pl.BlockSpec((None, hb, ts, hd), lambda a, r, s: (a, r, s, 0)),
        pl.BlockSpec((None, 1, hd), lambda a, r, s: (a, 0, 0)),
    ]
    args = [x, gains.reshape(a_n, 1, hd).astype(F32)]
    if rope:
        in_specs += [
            pl.BlockSpec((ts, hd), lambda a, r, s: (s, 0)),
            pl.BlockSpec((ts, hd), lambda a, r, s: (s, 0)),
            pl.BlockSpec((hd, hd), lambda a, r, s: (0, 0)),
        ]
        args += list(tables)
    return pl.pallas_call(
        functools.partial(_prep_kernel, rope=rope),
        grid=(a_n, r_n // hb, s_n // ts),
        in_specs=in_specs,
        out_specs=pl.BlockSpec((None, hb, ts, hd), lambda a, r, s: (a, r, s, 0)),
        out_shape=jax.ShapeDtypeStruct(x.shape, BF16),
        compiler_params=_params(("parallel", "parallel", "parallel")),
        name="qk_prep",
    )(*args)


def _rope_angles(pos, dim):
    inv = ROPE_THETA ** (-jnp.arange(0, dim, 2, dtype=F32) / dim)
    ang = pos.astype(F32)[:, None] * inv[None, :]
    ang = jnp.concatenate([ang, ang], axis=-1)
    return jnp.cos(ang), jnp.sin(ang)


def _rotate_half_matrix(dim, n_blocks):
    half = dim // 2
    i = jnp.arange(dim * n_blocks)
    blk, w = i // dim, i % dim
    src = blk * dim + jnp.where(w < half, w + half, w - half)
    sign = jnp.where(w < half, -1.0, 1.0)
    p = jnp.zeros((dim * n_blocks, dim * n_blocks), F32).at[src, i].set(sign)
    return p.astype(BF16)


def _na_bias_table(rpb):
    n_h = rpb.shape[0]
    qc = jnp.arange(GRID_W)[:, None]
    kc = jnp.arange(GRID_W)[None, :]
    c0 = jnp.clip(qc - NA_KW // 2, 0, GRID_W - NA_KW)
    ok = (kc >= c0) & (kc < c0 + NA_KW)
    dc = jnp.clip(kc - qc, -(NA_KW - 1), NA_KW - 1) + NA_KW - 1
    d0 = jnp.arange(NA_KH)[:, None]
    ik = jnp.arange(NA_KH)[None, :]
    dr = ik - d0 + NA_KH - 1
    bias = rpb[:, dr[:, :, None, None], dc[None, None, :, :]]
    bias = jnp.where(ok[None, None, None], bias, NEG)
    bias = bias.transpose(1, 0, 3, 2, 4).reshape(NA_KH, n_h, GRID_W, NA_KH * GRID_W)
    return bias.astype(BF16)


def _na_kernel(q_ref, kp_ref, kc_ref, kn_ref, vp_ref, vc_ref, vn_ref, tbl_ref,
               o_ref, ks_ref, vs_ref, *, geom):
    blk = NA_BLOCK
    j = pl.program_id(0)
    t0 = j * blk
    in_p = t0 < geom.n_p
    seq_blk0 = jnp.where(in_p, (t0 // geom.s_p) * (geom.s_p // blk),
                         geom.n_p // blk + ((t0 - geom.n_p) // geom.s_d) * (geom.s_d // blk))
    rows = jnp.where(in_p, geom.s_p // GRID_W, geom.s_d // GRID_W)
    jl = j - seq_blk0
    ks_ref[:, 0:blk, :] = kp_ref[...]
    ks_ref[:, blk:2 * blk, :] = kc_ref[...]
    ks_ref[:, 2 * blk:3 * blk, :] = kn_ref[...]
    vs_ref[:, 0:blk, :] = vp_ref[...]
    vs_ref[:, blk:2 * blk, :] = vc_ref[...]
    vs_ref[:, 2 * blk:3 * blk, :] = vn_ref[...]
    n_keys = NA_KH * GRID_W

    def body(i, carry):
        r = jl * NA_KH + i
        r0 = jnp.clip(r - NA_KH // 2, 0, rows - NA_KH)
        off = pl.multiple_of((r0 - (jl - 1) * NA_KH) * GRID_W, GRID_W)
        d0 = r - r0
        qoff = pl.multiple_of(i * GRID_W, GRID_W)
        for h in range(NA_HEADS):
            q = q_ref[h, pl.ds(qoff, GRID_W), :]
            k = ks_ref[h, pl.ds(off, n_keys), :]
            v = vs_ref[h, pl.ds(off, n_keys), :]
            s = lax.dot_general(q, k, (((1,), (1,)), ((), ())), preferred_element_type=F32)
            s = s + tbl_ref[d0, h].astype(F32)
            m = jnp.max(s, axis=-1, keepdims=True)
            p = jnp.exp(s - m)
            l = jnp.sum(p, axis=-1, keepdims=True)
            o = jnp.dot(p.astype(BF16), v, preferred_element_type=F32) / l
            o_ref[h, pl.ds(qoff, GRID_W), :] = o.astype(o_ref.dtype)
        return carry

    lax.fori_loop(0, NA_KH, body, 0)


def neighborhood_attention(q, k, v, tbl, geom):
    n_h, n_tok, hd = q.shape
    blk = NA_BLOCK
    nb = n_tok // blk
    cur = pl.BlockSpec((n_h, blk, hd), lambda j: (0, j, 0))
    prev = pl.BlockSpec((n_h, blk, hd), lambda j: (0, jnp.maximum(j - 1, 0), 0))
    nxt = pl.BlockSpec((n_h, blk, hd), lambda j: (0, jnp.minimum(j + 1, nb - 1), 0))
    return pl.pallas_call(
        functools.partial(_na_kernel, geom=geom),
        grid=(nb,),
        in_specs=[cur, prev, cur, nxt, prev, cur, nxt,
                  pl.BlockSpec(tbl.shape, lambda j: (0, 0, 0, 0))],
        out_specs=cur,
        out_shape=jax.ShapeDtypeStruct(q.shape, BF16),
        scratch_shapes=[pltpu.VMEM((n_h, 3 * blk, hd), BF16),
                        pltpu.VMEM((n_h, 3 * blk, hd), BF16)],
        compiler_params=_params(("parallel",), VMEM_LIMIT),
        name="neighborhood_attention",
    )(q, k, k, k, v, v, v, tbl)


def _dil_kernel(q_ref, kp_ref, kc_ref, kn_ref, vp_ref, vc_ref, vn_ref,
                o_ref, lse_ref, ks_ref, vs_ref, *, cls_len, side, chunk):
    c = pl.program_id(1)
    hl = DIL_HALO
    ks_ref[0:hl, :] = kp_ref[...]
    ks_ref[hl:hl + chunk, :] = kc_ref[...]
    ks_ref[hl + chunk:, :] = kn_ref[...]
    vs_ref[0:hl, :] = vp_ref[...]
    vs_ref[hl:hl + chunk, :] = vc_ref[...]
    vs_ref[hl + chunk:, :] = vn_ref[...]
    qb = DIL_QBLOCK
    kb = qb + 2 * side
    base = c * chunk
    hd = q_ref.shape[-1]

    def body(i, carry):
        qoff = pl.multiple_of(i * qb, qb)
        koff = pl.multiple_of(i * qb + hl - side, side)
        q = q_ref[pl.ds(qoff, qb), :]
        k = ks_ref[pl.ds(koff, kb), :]
        v = vs_ref[pl.ds(koff, kb), :]
        s = lax.dot_general(q, k, (((1,), (1,)), ((), ())), preferred_element_type=F32)
        pq = base + i * qb + lax.broadcasted_iota(jnp.int32, (qb, kb), 0)
        pk = base + i * qb - side + lax.broadcasted_iota(jnp.int32, (qb, kb), 1)
        same = (pk // cls_len) == (pq // cls_len)
        near = jnp.abs(pk - pq) <= side
        s = jnp.where(near, jnp.where(same, s, NEG), NEG)
        m = jnp.max(s, axis=-1, keepdims=True)
        p = jnp.exp(s - m)
        l = jnp.sum(p, axis=-1, keepdims=True)
        o = jnp.dot(p.astype(BF16), v, preferred_element_type=F32) / l
        o_ref[pl.ds(qoff, qb), :] = o.astype(o_ref.dtype)
        lse_ref[pl.ds(qoff, qb), :] = jnp.broadcast_to(m + jnp.log(l), (qb, hd))
        return carry

    lax.fori_loop(0, chunk // qb, body, 0)


def dilated_attention(q, k, v, cls_len, side):
    r_n, s_n, hd = q.shape
    assert side == DIL_QBLOCK // 2 and cls_len % DIL_QBLOCK == 0
    chunk = min(s_n, 2048)
    nc = s_n // chunk
    per = chunk // DIL_HALO
    n_halo = s_n // DIL_HALO
    cur = pl.BlockSpec((None, chunk, hd), lambda r, c: (r, c, 0))
    prev = pl.BlockSpec((None, DIL_HALO, hd), lambda r, c: (r, jnp.maximum(c * per - 1, 0), 0))
    nxt = pl.BlockSpec((None, DIL_HALO, hd),
                       lambda r, c: (r, jnp.minimum((c + 1) * per, n_halo - 1), 0))
    return pl.pallas_call(
        functools.partial(_dil_kernel, cls_len=cls_len, side=side, chunk=chunk),
        grid=(r_n, nc),
        in_specs=[cur, prev, cur, nxt, prev, cur, nxt],
        out_specs=[cur, cur],
        out_shape=[jax.ShapeDtypeStruct(q.shape, BF16),
                   jax.ShapeDtypeStruct(q.shape, F32)],
        scratch_shapes=[pltpu.VMEM((chunk + 2 * DIL_HALO, hd), BF16),
                        pltpu.VMEM((chunk + 2 * DIL_HALO, hd), BF16)],
        compiler_params=_params(("parallel", "parallel")),
        name="dilated_attention",
    )(q, k, k, k, v, v, v)


def _outproj_ab_kernel(x_ref, mod_ref, oa_ref, od_ref, lse_ref, w_ref, o_ref):
    half = oa_ref.shape[-1]
    l0, l1, l2 = lse_ref[0], lse_ref[1], lse_ref[2]
    m = jnp.maximum(jnp.maximum(l0, l1), l2)
    e0, e1, e2 = jnp.exp(l0 - m), jnp.exp(l1 - m), jnp.exp(l2 - m)
    ob = (e0 * od_ref[0].astype(F32) + e1 * od_ref[1].astype(F32)
          + e2 * od_ref[2].astype(F32)) / (e0 + e1 + e2)
    res = jnp.dot(oa_ref[...], w_ref[0:half, :], preferred_element_type=F32)
    res = res + jnp.dot(ob.astype(BF16), w_ref[half:, :], preferred_element_type=F32)
    o_ref[...] = x_ref[...] + mod_ref[2:3, :] * res


def outproj_ab(x, mod, oa, od, lse, w_bf16, layer, geom):
    n_tok, d = x.shape
    half = oa.shape[-1]
    tm = TOKEN_TILE // 2
    return pl.pallas_call(
        _outproj_ab_kernel,
        grid=(n_tok // tm,),
        in_specs=[
            pl.BlockSpec((tm, d), lambda i: (i, 0)),
            pl.BlockSpec((None, None, 6, d), lambda i: (layer, geom.seq_of_token(i * tm), 0, 0)),
            pl.BlockSpec((tm, half), lambda i: (i, 0)),
            pl.BlockSpec((N_DIL, tm, half), lambda i: (0, i, 0)),
            pl.BlockSpec((N_DIL, tm, half), lambda i: (0, i, 0)),
            pl.BlockSpec(w_bf16.shape, lambda i: (0, 0)),
        ],
        out_specs=pl.BlockSpec((tm, d), lambda i: (i, 0)),
        out_shape=jax.ShapeDtypeStruct(x.shape, F32),
        compiler_params=_params(("parallel",), VMEM_LIMIT),
        name="outproj_ab",
    )(x, mod, oa, od, lse, w_bf16)


def _outproj_kernel(x_ref, mod_ref, o_in_ref, w_ref, o_ref):
    res = jnp.dot(o_in_ref[...], w_ref[...], preferred_element_type=F32)
    o_ref[...] = x_ref[...] + mod_ref[2:3, :] * res


def outproj(x, mod, o_in, w_bf16, layer, geom):
    n_tok, d = x.shape
    tm = TOKEN_TILE
    return pl.pallas_call(
        _outproj_kernel,
        grid=(n_tok // tm,),
        in_specs=[
            pl.BlockSpec((tm, d), lambda i: (i, 0)),
            pl.BlockSpec((None, None, 6, d), lambda i: (layer, geom.seq_of_token(i * tm), 0, 0)),
            pl.BlockSpec((tm, o_in.shape[1]), lambda i: (i, 0)),
            pl.BlockSpec(w_bf16.shape, lambda i: (0, 0)),
        ],
        out_specs=pl.BlockSpec((tm, d), lambda i: (i, 0)),
        out_shape=jax.ShapeDtypeStruct(x.shape, F32),
        compiler_params=_params(("parallel",), VMEM_LIMIT),
        name="outproj",
    )(x, mod, o_in, w_bf16)


def _flash_kernel(qt_ref, k_ref, vt_ref, o_ref, sa_ref, sb_ref, m_ref, acc_ref):
    hd = k_ref.shape[-1]
    n_kt = k_ref.shape[0]
    qt = qt_ref[...]
    m_ref[...] = jnp.full(m_ref.shape, -jnp.inf, F32)
    acc_ref[...] = jnp.zeros(acc_ref.shape, F32)

    def scores(j, s_ref):
        s_ref[...] = jnp.dot(k_ref[j], qt, preferred_element_type=F32)

    def consume(j, s_ref):
        s = s_ref[...]
        m_old = m_ref[...]
        m_new = jnp.maximum(m_old, jnp.max(s, axis=0, keepdims=True))
        alpha = jnp.exp2(m_old - m_new)
        p = jnp.exp2(s - m_new).astype(BF16)
        acc_ref[...] = alpha * acc_ref[...] + jnp.dot(vt_ref[j], p, preferred_element_type=F32)
        m_ref[...] = m_new

    scores(0, sa_ref)

    def body(jj, carry):
        j = 2 * jj
        scores(j + 1, sb_ref)
        consume(j, sa_ref)
        scores(jnp.minimum(j + 2, n_kt - 1), sa_ref)
        consume(j + 1, sb_ref)
        return carry

    lax.fori_loop(0, n_kt // 2, body, 0)
    acc = acc_ref[...]
    o_ref[...] = (acc[:hd] / acc[hd:hd + 1]).astype(o_ref.dtype)


def flash_attention(qt, k_tiles, vt_tiles, tok0, n_b, s_n):
    n_kv, hd, _ = qt.shape
    hv = vt_tiles.shape[2]
    n_g = C_Q_HEADS // n_kv
    tq, tk = FLASH_TQ, FLASH_TK
    nq = s_n // tq
    kt_per = s_n // tk
    assert kt_per % 2 == 0
    wq = n_g * tq
    return pl.pallas_call(
        _flash_kernel,
        grid=(n_b, n_kv, nq),
        in_specs=[
            pl.BlockSpec((None, hd, wq), lambda b, h, i: (h, 0, (tok0 + b * s_n) // tq + i)),
            pl.BlockSpec((None, kt_per, tk, hd), lambda b, h, i: (h, (tok0 + b * s_n) // s_n, 0, 0)),
            pl.BlockSpec((None, kt_per, hv, tk), lambda b, h, i: (h, (tok0 + b * s_n) // s_n, 0, 0)),
        ],
        out_specs=pl.BlockSpec((None, hd, wq), lambda b, h, i: (h, 0, b * nq + i)),
        out_shape=jax.ShapeDtypeStruct((n_kv, hd, n_b * s_n * n_g), BF16),
        scratch_shapes=[pltpu.VMEM((tk, wq), F32), pltpu.VMEM((tk, wq), F32),
                        pltpu.VMEM((1, wq), F32), pltpu.VMEM((hv, wq), F32)],
        compiler_params=_params(("parallel", "parallel", "arbitrary"), VMEM_LIMIT),
        name="flash_attention",
    )(qt, k_tiles, vt_tiles)


def _router_kernel(x_ref, mod_ref, g_ref, wr_ref, br_ref, h_ref, idx_ref, gate_ref):
    h = _modulated_norm(x_ref[...], g_ref[...], mod_ref[3:4, :], mod_ref[4:5, :])
    h_ref[...] = h.astype(BF16)
    logits = jnp.dot(h, wr_ref[...], precision=HIGHEST,
                     preferred_element_type=F32) + br_ref[...]
    tm, n_e = logits.shape
    iota = lax.broadcasted_iota(jnp.int32, (tm, n_e), 1)
    vals, ids = [], []
    cur = logits
    for _ in range(TOP_K):
        m = jnp.max(cur, axis=-1, keepdims=True)
        am = jnp.min(jnp.where(cur == m, iota, n_e), axis=-1, keepdims=True)
        vals.append(m)
        ids.append(am)
        cur = jnp.where(iota == am, -jnp.inf, cur)
    es = [jnp.exp(v - vals[0]) for v in vals]
    den = es[0] + es[1] + es[2] + es[3]
    lane = lax.broadcasted_iota(jnp.int32, idx_ref.shape, 1)
    idx_out = jnp.zeros(idx_ref.shape, jnp.int32)
    gate_out = jnp.zeros(gate_ref.shape, F32)
    for kk in range(TOP_K):
        idx_out = jnp.where(lane == kk, ids[kk], idx_out)
        gate_out = jnp.where(lane == kk, es[kk] / den, gate_out)
    idx_ref[...] = idx_out
    gate_ref[...] = gate_out


def router(x, mod, g_norm, w_router, b_router, layer, geom):
    n_tok, d = x.shape
    tm = TOKEN_TILE // 2
    return pl.pallas_call(
        _router_kernel,
        grid=(n_tok // tm,),
        in_specs=[
            pl.BlockSpec((tm, d), lambda i: (i, 0)),
            pl.BlockSpec((None, None, 6, d), lambda i: (layer, geom.seq_of_token(i * tm), 0, 0)),
            pl.BlockSpec((None, 1, d), lambda i: (layer, 0, 0)),
            pl.BlockSpec((None, d, N_EXPERTS), lambda i: (layer, 0, 0)),
            pl.BlockSpec((None, 1, N_EXPERTS), lambda i: (layer, 0, 0)),
        ],
        out_specs=[
            pl.BlockSpec((tm, d), lambda i: (i, 0)),
            pl.BlockSpec((tm, 8), lambda i: (i, 0)),
            pl.BlockSpec((tm, 8), lambda i: (i, 0)),
        ],
        out_shape=[
            jax.ShapeDtypeStruct((n_tok, d), BF16),
            jax.ShapeDtypeStruct((n_tok, 8), jnp.int32),
            jax.ShapeDtypeStruct((n_tok, 8), F32),
        ],
        compiler_params=_params(("parallel",), VMEM_LIMIT),
        name="router",
    )(x, mod, g_norm.reshape(-1, 1, d), w_router, b_router.reshape(-1, 1, N_EXPERTS))


def _moe_kernel(be_ref, nu_ref, x_ref, wgu_ref, bgu_ref, wd_ref, bd_ref, o_ref,
                wgu_s, wd_s):
    i = pl.program_id(0)
    e = be_ref[i]
    e_prev = be_ref[jnp.maximum(i - 1, 0)]

    @pl.when((i == 0) | (e != e_prev))
    def _():
        wgu_s[...] = wgu_ref[...].astype(BF16)
        wd_s[...] = wd_ref[...].astype(BF16)

    @pl.when(i < nu_ref[0])
    def _():
        d_ff = wd_s.shape[0]
        gu = jnp.dot(x_ref[...], wgu_s[...], preferred_element_type=F32) + bgu_ref[...]
        glu = jnp.minimum(gu[:, :d_ff], SWIGLU_LIMIT)
        lin = jnp.clip(gu[:, d_ff:], -SWIGLU_LIMIT, SWIGLU_LIMIT)
        act = glu / (1.0 + jnp.exp(-SWIGLU_ALPHA * glu)) * (lin + 1.0)
        y = jnp.dot(act.astype(BF16), wd_s[...], preferred_element_type=F32) + bd_ref[...]
        o_ref[...] = y.astype(o_ref.dtype)

    @pl.when(i >= nu_ref[0])
    def _():
        o_ref[...] = jnp.zeros(o_ref.shape, o_ref.dtype)


def moe_experts(xs, blk_e, n_used, w_gu, b_gu, w_down, b_down, layer):
    n_rows, d = xs.shape
    d_ff = w_down.shape[2]
    nblk = n_rows // MOE_BLOCK
    grid_spec = pltpu.PrefetchScalarGridSpec(
        num_scalar_prefetch=2,
        grid=(nblk,),
        in_specs=[
            pl.BlockSpec((MOE_BLOCK, d), lambda i, be, nu: (jnp.minimum(i, nu[0] - 1), 0)),
            pl.BlockSpec((None, None, d, 2 * d_ff), lambda i, be, nu: (layer, be[i], 0, 0)),
            pl.BlockSpec((None, None, 1, 2 * d_ff), lambda i, be, nu: (layer, be[i], 0, 0)),
            pl.BlockSpec((None, None, d_ff, d), lambda i, be, nu: (layer, be[i], 0, 0)),
            pl.BlockSpec((None, None, 1, d), lambda i, be, nu: (layer, be[i], 0, 0)),
        ],
        out_specs=pl.BlockSpec((MOE_BLOCK, d), lambda i, be, nu: (i, 0)),
        scratch_shapes=[pltpu.VMEM((d, 2 * d_ff), BF16), pltpu.VMEM((d_ff, d), BF16)],
    )
    depth, n_e = w_gu.shape[:2]
    return pl.pallas_call(
        _moe_kernel,
        grid_spec=grid_spec,
        out_shape=jax.ShapeDtypeStruct((n_rows, d), BF16),
        compiler_params=_params(("arbitrary",), VMEM_LIMIT),
        name="moe_experts",
    )(blk_e, n_used, xs, w_gu, b_gu.reshape(depth, n_e, 1, 2 * d_ff), w_down,
      b_down.reshape(depth, n_e, 1, d))


def _combine_kernel(x_ref, mod_ref, y_ref, gate_ref, o_ref):
    g = gate_ref[...]
    acc = g[:, 0:1] * y_ref[0].astype(F32)
    for kk in range(1, TOP_K):
        acc = acc + g[:, kk:kk + 1] * y_ref[kk].astype(F32)
    o_ref[...] = x_ref[...] + mod_ref[5:6, :] * acc


def moe_combine(x, mod, yk, gates, layer, geom):
    n_tok, d = x.shape
    tm = TOKEN_TILE // 2
    return pl.pallas_call(
        _combine_kernel,
        grid=(n_tok // tm,),
        in_specs=[
            pl.BlockSpec((tm, d), lambda i: (i, 0)),
            pl.BlockSpec((None, None, 6, d), lambda i: (layer, geom.seq_of_token(i * tm), 0, 0)),
            pl.BlockSpec((TOP_K, tm, d), lambda i: (0, i, 0)),
            pl.BlockSpec((tm, 8), lambda i: (i, 0)),
        ],
        out_specs=pl.BlockSpec((tm, d), lambda i: (i, 0)),
        out_shape=jax.ShapeDtypeStruct(x.shape, F32),
        compiler_params=_params(("parallel",), VMEM_LIMIT),
        name="moe_combine",
    )(x, mod, yk, gates)


def moe_layer(x, mod, g_norm, w_router, b_router, w_gu, b_gu, w_down, b_down, layer, geom):
    n_tok, d = x.shape
    h, idx8, gates8 = router(x, mod, g_norm, w_router, b_router, layer, geom)
    idx = idx8[:, :TOP_K]
    chosen = jnp.sum(idx[:, :, None] == jnp.arange(N_EXPERTS, dtype=jnp.int32)[None, None, :],
                     axis=1, dtype=jnp.int32)
    incl = jnp.cumsum(chosen, axis=0)
    rank = jnp.take_along_axis(incl - chosen, idx, axis=1)
    counts = incl[-1]
    pcounts = ((counts + MOE_BLOCK - 1) // MOE_BLOCK) * MOE_BLOCK
    pend = jnp.cumsum(pcounts)
    pstart = pend - pcounts
    dest = pstart[idx] + rank
    nblk = (n_tok * TOP_K) // MOE_BLOCK + N_EXPERTS
    n_rows = nblk * MOE_BLOCK
    tok_ids = jnp.broadcast_to(jnp.arange(n_tok, dtype=jnp.int32)[:, None], dest.shape)
    tok = jnp.zeros((n_rows,), jnp.int32).at[dest.reshape(-1)].set(tok_ids.reshape(-1))
    n_used = (pend[-1] // MOE_BLOCK).astype(jnp.int32).reshape(1)
    blk_start = jnp.arange(nblk, dtype=jnp.int32) * MOE_BLOCK
    blk_e = jnp.searchsorted(pend, jnp.minimum(blk_start, pend[-1] - 1), side='right')
    blk_e = jnp.minimum(blk_e, N_EXPERTS - 1).astype(jnp.int32)
    xs = jnp.take(h, tok, axis=0)
    yb = moe_experts(xs, blk_e, n_used, w_gu, b_gu, w_down, b_down, layer)
    yk = jnp.take(yb, dest.T.reshape(-1), axis=0).reshape(TOP_K, n_tok, d)
    return moe_combine(x, mod, yk, gates8, layer, geom)


def _seq_positions(geom):
    return jnp.concatenate([jnp.arange(s, dtype=jnp.int32)
                            for _, n_b, s in geom.groups() for _ in range(n_b)])


def mixer_ab_layer(x, mod, g_norm, w_in, w_out, g_qn_a, g_kn_a, rpb, g_qn_b, g_kn_b, layer, geom):
    n_tok, d = x.shape
    hd = HEAD_DIM
    proj = norm_proj(x, mod, g_norm, w_in.astype(BF16), layer, geom, tn=1536)
    na_in = 3 * NA_HEADS * hd
    pa = proj[:, :na_in].reshape(n_tok, 3, NA_HEADS, hd).transpose(1, 2, 0, 3)
    qk = qk_prep(pa[:2], jnp.stack([g_qn_a * Q_SCALE, g_kn_a]))
    oa = neighborhood_attention(qk[0], qk[1], pa[2], _na_bias_table(rpb), geom)
    oa = oa.transpose(1, 0, 2).reshape(n_tok, NA_HEADS * hd)

    pb = proj[:, na_in:].reshape(n_tok, 3, N_DIL, DIL_HEADS, hd)
    perm = _rotate_half_matrix(hd, 1)
    od_all, lse_all = [], []
    for gi, (window, dil) in enumerate(DIL_PATTERNS):
        side = (window // 2) // dil
        od_g, lse_g = [], []
        for tok0, n_b, s_n in geom.groups():
            cls_len = s_n // dil
            xg = pb[tok0:tok0 + n_b * s_n, :, gi].reshape(n_b, cls_len, dil, 3, DIL_HEADS, hd)
            xg = xg.transpose(3, 0, 4, 2, 1, 5).reshape(3, n_b * DIL_HEADS, s_n, hd)
            pos = (jnp.arange(cls_len)[None, :] * dil + jnp.arange(dil)[:, None]).reshape(-1)
            cos, sin = _rope_angles(pos, hd)
            qk = qk_prep(xg[:2], jnp.stack([g_qn_b[gi] * Q_SCALE, g_kn_b[gi]]), (cos, sin, perm))
            o, lse = dilated_attention(qk[0], qk[1], xg[2], cls_len, side)

            def back(a):
                a = a.reshape(n_b, DIL_HEADS, dil, cls_len, hd).transpose(0, 3, 2, 1, 4)
                return a.reshape(n_b * s_n, DIL_HEADS * hd)

            od_g.append(back(o))
            lse_g.append(back(lse))
        od_all.append(jnp.concatenate(od_g, axis=0))
        lse_all.append(jnp.concatenate(lse_g, axis=0))
    od = jnp.stack(od_all)
    lse = jnp.stack(lse_all)
    return outproj_ab(x, mod, oa, od, lse, w_out.astype(BF16), layer, geom)


def mixer_c_layer(x, mod, g_norm, w_in, w_out, g_qn, g_kn, layer, geom):
    n_tok, d = x.shape
    hd = HEAD_DIM
    proj = norm_proj(x, mod, g_norm, w_in.astype(BF16), layer, geom, tn=w_in.shape[1])
    nq = C_Q_HEADS * hd
    nkv = C_KV_HEADS * hd
    q = proj[:, :nq].reshape(n_tok, C_Q_HEADS, hd).transpose(1, 0, 2)[None]
    k = proj[:, nq:nq + nkv].reshape(n_tok, C_KV_HEADS, hd).transpose(1, 0, 2)[None]
    v = proj[:, nq + nkv:].reshape(n_tok, C_KV_HEADS, hd)
    pos = _seq_positions(geom)
    half = hd // 2
    cr, sr = _rope_angles(pos // GRID_W, half)
    cc, sc = _rope_angles(pos % GRID_W, half)
    tables = (jnp.concatenate([cr, cc], axis=-1), jnp.concatenate([sr, sc], axis=-1),
              _rotate_half_matrix(half, 2))
    q = qk_prep(q, (g_qn * (Q_SCALE * LOG2_E))[None], tables)[0]
    k = qk_prep(k, g_kn[None], tables)[0]
    tk = FLASH_TK
    n_g = C_Q_HEADS // C_KV_HEADS
    tq = FLASH_TQ
    qt = q.reshape(C_KV_HEADS, n_g, n_tok // tq, tq, hd).transpose(0, 4, 2, 1, 3)
    qt = qt.reshape(C_KV_HEADS, hd, n_tok * n_g)
    k_tiles = k.reshape(C_KV_HEADS, n_tok // tk, tk, hd)
    vt_tiles = v.reshape(n_tok // tk, tk, C_KV_HEADS, hd).transpose(2, 0, 3, 1)
    ones = jnp.ones((C_KV_HEADS, n_tok // tk, BF16_SUBLANES, tk), BF16)
    vt_tiles = jnp.concatenate([vt_tiles, ones], axis=2)
    ot = jnp.concatenate([flash_attention(qt, k_tiles, vt_tiles, tok0, n_b, s_n)
                          for tok0, n_b, s_n in geom.groups()], axis=-1)
    o = ot.reshape(C_KV_HEADS, hd, n_tok // tq, n_g, tq).transpose(2, 4, 0, 3, 1)
    o = o.reshape(n_tok, C_Q_HEADS * hd)
    return outproj(x, mod, o, w_out.astype(BF16), layer, geom)


def kernel(x_prompt, x_sample, c_prompt, c_sample, w_ada, b_ada, g_norm_mix, g_norm_ffn, w_in_ab, w_out_ab, g_qn_a, g_kn_a, rpb_a, g_qn_b, g_kn_b, w_in_c, w_out_c, g_qn_c, g_kn_c, w_router, b_router, w_gate_up, b_gate_up, w_down, b_down):
    b_p, s_p, d = x_prompt.shape
    b_d, s_d, _ = x_sample.shape
    geom = Geom(b_p, s_p, b_d, s_d)
    depth = w_ada.shape[0]
    x = jnp.concatenate([x_prompt.reshape(-1, d), x_sample.reshape(-1, d)], axis=0)
    mod = ada_modulation(jnp.concatenate([c_prompt, c_sample], axis=0), w_ada, b_ada)
    for layer in range(depth):
        i = layer // 2
        if layer % 2 == 0:
            x = mixer_ab_layer(x, mod, g_norm_mix, w_in_ab[i], w_out_ab[i], g_qn_a[i], g_kn_a[i],
                               rpb_a[i], g_qn_b[i], g_kn_b[i], layer, geom)
        else:
            x = mixer_c_layer(x, mod, g_norm_mix, w_in_c[i], w_out_c[i], g_qn_c[i], g_kn_c[i],
                              layer, geom)
        x = moe_layer(x, mod, g_norm_ffn, w_router, b_router, w_gate_up, b_gate_up, w_down,
                      b_down, layer, geom)
    y_prompt = x[:geom.n_p].reshape(b_p, s_p, d)
    y_sample = x[geom.n_p:].reshape(b_d, s_d, d)
    return (y_prompt, y_sample)
```

```python
import functools
import math
from typing import NamedTuple

import jax
import jax.numpy as jnp
from jax import lax
from jax.experimental import pallas as pl
from jax.experimental.pallas import tpu as pltpu

F32 = jnp.float32
BF16 = jnp.bfloat16
HIGHEST = lax.Precision.HIGHEST

GRID_W = 64
HEAD_DIM = 64
ROPE_THETA = 10000.0
EPS = 1e-6
NEG = -1e30
NA_HEADS = 8
NA_KH = 8
NA_KW = 16
DIL_HEADS = 8
DIL_PATTERNS = ((128, 1), (512, 4), (2048, 16))
N_DIL = len(DIL_PATTERNS)
C_Q_HEADS = 16
C_KV_HEADS = 4
N_EXPERTS = 32
TOP_K = 4
SWIGLU_LIMIT = 7.0
SWIGLU_ALPHA = 1.702
Q_SCALE = HEAD_DIM ** -0.5
LOG2_E = math.log2(math.e)
BF16_SUBLANES = 16

V7X_VMEM_BYTES = 64 * 1024 * 1024
VMEM_LIMIT = V7X_VMEM_BYTES - 8 * 1024 * 1024

TOKEN_TILE = 1024
NA_BLOCK = 8 * GRID_W
DIL_QBLOCK = 128
DIL_HALO = 128
NA_ROWS_PER_TRIP = 2
DIL_BLOCKS_PER_TRIP = 4
FLASH_TQ = 256
FLASH_TK = 512
FLASH_UNROLL = 4
MOE_BLOCK = 512


class Geom(NamedTuple):
    b_p: int
    s_p: int
    b_d: int
    s_d: int

    @property
    def n_p(self):
        return self.b_p * self.s_p

    @property
    def n_tok(self):
        return self.n_p + self.b_d * self.s_d

    @property
    def n_seq(self):
        return self.b_p + self.b_d

    def groups(self):
        return ((0, self.b_p, self.s_p), (self.n_p, self.b_d, self.s_d))

    def seq_of_token(self, t0):
        return jnp.where(t0 < self.n_p, t0 // self.s_p,
                         self.b_p + (t0 - self.n_p) // self.s_d)


def _params(semantics, vmem=None):
    return pltpu.CompilerParams(dimension_semantics=semantics,
                                vmem_limit_bytes=vmem)


def _ada_kernel(c_ref, w_ref, b_ref, o_ref):
    c = c_ref[...]
    a = c / (1.0 + jnp.exp(-c))
    o_ref[...] = jnp.dot(a, w_ref[...], precision=HIGHEST,
                         preferred_element_type=F32) + b_ref[...]


def ada_modulation(c_all, w_ada, b_ada):
    depth, d, d6 = w_ada.shape
    n_seq = c_all.shape[0]
    rows = 8
    c_pad = jnp.zeros((rows, d), F32).at[:n_seq].set(c_all)
    out = pl.pallas_call(
        _ada_kernel,
        grid=(depth, d6 // d),
        in_specs=[
            pl.BlockSpec((rows, d), lambda l, j: (0, 0)),
            pl.BlockSpec((None, d, d), lambda l, j: (l, 0, j)),
            pl.BlockSpec((None, 1, d), lambda l, j: (l, 0, j)),
        ],
        out_specs=pl.BlockSpec((None, rows, d), lambda l, j: (l, 0, j)),
        out_shape=jax.ShapeDtypeStruct((depth, rows, d6), F32),
        compiler_params=_params(("arbitrary", "arbitrary")),
        name="ada_modulation",
    )(c_pad, w_ada, b_ada.reshape(depth, 1, d6))
    return out[:, :n_seq].reshape(depth, n_seq, 6, d)


def _modulated_norm(x, g, shift, scale):
    r = lax.rsqrt(jnp.mean(x * x, axis=-1, keepdims=True) + EPS)
    return (x * r * g) * (1.0 + scale) + shift


def _norm_proj_kernel(x_ref, mod_ref, g_ref, w_ref, o_ref, h_ref):
    @pl.when(pl.program_id(1) == 0)
    def _():
        h = _modulated_norm(x_ref[...], g_ref[...], mod_ref[0:1, :], mod_ref[1:2, :])
        h_ref[...] = h.astype(BF16)

    o_ref[...] = jnp.dot(h_ref[...], w_ref[...],
                         preferred_element_type=F32).astype(o_ref.dtype)


def norm_proj(x, mod, g_norm, w_bf16, layer, geom, tn):
    n_tok, d = x.shape
    n_out = w_bf16.shape[1]
    tm = TOKEN_TILE
    return pl.pallas_call(
        _norm_proj_kernel,
        grid=(n_tok // tm, n_out // tn),
        in_specs=[
            pl.BlockSpec((tm, d), lambda i, j: (i, 0)),
            pl.BlockSpec((None, None, 6, d),
                         lambda i, j: (layer, geom.seq_of_token(i * tm), 0, 0)),
            pl.BlockSpec((None, 1, d), lambda i, j: (layer, 0, 0)),
            pl.BlockSpec((d, tn), lambda i, j: (0, j)),
        ],
        out_specs=pl.BlockSpec((tm, tn), lambda i, j: (i, j)),
        out_shape=jax.ShapeDtypeStruct((n_tok, n_out), BF16),
        scratch_shapes=[pltpu.VMEM((tm, d), BF16)],
        compiler_params=_params(("parallel", "arbitrary"), VMEM_LIMIT),
        name="norm_proj",
    )(x, mod, g_norm.reshape(-1, 1, d), w_bf16)


def _prep_kernel(x_ref, g_ref, *rest, rope):
    o_ref = rest[-1]
    hb, ts, hd = x_ref.shape
    x = x_ref[...].astype(F32)
    r = lax.rsqrt(jnp.mean(x * x, axis=-1, keepdims=True) + EPS)
    y = x * r * g_ref[...]
    if rope:
        cos_ref, sin_ref, p_ref = rest[:3]
        yr = jnp.dot(y.reshape(hb * ts, hd).astype(BF16), p_ref[...],
                     preferred_element_type=F32).reshape(hb, ts, hd)
        y = y * cos_ref[...] + yr * sin_ref[...]
    o_ref[...] = y.astype(o_ref.dtype)


def qk_prep(x, gains, tables=None):
    a_n, r_n, s_n, hd = x.shape
    hb = math.gcd(r_n, 8)
    ts = min(s_n, 2048)
    rope = tables is not None
    in_specs = [
        pl.BlockSpec((None, hb, ts, hd), lambda a, r, s: (a, r, s, 0)),
        pl.BlockSpec((None, 1, hd), lambda a, r, s: (a, 0, 0)),
    ]
    args = [x, gains.reshape(a_n, 1, hd).astype(F32)]
    if rope:
        in_specs += [
            pl.BlockSpec((ts, hd), lambda a, r, s: (s, 0)),
            pl.BlockSpec((ts, hd), lambda a, r, s: (s, 0)),
            pl.BlockSpec((hd, hd), lambda a, r, s: (0, 0)),
        ]
        args += list(tables)
    return pl.pallas_call(
        functools.partial(_prep_kernel, rope=rope),
        grid=(a_n, r_n // hb, s_n // ts),
        in_specs=in_specs,
        out_specs=pl.BlockSpec((None, hb, ts, hd), lambda a, r, s: (a, r, s, 0)),
        out_shape=jax.ShapeDtypeStruct(x.shape, BF16),
        compiler_params=_params(("parallel", "parallel", "parallel")),
        name="qk_prep",
    )(*args)


def _rope_angles(pos, dim):
    inv = ROPE_THETA ** (-jnp.arange(0, dim, 2, dtype=F32) / dim)
    ang = pos.astype(F32)[:, None] * inv[None, :]
    ang = jnp.concatenate([ang, ang], axis=-1)
    return jnp.cos(ang), jnp.sin(ang)


def _rotate_half_matrix(dim, n_blocks):
    half = dim // 2
    i = jnp.arange(dim * n_blocks)
    blk, w = i // dim, i % dim
    src = blk * dim + jnp.where(w < half, w + half, w - half)
    sign = jnp.where(w < half, -1.0, 1.0)
    p = jnp.zeros((dim * n_blocks, dim * n_blocks), F32).at[src, i].set(sign)
    return p.astype(BF16)


def _na_bias_table(rpb):
    n_h = rpb.shape[0]
    qc = jnp.arange(GRID_W)[:, None]
    kc = jnp.arange(GRID_W)[None, :]
    c0 = jnp.clip(qc - NA_KW // 2, 0, GRID_W - NA_KW)
    ok = (kc >= c0) & (kc < c0 + NA_KW)
    n_dr = rpb.shape[1]
    period = 2 * GRID_W
    ext = jnp.concatenate([rpb[..., NA_KW - 1:],
                           jnp.zeros((n_h, n_dr, period - (2 * NA_KW - 1)), rpb.dtype),
                           rpb[..., :NA_KW - 1]], axis=-1)
    toep = jnp.tile(ext, (1, 1, GRID_W))[..., :GRID_W * (period - 1)]
    toep = toep.reshape(n_h, n_dr, GRID_W, period - 1)[..., :GRID_W]
    toep = jnp.where(ok[None, None], toep, NEG)
    bias = jnp.stack([toep[:, NA_KH - 1 - d0:2 * NA_KH - 1 - d0] for d0 in range(NA_KH)])
    bias = bias.transpose(0, 1, 3, 2, 4).reshape(NA_KH, n_h, GRID_W, NA_KH * GRID_W)
    return bias.astype(BF16)


def _na_kernel(q_ref, kp_ref, kc_ref, kn_ref, vp_ref, vc_ref, vn_ref, tbl_ref,
               o_ref, ks_ref, vs_ref, *, geom):
    blk = NA_BLOCK
    j = pl.program_id(0)
    t0 = j * blk
    in_p = t0 < geom.n_p
    seq_blk0 = jnp.where(in_p, (t0 // geom.s_p) * (geom.s_p // blk),
                         geom.n_p // blk + ((t0 - geom.n_p) // geom.s_d) * (geom.s_d // blk))
    rows = jnp.where(in_p, geom.s_p // GRID_W, geom.s_d // GRID_W)
    jl = j - seq_blk0
    ks_ref[:, 0:blk, :] = kp_ref[...]
    ks_ref[:, blk:2 * blk, :] = kc_ref[...]
    ks_ref[:, 2 * blk:3 * blk, :] = kn_ref[...]
    vs_ref[:, 0:blk, :] = vp_ref[...]
    vs_ref[:, blk:2 * blk, :] = vc_ref[...]
    vs_ref[:, 2 * blk:3 * blk, :] = vn_ref[...]
    n_keys = NA_KH * GRID_W

    def body(ii, carry):
        probs = []
        for u in range(NA_ROWS_PER_TRIP):
            i = ii * NA_ROWS_PER_TRIP + u
            r = jl * NA_KH + i
            r0 = jnp.clip(r - NA_KH // 2, 0, rows - NA_KH)
            off = pl.multiple_of((r0 - (jl - 1) * NA_KH) * GRID_W, GRID_W)
            d0 = r - r0
            qoff = pl.multiple_of(i * GRID_W, GRID_W)
            for h in range(NA_HEADS):
                q = q_ref[h, pl.ds(qoff, GRID_W), :]
                k = ks_ref[h, pl.ds(off, n_keys), :]
                s = lax.dot_general(q, k, (((1,), (1,)), ((), ())), preferred_element_type=F32)
                probs.append((h, off, qoff, s + tbl_ref[d0, h].astype(F32)))
        soft = []
        for h, off, qoff, s in probs:
            m = jnp.max(s, axis=-1, keepdims=True)
            p = jnp.exp(s - m)
            soft.append((h, off, qoff, p.astype(BF16), jnp.sum(p, axis=-1, keepdims=True)))
        for h, off, qoff, p, l in soft:
            v = vs_ref[h, pl.ds(off, n_keys), :]
            o = jnp.dot(p, v, preferred_element_type=F32) / l
            o_ref[h, pl.ds(qoff, GRID_W), :] = o.astype(o_ref.dtype)
        return carry

    lax.fori_loop(0, NA_KH // NA_ROWS_PER_TRIP, body, 0)


def neighborhood_attention(q, k, v, tbl, geom):
    n_h, n_tok, hd = q.shape
    blk = NA_BLOCK
    nb = n_tok // blk
    cur = pl.BlockSpec((n_h, blk, hd), lambda j: (0, j, 0))
    prev = pl.BlockSpec((n_h, blk, hd), lambda j: (0, jnp.maximum(j - 1, 0), 0))
    nxt = pl.BlockSpec((n_h, blk, hd), lambda j: (0, jnp.minimum(j + 1, nb - 1), 0))
    return pl.pallas_call(
        functools.partial(_na_kernel, geom=geom),
        grid=(nb,),
        in_specs=[cur, prev, cur, nxt, prev, cur, nxt,
                  pl.BlockSpec(tbl.shape, lambda j: (0, 0, 0, 0))],
        out_specs=cur,
        out_shape=jax.ShapeDtypeStruct(q.shape, BF16),
        scratch_shapes=[pltpu.VMEM((n_h, 3 * blk, hd), BF16),
                        pltpu.VMEM((n_h, 3 * blk, hd), BF16)],
        compiler_params=_params(("parallel",), VMEM_LIMIT),
        name="neighborhood_attention",
    )(q, k, k, k, v, v, v, tbl)


def _dil_kernel(q_ref, kp_ref, kc_ref, kn_ref, vp_ref, vc_ref, vn_ref,
                o_ref, lse_ref, ks_ref, vs_ref, *, cls_len, side, chunk):
    c = pl.program_id(1)
    hl = DIL_HALO
    ks_ref[0:hl, :] = kp_ref[...]
    ks_ref[hl:hl + chunk, :] = kc_ref[...]
    ks_ref[hl + chunk:, :] = kn_ref[...]
    vs_ref[0:hl, :] = vp_ref[...]
    vs_ref[hl:hl + chunk, :] = vc_ref[...]
    vs_ref[hl + chunk:, :] = vn_ref[...]
    qb = DIL_QBLOCK
    kb = qb + 2 * side
    base = c * chunk
    hd = q_ref.shape[-1]

    jq = lax.broadcasted_iota(jnp.int32, (qb, kb), 0)
    jk = lax.broadcasted_iota(jnp.int32, (qb, kb), 1) - side
    near_bias = jnp.where(jnp.abs(jk - jq) <= side, 0.0, NEG)
    before_bias = jnp.where(jk < 0, NEG, 0.0)
    after_bias = jnp.where(jk >= qb, NEG, 0.0)
    n_trip = DIL_BLOCKS_PER_TRIP

    def body(ii, carry):
        probs = []
        for u in range(n_trip):
            i = ii * n_trip + u
            qoff = pl.multiple_of(i * qb, qb)
            koff = pl.multiple_of(i * qb + hl - side, side)
            p0 = base + i * qb
            q = q_ref[pl.ds(qoff, qb), :]
            k = ks_ref[pl.ds(koff, kb), :]
            s = lax.dot_general(q, k, (((1,), (1,)), ((), ())), preferred_element_type=F32)
            bias = near_bias + jnp.where(p0 % cls_len == 0, before_bias, 0.0)
            bias = bias + jnp.where((p0 + qb) % cls_len == 0, after_bias, 0.0)
            probs.append((qoff, koff, s + bias))
        soft = []
        for qoff, koff, s in probs:
            m = jnp.max(s, axis=-1, keepdims=True)
            p = jnp.exp(s - m)
            soft.append((qoff, koff, p.astype(BF16), m, jnp.sum(p, axis=-1, keepdims=True)))
        for qoff, koff, p, m, l in soft:
            v = vs_ref[pl.ds(koff, kb), :]
            o = jnp.dot(p, v, preferred_element_type=F32) / l
            o_ref[pl.ds(qoff, qb), :] = o.astype(o_ref.dtype)
            lse_ref[pl.ds(qoff, qb), :] = jnp.broadcast_to(m + jnp.log(l), (qb, hd))
        return carry

    lax.fori_loop(0, chunk // (qb * n_trip), body, 0)


def dilated_attention(q, k, v, cls_len, side):
    r_n, s_n, hd = q.shape
    assert side == DIL_QBLOCK // 2 and cls_len % DIL_QBLOCK == 0
    chunk = min(s_n, 2048)
    nc = s_n // chunk
    per = chunk // DIL_HALO
    n_halo = s_n // DIL_HALO
    cur = pl.BlockSpec((None, chunk, hd), lambda r, c: (r, c, 0))
    prev = pl.BlockSpec((None, DIL_HALO, hd), lambda r, c: (r, jnp.maximum(c * per - 1, 0), 0))
    nxt = pl.BlockSpec((None, DIL_HALO, hd),
                       lambda r, c: (r, jnp.minimum((c + 1) * per, n_halo - 1), 0))
    return pl.pallas_call(
        functools.partial(_dil_kernel, cls_len=cls_len, side=side, chunk=chunk),
        grid=(r_n, nc),
        in_specs=[cur, prev, cur, nxt, prev, cur, nxt],
        out_specs=[cur, cur],
        out_shape=[jax.ShapeDtypeStruct(q.shape, BF16),
                   jax.ShapeDtypeStruct(q.shape, F32)],
        scratch_shapes=[pltpu.VMEM((chunk + 2 * DIL_HALO, hd), BF16),
                        pltpu.VMEM((chunk + 2 * DIL_HALO, hd), BF16)],
        compiler_params=_params(("parallel", "parallel")),
        name="dilated_attention",
    )(q, k, k, k, v, v, v)


def _outproj_ab_kernel(x_ref, mod_ref, oa_ref, od_ref, lse_ref, w_ref, o_ref):
    half = oa_ref.shape[-1]
    l0, l1, l2 = lse_ref[0], lse_ref[1], lse_ref[2]
    m = jnp.maximum(jnp.maximum(l0, l1), l2)
    e0, e1, e2 = jnp.exp(l0 - m), jnp.exp(l1 - m), jnp.exp(l2 - m)
    ob = (e0 * od_ref[0].astype(F32) + e1 * od_ref[1].astype(F32)
          + e2 * od_ref[2].astype(F32)) / (e0 + e1 + e2)
    res = jnp.dot(oa_ref[...], w_ref[0:half, :], preferred_element_type=F32)
    res = res + jnp.dot(ob.astype(BF16), w_ref[half:, :], preferred_element_type=F32)
    o_ref[...] = x_ref[...] + mod_ref[2:3, :] * res


def outproj_ab(x, mod, oa, od, lse, w_bf16, layer, geom):
    n_tok, d = x.shape
    half = oa.shape[-1]
    tm = TOKEN_TILE // 2
    return pl.pallas_call(
        _outproj_ab_kernel,
        grid=(n_tok // tm,),
        in_specs=[
            pl.BlockSpec((tm, d), lambda i: (i, 0)),
            pl.BlockSpec((None, None, 6, d), lambda i: (layer, geom.seq_of_token(i * tm), 0, 0)),
            pl.BlockSpec((tm, half), lambda i: (i, 0)),
            pl.BlockSpec((N_DIL, tm, half), lambda i: (0, i, 0)),
            pl.BlockSpec((N_DIL, tm, half), lambda i: (0, i, 0)),
            pl.BlockSpec(w_bf16.shape, lambda i: (0, 0)),
        ],
        out_specs=pl.BlockSpec((tm, d), lambda i: (i, 0)),
        out_shape=jax.ShapeDtypeStruct(x.shape, F32),
        compiler_params=_params(("parallel",), VMEM_LIMIT),
        name="outproj_ab",
    )(x, mod, oa, od, lse, w_bf16)


def _outproj_kernel(x_ref, mod_ref, o_in_ref, w_ref, o_ref):
    res = jnp.dot(o_in_ref[...], w_ref[...], preferred_element_type=F32)
    o_ref[...] = x_ref[...] + mod_ref[2:3, :] * res


def outproj(x, mod, o_in, w_bf16, layer, geom):
    n_tok, d = x.shape
    tm = TOKEN_TILE
    return pl.pallas_call(
        _outproj_kernel,
        grid=(n_tok // tm,),
        in_specs=[
            pl.BlockSpec((tm, d), lambda i: (i, 0)),
            pl.BlockSpec((None, None, 6, d), lambda i: (layer, geom.seq_of_token(i * tm), 0, 0)),
            pl.BlockSpec((tm, o_in.shape[1]), lambda i: (i, 0)),
            pl.BlockSpec(w_bf16.shape, lambda i: (0, 0)),
        ],
        out_specs=pl.BlockSpec((tm, d), lambda i: (i, 0)),
        out_shape=jax.ShapeDtypeStruct(x.shape, F32),
        compiler_params=_params(("parallel",), VMEM_LIMIT),
        name="outproj",
    )(x, mod, o_in, w_bf16)


def _flash_kernel(qt_ref, k_ref, vt_ref, o_ref, sa_ref, sb_ref, m_ref, acc_ref):
    hd = k_ref.shape[-1]
    n_kt = k_ref.shape[0]
    qt = qt_ref[...]
    m_ref[...] = jnp.full(m_ref.shape, -jnp.inf, F32)
    acc_ref[...] = jnp.zeros(acc_ref.shape, F32)

    def scores(j, s_ref):
        s_ref[...] = jnp.dot(k_ref[j], qt, preferred_element_type=F32)

    def consume(j, s_ref):
        s = s_ref[...]
        m_old = m_ref[...]
        m_new = jnp.maximum(m_old, jnp.max(s, axis=0, keepdims=True))
        alpha = jnp.exp2(m_old - m_new)
        p = jnp.exp2(s - m_new).astype(BF16)
        acc_ref[...] = alpha * acc_ref[...] + jnp.dot(vt_ref[j], p, preferred_element_type=F32)
        m_ref[...] = m_new

    scores(0, sa_ref)

    def body(jj, carry):
        j = FLASH_UNROLL * jj
        for u in range(0, FLASH_UNROLL, 2):
            scores(j + u + 1, sb_ref)
            consume(j + u, sa_ref)
            scores(jnp.minimum(j + u + 2, n_kt - 1), sa_ref)
            consume(j + u + 1, sb_ref)
        return carry

    lax.fori_loop(0, n_kt // FLASH_UNROLL, body, 0)
    acc = acc_ref[...]
    o_ref[...] = (acc[:hd] / acc[hd:hd + 1]).astype(o_ref.dtype)


def flash_attention(qt, k_tiles, vt_tiles, tok0, n_b, s_n):
    n_kv, hd, _ = qt.shape
    hv = vt_tiles.shape[2]
    n_g = C_Q_HEADS // n_kv
    tq, tk = FLASH_TQ, FLASH_TK
    nq = s_n // tq
    kt_per = s_n // tk
    assert kt_per % FLASH_UNROLL == 0
    wq = n_g * tq
    return pl.pallas_call(
        _flash_kernel,
        grid=(n_b, n_kv, nq),
        in_specs=[
            pl.BlockSpec((None, hd, wq), lambda b, h, i: (h, 0, (tok0 + b * s_n) // tq + i)),
            pl.BlockSpec((None, kt_per, tk, hd), lambda b, h, i: (h, (tok0 + b * s_n) // s_n, 0, 0)),
            pl.BlockSpec((None, kt_per, hv, tk), lambda b, h, i: (h, (tok0 + b * s_n) // s_n, 0, 0)),
        ],
        out_specs=pl.BlockSpec((None, hd, wq), lambda b, h, i: (h, 0, b * nq + i)),
        out_shape=jax.ShapeDtypeStruct((n_kv, hd, n_b * s_n * n_g), BF16),
        scratch_shapes=[pltpu.VMEM((tk, wq), F32), pltpu.VMEM((tk, wq), F32),
                        pltpu.VMEM((1, wq), F32), pltpu.VMEM((hv, wq), F32)],
        compiler_params=_params(("parallel", "parallel", "arbitrary"), VMEM_LIMIT),
        name="flash_attention",
    )(qt, k_tiles, vt_tiles)


def _router_kernel(x_ref, mod_ref, g_ref, wr_ref, br_ref, h_ref, idx_ref, gate_ref):
    h = _modulated_norm(x_ref[...], g_ref[...], mod_ref[3:4, :], mod_ref[4:5, :])
    h_ref[...] = h.astype(BF16)
    logits = jnp.dot(h, wr_ref[...], precision=HIGHEST,
                     preferred_element_type=F32) + br_ref[...]
    tm, n_e = logits.shape
    iota = lax.broadcasted_iota(jnp.int32, (tm, n_e), 1)
    vals, ids = [], []
    cur = logits
    for _ in range(TOP_K):
        m = jnp.max(cur, axis=-1, keepdims=True)
        am = jnp.min(jnp.where(cur == m, iota, n_e), axis=-1, keepdims=True)
        vals.append(m)
        ids.append(am)
        cur = jnp.where(iota == am, -jnp.inf, cur)
    es = [jnp.exp(v - vals[0]) for v in vals]
    den = es[0] + es[1] + es[2] + es[3]
    lane = lax.broadcasted_iota(jnp.int32, idx_ref.shape, 1)
    idx_out = jnp.zeros(idx_ref.shape, jnp.int32)
    gate_out = jnp.zeros(gate_ref.shape, F32)
    for kk in range(TOP_K):
        idx_out = jnp.where(lane == kk, ids[kk], idx_out)
        gate_out = jnp.where(lane == kk, es[kk] / den, gate_out)
    idx_ref[...] = idx_out
    gate_ref[...] = gate_out


def router(x, mod, g_norm, w_router, b_router, layer, geom):
    n_tok, d = x.shape
    tm = TOKEN_TILE // 2
    return pl.pallas_call(
        _router_kernel,
        grid=(n_tok // tm,),
        in_specs=[
            pl.BlockSpec((tm, d), lambda i: (i, 0)),
            pl.BlockSpec((None, None, 6, d), lambda i: (layer, geom.seq_of_token(i * tm), 0, 0)),
            pl.BlockSpec((None, 1, d), lambda i: (layer, 0, 0)),
            pl.BlockSpec((None, d, N_EXPERTS), lambda i: (layer, 0, 0)),
            pl.BlockSpec((None, 1, N_EXPERTS), lambda i: (layer, 0, 0)),
        ],
        out_specs=[
            pl.BlockSpec((tm, d), lambda i: (i, 0)),
            pl.BlockSpec((tm, 8), lambda i: (i, 0)),
            pl.BlockSpec((tm, 8), lambda i: (i, 0)),
        ],
        out_shape=[
            jax.ShapeDtypeStruct((n_tok, d), BF16),
            jax.ShapeDtypeStruct((n_tok, 8), jnp.int32),
            jax.ShapeDtypeStruct((n_tok, 8), F32),
        ],
        compiler_params=_params(("parallel",), VMEM_LIMIT),
        name="router",
    )(x, mod, g_norm.reshape(-1, 1, d), w_router, b_router.reshape(-1, 1, N_EXPERTS))


def _moe_kernel(be_ref, nu_ref, x_ref, wgu_ref, bgu_ref, wd_ref, bd_ref, o_ref,
                wgu_s, wd_s):
    i = pl.program_id(0)
    e = be_ref[i]
    e_prev = be_ref[jnp.maximum(i - 1, 0)]

    @pl.when((i == 0) | (e != e_prev))
    def _():
        wgu_s[...] = wgu_ref[...].astype(BF16)
        wd_s[...] = wd_ref[...].astype(BF16)

    @pl.when(i < nu_ref[0])
    def _():
        d_ff = wd_s.shape[0]
        gu = jnp.dot(x_ref[...], wgu_s[...], preferred_element_type=F32) + bgu_ref[...]
        glu = jnp.minimum(gu[:, :d_ff], SWIGLU_LIMIT)
        lin = jnp.clip(gu[:, d_ff:], -SWIGLU_LIMIT, SWIGLU_LIMIT)
        act = glu / (1.0 + jnp.exp(-SWIGLU_ALPHA * glu)) * (lin + 1.0)
        y = jnp.dot(act.astype(BF16), wd_s[...], preferred_element_type=F32) + bd_ref[...]
        o_ref[...] = y.astype(o_ref.dtype)

    @pl.when(i >= nu_ref[0])
    def _():
        o_ref[...] = jnp.zeros(o_ref.shape, o_ref.dtype)


def moe_experts(xs, blk_e, n_used, w_gu, b_gu, w_down, b_down, layer):
    n_rows, d = xs.shape
    d_ff = w_down.shape[2]
    nblk = n_rows // MOE_BLOCK
    grid_spec = pltpu.PrefetchScalarGridSpec(
        num_scalar_prefetch=2,
        grid=(nblk,),
        in_specs=[
            pl.BlockSpec((MOE_BLOCK, d), lambda i, be, nu: (jnp.minimum(i, nu[0] - 1), 0)),
            pl.BlockSpec((None, None, d, 2 * d_ff), lambda i, be, nu: (layer, be[i], 0, 0)),
            pl.BlockSpec((None, None, 1, 2 * d_ff), lambda i, be, nu: (layer, be[i], 0, 0)),
            pl.BlockSpec((None, None, d_ff, d), lambda i, be, nu: (layer, be[i], 0, 0)),
            pl.BlockSpec((None, None, 1, d), lambda i, be, nu: (layer, be[i], 0, 0)),
        ],
        out_specs=pl.BlockSpec((MOE_BLOCK, d), lambda i, be, nu: (i, 0)),
        scratch_shapes=[pltpu.VMEM((d, 2 * d_ff), BF16), pltpu.VMEM((d_ff, d), BF16)],
    )
    depth, n_e = w_gu.shape[:2]
    return pl.pallas_call(
        _moe_kernel,
        grid_spec=grid_spec,
        out_shape=jax.ShapeDtypeStruct((n_rows, d), BF16),
        compiler_params=_params(("arbitrary",), VMEM_LIMIT),
        name="moe_experts",
    )(blk_e, n_used, xs, w_gu, b_gu.reshape(depth, n_e, 1, 2 * d_ff), w_down,
      b_down.reshape(depth, n_e, 1, d))


def _combine_kernel(x_ref, mod_ref, y_ref, gate_ref, o_ref):
    g = gate_ref[...]
    acc = g[:, 0:1] * y_ref[0].astype(F32)
    for kk in range(1, TOP_K):
        acc = acc + g[:, kk:kk + 1] * y_ref[kk].astype(F32)
    o_ref[...] = x_ref[...] + mod_ref[5:6, :] * acc


def moe_combine(x, mod, yk, gates, layer, geom):
    n_tok, d = x.shape
    tm = TOKEN_TILE // 2
    return pl.pallas_call(
        _combine_kernel,
        grid=(n_tok // tm,),
        in_specs=[
            pl.BlockSpec((tm, d), lambda i: (i, 0)),
            pl.BlockSpec((None, None, 6, d), lambda i: (layer, geom.seq_of_token(i * tm), 0, 0)),
            pl.BlockSpec((TOP_K, tm, d), lambda i: (0, i, 0)),
            pl.BlockSpec((tm, 8), lambda i: (i, 0)),
        ],
        out_specs=pl.BlockSpec((tm, d), lambda i: (i, 0)),
        out_shape=jax.ShapeDtypeStruct(x.shape, F32),
        compiler_params=_params(("parallel",), VMEM_LIMIT),
        name="moe_combine",
    )(x, mod, yk, gates)


def moe_layer(x, mod, g_norm, w_router, b_router, w_gu, b_gu, w_down, b_down, layer, geom):
    n_tok, d = x.shape
    h, idx8, gates8 = router(x, mod, g_norm, w_router, b_router, layer, geom)
    idx = idx8[:, :TOP_K]
    chosen = jnp.sum(idx[:, :, None] == jnp.arange(N_EXPERTS, dtype=jnp.int32)[None, None, :],
                     axis=1, dtype=jnp.int32)
    incl = jnp.cumsum(chosen, axis=0)
    rank = jnp.take_along_axis(incl - chosen, idx, axis=1)
    counts = incl[-1]
    pcounts = ((counts + MOE_BLOCK - 1) // MOE_BLOCK) * MOE_BLOCK
    pend = jnp.cumsum(pcounts)
    pstart = pend - pcounts
    dest = pstart[idx] + rank
    nblk = (n_tok * TOP_K) // MOE_BLOCK + N_EXPERTS
    n_rows = nblk * MOE_BLOCK
    tok_ids = jnp.broadcast_to(jnp.arange(n_tok, dtype=jnp.int32)[:, None], dest.shape)
    tok = jnp.zeros((n_rows,), jnp.int32).at[dest.reshape(-1)].set(tok_ids.reshape(-1))
    n_used = (pend[-1] // MOE_BLOCK).astype(jnp.int32).reshape(1)
    blk_start = jnp.arange(nblk, dtype=jnp.int32) * MOE_BLOCK
    blk_e = jnp.searchsorted(pend, jnp.minimum(blk_start, pend[-1] - 1), side='right')
    blk_e = jnp.minimum(blk_e, N_EXPERTS - 1).astype(jnp.int32)
    xs = jnp.take(h, tok, axis=0)
    yb = moe_experts(xs, blk_e, n_used, w_gu, b_gu, w_down, b_down, layer)
    yk = jnp.take(yb, dest.T.reshape(-1), axis=0).reshape(TOP_K, n_tok, d)
    return moe_combine(x, mod, yk, gates8, layer, geom)


def _seq_positions(geom):
    return jnp.concatenate([jnp.arange(s, dtype=jnp.int32)
                            for _, n_b, s in geom.groups() for _ in range(n_b)])


def mixer_ab_layer(x, mod, g_norm, w_in, w_out, g_qn_a, g_kn_a, rpb, g_qn_b, g_kn_b, layer, geom):
    n_tok, d = x.shape
    hd = HEAD_DIM
    proj = norm_proj(x, mod, g_norm, w_in.astype(BF16), layer, geom, tn=1536)
    na_in = 3 * NA_HEADS * hd
    pa = proj[:, :na_in].reshape(n_tok, 3, NA_HEADS, hd).transpose(1, 2, 0, 3)
    qk = qk_prep(pa[:2], jnp.stack([g_qn_a * Q_SCALE, g_kn_a]))
    oa = neighborhood_attention(qk[0], qk[1], pa[2], _na_bias_table(rpb), geom)
    oa = oa.transpose(1, 0, 2).reshape(n_tok, NA_HEADS * hd)

    pb = proj[:, na_in:].reshape(n_tok, 3, N_DIL, DIL_HEADS, hd)
    perm = _rotate_half_matrix(hd, 1)
    od_all, lse_all = [], []
    for gi, (window, dil) in enumerate(DIL_PATTERNS):
        side = (window // 2) // dil
        od_g, lse_g = [], []
        for tok0, n_b, s_n in geom.groups():
            cls_len = s_n // dil
            xg = pb[tok0:tok0 + n_b * s_n, :, gi].reshape(n_b, cls_len, dil, 3, DIL_HEADS, hd)
            xg = xg.transpose(3, 0, 4, 2, 1, 5).reshape(3, n_b * DIL_HEADS, s_n, hd)
            pos = (jnp.arange(cls_len)[None, :] * dil + jnp.arange(dil)[:, None]).reshape(-1)
            cos, sin = _rope_angles(pos, hd)
            qk = qk_prep(xg[:2], jnp.stack([g_qn_b[gi] * Q_SCALE, g_kn_b[gi]]), (cos, sin, perm))
            o, lse = dilated_attention(qk[0], qk[1], xg[2], cls_len, side)

            def back(a):
                a = a.reshape(n_b, DIL_HEADS, dil, cls_len, hd).transpose(0, 3, 2, 1, 4)
                return a.reshape(n_b * s_n, DIL_HEADS * hd)

            od_g.append(back(o))
            lse_g.append(back(lse))
        od_all.append(jnp.concatenate(od_g, axis=0))
        lse_all.append(jnp.concatenate(lse_g, axis=0))
    od = jnp.stack(od_all)
    lse = jnp.stack(lse_all)
    return outproj_ab(x, mod, oa, od, lse, w_out.astype(BF16), layer, geom)


def mixer_c_layer(x, mod, g_norm, w_in, w_out, g_qn, g_kn, layer, geom):
    n_tok, d = x.shape
    hd = HEAD_DIM
    proj = norm_proj(x, mod, g_norm, w_in.astype(BF16), layer, geom, tn=w_in.shape[1])
    nq = C_Q_HEADS * hd
    nkv = C_KV_HEADS * hd
    q = proj[:, :nq].reshape(n_tok, C_Q_HEADS, hd).transpose(1, 0, 2)[None]
    k = proj[:, nq:nq + nkv].reshape(n_tok, C_KV_HEADS, hd).transpose(1, 0, 2)[None]
    v = proj[:, nq + nkv:].reshape(n_tok, C_KV_HEADS, hd)
    pos = _seq_positions(geom)
    half = hd // 2
    cr, sr = _rope_angles(pos // GRID_W, half)
    cc, sc = _rope_angles(pos % GRID_W, half)
    tables = (jnp.concatenate([cr, cc], axis=-1), jnp.concatenate([sr, sc], axis=-1),
              _rotate_half_matrix(half, 2))
    q = qk_prep(q, (g_qn * (Q_SCALE * LOG2_E))[None], tables)[0]
    k = qk_prep(k, g_kn[None], tables)[0]
    tk = FLASH_TK
    n_g = C_Q_HEADS // C_KV_HEADS
    tq = FLASH_TQ
    qt = q.reshape(C_KV_HEADS, n_g, n_tok // tq, tq, hd).transpose(0, 4, 2, 1, 3)
    qt = qt.reshape(C_KV_HEADS, hd, n_tok * n_g)
    k_tiles = k.reshape(C_KV_HEADS, n_tok // tk, tk, hd)
    vt_tiles = v.reshape(n_tok // tk, tk, C_KV_HEADS, hd).transpose(2, 0, 3, 1)
    ones = jnp.ones((C_KV_HEADS, n_tok // tk, BF16_SUBLANES, tk), BF16)
    vt_tiles = jnp.concatenate([vt_tiles, ones], axis=2)
    ot = jnp.concatenate([flash_attention(qt, k_tiles, vt_tiles, tok0, n_b, s_n)
                          for tok0, n_b, s_n in geom.groups()], axis=-1)
    o = ot.reshape(C_KV_HEADS, hd, n_tok // tq, n_g, tq).transpose(2, 4, 0, 3, 1)
    o = o.reshape(n_tok, C_Q_HEADS * hd)
    return outproj(x, mod, o, w_out.astype(BF16), layer, geom)


def kernel(x_prompt, x_sample, c_prompt, c_sample, w_ada, b_ada, g_norm_mix, g_norm_ffn, w_in_ab, w_out_ab, g_qn_a, g_kn_a, rpb_a, g_qn_b, g_kn_b, w_in_c, w_out_c, g_qn_c, g_kn_c, w_router, b_router, w_gate_up, b_gate_up, w_down, b_down):
    b_p, s_p, d = x_prompt.shape
    b_d, s_d, _ = x_sample.shape
    geom = Geom(b_p, s_p, b_d, s_d)
    depth = w_ada.shape[0]
    x = jnp.concatenate([x_prompt.reshape(-1, d), x_sample.reshape(-1, d)], axis=0)
    mod = ada_modulation(jnp.concatenate([c_prompt, c_sample], axis=0), w_ada, b_ada)
    for layer in range(depth):
        i = layer // 2
        if layer % 2 == 0:
            x = mixer_ab_layer(x, mod, g_norm_mix, w_in_ab[i], w_out_ab[i], g_qn_a[i], g_kn_a[i],
                               rpb_a[i], g_qn_b[i], g_kn_b[i], layer, geom)
        else:
            x = mixer_c_layer(x, mod, g_norm_mix, w_in_c[i], w_out_c[i], g_qn_c[i], g_kn_c[i],
                              layer, geom)
        x = moe_layer(x, mod, g_norm_ffn, w_router, b_router, w_gate_up, b_gate_up, w_down,
                      b_down, layer, geom)
    y_prompt = x[:geom.n_p].reshape(b_p, s_p, d)
    y_sample = x[geom.n_p:].reshape(b_d, s_d, d)
    return (y_prompt, y_sample)
```

```python
import functools
import math
from typing import NamedTuple

import jax
import jax.numpy as jnp
from jax import lax
from jax.experimental import pallas as pl
from jax.experimental.pallas import tpu as pltpu

F32 = jnp.float32
BF16 = jnp.bfloat16
HIGHEST = lax.Precision.HIGHEST

GRID_W = 64
HEAD_DIM = 64
ROPE_THETA = 10000.0
EPS = 1e-6
NEG = -1e30
NA_HEADS = 8
NA_KH = 8
NA_KW = 16
DIL_HEADS = 8
DIL_PATTERNS = ((128, 1), (512, 4), (2048, 16))
N_DIL = len(DIL_PATTERNS)
C_Q_HEADS = 16
C_KV_HEADS = 4
N_EXPERTS = 32
TOP_K = 4
SWIGLU_LIMIT = 7.0
SWIGLU_ALPHA = 1.702
Q_SCALE = HEAD_DIM ** -0.5
LOG2_E = math.log2(math.e)
BF16_SUBLANES = 16
LANES = 128
MXU_TILE = 256

V7X_VMEM_BYTES = 64 * 1024 * 1024
VMEM_LIMIT = V7X_VMEM_BYTES - 8 * 1024 * 1024

TOKEN_TILE = 1024
PREP_TILE = 2048
NA_BLOCK = 8 * GRID_W
NA_ROWS_PER_TRIP = 2
DIL_QBLOCK = 128
DIL_SIDE = 64
DIL_CHUNK = 2048
DIL_PROBLEMS_PER_TRIP = 4
FLASH_TQ = 256
FLASH_TK = 512
FLASH_UNROLL = 4
MOE_BLOCK = 512

NA_Q0, NA_K0, NA_V0 = 0, NA_HEADS, 2 * NA_HEADS
DIL_Q0 = 3 * NA_HEADS
DIL_K0 = DIL_Q0 + N_DIL * DIL_HEADS
DIL_V0 = DIL_K0 + N_DIL * DIL_HEADS
AB_HEADS = DIL_V0 + N_DIL * DIL_HEADS
C_K0 = C_Q_HEADS
C_V0 = C_Q_HEADS + C_KV_HEADS


class Geom(NamedTuple):
    b_p: int
    s_p: int
    b_d: int
    s_d: int

    @property
    def n_p(self):
        return self.b_p * self.s_p

    @property
    def n_tok(self):
        return self.n_p + self.b_d * self.s_d

    @property
    def n_seq(self):
        return self.b_p + self.b_d

    def groups(self):
        return ((0, self.b_p, self.s_p), (self.n_p, self.b_d, self.s_d))

    def seq_of_token(self, t0):
        return jnp.where(t0 < self.n_p, t0 // self.s_p,
                         self.b_p + (t0 - self.n_p) // self.s_d)

    def seq_span(self, t0):
        in_p = t0 < self.n_p
        start = jnp.where(in_p, (t0 // self.s_p) * self.s_p,
                          self.n_p + ((t0 - self.n_p) // self.s_d) * self.s_d)
        return start, jnp.where(in_p, self.s_p, self.s_d)


def _params(semantics, vmem=None):
    return pltpu.CompilerParams(dimension_semantics=semantics,
                                vmem_limit_bytes=vmem)


def _ada_kernel(c_ref, w_ref, b_ref, o_ref):
    c = c_ref[...]
    a = c / (1.0 + jnp.exp(-c))
    o_ref[...] = jnp.dot(a, w_ref[...], precision=HIGHEST,
                         preferred_element_type=F32) + b_ref[...]


def ada_modulation(c_all, w_ada, b_ada):
    depth, d, d6 = w_ada.shape
    n_seq = c_all.shape[0]
    rows = 8
    c_pad = jnp.zeros((rows, d), F32).at[:n_seq].set(c_all)
    out = pl.pallas_call(
        _ada_kernel,
        grid=(depth, d6 // d),
        in_specs=[
            pl.BlockSpec((rows, d), lambda l, j: (0, 0)),
            pl.BlockSpec((None, d, d), lambda l, j: (l, 0, j)),
            pl.BlockSpec((None, 1, d), lambda l, j: (l, 0, j)),
        ],
        out_specs=pl.BlockSpec((None, rows, d), lambda l, j: (l, 0, j)),
        out_shape=jax.ShapeDtypeStruct((depth, rows, d6), F32),
        compiler_params=_params(("arbitrary", "arbitrary")),
        name="ada_modulation",
    )(c_pad, w_ada, b_ada.reshape(depth, 1, d6))
    return out[:, :n_seq].reshape(depth, n_seq, 6, d)


def _modulated_norm(x, g, shift, scale):
    r = lax.rsqrt(jnp.mean(x * x, axis=-1, keepdims=True) + EPS)
    return (x * r * g) * (1.0 + scale) + shift


def _norm_proj_kernel(x_ref, mod_ref, g_ref, w_ref, o_ref, h_ref):
    @pl.when(pl.program_id(1) == 0)
    def _():
        h = _modulated_norm(x_ref[...], g_ref[...], mod_ref[0:1, :], mod_ref[1:2, :])
        h_ref[...] = h.astype(BF16)

    acc = jnp.dot(h_ref[...], w_ref[...], preferred_element_type=F32)
    hd = o_ref.shape[-1]
    for c in range(o_ref.shape[0]):
        o_ref[c] = acc[:, c * hd:(c + 1) * hd].astype(o_ref.dtype)


def norm_proj(x, mod, g_norm, w_bf16, layer, geom, tn):
    n_tok, d = x.shape
    n_out = w_bf16.shape[1]
    tm = TOKEN_TILE
    hd = HEAD_DIM
    return pl.pallas_call(
        _norm_proj_kernel,
        grid=(n_tok // tm, n_out // tn),
        in_specs=[
            pl.BlockSpec((tm, d), lambda i, j: (i, 0)),
            pl.BlockSpec((None, None, 6, d),
                         lambda i, j: (layer, geom.seq_of_token(i * tm), 0, 0)),
            pl.BlockSpec((None, 1, d), lambda i, j: (layer, 0, 0)),
            pl.BlockSpec((d, tn), lambda i, j: (0, j)),
        ],
        out_specs=pl.BlockSpec((tn // hd, tm, hd), lambda i, j: (j, i, 0)),
        out_shape=jax.ShapeDtypeStruct((n_out // hd, n_tok, hd), BF16),
        scratch_shapes=[pltpu.VMEM((tm, d), BF16)],
        compiler_params=_params(("parallel", "arbitrary"), VMEM_LIMIT),
        name="norm_proj",
    )(x, mod, g_norm.reshape(-1, 1, d), w_bf16)


def _prep_kernel(x_ref, g_ref, cos_ref, sin_ref, p_ref, o_ref):
    hb, ts, hd = x_ref.shape
    x = x_ref[...].astype(F32)
    r = lax.rsqrt(jnp.mean(x * x, axis=-1, keepdims=True) + EPS)
    y = x * r * g_ref[...]
    yr = jnp.dot(y.reshape(hb * ts, hd).astype(BF16), p_ref[...],
                 preferred_element_type=F32).reshape(hb, ts, hd)
    o_ref[...] = (y * cos_ref[...] + yr * sin_ref[...]).astype(o_ref.dtype)


def qk_prep(x, gains, cos, sin, perm, hb, n_blocks, block_fn, table_fn):
    n_h, n_tok, hd = x.shape
    ts = PREP_TILE
    blk = pl.BlockSpec((hb, ts, hd), lambda i, s: (block_fn(i), s, 0))
    tab = pl.BlockSpec((None, ts, hd), lambda i, s: (table_fn(i), s, 0))
    return pl.pallas_call(
        _prep_kernel,
        grid=(n_blocks, n_tok // ts),
        in_specs=[blk, pl.BlockSpec((hb, 1, hd), lambda i, s: (block_fn(i), 0, 0)), tab, tab,
                  pl.BlockSpec((hd, hd), lambda i, s: (0, 0))],
        out_specs=blk,
        out_shape=jax.ShapeDtypeStruct(x.shape, BF16),
        input_output_aliases={0: 0},
        compiler_params=_params(("parallel", "parallel")),
        name="qk_prep",
    )(x, gains.reshape(n_h, 1, hd).astype(F32), cos, sin, perm)


def _rope_angles(pos, dim):
    inv = ROPE_THETA ** (-jnp.arange(0, dim, 2, dtype=F32) / dim)
    ang = pos.astype(F32)[:, None] * inv[None, :]
    ang = jnp.concatenate([ang, ang], axis=-1)
    return jnp.cos(ang), jnp.sin(ang)


def _rotate_half_matrix(dim, n_blocks):
    half = dim // 2
    i = jnp.arange(dim * n_blocks)
    blk, w = i // dim, i % dim
    src = blk * dim + jnp.where(w < half, w + half, w - half)
    sign = jnp.where(w < half, -1.0, 1.0)
    p = jnp.zeros((dim * n_blocks, dim * n_blocks), F32).at[src, i].set(sign)
    return p.astype(BF16)


def _na_bias_table(rpb):
    n_h = rpb.shape[0]
    qc = jnp.arange(GRID_W)[:, None]
    kc = jnp.arange(GRID_W)[None, :]
    c0 = jnp.clip(qc - NA_KW // 2, 0, GRID_W - NA_KW)
    ok = (kc >= c0) & (kc < c0 + NA_KW)
    n_dr = rpb.shape[1]
    period = 2 * GRID_W
    ext = jnp.concatenate([rpb[..., NA_KW - 1:],
                           jnp.zeros((n_h, n_dr, period - (2 * NA_KW - 1)), rpb.dtype),
                           rpb[..., :NA_KW - 1]], axis=-1)
    toep = jnp.tile(ext, (1, 1, GRID_W))[..., :GRID_W * (period - 1)]
    toep = toep.reshape(n_h, n_dr, GRID_W, period - 1)[..., :GRID_W]
    toep = jnp.where(ok[None, None], toep, NEG)
    bias = jnp.stack([toep[:, NA_KH - 1 - d0:2 * NA_KH - 1 - d0] for d0 in range(NA_KH)])
    bias = bias.transpose(0, 1, 3, 2, 4).reshape(NA_KH, n_h, GRID_W, NA_KH * GRID_W)
    return bias.astype(BF16)


def _na_kernel(q_ref, kp_ref, kc_ref, kn_ref, vp_ref, vc_ref, vn_ref, tbl_ref,
               o_ref, ks_ref, vs_ref, *, geom):
    blk = NA_BLOCK
    j = pl.program_id(0)
    seq_start, seq_len = geom.seq_span(j * blk)
    rows = seq_len // GRID_W
    jl = j - seq_start // blk
    ks_ref[:, 0:blk, :] = kp_ref[...]
    ks_ref[:, blk:2 * blk, :] = kc_ref[...]
    ks_ref[:, 2 * blk:3 * blk, :] = kn_ref[...]
    vs_ref[:, 0:blk, :] = vp_ref[...]
    vs_ref[:, blk:2 * blk, :] = vc_ref[...]
    vs_ref[:, 2 * blk:3 * blk, :] = vn_ref[...]
    n_keys = NA_KH * GRID_W

    def body(ii, carry):
        probs = []
        for u in range(NA_ROWS_PER_TRIP):
            i = ii * NA_ROWS_PER_TRIP + u
            r = jl * NA_KH + i
            r0 = jnp.clip(r - NA_KH // 2, 0, rows - NA_KH)
            off = pl.multiple_of((r0 - (jl - 1) * NA_KH) * GRID_W, GRID_W)
            d0 = r - r0
            qoff = pl.multiple_of(i * GRID_W, GRID_W)
            for h in range(NA_HEADS):
                q = q_ref[h, pl.ds(qoff, GRID_W), :]
                k = ks_ref[h, pl.ds(off, n_keys), :]
                s = lax.dot_general(q, k, (((1,), (1,)), ((), ())), preferred_element_type=F32)
                probs.append((h, off, qoff, s + tbl_ref[d0, h].astype(F32)))
        soft = []
        for h, off, qoff, s in probs:
            m = jnp.max(s, axis=-1, keepdims=True)
            p = jnp.exp(s - m)
            soft.append((h, off, qoff, p.astype(BF16), jnp.sum(p, axis=-1, keepdims=True)))
        for h, off, qoff, p, l in soft:
            v = vs_ref[h, pl.ds(off, n_keys), :]
            o = jnp.dot(p, v, preferred_element_type=F32) / l
            o_ref[h, pl.ds(qoff, GRID_W), :] = o.astype(o_ref.dtype)
        return carry

    lax.fori_loop(0, NA_KH // NA_ROWS_PER_TRIP, body, 0)


def neighborhood_attention(qkv, tbl, geom):
    _, n_tok, hd = qkv.shape
    n_h = NA_HEADS
    blk = NA_BLOCK
    nb = n_tok // blk

    def spec(head0, shift):
        hb = head0 // n_h
        return pl.BlockSpec((n_h, blk, hd), lambda j: (hb, jnp.clip(j + shift, 0, nb - 1), 0))

    return pl.pallas_call(
        functools.partial(_na_kernel, geom=geom),
        grid=(nb,),
        in_specs=[spec(NA_Q0, 0), spec(NA_K0, -1), spec(NA_K0, 0), spec(NA_K0, 1),
                  spec(NA_V0, -1), spec(NA_V0, 0), spec(NA_V0, 1),
                  pl.BlockSpec(tbl.shape, lambda j: (0, 0, 0, 0))],
        out_specs=pl.BlockSpec((n_h, blk, hd), lambda j: (0, j, 0)),
        out_shape=jax.ShapeDtypeStruct((n_h, n_tok, hd), BF16),
        scratch_shapes=[pltpu.VMEM((n_h, 3 * blk, hd), BF16),
                        pltpu.VMEM((n_h, 3 * blk, hd), BF16)],
        compiler_params=_params(("parallel",), VMEM_LIMIT),
        name="neighborhood_attention",
    )(qkv, qkv, qkv, qkv, qkv, qkv, qkv, tbl)


def _dil_kernel(q_ref, kp_ref, kc_ref, kn_ref, vp_ref, vc_ref, vn_ref,
                o_ref, lse_ref, ks_ref, vs_ref, *, geom, dil):
    c = pl.program_id(2)
    chunk, lw = q_ref.shape
    hd = HEAD_DIM
    side = DIL_SIDE
    qb = DIL_QBLOCK
    kb = qb + 2 * side
    ks_ref[0:side, :] = kp_ref[...]
    ks_ref[side:side + chunk, :] = kc_ref[...]
    ks_ref[side + chunk:, :] = kn_ref[...]
    vs_ref[0:side, :] = vp_ref[...]
    vs_ref[side:side + chunk, :] = vc_ref[...]
    vs_ref[side + chunk:, :] = vn_ref[...]
    t0 = c * (chunk * dil)
    seq_start, seq_len = geom.seq_span(t0)
    row0 = (t0 - seq_start) // dil
    cls_rows = seq_len // dil
    jq = lax.broadcasted_iota(jnp.int32, (qb, kb), 0)
    jk = lax.broadcasted_iota(jnp.int32, (qb, kb), 1) - side
    near_bias = jnp.where(jnp.abs(jk - jq) <= side, 0.0, NEG)
    before_bias = jnp.where(jk < 0, NEG, 0.0)
    after_bias = jnp.where(jk >= qb, NEG, 0.0)
    n_half = lw // hd
    n_trip = min(DIL_PROBLEMS_PER_TRIP // n_half, chunk // qb)
    assert chunk % (qb * n_trip) == 0

    def body(ii, carry):
        probs = []
        for u in range(n_trip):
            i = ii * n_trip + u
            qoff = pl.multiple_of(i * qb, qb)
            p0 = row0 + i * qb
            bias = near_bias + jnp.where(p0 == 0, before_bias, 0.0)
            bias = bias + jnp.where(p0 + qb == cls_rows, after_bias, 0.0)
            for half in range(n_half):
                lanes = slice(half * hd, (half + 1) * hd)
                q = q_ref[pl.ds(qoff, qb), lanes]
                k = ks_ref[pl.ds(qoff, kb), lanes]
                s = lax.dot_general(q, k, (((1,), (1,)), ((), ())), preferred_element_type=F32)
                probs.append((qoff, lanes, s + bias))
        soft = []
        for qoff, lanes, s in probs:
            m = jnp.max(s, axis=-1, keepdims=True)
            p = jnp.exp(s - m)
            soft.append((qoff, lanes, p.astype(BF16), m, jnp.sum(p, axis=-1, keepdims=True)))
        for qoff, lanes, p, m, l in soft:
            v = vs_ref[pl.ds(qoff, kb), lanes]
            o = jnp.dot(p, v, preferred_element_type=F32) / l
            o_ref[pl.ds(qoff, qb), lanes] = o.astype(o_ref.dtype)
            lse_ref[pl.ds(qoff, qb), lanes] = jnp.broadcast_to(m + jnp.log(l), (qb, hd))
        return carry

    lax.fori_loop(0, chunk // (qb * n_trip), body, 0)


def dilated_attention(qkv, group, dil, geom):
    n_all, n_tok, hd = qkv.shape
    n_h = DIL_HEADS
    rows = n_tok // dil
    width = dil * hd
    lw = min(width, LANES)
    view = qkv.reshape(n_all, rows, width)
    chunk = min(DIL_CHUNK, min(geom.s_p, geom.s_d) // dil)
    assert chunk % DIL_QBLOCK == 0
    per = chunk // DIL_SIDE
    n_side = rows // DIL_SIDE

    def cur(head0):
        return pl.BlockSpec((None, chunk, lw), lambda h, b, c: (head0 + h, c, b))

    def prev(head0):
        return pl.BlockSpec((None, DIL_SIDE, lw),
                            lambda h, b, c: (head0 + h, jnp.maximum(c * per - 1, 0), b))

    def nxt(head0):
        return pl.BlockSpec((None, DIL_SIDE, lw),
                            lambda h, b, c: (head0 + h, jnp.minimum((c + 1) * per, n_side - 1), b))

    q0, k0, v0 = (base + group * n_h for base in (DIL_Q0, DIL_K0, DIL_V0))
    out = pl.BlockSpec((None, chunk, lw), lambda h, b, c: (h, c, b))
    o, lse = pl.pallas_call(
        functools.partial(_dil_kernel, geom=geom, dil=dil),
        grid=(n_h, width // lw, rows // chunk),
        in_specs=[cur(q0), prev(k0), cur(k0), nxt(k0), prev(v0), cur(v0), nxt(v0)],
        out_specs=[out, out],
        out_shape=[jax.ShapeDtypeStruct((n_h, rows, width), BF16),
                   jax.ShapeDtypeStruct((n_h, rows, width), F32)],
        scratch_shapes=[pltpu.VMEM((chunk + 2 * DIL_SIDE, lw), BF16),
                        pltpu.VMEM((chunk + 2 * DIL_SIDE, lw), BF16)],
        compiler_params=_params(("parallel", "parallel", "parallel")),
        name="dilated_attention",
    )(view, view, view, view, view, view, view)
    return o.reshape(n_h, n_tok, hd), lse.reshape(n_h, n_tok, hd)


def _outproj_ab_kernel(x_ref, mod_ref, oa_ref, od0_ref, od1_ref, od2_ref,
                       l0_ref, l1_ref, l2_ref, w_ref, o_ref):
    n_h, tm, hd = oa_ref.shape
    res = jnp.zeros(o_ref.shape, F32)
    for h in range(n_h):
        l0, l1, l2 = l0_ref[h], l1_ref[h], l2_ref[h]
        m = jnp.maximum(jnp.maximum(l0, l1), l2)
        e0, e1, e2 = jnp.exp(l0 - m), jnp.exp(l1 - m), jnp.exp(l2 - m)
        ob = (e0 * od0_ref[h].astype(F32) + e1 * od1_ref[h].astype(F32)
              + e2 * od2_ref[h].astype(F32)) / (e0 + e1 + e2)
        res = res + jnp.dot(oa_ref[h], w_ref[h * hd:(h + 1) * hd, :],
                            preferred_element_type=F32)
        res = res + jnp.dot(ob.astype(BF16), w_ref[(n_h + h) * hd:(n_h + h + 1) * hd, :],
                            preferred_element_type=F32)
    o_ref[...] = x_ref[...] + mod_ref[2:3, :] * res


def outproj_ab(x, mod, oa, od, lse, w_bf16, layer, geom):
    n_tok, d = x.shape
    n_h, _, hd = oa.shape
    tm = TOKEN_TILE // 2
    heads = pl.BlockSpec((n_h, tm, hd), lambda i: (0, i, 0))
    return pl.pallas_call(
        _outproj_ab_kernel,
        grid=(n_tok // tm,),
        in_specs=[
            pl.BlockSpec((tm, d), lambda i: (i, 0)),
            pl.BlockSpec((None, None, 6, d), lambda i: (layer, geom.seq_of_token(i * tm), 0, 0)),
            heads, heads, heads, heads, heads, heads, heads,
            pl.BlockSpec(w_bf16.shape, lambda i: (0, 0)),
        ],
        out_specs=pl.BlockSpec((tm, d), lambda i: (i, 0)),
        out_shape=jax.ShapeDtypeStruct(x.shape, F32),
        compiler_params=_params(("parallel",), VMEM_LIMIT),
        name="outproj_ab",
    )(x, mod, oa, *od, *lse, w_bf16)


def _outproj_kernel(x_ref, mod_ref, o_in_ref, w_ref, o_ref):
    res = jnp.dot(o_in_ref[...], w_ref[...], preferred_element_type=F32)
    o_ref[...] = x_ref[...] + mod_ref[2:3, :] * res


def outproj(x, mod, o_in, w_bf16, layer, geom):
    n_tok, d = x.shape
    tm = TOKEN_TILE
    return pl.pallas_call(
        _outproj_kernel,
        grid=(n_tok // tm,),
        in_specs=[
            pl.BlockSpec((tm, d), lambda i: (i, 0)),
            pl.BlockSpec((None, None, 6, d), lambda i: (layer, geom.seq_of_token(i * tm), 0, 0)),
            pl.BlockSpec((tm, o_in.shape[1]), lambda i: (i, 0)),
            pl.BlockSpec(w_bf16.shape, lambda i: (0, 0)),
        ],
        out_specs=pl.BlockSpec((tm, d), lambda i: (i, 0)),
        out_shape=jax.ShapeDtypeStruct(x.shape, F32),
        compiler_params=_params(("parallel",), VMEM_LIMIT),
        name="outproj",
    )(x, mod, o_in, w_bf16)


def _flash_kernel(q_ref, k_ref, vt_ref, o_ref, sa_ref, sb_ref, mxa_ref, mxb_ref, m_ref, acc_ref):
    n_g, tq, hd = q_ref.shape
    n_kt = k_ref.shape[0]
    m_ref[...] = jnp.full(m_ref.shape, -jnp.inf, F32)
    acc_ref[...] = jnp.zeros(acc_ref.shape, F32)

    def scores(j, s_ref, mx_ref):
        q_all = q_ref[...].reshape(n_g * tq, hd)
        s = lax.dot_general(k_ref[j], q_all, (((1,), (1,)), ((), ())),
                            preferred_element_type=F32)
        s_ref[...] = s
        mx_ref[...] = jnp.max(s, axis=0, keepdims=True)

    def consume(j, s_ref, mx_ref):
        m_old = m_ref[...]
        m_new = jnp.maximum(m_old, mx_ref[...])
        alpha = jnp.exp2(m_old - m_new)
        m_ref[...] = m_new
        tk, wq = s_ref.shape
        for n in range(wq // MXU_TILE):
            cols = slice(n * MXU_TILE, (n + 1) * MXU_TILE)
            acc = alpha[:, cols] * acc_ref[:, cols]
            for kc in range(tk // MXU_TILE):
                rows = slice(kc * MXU_TILE, (kc + 1) * MXU_TILE)
                p = jnp.exp2(s_ref[rows, cols] - m_new[:, cols]).astype(BF16)
                acc = acc + jnp.dot(vt_ref[j, :, rows], p, preferred_element_type=F32)
            acc_ref[:, cols] = acc

    scores(0, sa_ref, mxa_ref)

    def body(jj, carry):
        j = FLASH_UNROLL * jj
        for u in range(0, FLASH_UNROLL, 2):
            scores(j + u + 1, sb_ref, mxb_ref)
            consume(j + u, sa_ref, mxa_ref)
            scores(jnp.minimum(j + u + 2, n_kt - 1), sa_ref, mxa_ref)
            consume(j + u + 1, sb_ref, mxb_ref)
        return carry

    lax.fori_loop(0, n_kt // FLASH_UNROLL, body, 0)
    acc = acc_ref[...]
    o_ref[...] = (acc[:hd] / acc[hd:hd + 1]).astype(o_ref.dtype)


def flash_attention(qkv, vt_tiles, tok0, n_b, s_n):
    n_all, n_tok, hd = qkv.shape
    n_kv, _, hv, tk = vt_tiles.shape
    n_g = C_Q_HEADS // n_kv
    tq = FLASH_TQ
    nq = s_n // tq
    kt_per = s_n // tk
    assert kt_per % FLASH_UNROLL == 0 and tok0 % s_n == 0
    wq = n_g * tq
    k_view = qkv.reshape(n_all, n_tok // tk, tk, hd)
    seq0 = tok0 // s_n
    return pl.pallas_call(
        _flash_kernel,
        grid=(n_b, n_kv, nq),
        in_specs=[
            pl.BlockSpec((n_g, tq, hd), lambda b, h, i: (h, (tok0 + b * s_n) // tq + i, 0)),
            pl.BlockSpec((None, kt_per, tk, hd), lambda b, h, i: (C_K0 + h, seq0 + b, 0, 0)),
            pl.BlockSpec((None, kt_per, hv, tk), lambda b, h, i: (h, seq0 + b, 0, 0)),
        ],
        out_specs=pl.BlockSpec((None, hd, wq), lambda b, h, i: (h, 0, b * nq + i)),
        out_shape=jax.ShapeDtypeStruct((n_kv, hd, n_b * s_n * n_g), BF16),
        scratch_shapes=[pltpu.VMEM((tk, wq), F32), pltpu.VMEM((tk, wq), F32),
                        pltpu.VMEM((1, wq), F32), pltpu.VMEM((1, wq), F32),
                        pltpu.VMEM((1, wq), F32), pltpu.VMEM((hv, wq), F32)],
        compiler_params=_params(("parallel", "parallel", "arbitrary"), VMEM_LIMIT),
        name="flash_attention",
    )(qkv, k_view, vt_tiles)


def _router_kernel(x_ref, mod_ref, g_ref, wr_ref, br_ref, h_ref, idx_ref, gate_ref, cnt_ref,
                   run_ref):
    @pl.when(pl.program_id(0) == 0)
    def _():
        run_ref[...] = jnp.zeros(run_ref.shape, F32)

    h = _modulated_norm(x_ref[...], g_ref[...], mod_ref[3:4, :], mod_ref[4:5, :])
    h_ref[...] = h.astype(BF16)
    logits = jnp.dot(h, wr_ref[...], precision=HIGHEST,
                     preferred_element_type=F32) + br_ref[...]
    tm, n_e = logits.shape
    iota = lax.broadcasted_iota(jnp.int32, (tm, n_e), 1)
    vals, ids = [], []
    cur = logits
    for _ in range(TOP_K):
        m = jnp.max(cur, axis=-1, keepdims=True)
        am = jnp.min(jnp.where(cur == m, iota, n_e), axis=-1, keepdims=True)
        vals.append(m)
        ids.append(am)
        cur = jnp.where(iota == am, -jnp.inf, cur)
    es = [jnp.exp(v - vals[0]) for v in vals]
    den = es[0] + es[1] + es[2] + es[3]
    chosen = jnp.zeros((tm, n_e), F32)
    for kk in range(TOP_K):
        chosen = chosen + jnp.where(iota == ids[kk], 1.0, 0.0)
    r_i = lax.broadcasted_iota(jnp.int32, (tm, tm), 0)
    c_i = lax.broadcasted_iota(jnp.int32, (tm, tm), 1)
    tri = jnp.where(c_i < r_i, 1.0, 0.0).astype(BF16)
    before = jnp.dot(tri, chosen.astype(BF16), preferred_element_type=F32) + run_ref[...]
    run_ref[...] = run_ref[...] + jnp.sum(chosen, axis=0, keepdims=True)
    cnt_ref[...] = run_ref[...]
    lane = lax.broadcasted_iota(jnp.int32, idx_ref.shape, 1)
    idx_out = jnp.zeros(idx_ref.shape, jnp.int32)
    gate_out = jnp.zeros(gate_ref.shape, F32)
    for kk in range(TOP_K):
        rank = jnp.sum(jnp.where(iota == ids[kk], before, 0.0), axis=-1, keepdims=True)
        idx_out = jnp.where(lane == kk, ids[kk], idx_out)
        idx_out = jnp.where(lane == TOP_K + kk, rank.astype(jnp.int32), idx_out)
        gate_out = jnp.where(lane == kk, es[kk] / den, gate_out)
    idx_ref[...] = idx_out
    gate_ref[...] = gate_out


def router(x, mod, g_norm, w_router, b_router, layer, geom):
    n_tok, d = x.shape
    tm = TOKEN_TILE // 2
    return pl.pallas_call(
        _router_kernel,
        grid=(n_tok // tm,),
        in_specs=[
            pl.BlockSpec((tm, d), lambda i: (i, 0)),
            pl.BlockSpec((None, None, 6, d), lambda i: (layer, geom.seq_of_token(i * tm), 0, 0)),
            pl.BlockSpec((None, 1, d), lambda i: (layer, 0, 0)),
            pl.BlockSpec((None, d, N_EXPERTS), lambda i: (layer, 0, 0)),
            pl.BlockSpec((None, 1, N_EXPERTS), lambda i: (layer, 0, 0)),
        ],
        out_specs=[
            pl.BlockSpec((tm, d), lambda i: (i, 0)),
            pl.BlockSpec((tm, 2 * TOP_K), lambda i: (i, 0)),
            pl.BlockSpec((tm, 2 * TOP_K), lambda i: (i, 0)),
            pl.BlockSpec((1, N_EXPERTS), lambda i: (0, 0)),
        ],
        out_shape=[
            jax.ShapeDtypeStruct((n_tok, d), BF16),
            jax.ShapeDtypeStruct((n_tok, 2 * TOP_K), jnp.int32),
            jax.ShapeDtypeStruct((n_tok, 2 * TOP_K), F32),
            jax.ShapeDtypeStruct((1, N_EXPERTS), F32),
        ],
        scratch_shapes=[pltpu.VMEM((1, N_EXPERTS), F32)],
        compiler_params=_params(("arbitrary",), VMEM_LIMIT),
        name="router",
    )(x, mod, g_norm.reshape(-1, 1, d), w_router, b_router.reshape(-1, 1, N_EXPERTS))


def _moe_kernel(be_ref, nu_ref, x_ref, wgu_ref, bgu_ref, wd_ref, bd_ref, o_ref,
                wgu_s, wd_s):
    i = pl.program_id(0)
    e = be_ref[i]
    e_prev = be_ref[jnp.maximum(i - 1, 0)]

    @pl.when((i == 0) | (e != e_prev))
    def _():
        wgu_s[...] = wgu_ref[...].astype(BF16)
        wd_s[...] = wd_ref[...].astype(BF16)

    @pl.when(i < nu_ref[0])
    def _():
        d_ff = wd_s.shape[0]
        gu = jnp.dot(x_ref[...], wgu_s[...], preferred_element_type=F32) + bgu_ref[...]
        glu = jnp.minimum(gu[:, :d_ff], SWIGLU_LIMIT)
        lin = jnp.clip(gu[:, d_ff:], -SWIGLU_LIMIT, SWIGLU_LIMIT)
        act = glu / (1.0 + jnp.exp(-SWIGLU_ALPHA * glu)) * (lin + 1.0)
        y = jnp.dot(act.astype(BF16), wd_s[...], preferred_element_type=F32) + bd_ref[...]
        o_ref[...] = y.astype(o_ref.dtype)

    @pl.when(i >= nu_ref[0])
    def _():
        o_ref[...] = jnp.zeros(o_ref.shape, o_ref.dtype)


def moe_experts(xs, blk_e, n_used, w_gu, b_gu, w_down, b_down, layer):
    n_rows, d = xs.shape
    d_ff = w_down.shape[2]
    nblk = n_rows // MOE_BLOCK
    grid_spec = pltpu.PrefetchScalarGridSpec(
        num_scalar_prefetch=2,
        grid=(nblk,),
        in_specs=[
            pl.BlockSpec((MOE_BLOCK, d), lambda i, be, nu: (jnp.minimum(i, nu[0] - 1), 0)),
            pl.BlockSpec((None, None, d, 2 * d_ff), lambda i, be, nu: (layer, be[i], 0, 0)),
            pl.BlockSpec((None, None, 1, 2 * d_ff), lambda i, be, nu: (layer, be[i], 0, 0)),
            pl.BlockSpec((None, None, d_ff, d), lambda i, be, nu: (layer, be[i], 0, 0)),
            pl.BlockSpec((None, None, 1, d), lambda i, be, nu: (layer, be[i], 0, 0)),
        ],
        out_specs=pl.BlockSpec((MOE_BLOCK, d), lambda i, be, nu: (i, 0)),
        scratch_shapes=[pltpu.VMEM((d, 2 * d_ff), BF16), pltpu.VMEM((d_ff, d), BF16)],
    )
    depth, n_e = w_gu.shape[:2]
    return pl.pallas_call(
        _moe_kernel,
        grid_spec=grid_spec,
        out_shape=jax.ShapeDtypeStruct((n_rows, d), BF16),
        compiler_params=_params(("arbitrary",), VMEM_LIMIT),
        name="moe_experts",
    )(blk_e, n_used, xs, w_gu, b_gu.reshape(depth, n_e, 1, 2 * d_ff), w_down,
      b_down.reshape(depth, n_e, 1, d))


def _combine_kernel(x_ref, mod_ref, y_ref, gate_ref, o_ref):
    g = gate_ref[...]
    acc = g[:, 0:1] * y_ref[0].astype(F32)
    for kk in range(1, TOP_K):
        acc = acc + g[:, kk:kk + 1] * y_ref[kk].astype(F32)
    o_ref[...] = x_ref[...] + mod_ref[5:6, :] * acc


def moe_combine(x, mod, yk, gates, layer, geom):
    n_tok, d = x.shape
    tm = TOKEN_TILE // 2
    return pl.pallas_call(
        _combine_kernel,
        grid=(n_tok // tm,),
        in_specs=[
            pl.BlockSpec((tm, d), lambda i: (i, 0)),
            pl.BlockSpec((None, None, 6, d), lambda i: (layer, geom.seq_of_token(i * tm), 0, 0)),
            pl.BlockSpec((TOP_K, tm, d), lambda i: (0, i, 0)),
            pl.BlockSpec((tm, 2 * TOP_K), lambda i: (i, 0)),
        ],
        out_specs=pl.BlockSpec((tm, d), lambda i: (i, 0)),
        out_shape=jax.ShapeDtypeStruct(x.shape, F32),
        compiler_params=_params(("parallel",), VMEM_LIMIT),
        name="moe_combine",
    )(x, mod, yk, gates)


def moe_layer(x, mod, g_norm, w_router, b_router, w_gu, b_gu, w_down, b_down, layer, geom):
    n_tok, d = x.shape
    h, idx8, gates8, counts = router(x, mod, g_norm, w_router, b_router, layer, geom)
    idx = idx8[:, :TOP_K]
    rank = idx8[:, TOP_K:]
    counts = counts[0].astype(jnp.int32)
    pcounts = ((counts + MOE_BLOCK - 1) // MOE_BLOCK) * MOE_BLOCK
    pend = jnp.cumsum(pcounts)
    pstart = pend - pcounts
    experts = jnp.arange(N_EXPERTS, dtype=jnp.int32)
    dest = rank + jnp.sum(jnp.where(idx[:, :, None] == experts, pstart, 0), axis=-1)
    nblk = (n_tok * TOP_K) // MOE_BLOCK + N_EXPERTS
    n_rows = nblk * MOE_BLOCK
    tok_ids = jnp.broadcast_to(jnp.arange(n_tok, dtype=jnp.int32)[:, None], dest.shape)
    tok = jnp.zeros((n_rows,), jnp.int32).at[dest.reshape(-1)].set(tok_ids.reshape(-1))
    n_used = (pend[-1] // MOE_BLOCK).astype(jnp.int32).reshape(1)
    blk_start = jnp.arange(nblk, dtype=jnp.int32) * MOE_BLOCK
    blk_last = jnp.minimum(blk_start, pend[-1] - 1)
    blk_e = jnp.sum(pend[None, :] <= blk_last[:, None], axis=-1, dtype=jnp.int32)
    blk_e = jnp.minimum(blk_e, N_EXPERTS - 1)
    xs = jnp.take(h, tok, axis=0)
    yb = moe_experts(xs, blk_e, n_used, w_gu, b_gu, w_down, b_down, layer)
    yk = jnp.take(yb, dest.T.reshape(-1), axis=0).reshape(TOP_K, n_tok, d)
    return moe_combine(x, mod, yk, gates8, layer, geom)


def _seq_positions(geom):
    return jnp.concatenate([jnp.arange(s, dtype=jnp.int32)
                            for _, n_b, s in geom.groups() for _ in range(n_b)])


def mixer_ab_layer(x, mod, g_norm, w_in, w_out, g_qn_a, g_kn_a, rpb, g_qn_b, g_kn_b, layer, geom):
    n_tok, d = x.shape
    hd = HEAD_DIM
    qkv = norm_proj(x, mod, g_norm, w_in.astype(BF16), layer, geom, tn=1536)
    ones = jnp.ones((NA_HEADS, hd), F32)
    gains = jnp.concatenate([
        jnp.broadcast_to(g_qn_a * Q_SCALE, (NA_HEADS, hd)),
        jnp.broadcast_to(g_kn_a, (NA_HEADS, hd)),
        ones,
        jnp.repeat(g_qn_b * Q_SCALE, DIL_HEADS, axis=0),
        jnp.repeat(g_kn_b, DIL_HEADS, axis=0),
        ones, ones, ones])
    cos, sin = _rope_angles(_seq_positions(geom), hd)
    cos = jnp.stack([jnp.ones_like(cos), cos])
    sin = jnp.stack([jnp.zeros_like(sin), sin])
    na_blocks = 2 * NA_HEADS // 8
    qkv = qk_prep(qkv, gains, cos, sin, _rotate_half_matrix(hd, 1), hb=8,
                  n_blocks=na_blocks + 2 * N_DIL,
                  block_fn=lambda i: jnp.where(i < na_blocks, i, i + 1),
                  table_fn=lambda i: jnp.where(i < na_blocks, 0, 1))
    oa = neighborhood_attention(qkv, _na_bias_table(rpb), geom)
    od, lse = [], []
    for gi, (window, dil) in enumerate(DIL_PATTERNS):
        assert (window // 2) // dil == DIL_SIDE
        o_g, lse_g = dilated_attention(qkv, gi, dil, geom)
        od.append(o_g)
        lse.append(lse_g)
    return outproj_ab(x, mod, oa, od, lse, w_out.astype(BF16), layer, geom)


def mixer_c_layer(x, mod, g_norm, w_in, w_out, g_qn, g_kn, layer, geom):
    n_tok, d = x.shape
    hd = HEAD_DIM
    qkv = norm_proj(x, mod, g_norm, w_in.astype(BF16), layer, geom, tn=w_in.shape[1])
    pos = _seq_positions(geom)
    half = hd // 2
    cr, sr = _rope_angles(pos // GRID_W, half)
    cc, sc = _rope_angles(pos % GRID_W, half)
    gains = jnp.concatenate([
        jnp.broadcast_to(g_qn * (Q_SCALE * LOG2_E), (C_Q_HEADS, hd)),
        jnp.broadcast_to(g_kn, (C_KV_HEADS, hd)),
        jnp.ones((C_KV_HEADS, hd), F32)])
    qkv = qk_prep(qkv, gains, jnp.concatenate([cr, cc], axis=-1)[None],
                  jnp.concatenate([sr, sc], axis=-1)[None], _rotate_half_matrix(half, 2),
                  hb=C_KV_HEADS, n_blocks=(C_Q_HEADS + C_KV_HEADS) // C_KV_HEADS,
                  block_fn=lambda i: i, table_fn=lambda i: 0)
    tk, tq = FLASH_TK, FLASH_TQ
    n_g = C_Q_HEADS // C_KV_HEADS
    vt_tiles = qkv[C_V0:].reshape(C_KV_HEADS, n_tok // tk, tk, hd).transpose(0, 1, 3, 2)
    ones = jnp.ones((C_KV_HEADS, n_tok // tk, BF16_SUBLANES, tk), BF16)
    vt_tiles = jnp.concatenate([vt_tiles, ones], axis=2)
    ot = jnp.concatenate([flash_attention(qkv, vt_tiles, tok0, n_b, s_n)
                          for tok0, n_b, s_n in geom.groups()], axis=-1)
    o = ot.reshape(C_KV_HEADS, hd, n_tok // tq, n_g, tq).transpose(2, 4, 0, 3, 1)
    o = o.reshape(n_tok, C_Q_HEADS * hd)
    return outproj(x, mod, o, w_out.astype(BF16), layer, geom)


def kernel(x_prompt, x_sample, c_prompt, c_sample, w_ada, b_ada, g_norm_mix, g_norm_ffn, w_in_ab, w_out_ab, g_qn_a, g_kn_a, rpb_a, g_qn_b, g_kn_b, w_in_c, w_out_c, g_qn_c, g_kn_c, w_router, b_router, w_gate_up, b_gate_up, w_down, b_down):
    b_p, s_p, d = x_prompt.shape
    b_d, s_d, _ = x_sample.shape
    geom = Geom(b_p, s_p, b_d, s_d)
    depth = w_ada.shape[0]
    x = jnp.concatenate([x_prompt.reshape(-1, d), x_sample.reshape(-1, d)], axis=0)
    mod = ada_modulation(jnp.concatenate([c_prompt, c_sample], axis=0), w_ada, b_ada)
    for layer in range(depth):
        i = layer // 2
        if layer % 2 == 0:
            x = mixer_ab_layer(x, mod, g_norm_mix, w_in_ab[i], w_out_ab[i], g_qn_a[i], g_kn_a[i],
                               rpb_a[i], g_qn_b[i], g_kn_b[i], layer, geom)
        else:
            x = mixer_c_layer(x, mod, g_norm_mix, w_in_c[i], w_out_c[i], g_qn_c[i], g_kn_c[i],
                              layer, geom)
        x = moe_layer(x, mod, g_norm_ffn, w_router, b_router, w_gate_up, b_gate_up, w_down,
                      b_down, layer, geom)
    y_prompt = x[:geom.n_p].reshape(b_p, s_p, d)
    y_sample = x[geom.n_p:].reshape(b_d, s_d, d)
    return (y_prompt, y_sample)
```

```python
import functools
import math
from typing import NamedTuple

import jax
import jax.numpy as jnp
from jax import lax
from jax.experimental import pallas as pl
from jax.experimental.pallas import tpu as pltpu

F32 = jnp.float32
BF16 = jnp.bfloat16
HIGHEST = lax.Precision.HIGHEST

GRID_W = 64
HEAD_DIM = 64
ROPE_THETA = 10000.0
EPS = 1e-6
NEG = -1e30
NA_HEADS = 8
NA_KH = 8
NA_KW = 16
DIL_HEADS = 8
DIL_PATTERNS = ((128, 1), (512, 4), (2048, 16))
N_DIL = len(DIL_PATTERNS)
C_Q_HEADS = 16
C_KV_HEADS = 4
N_EXPERTS = 32
TOP_K = 4
SWIGLU_LIMIT = 7.0
SWIGLU_ALPHA = 1.702
Q_SCALE = HEAD_DIM ** -0.5
LOG2_E = math.log2(math.e)
BF16_SUBLANES = 16
LANES = 128
MXU_TILE = 256

V7X_VMEM_BYTES = 64 * 1024 * 1024
VMEM_LIMIT = V7X_VMEM_BYTES - 8 * 1024 * 1024

TOKEN_TILE = 1024
PREP_TILE = 2048
NA_BLOCK = 8 * GRID_W
NA_ROWS_PER_TRIP = 2
DIL_QBLOCK = 128
DIL_SIDE = 64
DIL_CHUNK = 2048
DIL_PROBLEMS_PER_TRIP = 4
FLASH_TQ = 256
FLASH_TK = 512
FLASH_UNROLL = 8
MOE_BLOCK = 512

NA_Q0, NA_K0, NA_V0 = 0, NA_HEADS, 2 * NA_HEADS
C_K0 = C_Q_HEADS
C_V0 = C_Q_HEADS + C_KV_HEADS


class Geom(NamedTuple):
    b_p: int
    s_p: int
    b_d: int
    s_d: int

    @property
    def n_p(self):
        return self.b_p * self.s_p

    @property
    def n_tok(self):
        return self.n_p + self.b_d * self.s_d

    @property
    def n_seq(self):
        return self.b_p + self.b_d

    def groups(self):
        return ((0, self.b_p, self.s_p), (self.n_p, self.b_d, self.s_d))

    def seq_of_token(self, t0):
        return jnp.where(t0 < self.n_p, t0 // self.s_p,
                         self.b_p + (t0 - self.n_p) // self.s_d)

    def seq_span(self, t0):
        in_p = t0 < self.n_p
        start = jnp.where(in_p, (t0 // self.s_p) * self.s_p,
                          self.n_p + ((t0 - self.n_p) // self.s_d) * self.s_d)
        return start, jnp.where(in_p, self.s_p, self.s_d)


def _params(semantics, vmem=None):
    return pltpu.CompilerParams(dimension_semantics=semantics,
                                vmem_limit_bytes=vmem)


def _ada_kernel(c_ref, w_ref, b_ref, o_ref):
    c = c_ref[...]
    a = c / (1.0 + jnp.exp(-c))
    o_ref[...] = jnp.dot(a, w_ref[...], precision=HIGHEST,
                         preferred_element_type=F32) + b_ref[...]


def ada_modulation(c_all, w_ada, b_ada):
    depth, d, d6 = w_ada.shape
    n_seq = c_all.shape[0]
    rows = 8
    c_pad = jnp.zeros((rows, d), F32).at[:n_seq].set(c_all)
    out = pl.pallas_call(
        _ada_kernel,
        grid=(depth, d6 // d),
        in_specs=[
            pl.BlockSpec((rows, d), lambda l, j: (0, 0)),
            pl.BlockSpec((None, d, d), lambda l, j: (l, 0, j)),
            pl.BlockSpec((None, 1, d), lambda l, j: (l, 0, j)),
        ],
        out_specs=pl.BlockSpec((None, rows, d), lambda l, j: (l, 0, j)),
        out_shape=jax.ShapeDtypeStruct((depth, rows, d6), F32),
        compiler_params=_params(("arbitrary", "arbitrary")),
        name="ada_modulation",
    )(c_pad, w_ada, b_ada.reshape(depth, 1, d6))
    return out[:, :n_seq].reshape(depth, n_seq, 6, d)


def _modulated_norm(x, g, shift, scale):
    r = lax.rsqrt(jnp.mean(x * x, axis=-1, keepdims=True) + EPS)
    return (x * r * g) * (1.0 + scale) + shift


def _norm_proj_kernel(x_ref, mod_ref, g_ref, w_ref, o_ref, *scratch, dil):
    h = _modulated_norm(x_ref[...], g_ref[...], mod_ref[0:1, :], mod_ref[1:2, :])
    acc = jnp.dot(h.astype(BF16), w_ref[...], preferred_element_type=F32)
    tm = x_ref.shape[0]
    n_h, _, _, hd = o_ref.shape
    if dil == 1:
        for c in range(n_h):
            o_ref[c, 0] = acc[:, c * hd:(c + 1) * hd].astype(o_ref.dtype)
        return
    acc_ref, = scratch
    for s in range(n_h // 2):
        acc_ref[s] = acc[:, s * LANES:(s + 1) * LANES]
    for s in range(n_h // 2):
        for r in range(dil):
            pair = acc_ref[s, pl.ds(r, tm // dil, stride=dil), :].astype(o_ref.dtype)
            o_ref[2 * s, r] = pair[:, :hd]
            o_ref[2 * s + 1, r] = pair[:, hd:]


def norm_proj(x, mod, g_norm, w_bf16, layer, geom, dil):
    n_tok, d = x.shape
    n_out = w_bf16.shape[1]
    tm = TOKEN_TILE
    hd = HEAD_DIM
    n_h = n_out // hd
    scratch = [] if dil == 1 else [pltpu.VMEM((n_h // 2, tm, LANES), F32)]
    return pl.pallas_call(
        functools.partial(_norm_proj_kernel, dil=dil),
        grid=(n_tok // tm,),
        in_specs=[
            pl.BlockSpec((tm, d), lambda i: (i, 0)),
            pl.BlockSpec((None, None, 6, d), lambda i: (layer, geom.seq_of_token(i * tm), 0, 0)),
            pl.BlockSpec((None, 1, d), lambda i: (layer, 0, 0)),
            pl.BlockSpec((d, n_out), lambda i: (0, 0)),
        ],
        out_specs=pl.BlockSpec((n_h, dil, tm // dil, hd), lambda i: (0, 0, i, 0)),
        out_shape=jax.ShapeDtypeStruct((n_h, dil, n_tok // dil, hd), BF16),
        scratch_shapes=scratch,
        compiler_params=_params(("parallel",), VMEM_LIMIT),
        name="norm_proj",
    )(x, mod, g_norm.reshape(-1, 1, d), w_bf16)


def _prep_kernel(x_ref, g_ref, *rest, rope):
    o_ref = rest[-1]
    hb, ts, hd = x_ref.shape
    x = x_ref[...].astype(F32)
    r = lax.rsqrt(jnp.mean(x * x, axis=-1, keepdims=True) + EPS)
    y = x * r * g_ref[...]
    if rope:
        cos_ref, sin_ref, p_ref = rest[:3]
        yr = jnp.dot(y.reshape(hb * ts, hd).astype(BF16), p_ref[...],
                     preferred_element_type=F32).reshape(hb, ts, hd)
        y = y * cos_ref[...] + yr * sin_ref[...]
    o_ref[...] = y.astype(o_ref.dtype)


def qk_prep(x, gains, hb, n_blocks, tables=None):
    n_h, n_tok, hd = x.shape
    ts = PREP_TILE
    rope = tables is not None
    blk = pl.BlockSpec((hb, ts, hd), lambda i, s: (i, s, 0))
    in_specs = [blk, pl.BlockSpec((hb, 1, hd), lambda i, s: (i, 0, 0))]
    args = [x, gains.reshape(n_blocks * hb, 1, hd).astype(F32)]
    if rope:
        tab = pl.BlockSpec((ts, hd), lambda i, s: (s, 0))
        in_specs += [tab, tab, pl.BlockSpec((hd, hd), lambda i, s: (0, 0))]
        args += list(tables)
    return pl.pallas_call(
        functools.partial(_prep_kernel, rope=rope),
        grid=(n_blocks, n_tok // ts),
        in_specs=in_specs,
        out_specs=blk,
        out_shape=jax.ShapeDtypeStruct(x.shape, BF16),
        input_output_aliases={0: 0},
        compiler_params=_params(("parallel", "parallel")),
        name="qk_prep",
    )(*args)


def _rope_angles(pos, dim):
    inv = ROPE_THETA ** (-jnp.arange(0, dim, 2, dtype=F32) / dim)
    ang = pos.astype(F32)[:, None] * inv[None, :]
    ang = jnp.concatenate([ang, ang], axis=-1)
    return jnp.cos(ang), jnp.sin(ang)


def _rotate_half_matrix(dim, n_blocks):
    half = dim // 2
    i = jnp.arange(dim * n_blocks)
    blk, w = i // dim, i % dim
    src = blk * dim + jnp.where(w < half, w + half, w - half)
    sign = jnp.where(w < half, -1.0, 1.0)
    p = jnp.zeros((dim * n_blocks, dim * n_blocks), F32).at[src, i].set(sign)
    return p.astype(BF16)


def _na_bias_table(rpb):
    n_h = rpb.shape[0]
    qc = jnp.arange(GRID_W)[:, None]
    kc = jnp.arange(GRID_W)[None, :]
    c0 = jnp.clip(qc - NA_KW // 2, 0, GRID_W - NA_KW)
    ok = (kc >= c0) & (kc < c0 + NA_KW)
    n_dr = rpb.shape[1]
    period = 2 * GRID_W
    ext = jnp.concatenate([rpb[..., NA_KW - 1:],
                           jnp.zeros((n_h, n_dr, period - (2 * NA_KW - 1)), rpb.dtype),
                           rpb[..., :NA_KW - 1]], axis=-1)
    toep = jnp.tile(ext, (1, 1, GRID_W))[..., :GRID_W * (period - 1)]
    toep = toep.reshape(n_h, n_dr, GRID_W, period - 1)[..., :GRID_W]
    toep = jnp.where(ok[None, None], toep, NEG)
    bias = jnp.stack([toep[:, NA_KH - 1 - d0:2 * NA_KH - 1 - d0] for d0 in range(NA_KH)])
    bias = bias.transpose(0, 1, 3, 2, 4).reshape(NA_KH, n_h, GRID_W, NA_KH * GRID_W)
    return bias.astype(BF16)


def _na_kernel(q_ref, kp_ref, kc_ref, kn_ref, vp_ref, vc_ref, vn_ref, tbl_ref,
               o_ref, ks_ref, vs_ref, *, geom):
    blk = NA_BLOCK
    j = pl.program_id(0)
    seq_start, seq_len = geom.seq_span(j * blk)
    rows = seq_len // GRID_W
    jl = j - seq_start // blk
    ks_ref[:, 0:blk, :] = kp_ref[...]
    ks_ref[:, blk:2 * blk, :] = kc_ref[...]
    ks_ref[:, 2 * blk:3 * blk, :] = kn_ref[...]
    vs_ref[:, 0:blk, :] = vp_ref[...]
    vs_ref[:, blk:2 * blk, :] = vc_ref[...]
    vs_ref[:, 2 * blk:3 * blk, :] = vn_ref[...]
    n_keys = NA_KH * GRID_W

    def body(ii, carry):
        probs = []
        for u in range(NA_ROWS_PER_TRIP):
            i = ii * NA_ROWS_PER_TRIP + u
            r = jl * NA_KH + i
            r0 = jnp.clip(r - NA_KH // 2, 0, rows - NA_KH)
            off = pl.multiple_of((r0 - (jl - 1) * NA_KH) * GRID_W, GRID_W)
            d0 = r - r0
            qoff = pl.multiple_of(i * GRID_W, GRID_W)
            for h in range(NA_HEADS):
                q = q_ref[h, pl.ds(qoff, GRID_W), :]
                k = ks_ref[h, pl.ds(off, n_keys), :]
                s = lax.dot_general(q, k, (((1,), (1,)), ((), ())), preferred_element_type=F32)
                probs.append((h, off, qoff, s + tbl_ref[d0, h].astype(F32)))
        soft = []
        for h, off, qoff, s in probs:
            m = jnp.max(s, axis=-1, keepdims=True)
            p = jnp.exp(s - m)
            soft.append((h, off, qoff, p.astype(BF16), jnp.sum(p, axis=-1, keepdims=True)))
        for h, off, qoff, p, l in soft:
            v = vs_ref[h, pl.ds(off, n_keys), :]
            o = jnp.dot(p, v, preferred_element_type=F32) / l
            o_ref[h, pl.ds(qoff, GRID_W), :] = o.astype(o_ref.dtype)
        return carry

    lax.fori_loop(0, NA_KH // NA_ROWS_PER_TRIP, body, 0)


def neighborhood_attention(qkv, tbl, geom):
    _, n_tok, hd = qkv.shape
    n_h = NA_HEADS
    blk = NA_BLOCK
    nb = n_tok // blk

    def spec(head0, shift):
        hb = head0 // n_h
        return pl.BlockSpec((n_h, blk, hd), lambda j: (hb, jnp.clip(j + shift, 0, nb - 1), 0))

    return pl.pallas_call(
        functools.partial(_na_kernel, geom=geom),
        grid=(nb,),
        in_specs=[spec(NA_Q0, 0), spec(NA_K0, -1), spec(NA_K0, 0), spec(NA_K0, 1),
                  spec(NA_V0, -1), spec(NA_V0, 0), spec(NA_V0, 1),
                  pl.BlockSpec(tbl.shape, lambda j: (0, 0, 0, 0))],
        out_specs=pl.BlockSpec((n_h, blk, hd), lambda j: (0, j, 0)),
        out_shape=jax.ShapeDtypeStruct((n_h, n_tok, hd), BF16),
        scratch_shapes=[pltpu.VMEM((n_h, 3 * blk, hd), BF16),
                        pltpu.VMEM((n_h, 3 * blk, hd), BF16)],
        compiler_params=_params(("parallel",), VMEM_LIMIT),
        name="neighborhood_attention",
    )(qkv, qkv, qkv, qkv, qkv, qkv, qkv, tbl)


def _dil_kernel(q_ref, kp_ref, kc_ref, kn_ref, vp_ref, vc_ref, vn_ref,
                o_ref, lse_ref, ks_ref, vs_ref, *, geom, dil):
    c = pl.program_id(1)
    chunk, hd = q_ref.shape
    side = DIL_SIDE
    qb = DIL_QBLOCK
    kb = qb + 2 * side
    ks_ref[0:side, :] = kp_ref[...]
    ks_ref[side:side + chunk, :] = kc_ref[...]
    ks_ref[side + chunk:, :] = kn_ref[...]
    vs_ref[0:side, :] = vp_ref[...]
    vs_ref[side:side + chunk, :] = vc_ref[...]
    vs_ref[side + chunk:, :] = vn_ref[...]
    plane_rows = geom.n_tok // dil
    t0 = ((c * chunk) % plane_rows) * dil
    seq_start, seq_len = geom.seq_span(t0)
    row0 = (t0 - seq_start) // dil
    cls_rows = seq_len // dil
    jq = lax.broadcasted_iota(jnp.int32, (qb, kb), 0)
    jk = lax.broadcasted_iota(jnp.int32, (qb, kb), 1) - side
    near_bias = jnp.where(jnp.abs(jk - jq) <= side, 0.0, NEG)
    before_bias = jnp.where(jk < 0, NEG, 0.0)
    after_bias = jnp.where(jk >= qb, NEG, 0.0)
    n_trip = min(DIL_PROBLEMS_PER_TRIP, chunk // qb)
    assert chunk % (qb * n_trip) == 0

    def body(ii, carry):
        probs = []
        for u in range(n_trip):
            i = ii * n_trip + u
            qoff = pl.multiple_of(i * qb, qb)
            p0 = row0 + i * qb
            bias = near_bias + jnp.where(p0 == 0, before_bias, 0.0)
            bias = bias + jnp.where(p0 + qb == cls_rows, after_bias, 0.0)
            q = q_ref[pl.ds(qoff, qb), :]
            k = ks_ref[pl.ds(qoff, kb), :]
            s = lax.dot_general(q, k, (((1,), (1,)), ((), ())), preferred_element_type=F32)
            probs.append((qoff, s + bias))
        soft = []
        for qoff, s in probs:
            m = jnp.max(s, axis=-1, keepdims=True)
            p = jnp.exp(s - m)
            soft.append((qoff, p.astype(BF16), m, jnp.sum(p, axis=-1, keepdims=True)))
        for qoff, p, m, l in soft:
            v = vs_ref[pl.ds(qoff, kb), :]
            o = jnp.dot(p, v, preferred_element_type=F32) / l
            o_ref[pl.ds(qoff, qb), :] = o.astype(o_ref.dtype)
            lse_ref[pl.ds(qoff, qb), :] = jnp.broadcast_to(m + jnp.log(l), (qb, hd))
        return carry

    lax.fori_loop(0, chunk // (qb * n_trip), body, 0)


def dilated_attention(qkv, dil, geom):
    _, n_tok, hd = qkv.shape
    n_h = DIL_HEADS
    chunk = min(DIL_CHUNK, min(geom.s_p, geom.s_d) // dil)
    assert chunk % DIL_QBLOCK == 0
    per = chunk // DIL_SIDE
    n_side = n_tok // DIL_SIDE

    def cur(head0):
        return pl.BlockSpec((None, chunk, hd), lambda h, c: (head0 + h, c, 0))

    def prev(head0):
        return pl.BlockSpec((None, DIL_SIDE, hd),
                            lambda h, c: (head0 + h, jnp.maximum(c * per - 1, 0), 0))

    def nxt(head0):
        return pl.BlockSpec((None, DIL_SIDE, hd),
                            lambda h, c: (head0 + h, jnp.minimum((c + 1) * per, n_side - 1), 0))

    q0, k0, v0 = 0, n_h, 2 * n_h
    return pl.pallas_call(
        functools.partial(_dil_kernel, geom=geom, dil=dil),
        grid=(n_h, n_tok // chunk),
        in_specs=[cur(q0), prev(k0), cur(k0), nxt(k0), prev(v0), cur(v0), nxt(v0)],
        out_specs=[cur(0), cur(0)],
        out_shape=[jax.ShapeDtypeStruct((n_h, n_tok, hd), BF16),
                   jax.ShapeDtypeStruct((n_h, n_tok, hd), F32)],
        scratch_shapes=[pltpu.VMEM((chunk + 2 * DIL_SIDE, hd), BF16),
                        pltpu.VMEM((chunk + 2 * DIL_SIDE, hd), BF16)],
        compiler_params=_params(("parallel", "parallel")),
        name="dilated_attention",
    )(qkv, qkv, qkv, qkv, qkv, qkv, qkv)


def _to_natural(src_ref, dst_ref):
    n_h, dil, n, _ = src_ref.shape
    for h in range(0, n_h, 2):
        for r in range(dil):
            pair = jnp.concatenate([src_ref[h, r].astype(F32), src_ref[h + 1, r].astype(F32)],
                                   axis=-1)
            if dil == 1:
                dst_ref[h // 2] = pair
            else:
                dst_ref[h // 2, pl.ds(r, n, stride=dil), :] = pair


def _outproj_ab_kernel(x_ref, mod_ref, oa_ref, od0_ref, od1_ref, od2_ref,
                       l0_ref, l1_ref, l2_ref, w_ref, o_ref, *scratch):
    so = scratch[:N_DIL]
    sl = scratch[N_DIL:]
    for src, dst in zip((od0_ref, od1_ref, od2_ref, l0_ref, l1_ref, l2_ref), so + sl):
        _to_natural(src, dst)
    n_h = oa_ref.shape[0]
    slabs = [jnp.concatenate([oa_ref[h], oa_ref[h + 1]], axis=-1) for h in range(0, n_h, 2)]
    for s in range(n_h // 2):
        l0, l1, l2 = sl[0][s], sl[1][s], sl[2][s]
        m = jnp.maximum(jnp.maximum(l0, l1), l2)
        e0, e1, e2 = jnp.exp(l0 - m), jnp.exp(l1 - m), jnp.exp(l2 - m)
        ob = (e0 * so[0][s] + e1 * so[1][s] + e2 * so[2][s]) / (e0 + e1 + e2)
        slabs.append(ob.astype(BF16))
    cat = jnp.concatenate(slabs, axis=-1)
    res = jnp.dot(cat, w_ref[...], preferred_element_type=F32)
    o_ref[...] = x_ref[...] + mod_ref[2:3, :] * res


def outproj_ab(x, mod, oa, od, lse, w_bf16, layer, geom):
    n_tok, d = x.shape
    n_h, _, hd = oa.shape
    tm = TOKEN_TILE // 2

    def cls(a):
        dil = a.shape[1]
        return pl.BlockSpec((n_h, dil, tm // dil, hd), lambda i: (0, 0, i, 0))

    return pl.pallas_call(
        _outproj_ab_kernel,
        grid=(n_tok // tm,),
        in_specs=[
            pl.BlockSpec((tm, d), lambda i: (i, 0)),
            pl.BlockSpec((None, None, 6, d), lambda i: (layer, geom.seq_of_token(i * tm), 0, 0)),
            pl.BlockSpec((n_h, tm, hd), lambda i: (0, i, 0)),
            *[cls(a) for a in od], *[cls(a) for a in lse],
            pl.BlockSpec(w_bf16.shape, lambda i: (0, 0)),
        ],
        out_specs=pl.BlockSpec((tm, d), lambda i: (i, 0)),
        out_shape=jax.ShapeDtypeStruct(x.shape, F32),
        scratch_shapes=[pltpu.VMEM((n_h // 2, tm, LANES), F32) for _ in range(2 * N_DIL)],
        compiler_params=_params(("parallel",), VMEM_LIMIT),
        name="outproj_ab",
    )(x, mod, oa, *od, *lse, w_bf16)


def _outproj_kernel(x_ref, mod_ref, o_in_ref, w_ref, o_ref):
    res = jnp.dot(o_in_ref[...], w_ref[...], preferred_element_type=F32)
    o_ref[...] = x_ref[...] + mod_ref[2:3, :] * res


def outproj(x, mod, o_in, w_bf16, layer, geom):
    n_tok, d = x.shape
    tm = TOKEN_TILE
    return pl.pallas_call(
        _outproj_kernel,
        grid=(n_tok // tm,),
        in_specs=[
            pl.BlockSpec((tm, d), lambda i: (i, 0)),
            pl.BlockSpec((None, None, 6, d), lambda i: (layer, geom.seq_of_token(i * tm), 0, 0)),
            pl.BlockSpec((tm, o_in.shape[1]), lambda i: (i, 0)),
            pl.BlockSpec(w_bf16.shape, lambda i: (0, 0)),
        ],
        out_specs=pl.BlockSpec((tm, d), lambda i: (i, 0)),
        out_shape=jax.ShapeDtypeStruct(x.shape, F32),
        compiler_params=_params(("parallel",), VMEM_LIMIT),
        name="outproj",
    )(x, mod, o_in, w_bf16)


def _flash_kernel(q_ref, k_ref, vt_ref, o_ref, sa_ref, sb_ref, mxa_ref, mxb_ref, m_ref, acc_ref,
                  *, unroll):
    n_g, tq, hd = q_ref.shape
    n_kt = k_ref.shape[0]
    m_ref[...] = jnp.full(m_ref.shape, -jnp.inf, F32)
    acc_ref[...] = jnp.zeros(acc_ref.shape, F32)
    n_col = (n_g * tq) // MXU_TILE

    def scores(j, s_ref, mx_ref, n):
        cols = slice(n * MXU_TILE, (n + 1) * MXU_TILE)
        q_n = q_ref[...].reshape(n_g * tq, hd)[n * MXU_TILE:(n + 1) * MXU_TILE]
        s = lax.dot_general(k_ref[j], q_n, (((1,), (1,)), ((), ())),
                            preferred_element_type=F32)
        s_ref[:, cols] = s
        mx_ref[:, cols] = jnp.max(s, axis=0, keepdims=True)

    def consume(j, s_ref, mx_ref, n):
        cols = slice(n * MXU_TILE, (n + 1) * MXU_TILE)
        m_old = m_ref[:, cols]
        m_new = jnp.maximum(m_old, mx_ref[:, cols])
        alpha = jnp.exp2(m_old - m_new)
        m_ref[:, cols] = m_new
        tk = s_ref.shape[0]
        acc = alpha * acc_ref[:, cols]
        for kc in range(tk // MXU_TILE):
            rows = slice(kc * MXU_TILE, (kc + 1) * MXU_TILE)
            p = jnp.exp2(s_ref[rows, cols] - m_new).astype(BF16)
            acc = acc + jnp.dot(vt_ref[j, :, rows], p, preferred_element_type=F32)
        acc_ref[:, cols] = acc

    for n in range(n_col):
        scores(0, sa_ref, mxa_ref, n)

    def body(jj, carry):
        j = unroll * jj
        for u in range(0, unroll, 2):
            for n in range(n_col):
                scores(j + u + 1, sb_ref, mxb_ref, n)
                consume(j + u, sa_ref, mxa_ref, n)
            for n in range(n_col):
                scores(jnp.minimum(j + u + 2, n_kt - 1), sa_ref, mxa_ref, n)
                consume(j + u + 1, sb_ref, mxb_ref, n)
        return carry

    lax.fori_loop(0, n_kt // unroll, body, 0)
    acc = acc_ref[...]
    o_ref[...] = (acc[:hd] / acc[hd:hd + 1]).astype(o_ref.dtype)


def flash_attention(qkv, vt_tiles, tok0, n_b, s_n):
    n_all, n_tok, hd = qkv.shape
    n_kv, _, hv, tk = vt_tiles.shape
    n_g = C_Q_HEADS // n_kv
    tq = FLASH_TQ
    nq = s_n // tq
    kt_per = s_n // tk
    unroll = min(FLASH_UNROLL, kt_per)
    assert kt_per % unroll == 0 and unroll % 2 == 0 and tok0 % s_n == 0
    wq = n_g * tq
    k_view = qkv.reshape(n_all, n_tok // tk, tk, hd)
    seq0 = tok0 // s_n
    return pl.pallas_call(
        functools.partial(_flash_kernel, unroll=unroll),
        grid=(n_b, n_kv, nq),
        in_specs=[
            pl.BlockSpec((n_g, tq, hd), lambda b, h, i: (h, (tok0 + b * s_n) // tq + i, 0)),
            pl.BlockSpec((None, kt_per, tk, hd), lambda b, h, i: (C_K0 + h, seq0 + b, 0, 0)),
            pl.BlockSpec((None, kt_per, hv, tk), lambda b, h, i: (h, seq0 + b, 0, 0)),
        ],
        out_specs=pl.BlockSpec((None, hd, wq), lambda b, h, i: (h, 0, b * nq + i)),
        out_shape=jax.ShapeDtypeStruct((n_kv, hd, n_b * s_n * n_g), BF16),
        scratch_shapes=[pltpu.VMEM((tk, wq), F32), pltpu.VMEM((tk, wq), F32),
                        pltpu.VMEM((1, wq), F32), pltpu.VMEM((1, wq), F32),
                        pltpu.VMEM((1, wq), F32), pltpu.VMEM((hv, wq), F32)],
        compiler_params=_params(("parallel", "parallel", "arbitrary"), VMEM_LIMIT),
        name="flash_attention",
    )(qkv, k_view, vt_tiles)


def _router_kernel(x_ref, mod_ref, g_ref, wr_ref, br_ref, h_ref, idx_ref, gate_ref, cnt_ref,
                   run_ref):
    @pl.when(pl.program_id(0) == 0)
    def _():
        run_ref[...] = jnp.zeros(run_ref.shape, F32)

    h = _modulated_norm(x_ref[...], g_ref[...], mod_ref[3:4, :], mod_ref[4:5, :])
    h_ref[...] = h.astype(BF16)
    logits = jnp.dot(h, wr_ref[...], precision=HIGHEST,
                     preferred_element_type=F32) + br_ref[...]
    tm, n_e = logits.shape
    iota = lax.broadcasted_iota(jnp.int32, (tm, n_e), 1)
    vals, ids = [], []
    cur = logits
    for _ in range(TOP_K):
        m = jnp.max(cur, axis=-1, keepdims=True)
        am = jnp.min(jnp.where(cur == m, iota, n_e), axis=-1, keepdims=True)
        vals.append(m)
        ids.append(am)
        cur = jnp.where(iota == am, -jnp.inf, cur)
    es = [jnp.exp(v - vals[0]) for v in vals]
    den = es[0] + es[1] + es[2] + es[3]
    chosen = jnp.zeros((tm, n_e), F32)
    for kk in range(TOP_K):
        chosen = chosen + jnp.where(iota == ids[kk], 1.0, 0.0)
    r_i = lax.broadcasted_iota(jnp.int32, (tm, tm), 0)
    c_i = lax.broadcasted_iota(jnp.int32, (tm, tm), 1)
    tri = jnp.where(c_i < r_i, 1.0, 0.0).astype(BF16)
    before = jnp.dot(tri, chosen.astype(BF16), preferred_element_type=F32) + run_ref[...]
    run_ref[...] = run_ref[...] + jnp.sum(chosen, axis=0, keepdims=True)
    cnt_ref[...] = run_ref[...]
    lane = lax.broadcasted_iota(jnp.int32, idx_ref.shape, 1)
    idx_out = jnp.zeros(idx_ref.shape, jnp.int32)
    gate_out = jnp.zeros(gate_ref.shape, F32)
    for kk in range(TOP_K):
        rank = jnp.sum(jnp.where(iota == ids[kk], before, 0.0), axis=-1, keepdims=True)
        idx_out = jnp.where(lane == kk, ids[kk], idx_out)
        idx_out = jnp.where(lane == TOP_K + kk, rank.astype(jnp.int32), idx_out)
        gate_out = jnp.where(lane == kk, es[kk] / den, gate_out)
    idx_ref[...] = idx_out
    gate_ref[...] = gate_out


def router(x, mod, g_norm, w_router, b_router, layer, geom):
    n_tok, d = x.shape
    tm = TOKEN_TILE // 2
    return pl.pallas_call(
        _router_kernel,
        grid=(n_tok // tm,),
        in_specs=[
            pl.BlockSpec((tm, d), lambda i: (i, 0)),
            pl.BlockSpec((None, None, 6, d), lambda i: (layer, geom.seq_of_token(i * tm), 0, 0)),
            pl.BlockSpec((None, 1, d), lambda i: (layer, 0, 0)),
            pl.BlockSpec((None, d, N_EXPERTS), lambda i: (layer, 0, 0)),
            pl.BlockSpec((None, 1, N_EXPERTS), lambda i: (layer, 0, 0)),
        ],
        out_specs=[
            pl.BlockSpec((tm, d), lambda i: (i, 0)),
            pl.BlockSpec((tm, 2 * TOP_K), lambda i: (i, 0)),
            pl.BlockSpec((tm, 2 * TOP_K), lambda i: (i, 0)),
            pl.BlockSpec((1, N_EXPERTS), lambda i: (0, 0)),
        ],
        out_shape=[
            jax.ShapeDtypeStruct((n_tok, d), BF16),
            jax.ShapeDtypeStruct((n_tok, 2 * TOP_K), jnp.int32),
            jax.ShapeDtypeStruct((n_tok, 2 * TOP_K), F32),
            jax.ShapeDtypeStruct((1, N_EXPERTS), F32),
        ],
        scratch_shapes=[pltpu.VMEM((1, N_EXPERTS), F32)],
        compiler_params=_params(("arbitrary",), VMEM_LIMIT),
        name="router",
    )(x, mod, g_norm.reshape(-1, 1, d), w_router, b_router.reshape(-1, 1, N_EXPERTS))


def _moe_kernel(be_ref, nu_ref, x_ref, wgu_ref, bgu_ref, wd_ref, bd_ref, o_ref,
                wgu_s, wd_s):
    i = pl.program_id(0)
    e = be_ref[i]
    e_prev = be_ref[jnp.maximum(i - 1, 0)]

    @pl.when((i == 0) | (e != e_prev))
    def _():
        wgu_s[...] = wgu_ref[...].astype(BF16)
        wd_s[...] = wd_ref[...].astype(BF16)

    @pl.when(i < nu_ref[0])
    def _():
        d_ff = wd_s.shape[0]
        gu = jnp.dot(x_ref[...], wgu_s[...], preferred_element_type=F32) + bgu_ref[...]
        glu = jnp.minimum(gu[:, :d_ff], SWIGLU_LIMIT)
        lin = jnp.clip(gu[:, d_ff:], -SWIGLU_LIMIT, SWIGLU_LIMIT)
        act = glu / (1.0 + jnp.exp(-SWIGLU_ALPHA * glu)) * (lin + 1.0)
        y = jnp.dot(act.astype(BF16), wd_s[...], preferred_element_type=F32) + bd_ref[...]
        o_ref[...] = y.astype(o_ref.dtype)

    @pl.when(i >= nu_ref[0])
    def _():
        o_ref[...] = jnp.zeros(o_ref.shape, o_ref.dtype)


def moe_experts(xs, blk_e, n_used, w_gu, b_gu, w_down, b_down, layer):
    n_rows, d = xs.shape
    d_ff = w_down.shape[2]
    nblk = n_rows // MOE_BLOCK
    grid_spec = pltpu.PrefetchScalarGridSpec(
        num_scalar_prefetch=2,
        grid=(nblk,),
        in_specs=[
            pl.BlockSpec((MOE_BLOCK, d), lambda i, be, nu: (jnp.minimum(i, nu[0] - 1), 0)),
            pl.BlockSpec((None, None, d, 2 * d_ff), lambda i, be, nu: (layer, be[i], 0, 0)),
            pl.BlockSpec((None, None, 1, 2 * d_ff), lambda i, be, nu: (layer, be[i], 0, 0)),
            pl.BlockSpec((None, None, d_ff, d), lambda i, be, nu: (layer, be[i], 0, 0)),
            pl.BlockSpec((None, None, 1, d), lambda i, be, nu: (layer, be[i], 0, 0)),
        ],
        out_specs=pl.BlockSpec((MOE_BLOCK, d), lambda i, be, nu: (i, 0)),
        scratch_shapes=[pltpu.VMEM((d, 2 * d_ff), BF16), pltpu.VMEM((d_ff, d), BF16)],
    )
    depth, n_e = w_gu.shape[:2]
    return pl.pallas_call(
        _moe_kernel,
        grid_spec=grid_spec,
        out_shape=jax.ShapeDtypeStruct((n_rows, d), BF16),
        compiler_params=_params(("arbitrary",), VMEM_LIMIT),
        name="moe_experts",
    )(blk_e, n_used, xs, w_gu, b_gu.reshape(depth, n_e, 1, 2 * d_ff), w_down,
      b_down.reshape(depth, n_e, 1, d))


def _combine_kernel(x_ref, mod_ref, y_ref, gate_ref, o_ref):
    g = gate_ref[...]
    acc = g[:, 0:1] * y_ref[0].astype(F32)
    for kk in range(1, TOP_K):
        acc = acc + g[:, kk:kk + 1] * y_ref[kk].astype(F32)
    o_ref[...] = x_ref[...] + mod_ref[5:6, :] * acc


def moe_combine(x, mod, yk, gates, layer, geom):
    n_tok, d = x.shape
    tm = TOKEN_TILE // 2
    return pl.pallas_call(
        _combine_kernel,
        grid=(n_tok // tm,),
        in_specs=[
            pl.BlockSpec((tm, d), lambda i: (i, 0)),
            pl.BlockSpec((None, None, 6, d), lambda i: (layer, geom.seq_of_token(i * tm), 0, 0)),
            pl.BlockSpec((TOP_K, tm, d), lambda i: (0, i, 0)),
            pl.BlockSpec((tm, 2 * TOP_K), lambda i: (i, 0)),
        ],
        out_specs=pl.BlockSpec((tm, d), lambda i: (i, 0)),
        out_shape=jax.ShapeDtypeStruct(x.shape, F32),
        compiler_params=_params(("parallel",), VMEM_LIMIT),
        name="moe_combine",
    )(x, mod, yk, gates)


def moe_layer(x, mod, g_norm, w_router, b_router, w_gu, b_gu, w_down, b_down, layer, geom):
    n_tok, d = x.shape
    h, idx8, gates8, counts = router(x, mod, g_norm, w_router, b_router, layer, geom)
    idx = idx8[:, :TOP_K]
    rank = idx8[:, TOP_K:]
    counts = counts[0].astype(jnp.int32)
    pcounts = ((counts + MOE_BLOCK - 1) // MOE_BLOCK) * MOE_BLOCK
    pend = jnp.cumsum(pcounts)
    pstart = pend - pcounts
    experts = jnp.arange(N_EXPERTS, dtype=jnp.int32)
    dest = rank + jnp.sum(jnp.where(idx[:, :, None] == experts, pstart, 0), axis=-1)
    nblk = (n_tok * TOP_K) // MOE_BLOCK + N_EXPERTS
    n_rows = nblk * MOE_BLOCK
    tok_ids = jnp.broadcast_to(jnp.arange(n_tok, dtype=jnp.int32)[:, None], dest.shape)
    tok = (jnp.arange(n_rows, dtype=jnp.int32) % n_tok).at[dest.reshape(-1)].set(
        tok_ids.reshape(-1), unique_indices=True)
    n_used = (pend[-1] // MOE_BLOCK).astype(jnp.int32).reshape(1)
    blk_start = jnp.arange(nblk, dtype=jnp.int32) * MOE_BLOCK
    blk_last = jnp.minimum(blk_start, pend[-1] - 1)
    blk_e = jnp.sum(pend[None, :] <= blk_last[:, None], axis=-1, dtype=jnp.int32)
    blk_e = jnp.minimum(blk_e, N_EXPERTS - 1)
    xs = jnp.take(h, tok, axis=0, mode="clip")
    yb = moe_experts(xs, blk_e, n_used, w_gu, b_gu, w_down, b_down, layer)
    yk = jnp.take(yb, dest.T.reshape(-1), axis=0, mode="clip").reshape(TOP_K, n_tok, d)
    return moe_combine(x, mod, yk, gates8, layer, geom)


def _seq_positions(geom):
    return jnp.concatenate([jnp.arange(s, dtype=jnp.int32)
                            for _, n_b, s in geom.groups() for _ in range(n_b)])


def _qk_gains(g_q, g_k, n_q, n_k, q_scale):
    hd = g_q.shape[-1]
    return jnp.concatenate([jnp.broadcast_to(g_q * q_scale, (n_q, hd)),
                            jnp.broadcast_to(g_k, (n_k, hd))])


def _dilated_group_weight(w_in, gi):
    blk = DIL_HEADS * HEAD_DIM
    base = 3 * NA_HEADS * HEAD_DIM
    return jnp.concatenate([w_in[:, base + (a * N_DIL + gi) * blk:base + (a * N_DIL + gi + 1) * blk]
                            for a in range(3)], axis=1)


def mixer_ab_layer(x, mod, g_norm, w_in, w_out, g_qn_a, g_kn_a, rpb, g_qn_b, g_kn_b, layer, geom):
    n_tok, d = x.shape
    hd = HEAD_DIM
    w = w_in.astype(BF16)
    n_h = NA_HEADS
    qkv = norm_proj(x, mod, g_norm, w[:, :3 * n_h * hd], layer, geom, dil=1)
    qkv = qk_prep(qkv.reshape(3 * n_h, n_tok, hd), _qk_gains(g_qn_a, g_kn_a, n_h, n_h, Q_SCALE),
                  hb=n_h, n_blocks=2)
    oa = neighborhood_attention(qkv, _na_bias_table(rpb), geom)
    cos, sin = _rope_angles(_seq_positions(geom), hd)
    perm = _rotate_half_matrix(hd, 1)
    od, lse = [], []
    for gi, (window, dil) in enumerate(DIL_PATTERNS):
        assert (window // 2) // dil == DIL_SIDE
        n_h = DIL_HEADS

        def class_major(t):
            return t.reshape(n_tok // dil, dil, hd).transpose(1, 0, 2).reshape(n_tok, hd)

        qkv = norm_proj(x, mod, g_norm, _dilated_group_weight(w, gi), layer, geom, dil=dil)
        qkv = qk_prep(qkv.reshape(3 * n_h, n_tok, hd),
                      _qk_gains(g_qn_b[gi], g_kn_b[gi], n_h, n_h, Q_SCALE), hb=n_h, n_blocks=2,
                      tables=(class_major(cos), class_major(sin), perm))
        o_g, lse_g = dilated_attention(qkv, dil, geom)
        od.append(o_g.reshape(n_h, dil, n_tok // dil, hd))
        lse.append(lse_g.reshape(n_h, dil, n_tok // dil, hd))
    return outproj_ab(x, mod, oa, od, lse, w_out.astype(BF16), layer, geom)


def mixer_c_layer(x, mod, g_norm, w_in, w_out, g_qn, g_kn, layer, geom):
    n_tok, d = x.shape
    hd = HEAD_DIM
    n_heads = C_Q_HEADS + 2 * C_KV_HEADS
    qkv = norm_proj(x, mod, g_norm, w_in.astype(BF16), layer, geom, dil=1).reshape(n_heads, n_tok, hd)
    pos = _seq_positions(geom)
    half = hd // 2
    cr, sr = _rope_angles(pos // GRID_W, half)
    cc, sc = _rope_angles(pos % GRID_W, half)
    tables = (jnp.concatenate([cr, cc], axis=-1), jnp.concatenate([sr, sc], axis=-1),
              _rotate_half_matrix(half, 2))
    qkv = qk_prep(qkv, _qk_gains(g_qn, g_kn, C_Q_HEADS, C_KV_HEADS, Q_SCALE * LOG2_E),
                  hb=C_KV_HEADS, n_blocks=(C_Q_HEADS + C_KV_HEADS) // C_KV_HEADS, tables=tables)
    tk, tq = FLASH_TK, FLASH_TQ
    n_g = C_Q_HEADS // C_KV_HEADS
    vt_tiles = qkv[C_V0:].reshape(C_KV_HEADS, n_tok // tk, tk, hd).transpose(0, 1, 3, 2)
    ones = jnp.ones((C_KV_HEADS, n_tok // tk, BF16_SUBLANES, tk), BF16)
    vt_tiles = jnp.concatenate([vt_tiles, ones], axis=2)
    ot = jnp.concatenate([flash_attention(qkv, vt_tiles, tok0, n_b, s_n)
                          for tok0, n_b, s_n in geom.groups()], axis=-1)
    o = ot.reshape(C_KV_HEADS, hd, n_tok // tq, n_g, tq).transpose(2, 4, 0, 3, 1)
    o = o.reshape(n_tok, C_Q_HEADS * hd)
    return outproj(x, mod, o, w_out.astype(BF16), layer, geom)


def kernel(x_prompt, x_sample, c_prompt, c_sample, w_ada, b_ada, g_norm_mix, g_norm_ffn, w_in_ab, w_out_ab, g_qn_a, g_kn_a, rpb_a, g_qn_b, g_kn_b, w_in_c, w_out_c, g_qn_c, g_kn_c, w_router, b_router, w_gate_up, b_gate_up, w_down, b_down):
    b_p, s_p, d = x_prompt.shape
    b_d, s_d, _ = x_sample.shape
    geom = Geom(b_p, s_p, b_d, s_d)
    depth = w_ada.shape[0]
    x = jnp.concatenate([x_prompt.reshape(-1, d), x_sample.reshape(-1, d)], axis=0)
    mod = ada_modulation(jnp.concatenate([c_prompt, c_sample], axis=0), w_ada, b_ada)
    for layer in range(depth):
        i = layer // 2
        if layer % 2 == 0:
            x = mixer_ab_layer(x, mod, g_norm_mix, w_in_ab[i], w_out_ab[i], g_qn_a[i], g_kn_a[i],
                               rpb_a[i], g_qn_b[i], g_kn_b[i], layer, geom)
        else:
            x = mixer_c_layer(x, mod, g_norm_mix, w_in_c[i], w_out_c[i], g_qn_c[i], g_kn_c[i],
                              layer, geom)
        x = moe_layer(x, mod, g_norm_ffn, w_router, b_router, w_gate_up, b_gate_up, w_down,
                      b_down, layer, geom)
    y_prompt = x[:geom.n_p].reshape(b_p, s_p, d)
    y_sample = x[geom.n_p:].reshape(b_d, s_d, d)
    return (y_prompt, y_sample)
```

```python
import functools
import math
from typing import NamedTuple

import jax
import jax.numpy as jnp
from jax import lax
from jax.experimental import pallas as pl
from jax.experimental.pallas import tpu as pltpu

F32 = jnp.float32
BF16 = jnp.bfloat16
HIGHEST = lax.Precision.HIGHEST

GRID_W = 64
HEAD_DIM = 64
ROPE_THETA = 10000.0
EPS = 1e-6
NEG = -1e30
NA_HEADS = 8
NA_KH = 8
NA_KW = 16
DIL_HEADS = 8
DIL_PATTERNS = ((128, 1), (512, 4), (2048, 16))
N_DIL = len(DIL_PATTERNS)
C_Q_HEADS = 16
C_KV_HEADS = 4
N_EXPERTS = 32
TOP_K = 4
SWIGLU_LIMIT = 7.0
SWIGLU_ALPHA = 1.702
Q_SCALE = HEAD_DIM ** -0.5
LOG2_E = math.log2(math.e)
BF16_SUBLANES = 16
LANES = 128
MXU_TILE = 256

V7X_VMEM_BYTES = 64 * 1024 * 1024
VMEM_LIMIT = V7X_VMEM_BYTES - 8 * 1024 * 1024

TOKEN_TILE = 1024
PREP_TILE = 2048
NA_BLOCK = 8 * GRID_W
NA_ROWS_PER_TRIP = 2
DIL_QBLOCK = 128
DIL_SIDE = 64
DIL_CHUNK = 1024
DIL_PROBLEMS_PER_TRIP = 4
FLASH_TQ = 256
FLASH_TK = 512
FLASH_UNROLL = 8
MOE_BLOCK = 512

NA_Q0, NA_K0, NA_V0 = 0, NA_HEADS, 2 * NA_HEADS
C_K0 = C_Q_HEADS
C_V0 = C_Q_HEADS + C_KV_HEADS


class Geom(NamedTuple):
    b_p: int
    s_p: int
    b_d: int
    s_d: int

    @property
    def n_p(self):
        return self.b_p * self.s_p

    @property
    def n_tok(self):
        return self.n_p + self.b_d * self.s_d

    @property
    def n_seq(self):
        return self.b_p + self.b_d

    def groups(self):
        return ((0, self.b_p, self.s_p), (self.n_p, self.b_d, self.s_d))

    def seq_of_token(self, t0):
        return jnp.where(t0 < self.n_p, t0 // self.s_p,
                         self.b_p + (t0 - self.n_p) // self.s_d)

    def seq_span(self, t0):
        in_p = t0 < self.n_p
        start = jnp.where(in_p, (t0 // self.s_p) * self.s_p,
                          self.n_p + ((t0 - self.n_p) // self.s_d) * self.s_d)
        return start, jnp.where(in_p, self.s_p, self.s_d)


def _params(semantics, vmem=None):
    return pltpu.CompilerParams(dimension_semantics=semantics,
                                vmem_limit_bytes=vmem)


def _ada_kernel(c_ref, w_ref, b_ref, o_ref):
    c = c_ref[...]
    a = c / (1.0 + jnp.exp(-c))
    o_ref[...] = jnp.dot(a, w_ref[...], precision=HIGHEST,
                         preferred_element_type=F32) + b_ref[...]


def ada_modulation(c_all, w_ada, b_ada):
    depth, d, d6 = w_ada.shape
    n_seq = c_all.shape[0]
    rows = 8
    c_pad = jnp.zeros((rows, d), F32).at[:n_seq].set(c_all)
    out = pl.pallas_call(
        _ada_kernel,
        grid=(depth, d6 // d),
        in_specs=[
            pl.BlockSpec((rows, d), lambda l, j: (0, 0)),
            pl.BlockSpec((None, d, d), lambda l, j: (l, 0, j)),
            pl.BlockSpec((None, 1, d), lambda l, j: (l, 0, j)),
        ],
        out_specs=pl.BlockSpec((None, rows, d), lambda l, j: (l, 0, j)),
        out_shape=jax.ShapeDtypeStruct((depth, rows, d6), F32),
        compiler_params=_params(("arbitrary", "arbitrary")),
        name="ada_modulation",
    )(c_pad, w_ada, b_ada.reshape(depth, 1, d6))
    return out[:, :n_seq].reshape(depth, n_seq, 6, d)


def _modulated_norm(x, g, shift, scale):
    r = lax.rsqrt(jnp.mean(x * x, axis=-1, keepdims=True) + EPS)
    return (x * r * g) * (1.0 + scale) + shift


def _norm_proj_kernel(x_ref, mod_ref, g_ref, w_ref, o_ref, *scratch, dil):
    h = _modulated_norm(x_ref[...], g_ref[...], mod_ref[0:1, :], mod_ref[1:2, :])
    acc = jnp.dot(h.astype(BF16), w_ref[...], preferred_element_type=F32)
    tm = x_ref.shape[0]
    n_h, _, _, hd = o_ref.shape
    if dil == 1:
        for c in range(n_h):
            o_ref[c, 0] = acc[:, c * hd:(c + 1) * hd].astype(o_ref.dtype)
        return
    acc_ref, = scratch
    for s in range(n_h // 2):
        acc_ref[s] = acc[:, s * LANES:(s + 1) * LANES]
    for s in range(n_h // 2):
        for r in range(dil):
            pair = acc_ref[s, pl.ds(r, tm // dil, stride=dil), :].astype(o_ref.dtype)
            o_ref[2 * s, r] = pair[:, :hd]
            o_ref[2 * s + 1, r] = pair[:, hd:]


def norm_proj(x, mod, g_norm, w_bf16, layer, geom, dil):
    n_tok, d = x.shape
    n_out = w_bf16.shape[1]
    tm = TOKEN_TILE
    hd = HEAD_DIM
    n_h = n_out // hd
    scratch = [] if dil == 1 else [pltpu.VMEM((n_h // 2, tm, LANES), F32)]
    return pl.pallas_call(
        functools.partial(_norm_proj_kernel, dil=dil),
        grid=(n_tok // tm,),
        in_specs=[
            pl.BlockSpec((tm, d), lambda i: (i, 0)),
            pl.BlockSpec((None, None, 6, d), lambda i: (layer, geom.seq_of_token(i * tm), 0, 0)),
            pl.BlockSpec((None, 1, d), lambda i: (layer, 0, 0)),
            pl.BlockSpec((d, n_out), lambda i: (0, 0)),
        ],
        out_specs=pl.BlockSpec((n_h, dil, tm // dil, hd), lambda i: (0, 0, i, 0)),
        out_shape=jax.ShapeDtypeStruct((n_h, dil, n_tok // dil, hd), BF16),
        scratch_shapes=scratch,
        compiler_params=_params(("parallel",), VMEM_LIMIT),
        name="norm_proj",
    )(x, mod, g_norm.reshape(-1, 1, d), w_bf16)


def _prep_kernel(x_ref, g_ref, *rest, rope):
    o_ref = rest[-1]
    hb, ts, hd = x_ref.shape
    x = x_ref[...].astype(F32)
    r = lax.rsqrt(jnp.mean(x * x, axis=-1, keepdims=True) + EPS)
    y = x * r * g_ref[...]
    if rope:
        cos_ref, sin_ref, p_ref = rest[:3]
        yr = jnp.dot(y.reshape(hb * ts, hd).astype(BF16), p_ref[...],
                     preferred_element_type=F32).reshape(hb, ts, hd)
        y = y * cos_ref[...] + yr * sin_ref[...]
    o_ref[...] = y.astype(o_ref.dtype)


def qk_prep(x, gains, hb, n_blocks, tables=None):
    n_h, n_tok, hd = x.shape
    ts = PREP_TILE
    rope = tables is not None
    blk = pl.BlockSpec((hb, ts, hd), lambda i, s: (i, s, 0))
    in_specs = [blk, pl.BlockSpec((hb, 1, hd), lambda i, s: (i, 0, 0))]
    args = [x, gains.reshape(n_blocks * hb, 1, hd).astype(F32)]
    if rope:
        tab = pl.BlockSpec((ts, hd), lambda i, s: (s, 0))
        in_specs += [tab, tab, pl.BlockSpec((hd, hd), lambda i, s: (0, 0))]
        args += list(tables)
    return pl.pallas_call(
        functools.partial(_prep_kernel, rope=rope),
        grid=(n_blocks, n_tok // ts),
        in_specs=in_specs,
        out_specs=blk,
        out_shape=jax.ShapeDtypeStruct(x.shape, BF16),
        input_output_aliases={0: 0},
        compiler_params=_params(("parallel", "parallel")),
        name="qk_prep",
    )(*args)


def _rope_angles(pos, dim):
    inv = ROPE_THETA ** (-jnp.arange(0, dim, 2, dtype=F32) / dim)
    ang = pos.astype(F32)[:, None] * inv[None, :]
    ang = jnp.concatenate([ang, ang], axis=-1)
    return jnp.cos(ang), jnp.sin(ang)


def _rotate_half_matrix(dim, n_blocks):
    half = dim // 2
    i = jnp.arange(dim * n_blocks)
    blk, w = i // dim, i % dim
    src = blk * dim + jnp.where(w < half, w + half, w - half)
    sign = jnp.where(w < half, -1.0, 1.0)
    p = jnp.zeros((dim * n_blocks, dim * n_blocks), F32).at[src, i].set(sign)
    return p.astype(BF16)


def _na_bias_table(rpb):
    n_h = rpb.shape[0]
    qc = jnp.arange(GRID_W)[:, None]
    kc = jnp.arange(GRID_W)[None, :]
    c0 = jnp.clip(qc - NA_KW // 2, 0, GRID_W - NA_KW)
    ok = (kc >= c0) & (kc < c0 + NA_KW)
    n_dr = rpb.shape[1]
    period = 2 * GRID_W
    ext = jnp.concatenate([rpb[..., NA_KW - 1:],
                           jnp.zeros((n_h, n_dr, period - (2 * NA_KW - 1)), rpb.dtype),
                           rpb[..., :NA_KW - 1]], axis=-1)
    toep = jnp.tile(ext, (1, 1, GRID_W))[..., :GRID_W * (period - 1)]
    toep = toep.reshape(n_h, n_dr, GRID_W, period - 1)[..., :GRID_W]
    toep = jnp.where(ok[None, None], toep, NEG)
    bias = jnp.stack([toep[:, NA_KH - 1 - d0:2 * NA_KH - 1 - d0] for d0 in range(NA_KH)])
    bias = bias.transpose(0, 1, 3, 2, 4).reshape(NA_KH, n_h, GRID_W, NA_KH * GRID_W)
    return bias.astype(BF16)


def _na_kernel(q_ref, kp_ref, kc_ref, kn_ref, vp_ref, vc_ref, vn_ref, tbl_ref,
               o_ref, ks_ref, vs_ref, *, geom):
    blk = NA_BLOCK
    j = pl.program_id(0)
    seq_start, seq_len = geom.seq_span(j * blk)
    rows = seq_len // GRID_W
    jl = j - seq_start // blk
    ks_ref[:, 0:blk, :] = kp_ref[...]
    ks_ref[:, blk:2 * blk, :] = kc_ref[...]
    ks_ref[:, 2 * blk:3 * blk, :] = kn_ref[...]
    vs_ref[:, 0:blk, :] = vp_ref[...]
    vs_ref[:, blk:2 * blk, :] = vc_ref[...]
    vs_ref[:, 2 * blk:3 * blk, :] = vn_ref[...]
    n_keys = NA_KH * GRID_W

    def body(ii, carry):
        probs = []
        for u in range(NA_ROWS_PER_TRIP):
            i = ii * NA_ROWS_PER_TRIP + u
            r = jl * NA_KH + i
            r0 = jnp.clip(r - NA_KH // 2, 0, rows - NA_KH)
            off = pl.multiple_of((r0 - (jl - 1) * NA_KH) * GRID_W, GRID_W)
            d0 = r - r0
            qoff = pl.multiple_of(i * GRID_W, GRID_W)
            for h in range(NA_HEADS):
                q = q_ref[h, pl.ds(qoff, GRID_W), :]
                k = ks_ref[h, pl.ds(off, n_keys), :]
                s = lax.dot_general(q, k, (((1,), (1,)), ((), ())), preferred_element_type=F32)
                probs.append((h, off, qoff, s + tbl_ref[d0, h].astype(F32)))
        soft = []
        for h, off, qoff, s in probs:
            m = jnp.max(s, axis=-1, keepdims=True)
            p = jnp.exp(s - m)
            soft.append((h, off, qoff, p.astype(BF16), jnp.sum(p, axis=-1, keepdims=True)))
        for h, off, qoff, p, l in soft:
            v = vs_ref[h, pl.ds(off, n_keys), :]
            o = jnp.dot(p, v, preferred_element_type=F32) / l
            o_ref[h, pl.ds(qoff, GRID_W), :] = o.astype(o_ref.dtype)
        return carry

    lax.fori_loop(0, NA_KH // NA_ROWS_PER_TRIP, body, 0)


def neighborhood_attention(qkv, tbl, geom):
    _, n_tok, hd = qkv.shape
    n_h = NA_HEADS
    blk = NA_BLOCK
    nb = n_tok // blk

    def spec(head0, shift):
        hb = head0 // n_h
        return pl.BlockSpec((n_h, blk, hd), lambda j: (hb, jnp.clip(j + shift, 0, nb - 1), 0))

    return pl.pallas_call(
        functools.partial(_na_kernel, geom=geom),
        grid=(nb,),
        in_specs=[spec(NA_Q0, 0), spec(NA_K0, -1), spec(NA_K0, 0), spec(NA_K0, 1),
                  spec(NA_V0, -1), spec(NA_V0, 0), spec(NA_V0, 1),
                  pl.BlockSpec(tbl.shape, lambda j: (0, 0, 0, 0))],
        out_specs=pl.BlockSpec((n_h, blk, hd), lambda j: (0, j, 0)),
        out_shape=jax.ShapeDtypeStruct((n_h, n_tok, hd), BF16),
        scratch_shapes=[pltpu.VMEM((n_h, 3 * blk, hd), BF16),
                        pltpu.VMEM((n_h, 3 * blk, hd), BF16)],
        compiler_params=_params(("parallel",), VMEM_LIMIT),
        name="neighborhood_attention",
    )(qkv, qkv, qkv, qkv, qkv, qkv, qkv, tbl)


def _dil_kernel(q_ref, kp_ref, kc_ref, kn_ref, vp_ref, vc_ref, vn_ref,
                o_ref, lse_ref, ks_ref, vs_ref, *, geom, dil):
    c = pl.program_id(0)
    n_h, chunk, hd = q_ref.shape
    side = DIL_SIDE
    qb = DIL_QBLOCK
    kb = qb + 2 * side
    ks_ref[:, 0:side, :] = kp_ref[...]
    ks_ref[:, side:side + chunk, :] = kc_ref[...]
    ks_ref[:, side + chunk:, :] = kn_ref[...]
    vs_ref[:, 0:side, :] = vp_ref[...]
    vs_ref[:, side:side + chunk, :] = vc_ref[...]
    vs_ref[:, side + chunk:, :] = vn_ref[...]
    plane_rows = geom.n_tok // dil
    t0 = ((c * chunk) % plane_rows) * dil
    seq_start, seq_len = geom.seq_span(t0)
    row0 = (t0 - seq_start) // dil
    cls_rows = seq_len // dil
    jq = lax.broadcasted_iota(jnp.int32, (qb, kb), 0)
    jk = lax.broadcasted_iota(jnp.int32, (qb, kb), 1) - side
    near_bias = jnp.where(jnp.abs(jk - jq) <= side, 0.0, NEG)
    before_bias = jnp.where(jk < 0, NEG, 0.0)
    after_bias = jnp.where(jk >= qb, NEG, 0.0)
    n_trip = min(DIL_PROBLEMS_PER_TRIP, chunk // qb)
    trips_per_head = chunk // (qb * n_trip)
    assert chunk % (qb * n_trip) == 0

    def body(it, carry):
        h = it // trips_per_head
        ii = it % trips_per_head
        probs = []
        for u in range(n_trip):
            i = ii * n_trip + u
            qoff = pl.multiple_of(i * qb, qb)
            p0 = row0 + i * qb
            bias = near_bias + jnp.where(p0 == 0, before_bias, 0.0)
            bias = bias + jnp.where(p0 + qb == cls_rows, after_bias, 0.0)
            q = q_ref[h, pl.ds(qoff, qb), :]
            k = ks_ref[h, pl.ds(qoff, kb), :]
            s = lax.dot_general(q, k, (((1,), (1,)), ((), ())), preferred_element_type=F32)
            probs.append((qoff, s + bias))
        soft = []
        for qoff, s in probs:
            m = jnp.max(s, axis=-1, keepdims=True)
            p = jnp.exp(s - m)
            soft.append((qoff, p.astype(BF16), m, jnp.sum(p, axis=-1, keepdims=True)))
        for qoff, p, m, l in soft:
            v = vs_ref[h, pl.ds(qoff, kb), :]
            o = jnp.dot(p, v, preferred_element_type=F32) / l
            o_ref[h, pl.ds(qoff, qb), :] = o.astype(o_ref.dtype)
            lse_ref[h, pl.ds(qoff, qb), :] = jnp.broadcast_to(m + jnp.log(l), (qb, hd))
        return carry

    lax.fori_loop(0, n_h * trips_per_head, body, 0)


def dilated_attention(qkv, dil, geom):
    _, n_tok, hd = qkv.shape
    n_h = DIL_HEADS
    chunk = min(DIL_CHUNK, min(geom.s_p, geom.s_d) // dil)
    assert chunk % DIL_QBLOCK == 0
    per = chunk // DIL_SIDE
    n_side = n_tok // DIL_SIDE

    def cur(hb):
        return pl.BlockSpec((n_h, chunk, hd), lambda c: (hb, c, 0))

    def prev(hb):
        return pl.BlockSpec((n_h, DIL_SIDE, hd), lambda c: (hb, jnp.maximum(c * per - 1, 0), 0))

    def nxt(hb):
        return pl.BlockSpec((n_h, DIL_SIDE, hd),
                            lambda c: (hb, jnp.minimum((c + 1) * per, n_side - 1), 0))

    return pl.pallas_call(
        functools.partial(_dil_kernel, geom=geom, dil=dil),
        grid=(n_tok // chunk,),
        in_specs=[cur(0), prev(1), cur(1), nxt(1), prev(2), cur(2), nxt(2)],
        out_specs=[cur(0), cur(0)],
        out_shape=[jax.ShapeDtypeStruct((n_h, n_tok, hd), BF16),
                   jax.ShapeDtypeStruct((n_h, n_tok, hd), F32)],
        scratch_shapes=[pltpu.VMEM((n_h, chunk + 2 * DIL_SIDE, hd), BF16),
                        pltpu.VMEM((n_h, chunk + 2 * DIL_SIDE, hd), BF16)],
        compiler_params=_params(("parallel",), VMEM_LIMIT),
        name="dilated_attention",
    )(qkv, qkv, qkv, qkv, qkv, qkv, qkv)


def _to_natural(src_ref, dst_ref):
    n_h, dil, n, _ = src_ref.shape
    for h in range(0, n_h, 2):
        for r in range(dil):
            pair = jnp.concatenate([src_ref[h, r].astype(F32), src_ref[h + 1, r].astype(F32)],
                                   axis=-1)
            if dil == 1:
                dst_ref[h // 2] = pair
            else:
                dst_ref[h // 2, pl.ds(r, n, stride=dil), :] = pair


def _outproj_ab_kernel(x_ref, mod_ref, oa_ref, od0_ref, od1_ref, od2_ref,
                       l0_ref, l1_ref, l2_ref, w_ref, o_ref, *scratch):
    so = scratch[:N_DIL]
    sl = scratch[N_DIL:]
    for src, dst in zip((od0_ref, od1_ref, od2_ref, l0_ref, l1_ref, l2_ref), so + sl):
        _to_natural(src, dst)
    n_h = oa_ref.shape[0]
    slabs = [jnp.concatenate([oa_ref[h], oa_ref[h + 1]], axis=-1) for h in range(0, n_h, 2)]
    for s in range(n_h // 2):
        l0, l1, l2 = sl[0][s], sl[1][s], sl[2][s]
        m = jnp.maximum(jnp.maximum(l0, l1), l2)
        e0, e1, e2 = jnp.exp(l0 - m), jnp.exp(l1 - m), jnp.exp(l2 - m)
        ob = (e0 * so[0][s] + e1 * so[1][s] + e2 * so[2][s]) / (e0 + e1 + e2)
        slabs.append(ob.astype(BF16))
    cat = jnp.concatenate(slabs, axis=-1)
    res = jnp.dot(cat, w_ref[...], preferred_element_type=F32)
    o_ref[...] = x_ref[...] + mod_ref[2:3, :] * res


def outproj_ab(x, mod, oa, od, lse, w_bf16, layer, geom):
    n_tok, d = x.shape
    n_h, _, hd = oa.shape
    tm = TOKEN_TILE // 2

    def cls(a):
        dil = a.shape[1]
        return pl.BlockSpec((n_h, dil, tm // dil, hd), lambda i: (0, 0, i, 0))

    return pl.pallas_call(
        _outproj_ab_kernel,
        grid=(n_tok // tm,),
        in_specs=[
            pl.BlockSpec((tm, d), lambda i: (i, 0)),
            pl.BlockSpec((None, None, 6, d), lambda i: (layer, geom.seq_of_token(i * tm), 0, 0)),
            pl.BlockSpec((n_h, tm, hd), lambda i: (0, i, 0)),
            *[cls(a) for a in od], *[cls(a) for a in lse],
            pl.BlockSpec(w_bf16.shape, lambda i: (0, 0)),
        ],
        out_specs=pl.BlockSpec((tm, d), lambda i: (i, 0)),
        out_shape=jax.ShapeDtypeStruct(x.shape, F32),
        scratch_shapes=[pltpu.VMEM((n_h // 2, tm, LANES), F32) for _ in range(2 * N_DIL)],
        compiler_params=_params(("parallel",), VMEM_LIMIT),
        name="outproj_ab",
    )(x, mod, oa, *od, *lse, w_bf16)


def _outproj_kernel(x_ref, mod_ref, o_in_ref, w_ref, o_ref):
    res = jnp.dot(o_in_ref[...], w_ref[...], preferred_element_type=F32)
    o_ref[...] = x_ref[...] + mod_ref[2:3, :] * res


def outproj(x, mod, o_in, w_bf16, layer, geom):
    n_tok, d = x.shape
    tm = TOKEN_TILE
    return pl.pallas_call(
        _outproj_kernel,
        grid=(n_tok // tm,),
        in_specs=[
            pl.BlockSpec((tm, d), lambda i: (i, 0)),
            pl.BlockSpec((None, None, 6, d), lambda i: (layer, geom.seq_of_token(i * tm), 0, 0)),
            pl.BlockSpec((tm, o_in.shape[1]), lambda i: (i, 0)),
            pl.BlockSpec(w_bf16.shape, lambda i: (0, 0)),
        ],
        out_specs=pl.BlockSpec((tm, d), lambda i: (i, 0)),
        out_shape=jax.ShapeDtypeStruct(x.shape, F32),
        compiler_params=_params(("parallel",), VMEM_LIMIT),
        name="outproj",
    )(x, mod, o_in, w_bf16)


def _flash_kernel(q_ref, k_ref, vt_ref, _o_in_ref, o_ref, sa_ref, sb_ref, mxa_ref, mxb_ref,
                  m_ref, acc_ref, *, unroll):
    n_g, tq, hd = q_ref.shape
    n_kt = k_ref.shape[0]
    m_ref[...] = jnp.full(m_ref.shape, -jnp.inf, F32)
    acc_ref[...] = jnp.zeros(acc_ref.shape, F32)
    n_col = (n_g * tq) // MXU_TILE

    def scores(j, s_ref, mx_ref, n):
        cols = slice(n * MXU_TILE, (n + 1) * MXU_TILE)
        q_n = q_ref[...].reshape(n_g * tq, hd)[n * MXU_TILE:(n + 1) * MXU_TILE]
        s = lax.dot_general(k_ref[j], q_n, (((1,), (1,)), ((), ())),
                            preferred_element_type=F32)
        s_ref[:, cols] = s
        mx_ref[:, cols] = jnp.max(s, axis=0, keepdims=True)

    def consume(j, s_ref, mx_ref, n):
        cols = slice(n * MXU_TILE, (n + 1) * MXU_TILE)
        m_old = m_ref[:, cols]
        m_new = jnp.maximum(m_old, mx_ref[:, cols])
        alpha = jnp.exp2(m_old - m_new)
        m_ref[:, cols] = m_new
        tk = s_ref.shape[0]
        acc = alpha * acc_ref[:, cols]
        for kc in range(tk // MXU_TILE):
            rows = slice(kc * MXU_TILE, (kc + 1) * MXU_TILE)
            p = jnp.exp2(s_ref[rows, cols] - m_new).astype(BF16)
            acc = acc + jnp.dot(vt_ref[j, :, rows], p, preferred_element_type=F32)
        acc_ref[:, cols] = acc

    for n in range(n_col):
        scores(0, sa_ref, mxa_ref, n)

    def body(jj, carry):
        j = unroll * jj
        for u in range(0, unroll, 2):
            for n in range(n_col):
                scores(j + u + 1, sb_ref, mxb_ref, n)
                consume(j + u, sa_ref, mxa_ref, n)
            for n in range(n_col):
                scores(jnp.minimum(j + u + 2, n_kt - 1), sa_ref, mxa_ref, n)
                consume(j + u + 1, sb_ref, mxb_ref, n)
        return carry

    lax.fori_loop(0, n_kt // unroll, body, 0)
    acc = acc_ref[...]
    o = acc[:hd] / acc[hd:hd + 1]
    pad = jnp.zeros((LANES - hd, tq), F32)
    for g in range(n_g):
        t = jnp.concatenate([o[:, g * tq:(g + 1) * tq], pad], axis=0).T
        o_ref[:, g * hd:(g + 1) * hd] = t[:, :hd].astype(o_ref.dtype)


def flash_attention(qkv, vt_tiles, o_buf, tok0, n_b, s_n):
    n_all, n_tok, hd = qkv.shape
    n_kv, _, hv, tk = vt_tiles.shape
    n_g = C_Q_HEADS // n_kv
    tq = FLASH_TQ
    nq = s_n // tq
    kt_per = s_n // tk
    unroll = min(FLASH_UNROLL, kt_per)
    assert kt_per % unroll == 0 and unroll % 2 == 0 and tok0 % s_n == 0
    wq = n_g * tq
    k_view = qkv.reshape(n_all, n_tok // tk, tk, hd)
    seq0 = tok0 // s_n
    return pl.pallas_call(
        functools.partial(_flash_kernel, unroll=unroll),
        grid=(n_b, n_kv, nq),
        in_specs=[
            pl.BlockSpec((n_g, tq, hd), lambda b, h, i: (h, (tok0 + b * s_n) // tq + i, 0)),
            pl.BlockSpec((None, kt_per, tk, hd), lambda b, h, i: (C_K0 + h, seq0 + b, 0, 0)),
            pl.BlockSpec((None, kt_per, hv, tk), lambda b, h, i: (h, seq0 + b, 0, 0)),
            pl.BlockSpec(memory_space=pl.ANY),
        ],
        out_specs=pl.BlockSpec((tq, n_g * hd), lambda b, h, i: ((tok0 + b * s_n) // tq + i, h)),
        out_shape=jax.ShapeDtypeStruct(o_buf.shape, o_buf.dtype),
        input_output_aliases={3: 0},
        scratch_shapes=[pltpu.VMEM((tk, wq), F32), pltpu.VMEM((tk, wq), F32),
                        pltpu.VMEM((1, wq), F32), pltpu.VMEM((1, wq), F32),
                        pltpu.VMEM((1, wq), F32), pltpu.VMEM((hv, wq), F32)],
        compiler_params=_params(("parallel", "parallel", "arbitrary"), VMEM_LIMIT),
        name="flash_attention",
    )(qkv, k_view, vt_tiles, o_buf)


def _router_kernel(x_ref, mod_ref, g_ref, wr_ref, br_ref, h_ref, idx_ref, gate_ref, cnt_ref,
                   run_ref):
    @pl.when(pl.program_id(0) == 0)
    def _():
        run_ref[...] = jnp.zeros(run_ref.shape, F32)

    h = _modulated_norm(x_ref[...], g_ref[...], mod_ref[3:4, :], mod_ref[4:5, :])
    h_ref[...] = h.astype(BF16)
    logits = lax.dot_general(wr_ref[...], h, (((1,), (1,)), ((), ())), precision=HIGHEST,
                             preferred_element_type=F32) + br_ref[...]
    n_e, tm = logits.shape
    iota = lax.broadcasted_iota(jnp.int32, (n_e, tm), 0)
    vals, ids = [], []
    cur = logits
    for _ in range(TOP_K):
        m = jnp.max(cur, axis=0, keepdims=True)
        am = jnp.min(jnp.where(cur == m, iota, n_e), axis=0, keepdims=True)
        vals.append(m)
        ids.append(am)
        cur = jnp.where(iota == am, -jnp.inf, cur)
    es = [jnp.exp(v - vals[0]) for v in vals]
    den = es[0] + es[1] + es[2] + es[3]
    chosen = jnp.zeros((n_e, tm), F32)
    for kk in range(TOP_K):
        chosen = chosen + jnp.where(iota == ids[kk], 1.0, 0.0)
    earlier = lax.broadcasted_iota(jnp.int32, (tm, tm), 0)
    token = lax.broadcasted_iota(jnp.int32, (tm, tm), 1)
    tri = jnp.where(earlier < token, 1.0, 0.0).astype(BF16)
    before = jnp.dot(chosen.astype(BF16), tri, preferred_element_type=F32) + run_ref[...]
    run_ref[...] = run_ref[...] + jnp.sum(chosen, axis=1, keepdims=True)
    cnt_ref[...] = run_ref[...]
    row = lax.broadcasted_iota(jnp.int32, idx_ref.shape, 0)
    idx_out = jnp.zeros(idx_ref.shape, jnp.int32)
    gate_out = jnp.zeros(gate_ref.shape, F32)
    for kk in range(TOP_K):
        rank = jnp.sum(jnp.where(iota == ids[kk], before, 0.0), axis=0, keepdims=True)
        idx_out = jnp.where(row == kk, ids[kk], idx_out)
        idx_out = jnp.where(row == TOP_K + kk, rank.astype(jnp.int32), idx_out)
        gate_out = jnp.where(row == kk, es[kk] / den, gate_out)
    idx_ref[...] = idx_out
    gate_ref[...] = gate_out


def router(x, mod, g_norm, w_router, b_router, layer, geom):
    n_tok, d = x.shape
    tm = TOKEN_TILE // 2
    return pl.pallas_call(
        _router_kernel,
        grid=(n_tok // tm,),
        in_specs=[
            pl.BlockSpec((tm, d), lambda i: (i, 0)),
            pl.BlockSpec((None, None, 6, d), lambda i: (layer, geom.seq_of_token(i * tm), 0, 0)),
            pl.BlockSpec((None, 1, d), lambda i: (layer, 0, 0)),
            pl.BlockSpec((None, N_EXPERTS, d), lambda i: (layer, 0, 0)),
            pl.BlockSpec((None, N_EXPERTS, 1), lambda i: (layer, 0, 0)),
        ],
        out_specs=[
            pl.BlockSpec((tm, d), lambda i: (i, 0)),
            pl.BlockSpec((2 * TOP_K, tm), lambda i: (0, i)),
            pl.BlockSpec((2 * TOP_K, tm), lambda i: (0, i)),
            pl.BlockSpec((N_EXPERTS, 1), lambda i: (0, 0)),
        ],
        out_shape=[
            jax.ShapeDtypeStruct((n_tok, d), BF16),
            jax.ShapeDtypeStruct((2 * TOP_K, n_tok), jnp.int32),
            jax.ShapeDtypeStruct((2 * TOP_K, n_tok), F32),
            jax.ShapeDtypeStruct((N_EXPERTS, 1), F32),
        ],
        scratch_shapes=[pltpu.VMEM((N_EXPERTS, 1), F32)],
        compiler_params=_params(("arbitrary",), VMEM_LIMIT),
        name="router",
    )(x, mod, g_norm.reshape(-1, 1, d), w_router.transpose(0, 2, 1),
      b_router.reshape(-1, N_EXPERTS, 1))


def _moe_kernel(be_ref, nu_ref, x_ref, wgu_ref, bgu_ref, wd_ref, bd_ref, o_ref,
                wgu_s, wd_s):
    i = pl.program_id(0)
    e = be_ref[i]
    e_prev = be_ref[jnp.maximum(i - 1, 0)]

    @pl.when((i == 0) | (e != e_prev))
    def _():
        wgu_s[...] = wgu_ref[...].astype(BF16)
        wd_s[...] = wd_ref[...].astype(BF16)

    @pl.when(i < nu_ref[0])
    def _():
        d_ff = wd_s.shape[0]
        gu = jnp.dot(x_ref[...], wgu_s[...], preferred_element_type=F32) + bgu_ref[...]
        glu = jnp.minimum(gu[:, :d_ff], SWIGLU_LIMIT)
        lin = jnp.clip(gu[:, d_ff:], -SWIGLU_LIMIT, SWIGLU_LIMIT)
        act = glu / (1.0 + jnp.exp(-SWIGLU_ALPHA * glu)) * (lin + 1.0)
        y = jnp.dot(act.astype(BF16), wd_s[...], preferred_element_type=F32) + bd_ref[...]
        o_ref[...] = y.astype(o_ref.dtype)

    @pl.when(i >= nu_ref[0])
    def _():
        o_ref[...] = jnp.zeros(o_ref.shape, o_ref.dtype)


def moe_experts(xs, blk_e, n_used, w_gu, b_gu, w_down, b_down, layer):
    n_rows, d = xs.shape
    d_ff = w_down.shape[2]
    nblk = n_rows // MOE_BLOCK
    grid_spec = pltpu.PrefetchScalarGridSpec(
        num_scalar_prefetch=2,
        grid=(nblk,),
        in_specs=[
            pl.BlockSpec((MOE_BLOCK, d), lambda i, be, nu: (jnp.minimum(i, nu[0] - 1), 0)),
            pl.BlockSpec((None, None, d, 2 * d_ff), lambda i, be, nu: (layer, be[i], 0, 0)),
            pl.BlockSpec((None, None, 1, 2 * d_ff), lambda i, be, nu: (layer, be[i], 0, 0)),
            pl.BlockSpec((None, None, d_ff, d), lambda i, be, nu: (layer, be[i], 0, 0)),
            pl.BlockSpec((None, None, 1, d), lambda i, be, nu: (layer, be[i], 0, 0)),
        ],
        out_specs=pl.BlockSpec((MOE_BLOCK, d), lambda i, be, nu: (i, 0)),
        scratch_shapes=[pltpu.VMEM((d, 2 * d_ff), BF16), pltpu.VMEM((d_ff, d), BF16)],
    )
    depth, n_e = w_gu.shape[:2]
    return pl.pallas_call(
        _moe_kernel,
        grid_spec=grid_spec,
        out_shape=jax.ShapeDtypeStruct((n_rows, d), BF16),
        compiler_params=_params(("arbitrary",), VMEM_LIMIT),
        name="moe_experts",
    )(blk_e, n_used, xs, w_gu, b_gu.reshape(depth, n_e, 1, 2 * d_ff), w_down,
      b_down.reshape(depth, n_e, 1, d))


def _combine_kernel(x_ref, mod_ref, y_ref, gate_ref, o_ref):
    g = gate_ref[...]
    acc = g[:, 0:1] * y_ref[0].astype(F32)
    for kk in range(1, TOP_K):
        acc = acc + g[:, kk:kk + 1] * y_ref[kk].astype(F32)
    o_ref[...] = x_ref[...] + mod_ref[5:6, :] * acc


def moe_combine(x, mod, yk, gates, layer, geom):
    n_tok, d = x.shape
    tm = TOKEN_TILE // 2
    return pl.pallas_call(
        _combine_kernel,
        grid=(n_tok // tm,),
        in_specs=[
            pl.BlockSpec((tm, d), lambda i: (i, 0)),
            pl.BlockSpec((None, None, 6, d), lambda i: (layer, geom.seq_of_token(i * tm), 0, 0)),
            pl.BlockSpec((TOP_K, tm, d), lambda i: (0, i, 0)),
            pl.BlockSpec((tm, 2 * TOP_K), lambda i: (i, 0)),
        ],
        out_specs=pl.BlockSpec((tm, d), lambda i: (i, 0)),
        out_shape=jax.ShapeDtypeStruct(x.shape, F32),
        compiler_params=_params(("parallel",), VMEM_LIMIT),
        name="moe_combine",
    )(x, mod, yk, gates)


def moe_layer(x, mod, g_norm, w_router, b_router, w_gu, b_gu, w_down, b_down, layer, geom):
    n_tok, d = x.shape
    h, idx8, gates8, counts = router(x, mod, g_norm, w_router, b_router, layer, geom)
    idx = idx8[:TOP_K].T
    rank = idx8[TOP_K:].T
    gates8 = gates8.T
    counts = counts[:, 0].astype(jnp.int32)
    pcounts = ((counts + MOE_BLOCK - 1) // MOE_BLOCK) * MOE_BLOCK
    pend = jnp.cumsum(pcounts)
    pstart = pend - pcounts
    experts = jnp.arange(N_EXPERTS, dtype=jnp.int32)
    dest = rank + jnp.sum(jnp.where(idx[:, :, None] == experts, pstart, 0), axis=-1)
    nblk = (n_tok * TOP_K) // MOE_BLOCK + N_EXPERTS
    n_rows = nblk * MOE_BLOCK
    tok_ids = jnp.broadcast_to(jnp.arange(n_tok, dtype=jnp.int32)[:, None], dest.shape)
    tok = (jnp.arange(n_rows, dtype=jnp.int32) % n_tok).at[dest.reshape(-1)].set(
        tok_ids.reshape(-1), unique_indices=True)
    n_used = (pend[-1] // MOE_BLOCK).astype(jnp.int32).reshape(1)
    blk_start = jnp.arange(nblk, dtype=jnp.int32) * MOE_BLOCK
    blk_last = jnp.minimum(blk_start, pend[-1] - 1)
    blk_e = jnp.sum(pend[None, :] <= blk_last[:, None], axis=-1, dtype=jnp.int32)
    blk_e = jnp.minimum(blk_e, N_EXPERTS - 1)
    xs = jnp.take(h, tok, axis=0, mode="clip")
    yb = moe_experts(xs, blk_e, n_used, w_gu, b_gu, w_down, b_down, layer)
    yk = jnp.take(yb, dest.T.reshape(-1), axis=0, mode="clip").reshape(TOP_K, n_tok, d)
    return moe_combine(x, mod, yk, gates8, layer, geom)


def _seq_positions(geom):
    return jnp.concatenate([jnp.arange(s, dtype=jnp.int32)
                            for _, n_b, s in geom.groups() for _ in range(n_b)])


def _qk_gains(g_q, g_k, n_q, n_k, q_scale):
    hd = g_q.shape[-1]
    return jnp.concatenate([jnp.broadcast_to(g_q * q_scale, (n_q, hd)),
                            jnp.broadcast_to(g_k, (n_k, hd))])


def _dilated_group_weight(w_in, gi):
    blk = DIL_HEADS * HEAD_DIM
    base = 3 * NA_HEADS * HEAD_DIM
    return jnp.concatenate([w_in[:, base + (a * N_DIL + gi) * blk:base + (a * N_DIL + gi + 1) * blk]
                            for a in range(3)], axis=1)


def mixer_ab_layer(x, mod, g_norm, w_in, w_out, g_qn_a, g_kn_a, rpb, g_qn_b, g_kn_b, layer, geom):
    n_tok, d = x.shape
    hd = HEAD_DIM
    w = w_in.astype(BF16)
    n_h = NA_HEADS
    qkv = norm_proj(x, mod, g_norm, w[:, :3 * n_h * hd], layer, geom, dil=1)
    qkv = qk_prep(qkv.reshape(3 * n_h, n_tok, hd), _qk_gains(g_qn_a, g_kn_a, n_h, n_h, Q_SCALE),
                  hb=n_h, n_blocks=2)
    oa = neighborhood_attention(qkv, _na_bias_table(rpb), geom)
    cos, sin = _rope_angles(_seq_positions(geom), hd)
    perm = _rotate_half_matrix(hd, 1)
    od, lse = [], []
    for gi, (window, dil) in enumerate(DIL_PATTERNS):
        assert (window // 2) // dil == DIL_SIDE
        n_h = DIL_HEADS

        def class_major(t):
            return t.reshape(n_tok // dil, dil, hd).transpose(1, 0, 2).reshape(n_tok, hd)

        qkv = norm_proj(x, mod, g_norm, _dilated_group_weight(w, gi), layer, geom, dil=dil)
        qkv = qk_prep(qkv.reshape(3 * n_h, n_tok, hd),
                      _qk_gains(g_qn_b[gi], g_kn_b[gi], n_h, n_h, Q_SCALE), hb=n_h, n_blocks=2,
                      tables=(class_major(cos), class_major(sin), perm))
        o_g, lse_g = dilated_attention(qkv, dil, geom)
        od.append(o_g.reshape(n_h, dil, n_tok // dil, hd))
        lse.append(lse_g.reshape(n_h, dil, n_tok // dil, hd))
    return outproj_ab(x, mod, oa, od, lse, w_out.astype(BF16), layer, geom)


def mixer_c_layer(x, mod, g_norm, w_in, w_out, g_qn, g_kn, layer, geom):
    n_tok, d = x.shape
    hd = HEAD_DIM
    n_heads = C_Q_HEADS + 2 * C_KV_HEADS
    qkv = norm_proj(x, mod, g_norm, w_in.astype(BF16), layer, geom, dil=1).reshape(n_heads, n_tok, hd)
    pos = _seq_positions(geom)
    half = hd // 2
    cr, sr = _rope_angles(pos // GRID_W, half)
    cc, sc = _rope_angles(pos % GRID_W, half)
    tables = (jnp.concatenate([cr, cc], axis=-1), jnp.concatenate([sr, sc], axis=-1),
              _rotate_half_matrix(half, 2))
    qkv = qk_prep(qkv, _qk_gains(g_qn, g_kn, C_Q_HEADS, C_KV_HEADS, Q_SCALE * LOG2_E),
                  hb=C_KV_HEADS, n_blocks=(C_Q_HEADS + C_KV_HEADS) // C_KV_HEADS, tables=tables)
    tk = FLASH_TK
    vt_tiles = qkv[C_V0:].reshape(C_KV_HEADS, n_tok // tk, tk, hd).transpose(0, 1, 3, 2)
    ones = jnp.ones((C_KV_HEADS, n_tok // tk, BF16_SUBLANES, tk), BF16)
    vt_tiles = jnp.concatenate([vt_tiles, ones], axis=2)
    o = jnp.zeros((n_tok, C_Q_HEADS * hd), BF16)
    for tok0, n_b, s_n in geom.groups():
        o = flash_attention(qkv, vt_tiles, o, tok0, n_b, s_n)
    return outproj(x, mod, o, w_out.astype(BF16), layer, geom)


def kernel(x_prompt, x_sample, c_prompt, c_sample, w_ada, b_ada, g_norm_mix, g_norm_ffn, w_in_ab, w_out_ab, g_qn_a, g_kn_a, rpb_a, g_qn_b, g_kn_b, w_in_c, w_out_c, g_qn_c, g_kn_c, w_router, b_router, w_gate_up, b_gate_up, w_down, b_down):
    b_p, s_p, d = x_prompt.shape
    b_d, s_d, _ = x_sample.shape
    geom = Geom(b_p, s_p, b_d, s_d)
    depth = w_ada.shape[0]
    x = jnp.concatenate([x_prompt.reshape(-1, d), x_sample.reshape(-1, d)], axis=0)
    mod = ada_modulation(jnp.concatenate([c_prompt, c_sample], axis=0), w_ada, b_ada)
    for layer in range(depth):
        i = layer // 2
        if layer % 2 == 0:
            x = mixer_ab_layer(x, mod, g_norm_mix, w_in_ab[i], w_out_ab[i], g_qn_a[i], g_kn_a[i],
                               rpb_a[i], g_qn_b[i], g_kn_b[i], layer, geom)
        else:
            x = mixer_c_layer(x, mod, g_norm_mix, w_in_c[i], w_out_c[i], g_qn_c[i], g_kn_c[i],
                              layer, geom)
        x = moe_layer(x, mod, g_norm_ffn, w_router, b_router, w_gate_up, b_gate_up, w_down,
                      b_down, layer, geom)
    y_prompt = x[:geom.n_p].reshape(b_p, s_p, d)
    y_sample = x[geom.n_p:].reshape(b_d, s_d, d)
    return (y_prompt, y_sample)
```

```python
import functools
import math
from typing import NamedTuple

import jax
import jax.numpy as jnp
from jax import lax
from jax.experimental import pallas as pl
from jax.experimental.pallas import tpu as pltpu

F32 = jnp.float32
BF16 = jnp.bfloat16
HIGHEST = lax.Precision.HIGHEST

GRID_W = 64
HEAD_DIM = 64
ROPE_THETA = 10000.0
EPS = 1e-6
NEG = -1e30
NA_HEADS = 8
NA_KH = 8
NA_KW = 16
DIL_HEADS = 8
DIL_PATTERNS = ((128, 1), (512, 4), (2048, 16))
N_DIL = len(DIL_PATTERNS)
C_Q_HEADS = 16
C_KV_HEADS = 4
N_EXPERTS = 32
TOP_K = 4
SWIGLU_LIMIT = 7.0
SWIGLU_ALPHA = 1.702
Q_SCALE = HEAD_DIM ** -0.5
LOG2_E = math.log2(math.e)
BF16_SUBLANES = 16
LANES = 128
MXU_TILE = 256

V7X_VMEM_BYTES = 64 * 1024 * 1024
VMEM_LIMIT = V7X_VMEM_BYTES - 8 * 1024 * 1024

TOKEN_TILE = 1024
PREP_TILE = 2048
NA_BLOCK = 8 * GRID_W
NA_ROWS_PER_TRIP = 2
DIL_QBLOCK = 128
DIL_SIDE = 64
DIL_CHUNK = 1024
DIL_PROBLEMS_PER_TRIP = 4
FLASH_TQ = 256
FLASH_TK = 512
FLASH_UNROLL = 8
FLASH_BOUNDED_UNROLL = 16
FLASH_LOOKAHEAD = 6
FLASH_SAFE_BOUND = 60.0
FLASH_BOUND_MARGIN = 1.0 + 2.0 ** -6
MOE_BLOCK = 512

NA_Q0, NA_K0, NA_V0 = 0, NA_HEADS, 2 * NA_HEADS
C_K0 = C_Q_HEADS
C_V0 = C_Q_HEADS + C_KV_HEADS


class Geom(NamedTuple):
    b_p: int
    s_p: int
    b_d: int
    s_d: int

    @property
    def n_p(self):
        return self.b_p * self.s_p

    @property
    def n_tok(self):
        return self.n_p + self.b_d * self.s_d

    @property
    def n_seq(self):
        return self.b_p + self.b_d

    def groups(self):
        return ((0, self.b_p, self.s_p), (self.n_p, self.b_d, self.s_d))

    def seq_of_token(self, t0):
        return jnp.where(t0 < self.n_p, t0 // self.s_p,
                         self.b_p + (t0 - self.n_p) // self.s_d)

    def seq_span(self, t0):
        in_p = t0 < self.n_p
        start = jnp.where(in_p, (t0 // self.s_p) * self.s_p,
                          self.n_p + ((t0 - self.n_p) // self.s_d) * self.s_d)
        return start, jnp.where(in_p, self.s_p, self.s_d)


def _params(semantics, vmem=None):
    return pltpu.CompilerParams(dimension_semantics=semantics,
                                vmem_limit_bytes=vmem)


def _ada_kernel(c_ref, w_ref, b_ref, o_ref):
    c = c_ref[...]
    a = c / (1.0 + jnp.exp(-c))
    o_ref[...] = jnp.dot(a, w_ref[...], precision=HIGHEST,
                         preferred_element_type=F32) + b_ref[...]


def ada_modulation(c_all, w_ada, b_ada):
    depth, d, d6 = w_ada.shape
    n_seq = c_all.shape[0]
    rows = 8
    c_pad = jnp.zeros((rows, d), F32).at[:n_seq].set(c_all)
    out = pl.pallas_call(
        _ada_kernel,
        grid=(depth, d6 // d),
        in_specs=[
            pl.BlockSpec((rows, d), lambda l, j: (0, 0)),
            pl.BlockSpec((None, d, d), lambda l, j: (l, 0, j)),
            pl.BlockSpec((None, 1, d), lambda l, j: (l, 0, j)),
        ],
        out_specs=pl.BlockSpec((None, rows, d), lambda l, j: (l, 0, j)),
        out_shape=jax.ShapeDtypeStruct((depth, rows, d6), F32),
        compiler_params=_params(("arbitrary", "arbitrary")),
        name="ada_modulation",
    )(c_pad, w_ada, b_ada.reshape(depth, 1, d6))
    return out[:, :n_seq].reshape(depth, n_seq, 6, d)


def _modulated_norm(x, g, shift, scale):
    r = lax.rsqrt(jnp.mean(x * x, axis=-1, keepdims=True) + EPS)
    return (x * r * g) * (1.0 + scale) + shift


def _norm_proj_kernel(x_ref, mod_ref, g_ref, w_ref, o_ref, *scratch, dil):
    h = _modulated_norm(x_ref[...], g_ref[...], mod_ref[0:1, :], mod_ref[1:2, :])
    acc = jnp.dot(h.astype(BF16), w_ref[...], preferred_element_type=F32)
    tm = x_ref.shape[0]
    n_h, _, _, hd = o_ref.shape
    if dil == 1:
        for c in range(n_h):
            o_ref[c, 0] = acc[:, c * hd:(c + 1) * hd].astype(o_ref.dtype)
        return
    acc_ref, = scratch
    for s in range(n_h // 2):
        acc_ref[s] = acc[:, s * LANES:(s + 1) * LANES]
    for s in range(n_h // 2):
        for r in range(dil):
            pair = acc_ref[s, pl.ds(r, tm // dil, stride=dil), :].astype(o_ref.dtype)
            o_ref[2 * s, r] = pair[:, :hd]
            o_ref[2 * s + 1, r] = pair[:, hd:]


def norm_proj(x, mod, g_norm, w_bf16, layer, geom, dil):
    n_tok, d = x.shape
    n_out = w_bf16.shape[1]
    tm = TOKEN_TILE
    hd = HEAD_DIM
    n_h = n_out // hd
    scratch = [] if dil == 1 else [pltpu.VMEM((n_h // 2, tm, LANES), F32)]
    return pl.pallas_call(
        functools.partial(_norm_proj_kernel, dil=dil),
        grid=(n_tok // tm,),
        in_specs=[
            pl.BlockSpec((tm, d), lambda i: (i, 0)),
            pl.BlockSpec((None, None, 6, d), lambda i: (layer, geom.seq_of_token(i * tm), 0, 0)),
            pl.BlockSpec((None, 1, d), lambda i: (layer, 0, 0)),
            pl.BlockSpec((d, n_out), lambda i: (0, 0)),
        ],
        out_specs=pl.BlockSpec((n_h, dil, tm // dil, hd), lambda i: (0, 0, i, 0)),
        out_shape=jax.ShapeDtypeStruct((n_h, dil, n_tok // dil, hd), BF16),
        scratch_shapes=scratch,
        compiler_params=_params(("parallel",), VMEM_LIMIT),
        name="norm_proj",
    )(x, mod, g_norm.reshape(-1, 1, d), w_bf16)


def _prep_kernel(x_ref, g_ref, *rest, rope, aug):
    o_ref = rest[-1]
    hb, ts, hd = x_ref.shape
    x = x_ref[...].astype(F32)
    r = lax.rsqrt(jnp.mean(x * x, axis=-1, keepdims=True) + EPS)
    y = x * r * g_ref[...]
    if rope:
        cos_ref, sin_ref, p_ref = rest[:3]
        yr = jnp.dot(y.reshape(hb * ts, hd).astype(BF16), p_ref[...],
                     preferred_element_type=F32).reshape(hb, ts, hd)
        y = y * cos_ref[...] + yr * sin_ref[...]
    if aug:
        side_ref = rest[-2]
        norm = jnp.sqrt(jnp.sum(y * y, axis=-1, keepdims=True))
        y = jnp.concatenate([y, norm * side_ref[...]], axis=-1)
    o_ref[...] = y.astype(o_ref.dtype)


def qk_prep(x, gains, hb, n_blocks, tables=None, side=None):
    n_h, n_tok, hd = x.shape
    ts = PREP_TILE
    rope = tables is not None
    aug = side is not None
    n_used = n_blocks * hb
    blk = pl.BlockSpec((hb, ts, hd), lambda i, s: (i, s, 0))
    per_head = pl.BlockSpec((hb, 1, hd), lambda i, s: (i, 0, 0))
    in_specs = [blk, per_head]
    args = [x, gains.reshape(n_used, 1, hd).astype(F32)]
    if rope:
        tab = pl.BlockSpec((ts, hd), lambda i, s: (s, 0))
        in_specs += [tab, tab, pl.BlockSpec((hd, hd), lambda i, s: (0, 0))]
        args += list(tables)
    if aug:
        in_specs.append(per_head)
        args.append(side.reshape(n_used, 1, hd).astype(F32))
        out_specs = pl.BlockSpec((hb, ts, 2 * hd), lambda i, s: (i, s, 0))
        out_shape = jax.ShapeDtypeStruct((n_used, n_tok, 2 * hd), BF16)
    else:
        out_specs = blk
        out_shape = jax.ShapeDtypeStruct(x.shape, BF16)
    return pl.pallas_call(
        functools.partial(_prep_kernel, rope=rope, aug=aug),
        grid=(n_blocks, n_tok // ts),
        in_specs=in_specs,
        out_specs=out_specs,
        out_shape=out_shape,
        input_output_aliases={} if aug else {0: 0},
        compiler_params=_params(("parallel", "parallel")),
        name="qk_prep",
    )(*args)


def _rope_angles(pos, dim):
    inv = ROPE_THETA ** (-jnp.arange(0, dim, 2, dtype=F32) / dim)
    ang = pos.astype(F32)[:, None] * inv[None, :]
    ang = jnp.concatenate([ang, ang], axis=-1)
    return jnp.cos(ang), jnp.sin(ang)


def _rotate_half_matrix(dim, n_blocks):
    half = dim // 2
    i = jnp.arange(dim * n_blocks)
    blk, w = i // dim, i % dim
    src = blk * dim + jnp.where(w < half, w + half, w - half)
    sign = jnp.where(w < half, -1.0, 1.0)
    p = jnp.zeros((dim * n_blocks, dim * n_blocks), F32).at[src, i].set(sign)
    return p.astype(BF16)


def _na_bias_table(rpb):
    n_h = rpb.shape[0]
    qc = jnp.arange(GRID_W)[:, None]
    kc = jnp.arange(GRID_W)[None, :]
    c0 = jnp.clip(qc - NA_KW // 2, 0, GRID_W - NA_KW)
    ok = (kc >= c0) & (kc < c0 + NA_KW)
    n_dr = rpb.shape[1]
    period = 2 * GRID_W
    ext = jnp.concatenate([rpb[..., NA_KW - 1:],
                           jnp.zeros((n_h, n_dr, period - (2 * NA_KW - 1)), rpb.dtype),
                           rpb[..., :NA_KW - 1]], axis=-1)
    toep = jnp.tile(ext, (1, 1, GRID_W))[..., :GRID_W * (period - 1)]
    toep = toep.reshape(n_h, n_dr, GRID_W, period - 1)[..., :GRID_W]
    toep = jnp.where(ok[None, None], toep, NEG)
    bias = jnp.stack([toep[:, NA_KH - 1 - d0:2 * NA_KH - 1 - d0] for d0 in range(NA_KH)])
    bias = bias.transpose(0, 1, 3, 2, 4).reshape(NA_KH, n_h, GRID_W, NA_KH * GRID_W)
    return bias.astype(BF16)


def _na_kernel(q_ref, kp_ref, kc_ref, kn_ref, vp_ref, vc_ref, vn_ref, tbl_ref,
               o_ref, ks_ref, vs_ref, *, geom):
    blk = NA_BLOCK
    j = pl.program_id(0)
    seq_start, seq_len = geom.seq_span(j * blk)
    rows = seq_len // GRID_W
    jl = j - seq_start // blk
    ks_ref[:, 0:blk, :] = kp_ref[...]
    ks_ref[:, blk:2 * blk, :] = kc_ref[...]
    ks_ref[:, 2 * blk:3 * blk, :] = kn_ref[...]
    vs_ref[:, 0:blk, :] = vp_ref[...]
    vs_ref[:, blk:2 * blk, :] = vc_ref[...]
    vs_ref[:, 2 * blk:3 * blk, :] = vn_ref[...]
    n_keys = NA_KH * GRID_W

    def body(ii, carry):
        probs = []
        for u in range(NA_ROWS_PER_TRIP):
            i = ii * NA_ROWS_PER_TRIP + u
            r = jl * NA_KH + i
            r0 = jnp.clip(r - NA_KH // 2, 0, rows - NA_KH)
            off = pl.multiple_of((r0 - (jl - 1) * NA_KH) * GRID_W, GRID_W)
            d0 = r - r0
            qoff = pl.multiple_of(i * GRID_W, GRID_W)
            for h in range(NA_HEADS):
                q = q_ref[h, pl.ds(qoff, GRID_W), :]
                k = ks_ref[h, pl.ds(off, n_keys), :]
                s = lax.dot_general(q, k, (((1,), (1,)), ((), ())), preferred_element_type=F32)
                probs.append((h, off, qoff, s + tbl_ref[d0, h].astype(F32)))
        soft = []
        for h, off, qoff, s in probs:
            m = jnp.max(s, axis=-1, keepdims=True)
            p = jnp.exp(s - m)
            soft.append((h, off, qoff, p.astype(BF16), jnp.sum(p, axis=-1, keepdims=True)))
        for h, off, qoff, p, l in soft:
            v = vs_ref[h, pl.ds(off, n_keys), :]
            o = jnp.dot(p, v, preferred_element_type=F32) / l
            o_ref[h, pl.ds(qoff, GRID_W), :] = o.astype(o_ref.dtype)
        return carry

    lax.fori_loop(0, NA_KH // NA_ROWS_PER_TRIP, body, 0)


def neighborhood_attention(qkv, tbl, geom):
    _, n_tok, hd = qkv.shape
    n_h = NA_HEADS
    blk = NA_BLOCK
    nb = n_tok // blk

    def spec(head0, shift):
        hb = head0 // n_h
        return pl.BlockSpec((n_h, blk, hd), lambda j: (hb, jnp.clip(j + shift, 0, nb - 1), 0))

    return pl.pallas_call(
        functools.partial(_na_kernel, geom=geom),
        grid=(nb,),
        in_specs=[spec(NA_Q0, 0), spec(NA_K0, -1), spec(NA_K0, 0), spec(NA_K0, 1),
                  spec(NA_V0, -1), spec(NA_V0, 0), spec(NA_V0, 1),
                  pl.BlockSpec(tbl.shape, lambda j: (0, 0, 0, 0))],
        out_specs=pl.BlockSpec((n_h, blk, hd), lambda j: (0, j, 0)),
        out_shape=jax.ShapeDtypeStruct((n_h, n_tok, hd), BF16),
        scratch_shapes=[pltpu.VMEM((n_h, 3 * blk, hd), BF16),
                        pltpu.VMEM((n_h, 3 * blk, hd), BF16)],
        compiler_params=_params(("parallel",), VMEM_LIMIT),
        name="neighborhood_attention",
    )(qkv, qkv, qkv, qkv, qkv, qkv, qkv, tbl)


def _dil_kernel(q_ref, kp_ref, kc_ref, kn_ref, vp_ref, vc_ref, vn_ref,
                o_ref, lse_ref, ks_ref, vs_ref, *, geom, dil):
    c = pl.program_id(0)
    n_h, chunk, hd = q_ref.shape
    side = DIL_SIDE
    qb = DIL_QBLOCK
    kb = qb + 2 * side
    ks_ref[:, 0:side, :] = kp_ref[...]
    ks_ref[:, side:side + chunk, :] = kc_ref[...]
    ks_ref[:, side + chunk:, :] = kn_ref[...]
    vs_ref[:, 0:side, :] = vp_ref[...]
    vs_ref[:, side:side + chunk, :] = vc_ref[...]
    vs_ref[:, side + chunk:, :] = vn_ref[...]
    plane_rows = geom.n_tok // dil
    t0 = ((c * chunk) % plane_rows) * dil
    seq_start, seq_len = geom.seq_span(t0)
    row0 = (t0 - seq_start) // dil
    cls_rows = seq_len // dil
    jq = lax.broadcasted_iota(jnp.int32, (qb, kb), 0)
    jk = lax.broadcasted_iota(jnp.int32, (qb, kb), 1) - side
    near_bias = jnp.where(jnp.abs(jk - jq) <= side, 0.0, NEG)
    before_bias = jnp.where(jk < 0, NEG, 0.0)
    after_bias = jnp.where(jk >= qb, NEG, 0.0)
    n_trip = min(DIL_PROBLEMS_PER_TRIP, chunk // qb)
    trips_per_head = chunk // (qb * n_trip)
    assert chunk % (qb * n_trip) == 0

    def body(it, carry):
        h = it // trips_per_head
        ii = it % trips_per_head
        probs = []
        for u in range(n_trip):
            i = ii * n_trip + u
            qoff = pl.multiple_of(i * qb, qb)
            p0 = row0 + i * qb
            bias = near_bias + jnp.where(p0 == 0, before_bias, 0.0)
            bias = bias + jnp.where(p0 + qb == cls_rows, after_bias, 0.0)
            q = q_ref[h, pl.ds(qoff, qb), :]
            k = ks_ref[h, pl.ds(qoff, kb), :]
            s = lax.dot_general(q, k, (((1,), (1,)), ((), ())), preferred_element_type=F32)
            probs.append((qoff, s + bias))
        soft = []
        for qoff, s in probs:
            m = jnp.max(s, axis=-1, keepdims=True)
            p = jnp.exp(s - m)
            soft.append((qoff, p.astype(BF16), m, jnp.sum(p, axis=-1, keepdims=True)))
        for qoff, p, m, l in soft:
            v = vs_ref[h, pl.ds(qoff, kb), :]
            o = jnp.dot(p, v, preferred_element_type=F32) / l
            o_ref[h, pl.ds(qoff, qb), :] = o.astype(o_ref.dtype)
            lse_ref[h, pl.ds(qoff, qb), :] = jnp.broadcast_to(m + jnp.log(l), (qb, hd))
        return carry

    lax.fori_loop(0, n_h * trips_per_head, body, 0)


def dilated_attention(qkv, dil, geom):
    _, n_tok, hd = qkv.shape
    n_h = DIL_HEADS
    chunk = min(DIL_CHUNK, min(geom.s_p, geom.s_d) // dil)
    assert chunk % DIL_QBLOCK == 0
    per = chunk // DIL_SIDE
    n_side = n_tok // DIL_SIDE

    def cur(hb):
        return pl.BlockSpec((n_h, chunk, hd), lambda c: (hb, c, 0))

    def prev(hb):
        return pl.BlockSpec((n_h, DIL_SIDE, hd), lambda c: (hb, jnp.maximum(c * per - 1, 0), 0))

    def nxt(hb):
        return pl.BlockSpec((n_h, DIL_SIDE, hd),
                            lambda c: (hb, jnp.minimum((c + 1) * per, n_side - 1), 0))

    return pl.pallas_call(
        functools.partial(_dil_kernel, geom=geom, dil=dil),
        grid=(n_tok // chunk,),
        in_specs=[cur(0), prev(1), cur(1), nxt(1), prev(2), cur(2), nxt(2)],
        out_specs=[cur(0), cur(0)],
        out_shape=[jax.ShapeDtypeStruct((n_h, n_tok, hd), BF16),
                   jax.ShapeDtypeStruct((n_h, n_tok, hd), F32)],
        scratch_shapes=[pltpu.VMEM((n_h, chunk + 2 * DIL_SIDE, hd), BF16),
                        pltpu.VMEM((n_h, chunk + 2 * DIL_SIDE, hd), BF16)],
        compiler_params=_params(("parallel",), VMEM_LIMIT),
        name="dilated_attention",
    )(qkv, qkv, qkv, qkv, qkv, qkv, qkv)


def _to_natural(src_ref, dst_ref):
    n_h, dil, n, _ = src_ref.shape
    for h in range(0, n_h, 2):
        for r in range(dil):
            pair = jnp.concatenate([src_ref[h, r].astype(F32), src_ref[h + 1, r].astype(F32)],
                                   axis=-1)
            if dil == 1:
                dst_ref[h // 2] = pair
            else:
                dst_ref[h // 2, pl.ds(r, n, stride=dil), :] = pair


def _outproj_ab_kernel(x_ref, mod_ref, oa_ref, od0_ref, od1_ref, od2_ref,
                       l0_ref, l1_ref, l2_ref, w_ref, o_ref, *scratch):
    so = scratch[:N_DIL]
    sl = scratch[N_DIL:]
    for src, dst in zip((od0_ref, od1_ref, od2_ref, l0_ref, l1_ref, l2_ref), so + sl):
        _to_natural(src, dst)
    n_h = oa_ref.shape[0]
    slabs = [jnp.concatenate([oa_ref[h], oa_ref[h + 1]], axis=-1) for h in range(0, n_h, 2)]
    for s in range(n_h // 2):
        l0, l1, l2 = sl[0][s], sl[1][s], sl[2][s]
        m = jnp.maximum(jnp.maximum(l0, l1), l2)
        e0, e1, e2 = jnp.exp(l0 - m), jnp.exp(l1 - m), jnp.exp(l2 - m)
        ob = (e0 * so[0][s] + e1 * so[1][s] + e2 * so[2][s]) / (e0 + e1 + e2)
        slabs.append(ob.astype(BF16))
    cat = jnp.concatenate(slabs, axis=-1)
    res = jnp.dot(cat, w_ref[...], preferred_element_type=F32)
    o_ref[...] = x_ref[...] + mod_ref[2:3, :] * res


def outproj_ab(x, mod, oa, od, lse, w_bf16, layer, geom):
    n_tok, d = x.shape
    n_h, _, hd = oa.shape
    tm = TOKEN_TILE // 2

    def cls(a):
        dil = a.shape[1]
        return pl.BlockSpec((n_h, dil, tm // dil, hd), lambda i: (0, 0, i, 0))

    return pl.pallas_call(
        _outproj_ab_kernel,
        grid=(n_tok // tm,),
        in_specs=[
            pl.BlockSpec((tm, d), lambda i: (i, 0)),
            pl.BlockSpec((None, None, 6, d), lambda i: (layer, geom.seq_of_token(i * tm), 0, 0)),
            pl.BlockSpec((n_h, tm, hd), lambda i: (0, i, 0)),
            *[cls(a) for a in od], *[cls(a) for a in lse],
            pl.BlockSpec(w_bf16.shape, lambda i: (0, 0)),
        ],
        out_specs=pl.BlockSpec((tm, d), lambda i: (i, 0)),
        out_shape=jax.ShapeDtypeStruct(x.shape, F32),
        scratch_shapes=[pltpu.VMEM((n_h // 2, tm, LANES), F32) for _ in range(2 * N_DIL)],
        compiler_params=_params(("parallel",), VMEM_LIMIT),
        name="outproj_ab",
    )(x, mod, oa, *od, *lse, w_bf16)


def _outproj_kernel(x_ref, mod_ref, o_in_ref, w_ref, o_ref):
    res = jnp.dot(o_in_ref[...], w_ref[...], preferred_element_type=F32)
    o_ref[...] = x_ref[...] + mod_ref[2:3, :] * res


def outproj(x, mod, o_in, w_bf16, layer, geom):
    n_tok, d = x.shape
    tm = TOKEN_TILE
    return pl.pallas_call(
        _outproj_kernel,
        grid=(n_tok // tm,),
        in_specs=[
            pl.BlockSpec((tm, d), lambda i: (i, 0)),
            pl.BlockSpec((None, None, 6, d), lambda i: (layer, geom.seq_of_token(i * tm), 0, 0)),
            pl.BlockSpec((tm, o_in.shape[1]), lambda i: (i, 0)),
            pl.BlockSpec(w_bf16.shape, lambda i: (0, 0)),
        ],
        out_specs=pl.BlockSpec((tm, d), lambda i: (i, 0)),
        out_shape=jax.ShapeDtypeStruct(x.shape, F32),
        compiler_params=_params(("parallel",), VMEM_LIMIT),
        name="outproj",
    )(x, mod, o_in, w_bf16)


def _flash_write_out(acc_ref, o_ref, n_g, tq):
    hd = HEAD_DIM
    acc = acc_ref[...]
    o = acc[:hd] / acc[hd:hd + 1]
    pad = jnp.zeros((LANES - hd, tq), F32)
    for g in range(n_g):
        t = jnp.concatenate([o[:, g * tq:(g + 1) * tq], pad], axis=0).T
        o_ref[:, g * hd:(g + 1) * hd] = t[:, :hd].astype(o_ref.dtype)


def _flash_bounded_kernel(q_ref, k_ref, vt_ref, kmx_ref, _o_in_ref, o_ref, acc_ref, *, unroll):
    n_g, tq, kw = q_ref.shape
    n_kt, tk, _ = k_ref.shape
    acc_ref[...] = jnp.zeros(acc_ref.shape, F32)
    kmx = kmx_ref[...]
    n_col = (n_g * tq) // MXU_TILE

    n_kc = tk // MXU_TILE

    def qk(j, n, kc):
        rows = slice(kc * MXU_TILE, (kc + 1) * MXU_TILE)
        q_n = q_ref[...].reshape(n_g * tq, kw)[n * MXU_TILE:(n + 1) * MXU_TILE]
        return lax.dot_general(k_ref[j, rows, :] + kmx, q_n, (((1,), (1,)), ((), ())),
                               preferred_element_type=F32)

    def body(jj, carry):
        units = [(unroll * jj + u, n, kc) for u in range(unroll) for n in range(n_col)
                 for kc in range(n_kc)]
        pending = [qk(*unit) for unit in units[:FLASH_LOOKAHEAD]]
        for idx, (j, n, kc) in enumerate(units):
            if idx + FLASH_LOOKAHEAD < len(units):
                pending.append(qk(*units[idx + FLASH_LOOKAHEAD]))
            cols = slice(n * MXU_TILE, (n + 1) * MXU_TILE)
            rows = slice(kc * MXU_TILE, (kc + 1) * MXU_TILE)
            p = jnp.exp2(pending.pop(0)).astype(BF16)
            acc_ref[:, cols] += jnp.dot(vt_ref[j, :, rows], p, preferred_element_type=F32)
        return carry

    lax.fori_loop(0, n_kt // unroll, body, 0)
    _flash_write_out(acc_ref, o_ref, n_g, tq)


def _flash_kernel(q_ref, k_ref, vt_ref, _o_in_ref, o_ref, sa_ref, sb_ref, mxa_ref, mxb_ref,
                  m_ref, acc_ref, *, unroll):
    n_g, tq, hd = q_ref.shape
    n_kt = k_ref.shape[0]
    m_ref[...] = jnp.full(m_ref.shape, -jnp.inf, F32)
    acc_ref[...] = jnp.zeros(acc_ref.shape, F32)
    n_col = (n_g * tq) // MXU_TILE

    def scores(j, s_ref, mx_ref, n):
        cols = slice(n * MXU_TILE, (n + 1) * MXU_TILE)
        q_n = q_ref[...].reshape(n_g * tq, hd)[n * MXU_TILE:(n + 1) * MXU_TILE]
        s = lax.dot_general(k_ref[j], q_n, (((1,), (1,)), ((), ())),
                            preferred_element_type=F32)
        s_ref[:, cols] = s
        mx_ref[:, cols] = jnp.max(s, axis=0, keepdims=True)

    def consume(j, s_ref, mx_ref, n):
        cols = slice(n * MXU_TILE, (n + 1) * MXU_TILE)
        m_old = m_ref[:, cols]
        m_new = jnp.maximum(m_old, mx_ref[:, cols])
        alpha = jnp.exp2(m_old - m_new)
        m_ref[:, cols] = m_new
        tk = s_ref.shape[0]
        acc = alpha * acc_ref[:, cols]
        for kc in range(tk // MXU_TILE):
            rows = slice(kc * MXU_TILE, (kc + 1) * MXU_TILE)
            p = jnp.exp2(s_ref[rows, cols] - m_new).astype(BF16)
            acc = acc + jnp.dot(vt_ref[j, :, rows], p, preferred_element_type=F32)
        acc_ref[:, cols] = acc

    for n in range(n_col):
        scores(0, sa_ref, mxa_ref, n)

    def body(jj, carry):
        j = unroll * jj
        for u in range(0, unroll, 2):
            for n in range(n_col):
                scores(j + u + 1, sb_ref, mxb_ref, n)
                consume(j + u, sa_ref, mxa_ref, n)
            for n in range(n_col):
                scores(jnp.minimum(j + u + 2, n_kt - 1), sa_ref, mxa_ref, n)
                consume(j + u + 1, sb_ref, mxb_ref, n)
        return carry

    lax.fori_loop(0, n_kt // unroll, body, 0)
    _flash_write_out(acc_ref, o_ref, n_g, tq)


def flash_attention(qk, vt_tiles, o_buf, tok0, n_b, s_n, seq_idx0, kmx=None):
    n_all, n_tok, kw = qk.shape
    n_kv, _, hv, tk = vt_tiles.shape
    n_g = C_Q_HEADS // n_kv
    hd = HEAD_DIM
    tq = FLASH_TQ
    nq = s_n // tq
    kt_per = s_n // tk
    unroll = min(FLASH_UNROLL if kmx is None else FLASH_BOUNDED_UNROLL, kt_per)
    assert kt_per % unroll == 0 and unroll % 2 == 0 and tok0 % s_n == 0
    wq = n_g * tq
    k_view = qk.reshape(n_all, n_tok // tk, tk, kw)
    seq0 = tok0 // s_n
    in_specs = [
        pl.BlockSpec((n_g, tq, kw), lambda b, h, i: (h, (tok0 + b * s_n) // tq + i, 0)),
        pl.BlockSpec((None, kt_per, tk, kw), lambda b, h, i: (C_K0 + h, seq0 + b, 0, 0)),
        pl.BlockSpec((None, kt_per, hv, tk), lambda b, h, i: (h, seq0 + b, 0, 0)),
    ]
    args = [qk, k_view, vt_tiles]
    if kmx is None:
        body = functools.partial(_flash_kernel, unroll=unroll)
        scratch = [pltpu.VMEM((tk, wq), F32), pltpu.VMEM((tk, wq), F32),
                   pltpu.VMEM((1, wq), F32), pltpu.VMEM((1, wq), F32),
                   pltpu.VMEM((1, wq), F32), pltpu.VMEM((hv, wq), F32)]
    else:
        body = functools.partial(_flash_bounded_kernel, unroll=unroll)
        scratch = [pltpu.VMEM((hv, wq), F32)]
        in_specs.append(pl.BlockSpec((None, None, 1, kw), lambda b, h, i: (h, seq_idx0 + b, 0, 0)))
        args.append(kmx)
    in_specs.append(pl.BlockSpec(memory_space=pl.ANY))
    args.append(o_buf)
    return pl.pallas_call(
        body,
        grid=(n_b, n_kv, nq),
        in_specs=in_specs,
        out_specs=pl.BlockSpec((tq, n_g * hd), lambda b, h, i: ((tok0 + b * s_n) // tq + i, h)),
        out_shape=jax.ShapeDtypeStruct(o_buf.shape, o_buf.dtype),
        input_output_aliases={len(args) - 1: 0},
        scratch_shapes=scratch,
        compiler_params=_params(("parallel", "parallel", "arbitrary"), VMEM_LIMIT),
        name="flash_attention",
    )(*args)


def _router_kernel(x_ref, mod_ref, g_ref, wr_ref, br_ref, h_ref, idx_ref, gate_ref, cnt_ref,
                   run_ref):
    @pl.when(pl.program_id(0) == 0)
    def _():
        run_ref[...] = jnp.zeros(run_ref.shape, F32)

    h = _modulated_norm(x_ref[...], g_ref[...], mod_ref[3:4, :], mod_ref[4:5, :])
    h_ref[...] = h.astype(BF16)
    logits = lax.dot_general(wr_ref[...], h, (((1,), (1,)), ((), ())), precision=HIGHEST,
                             preferred_element_type=F32) + br_ref[...]
    n_e, tm = logits.shape
    iota = lax.broadcasted_iota(jnp.int32, (n_e, tm), 0)
    vals, ids = [], []
    cur = logits
    for _ in range(TOP_K):
        m = jnp.max(cur, axis=0, keepdims=True)
        am = jnp.min(jnp.where(cur == m, iota, n_e), axis=0, keepdims=True)
        vals.append(m)
        ids.append(am)
        cur = jnp.where(iota == am, -jnp.inf, cur)
    es = [jnp.exp(v - vals[0]) for v in vals]
    den = es[0] + es[1] + es[2] + es[3]
    chosen = jnp.zeros((n_e, tm), F32)
    for kk in range(TOP_K):
        chosen = chosen + jnp.where(iota == ids[kk], 1.0, 0.0)
    earlier = lax.broadcasted_iota(jnp.int32, (tm, tm), 0)
    token = lax.broadcasted_iota(jnp.int32, (tm, tm), 1)
    tri = jnp.where(earlier < token, 1.0, 0.0).astype(BF16)
    before = jnp.dot(chosen.astype(BF16), tri, preferred_element_type=F32) + run_ref[...]
    run_ref[...] = run_ref[...] + jnp.sum(chosen, axis=1, keepdims=True)
    cnt_ref[...] = run_ref[...]
    row = lax.broadcasted_iota(jnp.int32, idx_ref.shape, 0)
    idx_out = jnp.zeros(idx_ref.shape, jnp.int32)
    gate_out = jnp.zeros(gate_ref.shape, F32)
    for kk in range(TOP_K):
        rank = jnp.sum(jnp.where(iota == ids[kk], before, 0.0), axis=0, keepdims=True)
        idx_out = jnp.where(row == kk, ids[kk], idx_out)
        idx_out = jnp.where(row == TOP_K + kk, rank.astype(jnp.int32), idx_out)
        gate_out = jnp.where(row == kk, es[kk] / den, gate_out)
    idx_ref[...] = idx_out
    gate_ref[...] = gate_out


def router(x, mod, g_norm, w_router, b_router, layer, geom):
    n_tok, d = x.shape
    tm = TOKEN_TILE // 2
    return pl.pallas_call(
        _router_kernel,
        grid=(n_tok // tm,),
        in_specs=[
            pl.BlockSpec((tm, d), lambda i: (i, 0)),
            pl.BlockSpec((None, None, 6, d), lambda i: (layer, geom.seq_of_token(i * tm), 0, 0)),
            pl.BlockSpec((None, 1, d), lambda i: (layer, 0, 0)),
            pl.BlockSpec((None, N_EXPERTS, d), lambda i: (layer, 0, 0)),
            pl.BlockSpec((None, N_EXPERTS, 1), lambda i: (layer, 0, 0)),
        ],
        out_specs=[
            pl.BlockSpec((tm, d), lambda i: (i, 0)),
            pl.BlockSpec((2 * TOP_K, tm), lambda i: (0, i)),
            pl.BlockSpec((2 * TOP_K, tm), lambda i: (0, i)),
            pl.BlockSpec((N_EXPERTS, 1), lambda i: (0, 0)),
        ],
        out_shape=[
            jax.ShapeDtypeStruct((n_tok, d), BF16),
            jax.ShapeDtypeStruct((2 * TOP_K, n_tok), jnp.int32),
            jax.ShapeDtypeStruct((2 * TOP_K, n_tok), F32),
            jax.ShapeDtypeStruct((N_EXPERTS, 1), F32),
        ],
        scratch_shapes=[pltpu.VMEM((N_EXPERTS, 1), F32)],
        compiler_params=_params(("arbitrary",), VMEM_LIMIT),
        name="router",
    )(x, mod, g_norm.reshape(-1, 1, d), w_router.transpose(0, 2, 1),
      b_router.reshape(-1, N_EXPERTS, 1))


def _moe_kernel(be_ref, nu_ref, x_ref, wgu_ref, bgu_ref, wd_ref, bd_ref, o_ref,
                wgu_s, wd_s):
    i = pl.program_id(0)
    e = be_ref[i]
    e_prev = be_ref[jnp.maximum(i - 1, 0)]

    @pl.when((i == 0) | (e != e_prev))
    def _():
        wgu_s[...] = wgu_ref[...].astype(BF16)
        wd_s[...] = wd_ref[...].astype(BF16)

    @pl.when(i < nu_ref[0])
    def _():
        d_ff = wd_s.shape[0]
        gu = jnp.dot(x_ref[...], wgu_s[...], preferred_element_type=F32) + bgu_ref[...]
        glu = jnp.minimum(gu[:, :d_ff], SWIGLU_LIMIT)
        lin = jnp.clip(gu[:, d_ff:], -SWIGLU_LIMIT, SWIGLU_LIMIT)
        act = glu / (1.0 + jnp.exp(-SWIGLU_ALPHA * glu)) * (lin + 1.0)
        y = jnp.dot(act.astype(BF16), wd_s[...], preferred_element_type=F32) + bd_ref[...]
        o_ref[...] = y.astype(o_ref.dtype)

    @pl.when(i >= nu_ref[0])
    def _():
        o_ref[...] = jnp.zeros(o_ref.shape, o_ref.dtype)


def moe_experts(xs, blk_e, n_used, w_gu, b_gu, w_down, b_down, layer):
    n_rows, d = xs.shape
    d_ff = w_down.shape[2]
    nblk = n_rows // MOE_BLOCK
    grid_spec = pltpu.PrefetchScalarGridSpec(
        num_scalar_prefetch=2,
        grid=(nblk,),
        in_specs=[
            pl.BlockSpec((MOE_BLOCK, d), lambda i, be, nu: (jnp.minimum(i, nu[0] - 1), 0)),
            pl.BlockSpec((None, None, d, 2 * d_ff), lambda i, be, nu: (layer, be[i], 0, 0)),
            pl.BlockSpec((None, None, 1, 2 * d_ff), lambda i, be, nu: (layer, be[i], 0, 0)),
            pl.BlockSpec((None, None, d_ff, d), lambda i, be, nu: (layer, be[i], 0, 0)),
            pl.BlockSpec((None, None, 1, d), lambda i, be, nu: (layer, be[i], 0, 0)),
        ],
        out_specs=pl.BlockSpec((MOE_BLOCK, d), lambda i, be, nu: (i, 0)),
        scratch_shapes=[pltpu.VMEM((d, 2 * d_ff), BF16), pltpu.VMEM((d_ff, d), BF16)],
    )
    depth, n_e = w_gu.shape[:2]
    return pl.pallas_call(
        _moe_kernel,
        grid_spec=grid_spec,
        out_shape=jax.ShapeDtypeStruct((n_rows, d), BF16),
        compiler_params=_params(("arbitrary",), VMEM_LIMIT),
        name="moe_experts",
    )(blk_e, n_used, xs, w_gu, b_gu.reshape(depth, n_e, 1, 2 * d_ff), w_down,
      b_down.reshape(depth, n_e, 1, d))


def _combine_kernel(x_ref, mod_ref, y_ref, gate_ref, o_ref):
    g = gate_ref[...]
    acc = g[:, 0:1] * y_ref[0].astype(F32)
    for kk in range(1, TOP_K):
        acc = acc + g[:, kk:kk + 1] * y_ref[kk].astype(F32)
    o_ref[...] = x_ref[...] + mod_ref[5:6, :] * acc


def moe_combine(x, mod, yk, gates, layer, geom):
    n_tok, d = x.shape
    tm = TOKEN_TILE // 2
    return pl.pallas_call(
        _combine_kernel,
        grid=(n_tok // tm,),
        in_specs=[
            pl.BlockSpec((tm, d), lambda i: (i, 0)),
            pl.BlockSpec((None, None, 6, d), lambda i: (layer, geom.seq_of_token(i * tm), 0, 0)),
            pl.BlockSpec((TOP_K, tm, d), lambda i: (0, i, 0)),
            pl.BlockSpec((tm, 2 * TOP_K), lambda i: (i, 0)),
        ],
        out_specs=pl.BlockSpec((tm, d), lambda i: (i, 0)),
        out_shape=jax.ShapeDtypeStruct(x.shape, F32),
        compiler_params=_params(("parallel",), VMEM_LIMIT),
        name="moe_combine",
    )(x, mod, yk, gates)


def moe_layer(x, mod, g_norm, w_router, b_router, w_gu, b_gu, w_down, b_down, layer, geom):
    n_tok, d = x.shape
    h, idx8, gates8, counts = router(x, mod, g_norm, w_router, b_router, layer, geom)
    idx = idx8[:TOP_K].T
    rank = idx8[TOP_K:].T
    gates8 = gates8.T
    counts = counts[:, 0].astype(jnp.int32)
    pcounts = ((counts + MOE_BLOCK - 1) // MOE_BLOCK) * MOE_BLOCK
    pend = jnp.cumsum(pcounts)
    pstart = pend - pcounts
    experts = jnp.arange(N_EXPERTS, dtype=jnp.int32)
    dest = rank + jnp.sum(jnp.where(idx[:, :, None] == experts, pstart, 0), axis=-1)
    nblk = (n_tok * TOP_K) // MOE_BLOCK + N_EXPERTS
    n_rows = nblk * MOE_BLOCK
    tok_ids = jnp.broadcast_to(jnp.arange(n_tok, dtype=jnp.int32)[:, None], dest.shape)
    tok = (jnp.arange(n_rows, dtype=jnp.int32) % n_tok).at[dest.reshape(-1)].set(
        tok_ids.reshape(-1), unique_indices=True)
    n_used = (pend[-1] // MOE_BLOCK).astype(jnp.int32).reshape(1)
    blk_start = jnp.arange(nblk, dtype=jnp.int32) * MOE_BLOCK
    blk_last = jnp.minimum(blk_start, pend[-1] - 1)
    blk_e = jnp.sum(pend[None, :] <= blk_last[:, None], axis=-1, dtype=jnp.int32)
    blk_e = jnp.minimum(blk_e, N_EXPERTS - 1)
    xs = jnp.take(h, tok, axis=0, mode="clip")
    yb = moe_experts(xs, blk_e, n_used, w_gu, b_gu, w_down, b_down, layer)
    yk = jnp.take(yb, dest.T.reshape(-1), axis=0, mode="clip").reshape(TOP_K, n_tok, d)
    return moe_combine(x, mod, yk, gates8, layer, geom)


def _seq_positions(geom):
    return jnp.concatenate([jnp.arange(s, dtype=jnp.int32)
                            for _, n_b, s in geom.groups() for _ in range(n_b)])


def _qk_gains(g_q, g_k, n_q, n_k, q_scale):
    hd = g_q.shape[-1]
    return jnp.concatenate([jnp.broadcast_to(g_q * q_scale, (n_q, hd)),
                            jnp.broadcast_to(g_k, (n_k, hd))])


def _dilated_group_weight(w_in, gi):
    blk = DIL_HEADS * HEAD_DIM
    base = 3 * NA_HEADS * HEAD_DIM
    return jnp.concatenate([w_in[:, base + (a * N_DIL + gi) * blk:base + (a * N_DIL + gi + 1) * blk]
                            for a in range(3)], axis=1)


def mixer_ab_layer(x, mod, g_norm, w_in, w_out, g_qn_a, g_kn_a, rpb, g_qn_b, g_kn_b, layer, geom):
    n_tok, d = x.shape
    hd = HEAD_DIM
    w = w_in.astype(BF16)
    n_h = NA_HEADS
    qkv = norm_proj(x, mod, g_norm, w[:, :3 * n_h * hd], layer, geom, dil=1)
    qkv = qk_prep(qkv.reshape(3 * n_h, n_tok, hd), _qk_gains(g_qn_a, g_kn_a, n_h, n_h, Q_SCALE),
                  hb=n_h, n_blocks=2)
    oa = neighborhood_attention(qkv, _na_bias_table(rpb), geom)
    cos, sin = _rope_angles(_seq_positions(geom), hd)
    perm = _rotate_half_matrix(hd, 1)
    od, lse = [], []
    for gi, (window, dil) in enumerate(DIL_PATTERNS):
        assert (window // 2) // dil == DIL_SIDE
        n_h = DIL_HEADS

        def class_major(t):
            return t.reshape(n_tok // dil, dil, hd).transpose(1, 0, 2).reshape(n_tok, hd)

        qkv = norm_proj(x, mod, g_norm, _dilated_group_weight(w, gi), layer, geom, dil=dil)
        qkv = qk_prep(qkv.reshape(3 * n_h, n_tok, hd),
                      _qk_gains(g_qn_b[gi], g_kn_b[gi], n_h, n_h, Q_SCALE), hb=n_h, n_blocks=2,
                      tables=(class_major(cos), class_major(sin), perm))
        o_g, lse_g = dilated_attention(qkv, dil, geom)
        od.append(o_g.reshape(n_h, dil, n_tok // dil, hd))
        lse.append(lse_g.reshape(n_h, dil, n_tok // dil, hd))
    return outproj_ab(x, mod, oa, od, lse, w_out.astype(BF16), layer, geom)


def mixer_c_layer(x, mod, g_norm, w_in, w_out, g_qn, g_kn, layer, geom):
    n_tok, d = x.shape
    hd = HEAD_DIM
    n_heads = C_Q_HEADS + 2 * C_KV_HEADS
    qkv = norm_proj(x, mod, g_norm, w_in.astype(BF16), layer, geom, dil=1).reshape(n_heads, n_tok, hd)
    pos = _seq_positions(geom)
    half = hd // 2
    cr, sr = _rope_angles(pos // GRID_W, half)
    cc, sc = _rope_angles(pos % GRID_W, half)
    tables = (jnp.concatenate([cr, cc], axis=-1), jnp.concatenate([sr, sc], axis=-1),
              _rotate_half_matrix(half, 2))
    side = jnp.zeros((C_Q_HEADS + C_KV_HEADS, hd), F32).at[:C_Q_HEADS, 0].set(-FLASH_BOUND_MARGIN)
    qk = qk_prep(qkv, _qk_gains(g_qn, g_kn, C_Q_HEADS, C_KV_HEADS, Q_SCALE * LOG2_E),
                 hb=C_KV_HEADS, n_blocks=(C_Q_HEADS + C_KV_HEADS) // C_KV_HEADS, tables=tables,
                 side=side)
    tk = FLASH_TK
    vt_tiles = qkv[C_V0:].reshape(C_KV_HEADS, n_tok // tk, tk, hd).transpose(0, 1, 3, 2)
    ones = jnp.ones((C_KV_HEADS, n_tok // tk, BF16_SUBLANES, tk), BF16)
    vt_tiles = jnp.concatenate([vt_tiles, ones], axis=2)

    def per_seq_max(a):
        return jnp.concatenate([a[:, t0:t0 + n_b * s_n].reshape(a.shape[0], n_b, s_n).max(-1)
                                for t0, n_b, s_n in geom.groups()], axis=1)

    n_g = C_Q_HEADS // C_KV_HEADS
    k_f32 = qk[C_K0:, :, :hd].astype(F32)
    k_max = per_seq_max(jnp.sqrt(jnp.sum(k_f32 * k_f32, axis=-1))).astype(BF16)
    q_max = per_seq_max(-qk[:C_K0, :, hd].astype(F32))
    bound = jnp.max(q_max.reshape(C_KV_HEADS, n_g, -1) * k_max.astype(F32)[:, None, :])
    kmx = jnp.zeros((C_KV_HEADS, geom.n_seq, 1, 2 * hd), BF16).at[:, :, 0, hd].set(k_max)
    o_buf = jnp.zeros((n_tok, C_Q_HEADS * hd), BF16)

    def attend(kmx_or_none):
        o = o_buf
        seq_idx0 = 0
        for tok0, n_b, s_n in geom.groups():
            o = flash_attention(qk, vt_tiles, o, tok0, n_b, s_n, seq_idx0, kmx_or_none)
            seq_idx0 += n_b
        return o

    o = lax.cond(bound <= FLASH_SAFE_BOUND, lambda: attend(kmx), lambda: attend(None))
    return outproj(x, mod, o, w_out.astype(BF16), layer, geom)


def kernel(x_prompt, x_sample, c_prompt, c_sample, w_ada, b_ada, g_norm_mix, g_norm_ffn, w_in_ab, w_out_ab, g_qn_a, g_kn_a, rpb_a, g_qn_b, g_kn_b, w_in_c, w_out_c, g_qn_c, g_kn_c, w_router, b_router, w_gate_up, b_gate_up, w_down, b_down):
    b_p, s_p, d = x_prompt.shape
    b_d, s_d, _ = x_sample.shape
    geom = Geom(b_p, s_p, b_d, s_d)
    depth = w_ada.shape[0]
    x = jnp.concatenate([x_prompt.reshape(-1, d), x_sample.reshape(-1, d)], axis=0)
    mod = ada_modulation(jnp.concatenate([c_prompt, c_sample], axis=0), w_ada, b_ada)
    for layer in range(depth):
        i = layer // 2
        if layer % 2 == 0:
            x = mixer_ab_layer(x, mod, g_norm_mix, w_in_ab[i], w_out_ab[i], g_qn_a[i], g_kn_a[i],
                               rpb_a[i], g_qn_b[i], g_kn_b[i], layer, geom)
        else:
            x = mixer_c_layer(x, mod, g_norm_mix, w_in_c[i], w_out_c[i], g_qn_c[i], g_kn_c[i],
                              layer, geom)
        x = moe_layer(x, mod, g_norm_ffn, w_router, b_router, w_gate_up, b_gate_up, w_down,
                      b_down, layer, geom)
    y_prompt = x[:geom.n_p].reshape(b_p, s_p, d)
    y_sample = x[geom.n_p:].reshape(b_d, s_d, d)
    return (y_prompt, y_sample)
```

```python
import functools
import math
from typing import NamedTuple

import jax
import jax.numpy as jnp
from jax import lax
from jax.experimental import pallas as pl
from jax.experimental.pallas import tpu as pltpu

F32 = jnp.float32
BF16 = jnp.bfloat16
HIGHEST = lax.Precision.HIGHEST

GRID_W = 64
HEAD_DIM = 64
ROPE_THETA = 10000.0
EPS = 1e-6
NEG = -1e30
NA_HEADS = 8
NA_KH = 8
NA_KW = 16
DIL_HEADS = 8
DIL_PATTERNS = ((128, 1), (512, 4), (2048, 16))
N_DIL = len(DIL_PATTERNS)
C_Q_HEADS = 16
C_KV_HEADS = 4
N_EXPERTS = 32
TOP_K = 4
SWIGLU_LIMIT = 7.0
SWIGLU_ALPHA = 1.702
Q_SCALE = HEAD_DIM ** -0.5
LOG2_E = math.log2(math.e)
BF16_SUBLANES = 16
LANES = 128
MXU_TILE = 256

V7X_VMEM_BYTES = 64 * 1024 * 1024
VMEM_LIMIT = V7X_VMEM_BYTES - 8 * 1024 * 1024

TOKEN_TILE = 1024
PREP_TILE = 2048
NA_BLOCK = 8 * GRID_W
NA_ROWS_PER_TRIP = 4
DIL_QBLOCK = 128
DIL_SIDE = 64
DIL_CHUNK = 1024
DIL_PROBLEMS_PER_TRIP = 4
FLASH_TQ = 256
FLASH_TK = 512
FLASH_UNROLL = 8
FLASH_BOUNDED_UNROLL = 16
FLASH_LOOKAHEAD = 6
FLASH_SAFE_BOUND = 60.0
FLASH_BOUND_MARGIN = 1.0 + 2.0 ** -6
MOE_BLOCK = 512
MOE_TOKEN_TILE = 512
MOE_PARTS = 2

NA_Q0, NA_K0, NA_V0 = 0, NA_HEADS, 2 * NA_HEADS
C_K0 = C_Q_HEADS
C_V0 = C_Q_HEADS + C_KV_HEADS


class Geom(NamedTuple):
    b_p: int
    s_p: int
    b_d: int
    s_d: int

    @property
    def n_p(self):
        return self.b_p * self.s_p

    @property
    def n_tok(self):
        return self.n_p + self.b_d * self.s_d

    @property
    def n_seq(self):
        return self.b_p + self.b_d

    def groups(self):
        return ((0, self.b_p, self.s_p), (self.n_p, self.b_d, self.s_d))

    def seq_of_token(self, t0):
        return jnp.where(t0 < self.n_p, t0 // self.s_p,
                         self.b_p + (t0 - self.n_p) // self.s_d)

    def seq_span(self, t0):
        in_p = t0 < self.n_p
        start = jnp.where(in_p, (t0 // self.s_p) * self.s_p,
                          self.n_p + ((t0 - self.n_p) // self.s_d) * self.s_d)
        return start, jnp.where(in_p, self.s_p, self.s_d)


def _params(semantics, vmem=None):
    return pltpu.CompilerParams(dimension_semantics=semantics,
                                vmem_limit_bytes=vmem)


def _ada_kernel(c_ref, w_ref, b_ref, o_ref):
    c = c_ref[...]
    a = c / (1.0 + jnp.exp(-c))
    o_ref[...] = jnp.dot(a, w_ref[...], precision=HIGHEST,
                         preferred_element_type=F32) + b_ref[...]


def ada_modulation(c_all, w_ada, b_ada):
    depth, d, d6 = w_ada.shape
    n_seq = c_all.shape[0]
    rows = 8
    c_pad = jnp.zeros((rows, d), F32).at[:n_seq].set(c_all)
    out = pl.pallas_call(
        _ada_kernel,
        grid=(depth, d6 // d),
        in_specs=[
            pl.BlockSpec((rows, d), lambda l, j: (0, 0)),
            pl.BlockSpec((None, d, d), lambda l, j: (l, 0, j)),
            pl.BlockSpec((None, 1, d), lambda l, j: (l, 0, j)),
        ],
        out_specs=pl.BlockSpec((None, rows, d), lambda l, j: (l, 0, j)),
        out_shape=jax.ShapeDtypeStruct((depth, rows, d6), F32),
        compiler_params=_params(("arbitrary", "arbitrary")),
        name="ada_modulation",
    )(c_pad, w_ada, b_ada.reshape(depth, 1, d6))
    return out[:, :n_seq].reshape(depth, n_seq, 6, d)


def _modulated_norm(x, g, shift, scale):
    r = lax.rsqrt(jnp.mean(x * x, axis=-1, keepdims=True) + EPS)
    return (x * r * g) * (1.0 + scale) + shift


def _norm_proj_kernel(x_ref, mod_ref, g_ref, w_ref, o_ref, *scratch, dil):
    h = _modulated_norm(x_ref[...], g_ref[...], mod_ref[0:1, :], mod_ref[1:2, :])
    acc = jnp.dot(h.astype(BF16), w_ref[...], preferred_element_type=F32)
    tm = x_ref.shape[0]
    n_h, _, _, hd = o_ref.shape
    if dil == 1:
        for c in range(n_h):
            o_ref[c, 0] = acc[:, c * hd:(c + 1) * hd].astype(o_ref.dtype)
        return
    acc_ref, = scratch
    for s in range(n_h // 2):
        acc_ref[s] = acc[:, s * LANES:(s + 1) * LANES]
    for s in range(n_h // 2):
        for r in range(dil):
            pair = acc_ref[s, pl.ds(r, tm // dil, stride=dil), :].astype(o_ref.dtype)
            o_ref[2 * s, r] = pair[:, :hd]
            o_ref[2 * s + 1, r] = pair[:, hd:]


def norm_proj(x, mod, g_norm, w_bf16, layer, geom, dil):
    n_tok, d = x.shape
    n_out = w_bf16.shape[1]
    tm = TOKEN_TILE
    hd = HEAD_DIM
    n_h = n_out // hd
    scratch = [] if dil == 1 else [pltpu.VMEM((n_h // 2, tm, LANES), F32)]
    return pl.pallas_call(
        functools.partial(_norm_proj_kernel, dil=dil),
        grid=(n_tok // tm,),
        in_specs=[
            pl.BlockSpec((tm, d), lambda i: (i, 0)),
            pl.BlockSpec((None, None, 6, d), lambda i: (layer, geom.seq_of_token(i * tm), 0, 0)),
            pl.BlockSpec((None, 1, d), lambda i: (layer, 0, 0)),
            pl.BlockSpec((d, n_out), lambda i: (0, 0)),
        ],
        out_specs=pl.BlockSpec((n_h, dil, tm // dil, hd), lambda i: (0, 0, i, 0)),
        out_shape=jax.ShapeDtypeStruct((n_h, dil, n_tok // dil, hd), BF16),
        scratch_shapes=scratch,
        compiler_params=_params(("parallel",), VMEM_LIMIT),
        name="norm_proj",
    )(x, mod, g_norm.reshape(-1, 1, d), w_bf16)


def _prep_kernel(x_ref, g_ref, *rest, rope, aug):
    o_ref = rest[-1]
    hb, ts, hd = x_ref.shape
    x = x_ref[...].astype(F32)
    r = lax.rsqrt(jnp.mean(x * x, axis=-1, keepdims=True) + EPS)
    y = x * r * g_ref[...]
    if rope:
        cos_ref, sin_ref, p_ref = rest[:3]
        yr = jnp.dot(y.reshape(hb * ts, hd).astype(BF16), p_ref[...],
                     preferred_element_type=F32).reshape(hb, ts, hd)
        y = y * cos_ref[...] + yr * sin_ref[...]
    if aug:
        side_ref = rest[-2]
        norm = jnp.sqrt(jnp.sum(y * y, axis=-1, keepdims=True))
        y = jnp.concatenate([y, norm * side_ref[...]], axis=-1)
    o_ref[...] = y.astype(o_ref.dtype)


def qk_prep(x, gains, hb, n_blocks, tables=None, side=None):
    n_h, n_tok, hd = x.shape
    ts = PREP_TILE
    rope = tables is not None
    aug = side is not None
    n_used = n_blocks * hb
    blk = pl.BlockSpec((hb, ts, hd), lambda i, s: (i, s, 0))
    per_head = pl.BlockSpec((hb, 1, hd), lambda i, s: (i, 0, 0))
    in_specs = [blk, per_head]
    args = [x, gains.reshape(n_used, 1, hd).astype(F32)]
    if rope:
        tab = pl.BlockSpec((ts, hd), lambda i, s: (s, 0))
        in_specs += [tab, tab, pl.BlockSpec((hd, hd), lambda i, s: (0, 0))]
        args += list(tables)
    if aug:
        in_specs.append(per_head)
        args.append(side.reshape(n_used, 1, hd).astype(F32))
        out_specs = pl.BlockSpec((hb, ts, 2 * hd), lambda i, s: (i, s, 0))
        out_shape = jax.ShapeDtypeStruct((n_used, n_tok, 2 * hd), BF16)
    else:
        out_specs = blk
        out_shape = jax.ShapeDtypeStruct(x.shape, BF16)
    return pl.pallas_call(
        functools.partial(_prep_kernel, rope=rope, aug=aug),
        grid=(n_blocks, n_tok // ts),
        in_specs=in_specs,
        out_specs=out_specs,
        out_shape=out_shape,
        input_output_aliases={} if aug else {0: 0},
        compiler_params=_params(("parallel", "parallel")),
        name="qk_prep",
    )(*args)


def _rope_angles(pos, dim):
    inv = ROPE_THETA ** (-jnp.arange(0, dim, 2, dtype=F32) / dim)
    ang = pos.astype(F32)[:, None] * inv[None, :]
    ang = jnp.concatenate([ang, ang], axis=-1)
    return jnp.cos(ang), jnp.sin(ang)


def _rotate_half_matrix(dim, n_blocks):
    half = dim // 2
    i = jnp.arange(dim * n_blocks)
    blk, w = i // dim, i % dim
    src = blk * dim + jnp.where(w < half, w + half, w - half)
    sign = jnp.where(w < half, -1.0, 1.0)
    p = jnp.zeros((dim * n_blocks, dim * n_blocks), F32).at[src, i].set(sign)
    return p.astype(BF16)


def _na_bias_table(rpb):
    n_h = rpb.shape[0]
    qc = jnp.arange(GRID_W)[:, None]
    kc = jnp.arange(GRID_W)[None, :]
    c0 = jnp.clip(qc - NA_KW // 2, 0, GRID_W - NA_KW)
    ok = (kc >= c0) & (kc < c0 + NA_KW)
    n_dr = rpb.shape[1]
    period = 2 * GRID_W
    ext = jnp.concatenate([rpb[..., NA_KW - 1:],
                           jnp.zeros((n_h, n_dr, period - (2 * NA_KW - 1)), rpb.dtype),
                           rpb[..., :NA_KW - 1]], axis=-1)
    toep = jnp.tile(ext, (1, 1, GRID_W))[..., :GRID_W * (period - 1)]
    toep = toep.reshape(n_h, n_dr, GRID_W, period - 1)[..., :GRID_W]
    toep = jnp.where(ok[None, None], toep, NEG)
    bias = jnp.stack([toep[:, NA_KH - 1 - d0:2 * NA_KH - 1 - d0] for d0 in range(NA_KH)])
    bias = bias.transpose(0, 1, 3, 2, 4).reshape(NA_KH, n_h, GRID_W, NA_KH * GRID_W)
    return bias.astype(BF16)


def _na_kernel(q_ref, kp_ref, kc_ref, kn_ref, vp_ref, vc_ref, vn_ref, tbl_ref,
               o_ref, ks_ref, vs_ref, *, geom):
    blk = NA_BLOCK
    j = pl.program_id(0)
    seq_start, seq_len = geom.seq_span(j * blk)
    rows = seq_len // GRID_W
    jl = j - seq_start // blk
    ks_ref[:, 0:blk, :] = kp_ref[...]
    ks_ref[:, blk:2 * blk, :] = kc_ref[...]
    ks_ref[:, 2 * blk:3 * blk, :] = kn_ref[...]
    vs_ref[:, 0:blk, :] = vp_ref[...]
    vs_ref[:, blk:2 * blk, :] = vc_ref[...]
    vs_ref[:, 2 * blk:3 * blk, :] = vn_ref[...]
    n_keys = NA_KH * GRID_W

    def body(ii, carry):
        probs = []
        for u in range(NA_ROWS_PER_TRIP):
            i = ii * NA_ROWS_PER_TRIP + u
            r = jl * NA_KH + i
            r0 = jnp.clip(r - NA_KH // 2, 0, rows - NA_KH)
            off = pl.multiple_of((r0 - (jl - 1) * NA_KH) * GRID_W, GRID_W)
            d0 = r - r0
            qoff = pl.multiple_of(i * GRID_W, GRID_W)
            for h in range(NA_HEADS):
                q = q_ref[h, pl.ds(qoff, GRID_W), :]
                k = ks_ref[h, pl.ds(off, n_keys), :]
                s = lax.dot_general(q, k, (((1,), (1,)), ((), ())), preferred_element_type=F32)
                probs.append((h, off, qoff, s + tbl_ref[d0, h].astype(F32)))
        soft = []
        for h, off, qoff, s in probs:
            m = jnp.max(s, axis=-1, keepdims=True)
            p = jnp.exp(s - m)
            soft.append((h, off, qoff, p.astype(BF16), jnp.sum(p, axis=-1, keepdims=True)))
        for h, off, qoff, p, l in soft:
            v = vs_ref[h, pl.ds(off, n_keys), :]
            o = jnp.dot(p, v, preferred_element_type=F32) / l
            o_ref[h, pl.ds(qoff, GRID_W), :] = o.astype(o_ref.dtype)
        return carry

    lax.fori_loop(0, NA_KH // NA_ROWS_PER_TRIP, body, 0)


def neighborhood_attention(qkv, tbl, geom):
    _, n_tok, hd = qkv.shape
    n_h = NA_HEADS
    blk = NA_BLOCK
    nb = n_tok // blk

    def spec(head0, shift):
        hb = head0 // n_h
        return pl.BlockSpec((n_h, blk, hd), lambda j: (hb, jnp.clip(j + shift, 0, nb - 1), 0))

    return pl.pallas_call(
        functools.partial(_na_kernel, geom=geom),
        grid=(nb,),
        in_specs=[spec(NA_Q0, 0), spec(NA_K0, -1), spec(NA_K0, 0), spec(NA_K0, 1),
                  spec(NA_V0, -1), spec(NA_V0, 0), spec(NA_V0, 1),
                  pl.BlockSpec(tbl.shape, lambda j: (0, 0, 0, 0))],
        out_specs=pl.BlockSpec((n_h, blk, hd), lambda j: (0, j, 0)),
        out_shape=jax.ShapeDtypeStruct((n_h, n_tok, hd), BF16),
        scratch_shapes=[pltpu.VMEM((n_h, 3 * blk, hd), BF16),
                        pltpu.VMEM((n_h, 3 * blk, hd), BF16)],
        compiler_params=_params(("parallel",), VMEM_LIMIT),
        name="neighborhood_attention",
    )(qkv, qkv, qkv, qkv, qkv, qkv, qkv, tbl)


def _dil_kernel(q_ref, kp_ref, kc_ref, kn_ref, vp_ref, vc_ref, vn_ref,
                o_ref, lse_ref, ks_ref, vs_ref, *, geom, dil):
    c = pl.program_id(0)
    n_h, chunk, hd = q_ref.shape
    side = DIL_SIDE
    qb = DIL_QBLOCK
    kb = qb + 2 * side
    ks_ref[:, 0:side, :] = kp_ref[...]
    ks_ref[:, side:side + chunk, :] = kc_ref[...]
    ks_ref[:, side + chunk:, :] = kn_ref[...]
    vs_ref[:, 0:side, :] = vp_ref[...]
    vs_ref[:, side:side + chunk, :] = vc_ref[...]
    vs_ref[:, side + chunk:, :] = vn_ref[...]
    plane_rows = geom.n_tok // dil
    t0 = ((c * chunk) % plane_rows) * dil
    seq_start, seq_len = geom.seq_span(t0)
    row0 = (t0 - seq_start) // dil
    cls_rows = seq_len // dil
    jq = lax.broadcasted_iota(jnp.int32, (qb, kb), 0)
    jk = lax.broadcasted_iota(jnp.int32, (qb, kb), 1) - side
    near_bias = jnp.where(jnp.abs(jk - jq) <= side, 0.0, NEG)
    before_bias = jnp.where(jk < 0, NEG, 0.0)
    after_bias = jnp.where(jk >= qb, NEG, 0.0)
    n_trip = min(DIL_PROBLEMS_PER_TRIP, chunk // qb)
    trips_per_head = chunk // (qb * n_trip)
    assert chunk % (qb * n_trip) == 0

    def body(it, carry):
        h = it // trips_per_head
        ii = it % trips_per_head
        probs = []
        for u in range(n_trip):
            i = ii * n_trip + u
            qoff = pl.multiple_of(i * qb, qb)
            p0 = row0 + i * qb
            bias = near_bias + jnp.where(p0 == 0, before_bias, 0.0)
            bias = bias + jnp.where(p0 + qb == cls_rows, after_bias, 0.0)
            q = q_ref[h, pl.ds(qoff, qb), :]
            k = ks_ref[h, pl.ds(qoff, kb), :]
            s = lax.dot_general(q, k, (((1,), (1,)), ((), ())), preferred_element_type=F32)
            probs.append((qoff, s + bias))
        soft = []
        for qoff, s in probs:
            m = jnp.max(s, axis=-1, keepdims=True)
            p = jnp.exp(s - m)
            soft.append((qoff, p.astype(BF16), m, jnp.sum(p, axis=-1, keepdims=True)))
        for qoff, p, m, l in soft:
            v = vs_ref[h, pl.ds(qoff, kb), :]
            o = jnp.dot(p, v, preferred_element_type=F32) / l
            o_ref[h, pl.ds(qoff, qb), :] = o.astype(o_ref.dtype)
            lse_ref[h, pl.ds(qoff, qb), :] = jnp.broadcast_to(m + jnp.log(l), (qb, hd))
        return carry

    lax.fori_loop(0, n_h * trips_per_head, body, 0)


def dilated_attention(qkv, dil, geom):
    _, n_tok, hd = qkv.shape
    n_h = DIL_HEADS
    chunk = min(DIL_CHUNK, min(geom.s_p, geom.s_d) // dil)
    assert chunk % DIL_QBLOCK == 0
    per = chunk // DIL_SIDE
    n_side = n_tok // DIL_SIDE

    def cur(hb):
        return pl.BlockSpec((n_h, chunk, hd), lambda c: (hb, c, 0))

    def prev(hb):
        return pl.BlockSpec((n_h, DIL_SIDE, hd), lambda c: (hb, jnp.maximum(c * per - 1, 0), 0))

    def nxt(hb):
        return pl.BlockSpec((n_h, DIL_SIDE, hd),
                            lambda c: (hb, jnp.minimum((c + 1) * per, n_side - 1), 0))

    return pl.pallas_call(
        functools.partial(_dil_kernel, geom=geom, dil=dil),
        grid=(n_tok // chunk,),
        in_specs=[cur(0), prev(1), cur(1), nxt(1), prev(2), cur(2), nxt(2)],
        out_specs=[cur(0), cur(0)],
        out_shape=[jax.ShapeDtypeStruct((n_h, n_tok, hd), BF16),
                   jax.ShapeDtypeStruct((n_h, n_tok, hd), F32)],
        scratch_shapes=[pltpu.VMEM((n_h, chunk + 2 * DIL_SIDE, hd), BF16),
                        pltpu.VMEM((n_h, chunk + 2 * DIL_SIDE, hd), BF16)],
        compiler_params=_params(("parallel",), VMEM_LIMIT),
        name="dilated_attention",
    )(qkv, qkv, qkv, qkv, qkv, qkv, qkv)


def _to_natural(src_ref, dst_ref):
    n_h, dil, n, _ = src_ref.shape
    for h in range(0, n_h, 2):
        for r in range(dil):
            pair = jnp.concatenate([src_ref[h, r].astype(F32), src_ref[h + 1, r].astype(F32)],
                                   axis=-1)
            if dil == 1:
                dst_ref[h // 2] = pair
            else:
                dst_ref[h // 2, pl.ds(r, n, stride=dil), :] = pair


def _outproj_ab_kernel(x_ref, mod_ref, oa_ref, od0_ref, od1_ref, od2_ref,
                       l0_ref, l1_ref, l2_ref, w_ref, o_ref, *scratch):
    so = scratch[:N_DIL]
    sl = scratch[N_DIL:]
    for src, dst in zip((od0_ref, od1_ref, od2_ref, l0_ref, l1_ref, l2_ref), so + sl):
        _to_natural(src, dst)
    n_h = oa_ref.shape[0]
    slabs = [jnp.concatenate([oa_ref[h], oa_ref[h + 1]], axis=-1) for h in range(0, n_h, 2)]
    for s in range(n_h // 2):
        l0, l1, l2 = sl[0][s], sl[1][s], sl[2][s]
        m = jnp.maximum(jnp.maximum(l0, l1), l2)
        e0, e1, e2 = jnp.exp(l0 - m), jnp.exp(l1 - m), jnp.exp(l2 - m)
        ob = (e0 * so[0][s] + e1 * so[1][s] + e2 * so[2][s]) / (e0 + e1 + e2)
        slabs.append(ob.astype(BF16))
    cat = jnp.concatenate(slabs, axis=-1)
    res = jnp.dot(cat, w_ref[...], preferred_element_type=F32)
    o_ref[...] = x_ref[...] + mod_ref[2:3, :] * res


def outproj_ab(x, mod, oa, od, lse, w_bf16, layer, geom):
    n_tok, d = x.shape
    n_h, _, hd = oa.shape
    tm = TOKEN_TILE // 2

    def cls(a):
        dil = a.shape[1]
        return pl.BlockSpec((n_h, dil, tm // dil, hd), lambda i: (0, 0, i, 0))

    return pl.pallas_call(
        _outproj_ab_kernel,
        grid=(n_tok // tm,),
        in_specs=[
            pl.BlockSpec((tm, d), lambda i: (i, 0)),
            pl.BlockSpec((None, None, 6, d), lambda i: (layer, geom.seq_of_token(i * tm), 0, 0)),
            pl.BlockSpec((n_h, tm, hd), lambda i: (0, i, 0)),
            *[cls(a) for a in od], *[cls(a) for a in lse],
            pl.BlockSpec(w_bf16.shape, lambda i: (0, 0)),
        ],
        out_specs=pl.BlockSpec((tm, d), lambda i: (i, 0)),
        out_shape=jax.ShapeDtypeStruct(x.shape, F32),
        scratch_shapes=[pltpu.VMEM((n_h // 2, tm, LANES), F32) for _ in range(2 * N_DIL)],
        compiler_params=_params(("parallel",), VMEM_LIMIT),
        name="outproj_ab",
    )(x, mod, oa, *od, *lse, w_bf16)


def _outproj_kernel(x_ref, mod_ref, o_in_ref, w_ref, o_ref):
    res = jnp.dot(o_in_ref[...], w_ref[...], preferred_element_type=F32)
    o_ref[...] = x_ref[...] + mod_ref[2:3, :] * res


def outproj(x, mod, o_in, w_bf16, layer, geom):
    n_tok, d = x.shape
    tm = TOKEN_TILE
    return pl.pallas_call(
        _outproj_kernel,
        grid=(n_tok // tm,),
        in_specs=[
            pl.BlockSpec((tm, d), lambda i: (i, 0)),
            pl.BlockSpec((None, None, 6, d), lambda i: (layer, geom.seq_of_token(i * tm), 0, 0)),
            pl.BlockSpec((tm, o_in.shape[1]), lambda i: (i, 0)),
            pl.BlockSpec(w_bf16.shape, lambda i: (0, 0)),
        ],
        out_specs=pl.BlockSpec((tm, d), lambda i: (i, 0)),
        out_shape=jax.ShapeDtypeStruct(x.shape, F32),
        compiler_params=_params(("parallel",), VMEM_LIMIT),
        name="outproj",
    )(x, mod, o_in, w_bf16)


def _flash_write_out(acc_ref, o_ref, n_g, tq):
    hd = HEAD_DIM
    acc = acc_ref[...]
    o = acc[:hd] / acc[hd:hd + 1]
    pad = jnp.zeros((LANES - hd, tq), F32)
    for g in range(n_g):
        t = jnp.concatenate([o[:, g * tq:(g + 1) * tq], pad], axis=0).T
        o_ref[:, g * hd:(g + 1) * hd] = t[:, :hd].astype(o_ref.dtype)


def _flash_bounded_kernel(q_ref, k_ref, vt_ref, kmx_ref, _o_in_ref, o_ref, acc_ref, *, unroll):
    n_g, tq, kw = q_ref.shape
    n_kt, tk, _ = k_ref.shape
    acc_ref[...] = jnp.zeros(acc_ref.shape, F32)
    kmx = kmx_ref[...]
    n_col = (n_g * tq) // MXU_TILE

    n_kc = tk // MXU_TILE

    def qk(j, n, kc):
        rows = slice(kc * MXU_TILE, (kc + 1) * MXU_TILE)
        q_n = q_ref[...].reshape(n_g * tq, kw)[n * MXU_TILE:(n + 1) * MXU_TILE]
        return lax.dot_general(k_ref[j, rows, :] + kmx, q_n, (((1,), (1,)), ((), ())),
                               preferred_element_type=F32)

    def body(jj, carry):
        units = [(unroll * jj + u, n, kc) for u in range(unroll) for n in range(n_col)
                 for kc in range(n_kc)]
        pending = [qk(*unit) for unit in units[:FLASH_LOOKAHEAD]]
        for idx, (j, n, kc) in enumerate(units):
            if idx + FLASH_LOOKAHEAD < len(units):
                pending.append(qk(*units[idx + FLASH_LOOKAHEAD]))
            cols = slice(n * MXU_TILE, (n + 1) * MXU_TILE)
            rows = slice(kc * MXU_TILE, (kc + 1) * MXU_TILE)
            p = jnp.exp2(pending.pop(0)).astype(BF16)
            acc_ref[:, cols] += jnp.dot(vt_ref[j, :, rows], p, preferred_element_type=F32)
        return carry

    lax.fori_loop(0, n_kt // unroll, body, 0)
    _flash_write_out(acc_ref, o_ref, n_g, tq)


def _flash_kernel(q_ref, k_ref, vt_ref, _o_in_ref, o_ref, sa_ref, sb_ref, mxa_ref, mxb_ref,
                  m_ref, acc_ref, *, unroll):
    n_g, tq, hd = q_ref.shape
    n_kt = k_ref.shape[0]
    m_ref[...] = jnp.full(m_ref.shape, -jnp.inf, F32)
    acc_ref[...] = jnp.zeros(acc_ref.shape, F32)
    n_col = (n_g * tq) // MXU_TILE

    def scores(j, s_ref, mx_ref, n):
        cols = slice(n * MXU_TILE, (n + 1) * MXU_TILE)
        q_n = q_ref[...].reshape(n_g * tq, hd)[n * MXU_TILE:(n + 1) * MXU_TILE]
        s = lax.dot_general(k_ref[j], q_n, (((1,), (1,)), ((), ())),
                            preferred_element_type=F32)
        s_ref[:, cols] = s
        mx_ref[:, cols] = jnp.max(s, axis=0, keepdims=True)

    def consume(j, s_ref, mx_ref, n):
        cols = slice(n * MXU_TILE, (n + 1) * MXU_TILE)
        m_old = m_ref[:, cols]
        m_new = jnp.maximum(m_old, mx_ref[:, cols])
        alpha = jnp.exp2(m_old - m_new)
        m_ref[:, cols] = m_new
        tk = s_ref.shape[0]
        acc = alpha * acc_ref[:, cols]
        for kc in range(tk // MXU_TILE):
            rows = slice(kc * MXU_TILE, (kc + 1) * MXU_TILE)
            p = jnp.exp2(s_ref[rows, cols] - m_new).astype(BF16)
            acc = acc + jnp.dot(vt_ref[j, :, rows], p, preferred_element_type=F32)
        acc_ref[:, cols] = acc

    for n in range(n_col):
        scores(0, sa_ref, mxa_ref, n)

    def body(jj, carry):
        j = unroll * jj
        for u in range(0, unroll, 2):
            for n in range(n_col):
                scores(j + u + 1, sb_ref, mxb_ref, n)
                consume(j + u, sa_ref, mxa_ref, n)
            for n in range(n_col):
                scores(jnp.minimum(j + u + 2, n_kt - 1), sa_ref, mxa_ref, n)
                consume(j + u + 1, sb_ref, mxb_ref, n)
        return carry

    lax.fori_loop(0, n_kt // unroll, body, 0)
    _flash_write_out(acc_ref, o_ref, n_g, tq)


def flash_attention(qk, vt_tiles, o_buf, tok0, n_b, s_n, seq_idx0, kmx=None):
    n_all, n_tok, kw = qk.shape
    n_kv, _, hv, tk = vt_tiles.shape
    n_g = C_Q_HEADS // n_kv
    hd = HEAD_DIM
    tq = FLASH_TQ
    nq = s_n // tq
    kt_per = s_n // tk
    unroll = min(FLASH_UNROLL if kmx is None else FLASH_BOUNDED_UNROLL, kt_per)
    assert kt_per % unroll == 0 and unroll % 2 == 0 and tok0 % s_n == 0
    wq = n_g * tq
    k_view = qk.reshape(n_all, n_tok // tk, tk, kw)
    seq0 = tok0 // s_n
    in_specs = [
        pl.BlockSpec((n_g, tq, kw), lambda b, h, i: (h, (tok0 + b * s_n) // tq + i, 0)),
        pl.BlockSpec((None, kt_per, tk, kw), lambda b, h, i: (C_K0 + h, seq0 + b, 0, 0)),
        pl.BlockSpec((None, kt_per, hv, tk), lambda b, h, i: (h, seq0 + b, 0, 0)),
    ]
    args = [qk, k_view, vt_tiles]
    if kmx is None:
        body = functools.partial(_flash_kernel, unroll=unroll)
        scratch = [pltpu.VMEM((tk, wq), F32), pltpu.VMEM((tk, wq), F32),
                   pltpu.VMEM((1, wq), F32), pltpu.VMEM((1, wq), F32),
                   pltpu.VMEM((1, wq), F32), pltpu.VMEM((hv, wq), F32)]
    else:
        body = functools.partial(_flash_bounded_kernel, unroll=unroll)
        scratch = [pltpu.VMEM((hv, wq), F32)]
        in_specs.append(pl.BlockSpec((None, None, 1, kw), lambda b, h, i: (h, seq_idx0 + b, 0, 0)))
        args.append(kmx)
    in_specs.append(pl.BlockSpec(memory_space=pl.ANY))
    args.append(o_buf)
    return pl.pallas_call(
        body,
        grid=(n_b, n_kv, nq),
        in_specs=in_specs,
        out_specs=pl.BlockSpec((tq, n_g * hd), lambda b, h, i: ((tok0 + b * s_n) // tq + i, h)),
        out_shape=jax.ShapeDtypeStruct(o_buf.shape, o_buf.dtype),
        input_output_aliases={len(args) - 1: 0},
        scratch_shapes=scratch,
        compiler_params=_params(("parallel", "parallel", "arbitrary"), VMEM_LIMIT),
        name="flash_attention",
    )(*args)


def _router_kernel(x_ref, mod_ref, g_ref, wr_ref, br_ref, h_ref, idx_ref, gate_ref, cnt_ref,
                   run_ref):
    @pl.when(pl.program_id(0) == 0)
    def _():
        run_ref[...] = jnp.zeros(run_ref.shape, F32)

    h = _modulated_norm(x_ref[...], g_ref[...], mod_ref[3:4, :], mod_ref[4:5, :])
    h_ref[...] = h.astype(BF16)
    logits = lax.dot_general(wr_ref[...], h, (((1,), (1,)), ((), ())), precision=HIGHEST,
                             preferred_element_type=F32) + br_ref[...]
    n_e, tm = logits.shape
    iota = lax.broadcasted_iota(jnp.int32, (n_e, tm), 0)
    vals, ids = [], []
    cur = logits
    for _ in range(TOP_K):
        m = jnp.max(cur, axis=0, keepdims=True)
        am = jnp.min(jnp.where(cur == m, iota, n_e), axis=0, keepdims=True)
        vals.append(m)
        ids.append(am)
        cur = jnp.where(iota == am, -jnp.inf, cur)
    es = [jnp.exp(v - vals[0]) for v in vals]
    den = es[0] + es[1] + es[2] + es[3]
    chosen = jnp.zeros((n_e, tm), F32)
    for kk in range(TOP_K):
        chosen = chosen + jnp.where(iota == ids[kk], 1.0, 0.0)
    earlier = lax.broadcasted_iota(jnp.int32, (tm, tm), 0)
    token = lax.broadcasted_iota(jnp.int32, (tm, tm), 1)
    tri = jnp.where(earlier < token, 1.0, 0.0).astype(BF16)
    before = jnp.dot(chosen.astype(BF16), tri, preferred_element_type=F32) + run_ref[...]
    run_ref[...] = run_ref[...] + jnp.sum(chosen, axis=1, keepdims=True)
    cnt_ref[...] = run_ref[...]
    row = lax.broadcasted_iota(jnp.int32, idx_ref.shape, 0)
    idx_out = jnp.zeros(idx_ref.shape, jnp.int32)
    gate_out = jnp.zeros(gate_ref.shape, F32)
    for kk in range(TOP_K):
        rank = jnp.sum(jnp.where(iota == ids[kk], before, 0.0), axis=0, keepdims=True)
        idx_out = jnp.where(row == kk, ids[kk], idx_out)
        idx_out = jnp.where(row == TOP_K + kk, rank.astype(jnp.int32), idx_out)
        gate_out = jnp.where(row == kk, es[kk] / den, gate_out)
    idx_ref[...] = idx_out
    gate_ref[...] = gate_out


def router(x, mod, g_norm, w_router, b_router, layer, geom, tile0, n_tiles):
    d = x.shape[1]
    tm = MOE_TOKEN_TILE
    n_tok = n_tiles * tm
    return pl.pallas_call(
        _router_kernel,
        grid=(n_tiles,),
        in_specs=[
            pl.BlockSpec((tm, d), lambda i: (tile0 + i, 0)),
            pl.BlockSpec((None, None, 6, d),
                         lambda i: (layer, geom.seq_of_token((tile0 + i) * tm), 0, 0)),
            pl.BlockSpec((None, 1, d), lambda i: (layer, 0, 0)),
            pl.BlockSpec((None, N_EXPERTS, d), lambda i: (layer, 0, 0)),
            pl.BlockSpec((None, N_EXPERTS, 1), lambda i: (layer, 0, 0)),
        ],
        out_specs=[
            pl.BlockSpec((tm, d), lambda i: (i, 0)),
            pl.BlockSpec((2 * TOP_K, tm), lambda i: (0, i)),
            pl.BlockSpec((2 * TOP_K, tm), lambda i: (0, i)),
            pl.BlockSpec((N_EXPERTS, 1), lambda i: (0, 0)),
        ],
        out_shape=[
            jax.ShapeDtypeStruct((n_tok, d), BF16),
            jax.ShapeDtypeStruct((2 * TOP_K, n_tok), jnp.int32),
            jax.ShapeDtypeStruct((2 * TOP_K, n_tok), F32),
            jax.ShapeDtypeStruct((N_EXPERTS, 1), F32),
        ],
        scratch_shapes=[pltpu.VMEM((N_EXPERTS, 1), F32)],
        compiler_params=_params(("arbitrary",), VMEM_LIMIT),
        name="router",
    )(x, mod, g_norm.reshape(-1, 1, d), w_router.transpose(0, 2, 1),
      b_router.reshape(-1, N_EXPERTS, 1))


def _moe_kernel(be_ref, nu_ref, x_ref, wgu_ref, bgu_ref, wd_ref, bd_ref, o_ref,
                wgu_s, wd_s):
    i = pl.program_id(0)
    e = be_ref[i]
    e_prev = be_ref[jnp.maximum(i - 1, 0)]

    @pl.when((i == 0) | (e != e_prev))
    def _():
        wgu_s[...] = wgu_ref[...].astype(BF16)
        wd_s[...] = wd_ref[...].astype(BF16)

    @pl.when(i < nu_ref[0])
    def _():
        d_ff = wd_s.shape[0]
        gu = jnp.dot(x_ref[...], wgu_s[...], preferred_element_type=F32) + bgu_ref[...]
        glu = jnp.minimum(gu[:, :d_ff], SWIGLU_LIMIT)
        lin = jnp.clip(gu[:, d_ff:], -SWIGLU_LIMIT, SWIGLU_LIMIT)
        act = glu / (1.0 + jnp.exp(-SWIGLU_ALPHA * glu)) * (lin + 1.0)
        y = jnp.dot(act.astype(BF16), wd_s[...], preferred_element_type=F32) + bd_ref[...]
        o_ref[...] = y.astype(o_ref.dtype)

    @pl.when(i >= nu_ref[0])
    def _():
        o_ref[...] = jnp.zeros(o_ref.shape, o_ref.dtype)


def moe_experts(xs, blk_e, n_used, w_gu, b_gu, w_down, b_down, layer):
    n_rows, d = xs.shape
    d_ff = w_down.shape[2]
    nblk = n_rows // MOE_BLOCK
    grid_spec = pltpu.PrefetchScalarGridSpec(
        num_scalar_prefetch=2,
        grid=(nblk,),
        in_specs=[
            pl.BlockSpec((MOE_BLOCK, d), lambda i, be, nu: (jnp.minimum(i, nu[0] - 1), 0)),
            pl.BlockSpec((None, None, d, 2 * d_ff), lambda i, be, nu: (layer, be[i], 0, 0)),
            pl.BlockSpec((None, None, 1, 2 * d_ff), lambda i, be, nu: (layer, be[i], 0, 0)),
            pl.BlockSpec((None, None, d_ff, d), lambda i, be, nu: (layer, be[i], 0, 0)),
            pl.BlockSpec((None, None, 1, d), lambda i, be, nu: (layer, be[i], 0, 0)),
        ],
        out_specs=pl.BlockSpec((MOE_BLOCK, d), lambda i, be, nu: (i, 0)),
        scratch_shapes=[pltpu.VMEM((d, 2 * d_ff), BF16), pltpu.VMEM((d_ff, d), BF16)],
    )
    depth, n_e = w_gu.shape[:2]
    return pl.pallas_call(
        _moe_kernel,
        grid_spec=grid_spec,
        out_shape=jax.ShapeDtypeStruct((n_rows, d), BF16),
        compiler_params=_params(("arbitrary",), VMEM_LIMIT),
        name="moe_experts",
    )(blk_e, n_used, xs, w_gu, b_gu.reshape(depth, n_e, 1, 2 * d_ff), w_down,
      b_down.reshape(depth, n_e, 1, d))


def _combine_kernel(x_ref, mod_ref, y_ref, gate_ref, o_ref):
    g = gate_ref[...]
    acc = g[:, 0:1] * y_ref[0].astype(F32)
    for kk in range(1, TOP_K):
        acc = acc + g[:, kk:kk + 1] * y_ref[kk].astype(F32)
    o_ref[...] = x_ref[...] + mod_ref[5:6, :] * acc


def moe_combine(x, mod, yk, gates, layer, geom, tile0, n_tiles):
    d = x.shape[1]
    tm = MOE_TOKEN_TILE
    return pl.pallas_call(
        _combine_kernel,
        grid=(n_tiles,),
        in_specs=[
            pl.BlockSpec((tm, d), lambda i: (tile0 + i, 0)),
            pl.BlockSpec((None, None, 6, d),
                         lambda i: (layer, geom.seq_of_token((tile0 + i) * tm), 0, 0)),
            pl.BlockSpec((TOP_K, tm, d), lambda i: (0, i, 0)),
            pl.BlockSpec((tm, 2 * TOP_K), lambda i: (i, 0)),
        ],
        out_specs=pl.BlockSpec((tm, d), lambda i: (tile0 + i, 0)),
        out_shape=jax.ShapeDtypeStruct(x.shape, F32),
        input_output_aliases={0: 0},
        compiler_params=_params(("parallel",), VMEM_LIMIT),
        name="moe_combine",
    )(x, mod, yk, gates)


def moe_layer(x, mod, g_norm, w_router, b_router, w_gu, b_gu, w_down, b_down, layer, geom):
    n_tiles = x.shape[0] // MOE_TOKEN_TILE
    assert n_tiles % MOE_PARTS == 0
    per = n_tiles // MOE_PARTS
    routed = [_moe_dispatch(x, mod, g_norm, w_router, b_router, w_gu, b_gu, w_down, b_down, layer,
                            geom, part * per, per) for part in range(MOE_PARTS)]
    for part, (yk, gates8) in enumerate(routed):
        x = moe_combine(x, mod, yk, gates8, layer, geom, part * per, per)
    return x


def _moe_dispatch(x, mod, g_norm, w_router, b_router, w_gu, b_gu, w_down, b_down, layer, geom,
                  tile0, n_tiles):
    d = x.shape[1]
    n_tok = n_tiles * MOE_TOKEN_TILE
    h, idx8, gates8, counts = router(x, mod, g_norm, w_router, b_router, layer, geom, tile0,
                                     n_tiles)
    idx = idx8[:TOP_K].T
    rank = idx8[TOP_K:].T
    gates8 = gates8.T
    counts = counts[:, 0].astype(jnp.int32)
    pcounts = ((counts + MOE_BLOCK - 1) // MOE_BLOCK) * MOE_BLOCK
    pend = jnp.cumsum(pcounts)
    pstart = pend - pcounts
    experts = jnp.arange(N_EXPERTS, dtype=jnp.int32)
    dest = rank + jnp.sum(jnp.where(idx[:, :, None] == experts, pstart, 0), axis=-1)
    nblk = (n_tok * TOP_K) // MOE_BLOCK + N_EXPERTS
    n_rows = nblk * MOE_BLOCK
    tok_ids = jnp.broadcast_to(jnp.arange(n_tok, dtype=jnp.int32)[:, None], dest.shape)
    tok = (jnp.arange(n_rows, dtype=jnp.int32) % n_tok).at[dest.reshape(-1)].set(
        tok_ids.reshape(-1), unique_indices=True)
    n_used = (pend[-1] // MOE_BLOCK).astype(jnp.int32).reshape(1)
    blk_start = jnp.arange(nblk, dtype=jnp.int32) * MOE_BLOCK
    blk_last = jnp.minimum(blk_start, pend[-1] - 1)
    blk_e = jnp.sum(pend[None, :] <= blk_last[:, None], axis=-1, dtype=jnp.int32)
    blk_e = jnp.minimum(blk_e, N_EXPERTS - 1)
    xs = jnp.take(h, tok, axis=0, mode="clip")
    yb = moe_experts(xs, blk_e, n_used, w_gu, b_gu, w_down, b_down, layer)
    yk = jnp.take(yb, dest.T.reshape(-1), axis=0, mode="clip").reshape(TOP_K, n_tok, d)
    return yk, gates8


def _seq_positions(geom):
    return jnp.concatenate([jnp.arange(s, dtype=jnp.int32)
                            for _, n_b, s in geom.groups() for _ in range(n_b)])


def _qk_gains(g_q, g_k, n_q, n_k, q_scale):
    hd = g_q.shape[-1]
    return jnp.concatenate([jnp.broadcast_to(g_q * q_scale, (n_q, hd)),
                            jnp.broadcast_to(g_k, (n_k, hd))])


def _dilated_group_weight(w_in, gi):
    blk = DIL_HEADS * HEAD_DIM
    base = 3 * NA_HEADS * HEAD_DIM
    return jnp.concatenate([w_in[:, base + (a * N_DIL + gi) * blk:base + (a * N_DIL + gi + 1) * blk]
                            for a in range(3)], axis=1)


def mixer_ab_layer(x, mod, g_norm, w_in, w_out, g_qn_a, g_kn_a, rpb, g_qn_b, g_kn_b, layer, geom):
    n_tok, d = x.shape
    hd = HEAD_DIM
    w = w_in.astype(BF16)
    n_h = NA_HEADS
    qkv = norm_proj(x, mod, g_norm, w[:, :3 * n_h * hd], layer, geom, dil=1)
    qkv = qk_prep(qkv.reshape(3 * n_h, n_tok, hd), _qk_gains(g_qn_a, g_kn_a, n_h, n_h, Q_SCALE),
                  hb=n_h, n_blocks=2)
    oa = neighborhood_attention(qkv, _na_bias_table(rpb), geom)
    cos, sin = _rope_angles(_seq_positions(geom), hd)
    perm = _rotate_half_matrix(hd, 1)
    od, lse = [], []
    for gi, (window, dil) in enumerate(DIL_PATTERNS):
        assert (window // 2) // dil == DIL_SIDE
        n_h = DIL_HEADS

        def class_major(t):
            return t.reshape(n_tok // dil, dil, hd).transpose(1, 0, 2).reshape(n_tok, hd)

        qkv = norm_proj(x, mod, g_norm, _dilated_group_weight(w, gi), layer, geom, dil=dil)
        qkv = qk_prep(qkv.reshape(3 * n_h, n_tok, hd),
                      _qk_gains(g_qn_b[gi], g_kn_b[gi], n_h, n_h, Q_SCALE), hb=n_h, n_blocks=2,
                      tables=(class_major(cos), class_major(sin), perm))
        o_g, lse_g = dilated_attention(qkv, dil, geom)
        od.append(o_g.reshape(n_h, dil, n_tok // dil, hd))
        lse.append(lse_g.reshape(n_h, dil, n_tok // dil, hd))
    return outproj_ab(x, mod, oa, od, lse, w_out.astype(BF16), layer, geom)


def mixer_c_layer(x, mod, g_norm, w_in, w_out, g_qn, g_kn, layer, geom):
    n_tok, d = x.shape
    hd = HEAD_DIM
    n_heads = C_Q_HEADS + 2 * C_KV_HEADS
    qkv = norm_proj(x, mod, g_norm, w_in.astype(BF16), layer, geom, dil=1).reshape(n_heads, n_tok, hd)
    pos = _seq_positions(geom)
    half = hd // 2
    cr, sr = _rope_angles(pos // GRID_W, half)
    cc, sc = _rope_angles(pos % GRID_W, half)
    tables = (jnp.concatenate([cr, cc], axis=-1), jnp.concatenate([sr, sc], axis=-1),
              _rotate_half_matrix(half, 2))
    side = jnp.zeros((C_Q_HEADS + C_KV_HEADS, hd), F32).at[:C_Q_HEADS, 0].set(-FLASH_BOUND_MARGIN)
    qk = qk_prep(qkv, _qk_gains(g_qn, g_kn, C_Q_HEADS, C_KV_HEADS, Q_SCALE * LOG2_E),
                 hb=C_KV_HEADS, n_blocks=(C_Q_HEADS + C_KV_HEADS) // C_KV_HEADS, tables=tables,
                 side=side)
    tk = FLASH_TK
    vt_tiles = qkv[C_V0:].reshape(C_KV_HEADS, n_tok // tk, tk, hd).transpose(0, 1, 3, 2)
    ones = jnp.ones((C_KV_HEADS, n_tok // tk, BF16_SUBLANES, tk), BF16)
    vt_tiles = jnp.concatenate([vt_tiles, ones], axis=2)

    def per_seq_max(a):
        return jnp.concatenate([a[:, t0:t0 + n_b * s_n].reshape(a.shape[0], n_b, s_n).max(-1)
                                for t0, n_b, s_n in geom.groups()], axis=1)

    n_g = C_Q_HEADS // C_KV_HEADS
    k_f32 = qk[C_K0:, :, :hd].astype(F32)
    k_max = per_seq_max(jnp.sqrt(jnp.sum(k_f32 * k_f32, axis=-1))).astype(BF16)
    q_max = per_seq_max(-qk[:C_K0, :, hd].astype(F32))
    bound = jnp.max(q_max.reshape(C_KV_HEADS, n_g, -1) * k_max.astype(F32)[:, None, :])
    kmx = jnp.zeros((C_KV_HEADS, geom.n_seq, 1, 2 * hd), BF16).at[:, :, 0, hd].set(k_max)
    o_buf = jnp.zeros((n_tok, C_Q_HEADS * hd), BF16)

    def attend(kmx_or_none):
        o = o_buf
        seq_idx0 = 0
        for tok0, n_b, s_n in geom.groups():
            o = flash_attention(qk, vt_tiles, o, tok0, n_b, s_n, seq_idx0, kmx_or_none)
            seq_idx0 += n_b
        return o

    o = lax.cond(bound <= FLASH_SAFE_BOUND, lambda: attend(kmx), lambda: attend(None))
    return outproj(x, mod, o, w_out.astype(BF16), layer, geom)


def kernel(x_prompt, x_sample, c_prompt, c_sample, w_ada, b_ada, g_norm_mix, g_norm_ffn, w_in_ab, w_out_ab, g_qn_a, g_kn_a, rpb_a, g_qn_b, g_kn_b, w_in_c, w_out_c, g_qn_c, g_kn_c, w_router, b_router, w_gate_up, b_gate_up, w_down, b_down):
    b_p, s_p, d = x_prompt.shape
    b_d, s_d, _ = x_sample.shape
    geom = Geom(b_p, s_p, b_d, s_d)
    depth = w_ada.shape[0]
    x = jnp.concatenate([x_prompt.reshape(-1, d), x_sample.reshape(-1, d)], axis=0)
    mod = ada_modulation(jnp.concatenate([c_prompt, c_sample], axis=0), w_ada, b_ada)
    for layer in range(depth):
        i = layer // 2
        if layer % 2 == 0:
            x = mixer_ab_layer(x, mod, g_norm_mix, w_in_ab[i], w_out_ab[i], g_qn_a[i], g_kn_a[i],
                               rpb_a[i], g_qn_b[i], g_kn_b[i], layer, geom)
        else:
            x = mixer_c_layer(x, mod, g_norm_mix, w_in_c[i], w_out_c[i], g_qn_c[i], g_kn_c[i],
                              layer, geom)
        x = moe_layer(x, mod, g_norm_ffn, w_router, b_router, w_gate_up, b_gate_up, w_down,
                      b_down, layer, geom)
    y_prompt = x[:geom.n_p].reshape(b_p, s_p, d)
    y_sample = x[geom.n_p:].reshape(b_d, s_d, d)
    return (y_prompt, y_sample)
```

```python
import functools
import math
from typing import NamedTuple

import jax
import jax.numpy as jnp
from jax import lax
from jax.experimental import pallas as pl
from jax.experimental.pallas import tpu as pltpu

F32 = jnp.float32
BF16 = jnp.bfloat16
HIGHEST = lax.Precision.HIGHEST

GRID_W = 64
HEAD_DIM = 64
ROPE_THETA = 10000.0
EPS = 1e-6
NEG = -1e30
NA_HEADS = 8
NA_KH = 8
NA_KW = 16
DIL_HEADS = 8
DIL_PATTERNS = ((128, 1), (512, 4), (2048, 16))
N_DIL = len(DIL_PATTERNS)
C_Q_HEADS = 16
C_KV_HEADS = 4
N_EXPERTS = 32
TOP_K = 4
SWIGLU_LIMIT = 7.0
SWIGLU_ALPHA = 1.702
Q_SCALE = HEAD_DIM ** -0.5
LOG2_E = math.log2(math.e)
BF16_SUBLANES = 16
LANES = 128
MXU_TILE = 256

V7X_VMEM_BYTES = 64 * 1024 * 1024
VMEM_LIMIT = V7X_VMEM_BYTES - 8 * 1024 * 1024

TOKEN_TILE = 1024
PREP_TILE = 2048
NA_BLOCK = 8 * GRID_W
NA_ROWS_PER_TRIP = 4
DIL_QBLOCK = 128
DIL_SIDE = 64
DIL_CHUNK = 1024
DIL_PROBLEMS_PER_TRIP = 4
FLASH_TQ = 256
FLASH_TK = 512
FLASH_UNROLL = 8
FLASH_BOUNDED_UNROLL = 16
FLASH_LOOKAHEAD = 6
FLASH_SAFE_BOUND = 60.0
FLASH_BOUND_MARGIN = 1.0 + 2.0 ** -6
MOE_BLOCK = 512
MOE_TOKEN_TILE = 512
MOE_PARTS = 1

NA_Q0, NA_K0, NA_V0 = 0, NA_HEADS, 2 * NA_HEADS
C_K0 = C_Q_HEADS
C_V0 = C_Q_HEADS + C_KV_HEADS


class Geom(NamedTuple):
    b_p: int
    s_p: int
    b_d: int
    s_d: int

    @property
    def n_p(self):
        return self.b_p * self.s_p

    @property
    def n_tok(self):
        return self.n_p + self.b_d * self.s_d

    @property
    def n_seq(self):
        return self.b_p + self.b_d

    def groups(self):
        return ((0, self.b_p, self.s_p), (self.n_p, self.b_d, self.s_d))

    def seq_of_token(self, t0):
        return jnp.where(t0 < self.n_p, t0 // self.s_p,
                         self.b_p + (t0 - self.n_p) // self.s_d)

    def seq_span(self, t0):
        in_p = t0 < self.n_p
        start = jnp.where(in_p, (t0 // self.s_p) * self.s_p,
                          self.n_p + ((t0 - self.n_p) // self.s_d) * self.s_d)
        return start, jnp.where(in_p, self.s_p, self.s_d)


def _params(semantics, vmem=None):
    return pltpu.CompilerParams(dimension_semantics=semantics,
                                vmem_limit_bytes=vmem)


def _ada_kernel(c_ref, w_ref, b_ref, o_ref):
    c = c_ref[...]
    a = c / (1.0 + jnp.exp(-c))
    o_ref[...] = jnp.dot(a, w_ref[...], precision=HIGHEST,
                         preferred_element_type=F32) + b_ref[...]


def ada_modulation(c_all, w_ada, b_ada):
    depth, d, d6 = w_ada.shape
    n_seq = c_all.shape[0]
    rows = 8
    c_pad = jnp.zeros((rows, d), F32).at[:n_seq].set(c_all)
    out = pl.pallas_call(
        _ada_kernel,
        grid=(depth, d6 // d),
        in_specs=[
            pl.BlockSpec((rows, d), lambda l, j: (0, 0)),
            pl.BlockSpec((None, d, d), lambda l, j: (l, 0, j)),
            pl.BlockSpec((None, 1, d), lambda l, j: (l, 0, j)),
        ],
        out_specs=pl.BlockSpec((None, rows, d), lambda l, j: (l, 0, j)),
        out_shape=jax.ShapeDtypeStruct((depth, rows, d6), F32),
        compiler_params=_params(("arbitrary", "arbitrary")),
        name="ada_modulation",
    )(c_pad, w_ada, b_ada.reshape(depth, 1, d6))
    return out[:, :n_seq].reshape(depth, n_seq, 6, d)


def _modulated_norm(x, g, shift, scale):
    r = lax.rsqrt(jnp.mean(x * x, axis=-1, keepdims=True) + EPS)
    return (x * r * g) * (1.0 + scale) + shift


def _norm_proj_kernel(x_ref, mod_ref, g_ref, w_ref, o_ref, *scratch, dil):
    h = _modulated_norm(x_ref[...], g_ref[...], mod_ref[0:1, :], mod_ref[1:2, :])
    acc = jnp.dot(h.astype(BF16), w_ref[...], preferred_element_type=F32)
    tm = x_ref.shape[0]
    n_h, _, _, hd = o_ref.shape
    if dil == 1:
        for c in range(n_h):
            o_ref[c, 0] = acc[:, c * hd:(c + 1) * hd].astype(o_ref.dtype)
        return
    acc_ref, = scratch
    for s in range(n_h // 2):
        acc_ref[s] = acc[:, s * LANES:(s + 1) * LANES]
    for s in range(n_h // 2):
        for r in range(dil):
            pair = acc_ref[s, pl.ds(r, tm // dil, stride=dil), :].astype(o_ref.dtype)
            o_ref[2 * s, r] = pair[:, :hd]
            o_ref[2 * s + 1, r] = pair[:, hd:]


def norm_proj(x, mod, g_norm, w_bf16, layer, geom, dil):
    n_tok, d = x.shape
    n_out = w_bf16.shape[1]
    tm = TOKEN_TILE
    hd = HEAD_DIM
    n_h = n_out // hd
    scratch = [] if dil == 1 else [pltpu.VMEM((n_h // 2, tm, LANES), F32)]
    return pl.pallas_call(
        functools.partial(_norm_proj_kernel, dil=dil),
        grid=(n_tok // tm,),
        in_specs=[
            pl.BlockSpec((tm, d), lambda i: (i, 0)),
            pl.BlockSpec((None, None, 6, d), lambda i: (layer, geom.seq_of_token(i * tm), 0, 0)),
            pl.BlockSpec((None, 1, d), lambda i: (layer, 0, 0)),
            pl.BlockSpec((d, n_out), lambda i: (0, 0)),
        ],
        out_specs=pl.BlockSpec((n_h, dil, tm // dil, hd), lambda i: (0, 0, i, 0)),
        out_shape=jax.ShapeDtypeStruct((n_h, dil, n_tok // dil, hd), BF16),
        scratch_shapes=scratch,
        compiler_params=_params(("parallel",), VMEM_LIMIT),
        name="norm_proj",
    )(x, mod, g_norm.reshape(-1, 1, d), w_bf16)


def _prep_kernel(x_ref, g_ref, *rest, rope, aug):
    o_ref = rest[-1]
    hb, ts, hd = x_ref.shape
    x = x_ref[...].astype(F32)
    r = lax.rsqrt(jnp.mean(x * x, axis=-1, keepdims=True) + EPS)
    y = x * r * g_ref[...]
    if rope:
        cos_ref, sin_ref, p_ref = rest[:3]
        yr = jnp.dot(y.reshape(hb * ts, hd).astype(BF16), p_ref[...],
                     preferred_element_type=F32).reshape(hb, ts, hd)
        y = y * cos_ref[...] + yr * sin_ref[...]
    if aug:
        side_ref = rest[-2]
        norm = jnp.sqrt(jnp.sum(y * y, axis=-1, keepdims=True))
        y = jnp.concatenate([y, norm * side_ref[...]], axis=-1)
    o_ref[...] = y.astype(o_ref.dtype)


def qk_prep(x, gains, hb, n_blocks, tables=None, side=None):
    n_h, n_tok, hd = x.shape
    ts = PREP_TILE
    rope = tables is not None
    aug = side is not None
    n_used = n_blocks * hb
    blk = pl.BlockSpec((hb, ts, hd), lambda i, s: (i, s, 0))
    per_head = pl.BlockSpec((hb, 1, hd), lambda i, s: (i, 0, 0))
    in_specs = [blk, per_head]
    args = [x, gains.reshape(n_used, 1, hd).astype(F32)]
    if rope:
        tab = pl.BlockSpec((ts, hd), lambda i, s: (s, 0))
        in_specs += [tab, tab, pl.BlockSpec((hd, hd), lambda i, s: (0, 0))]
        args += list(tables)
    if aug:
        in_specs.append(per_head)
        args.append(side.reshape(n_used, 1, hd).astype(F32))
        out_specs = pl.BlockSpec((hb, ts, 2 * hd), lambda i, s: (i, s, 0))
        out_shape = jax.ShapeDtypeStruct((n_used, n_tok, 2 * hd), BF16)
    else:
        out_specs = blk
        out_shape = jax.ShapeDtypeStruct(x.shape, BF16)
    return pl.pallas_call(
        functools.partial(_prep_kernel, rope=rope, aug=aug),
        grid=(n_blocks, n_tok // ts),
        in_specs=in_specs,
        out_specs=out_specs,
        out_shape=out_shape,
        input_output_aliases={} if aug else {0: 0},
        compiler_params=_params(("parallel", "parallel")),
        name="qk_prep",
    )(*args)


def _rope_angles(pos, dim):
    inv = ROPE_THETA ** (-jnp.arange(0, dim, 2, dtype=F32) / dim)
    ang = pos.astype(F32)[:, None] * inv[None, :]
    ang = jnp.concatenate([ang, ang], axis=-1)
    return jnp.cos(ang), jnp.sin(ang)


def _rotate_half_matrix(dim, n_blocks):
    half = dim // 2
    i = jnp.arange(dim * n_blocks)
    blk, w = i // dim, i % dim
    src = blk * dim + jnp.where(w < half, w + half, w - half)
    sign = jnp.where(w < half, -1.0, 1.0)
    p = jnp.zeros((dim * n_blocks, dim * n_blocks), F32).at[src, i].set(sign)
    return p.astype(BF16)


def _na_bias_table(rpb):
    n_h = rpb.shape[0]
    qc = jnp.arange(GRID_W)[:, None]
    kc = jnp.arange(GRID_W)[None, :]
    c0 = jnp.clip(qc - NA_KW // 2, 0, GRID_W - NA_KW)
    ok = (kc >= c0) & (kc < c0 + NA_KW)
    n_dr = rpb.shape[1]
    period = 2 * GRID_W
    ext = jnp.concatenate([rpb[..., NA_KW - 1:],
                           jnp.zeros((n_h, n_dr, period - (2 * NA_KW - 1)), rpb.dtype),
                           rpb[..., :NA_KW - 1]], axis=-1)
    toep = jnp.tile(ext, (1, 1, GRID_W))[..., :GRID_W * (period - 1)]
    toep = toep.reshape(n_h, n_dr, GRID_W, period - 1)[..., :GRID_W]
    toep = jnp.where(ok[None, None], toep, NEG)
    bias = jnp.stack([toep[:, NA_KH - 1 - d0:2 * NA_KH - 1 - d0] for d0 in range(NA_KH)])
    bias = bias.transpose(0, 1, 3, 2, 4).reshape(NA_KH, n_h, GRID_W, NA_KH * GRID_W)
    return bias.astype(BF16)


def _na_kernel(q_ref, kp_ref, kc_ref, kn_ref, vp_ref, vc_ref, vn_ref, tbl_ref,
               o_ref, ks_ref, vs_ref, *, geom):
    blk = NA_BLOCK
    j = pl.program_id(0)
    seq_start, seq_len = geom.seq_span(j * blk)
    rows = seq_len // GRID_W
    jl = j - seq_start // blk
    ks_ref[:, 0:blk, :] = kp_ref[...]
    ks_ref[:, blk:2 * blk, :] = kc_ref[...]
    ks_ref[:, 2 * blk:3 * blk, :] = kn_ref[...]
    vs_ref[:, 0:blk, :] = vp_ref[...]
    vs_ref[:, blk:2 * blk, :] = vc_ref[...]
    vs_ref[:, 2 * blk:3 * blk, :] = vn_ref[...]
    n_keys = NA_KH * GRID_W

    def body(ii, carry):
        probs = []
        for u in range(NA_ROWS_PER_TRIP):
            i = ii * NA_ROWS_PER_TRIP + u
            r = jl * NA_KH + i
            r0 = jnp.clip(r - NA_KH // 2, 0, rows - NA_KH)
            off = pl.multiple_of((r0 - (jl - 1) * NA_KH) * GRID_W, GRID_W)
            d0 = r - r0
            qoff = pl.multiple_of(i * GRID_W, GRID_W)
            for h in range(NA_HEADS):
                q = q_ref[h, pl.ds(qoff, GRID_W), :]
                k = ks_ref[h, pl.ds(off, n_keys), :]
                s = lax.dot_general(q, k, (((1,), (1,)), ((), ())), preferred_element_type=F32)
                probs.append((h, off, qoff, s + tbl_ref[d0, h].astype(F32)))
        soft = []
        for h, off, qoff, s in probs:
            m = jnp.max(s, axis=-1, keepdims=True)
            p = jnp.exp(s - m)
            soft.append((h, off, qoff, p.astype(BF16), jnp.sum(p, axis=-1, keepdims=True)))
        for h, off, qoff, p, l in soft:
            v = vs_ref[h, pl.ds(off, n_keys), :]
            o = jnp.dot(p, v, preferred_element_type=F32) / l
            o_ref[h, pl.ds(qoff, GRID_W), :] = o.astype(o_ref.dtype)
        return carry

    lax.fori_loop(0, NA_KH // NA_ROWS_PER_TRIP, body, 0)


def neighborhood_attention(qkv, tbl, geom):
    _, n_tok, hd = qkv.shape
    n_h = NA_HEADS
    blk = NA_BLOCK
    nb = n_tok // blk

    def spec(head0, shift):
        hb = head0 // n_h
        return pl.BlockSpec((n_h, blk, hd), lambda j: (hb, jnp.clip(j + shift, 0, nb - 1), 0))

    return pl.pallas_call(
        functools.partial(_na_kernel, geom=geom),
        grid=(nb,),
        in_specs=[spec(NA_Q0, 0), spec(NA_K0, -1), spec(NA_K0, 0), spec(NA_K0, 1),
                  spec(NA_V0, -1), spec(NA_V0, 0), spec(NA_V0, 1),
                  pl.BlockSpec(tbl.shape, lambda j: (0, 0, 0, 0))],
        out_specs=pl.BlockSpec((n_h, blk, hd), lambda j: (0, j, 0)),
        out_shape=jax.ShapeDtypeStruct((n_h, n_tok, hd), BF16),
        scratch_shapes=[pltpu.VMEM((n_h, 3 * blk, hd), BF16),
                        pltpu.VMEM((n_h, 3 * blk, hd), BF16)],
        compiler_params=_params(("parallel",), VMEM_LIMIT),
        name="neighborhood_attention",
    )(qkv, qkv, qkv, qkv, qkv, qkv, qkv, tbl)


def _dil_kernel(q_ref, kp_ref, kc_ref, kn_ref, vp_ref, vc_ref, vn_ref,
                o_ref, lse_ref, ks_ref, vs_ref, *, geom, dil):
    c = pl.program_id(0)
    n_h, chunk, hd = q_ref.shape
    side = DIL_SIDE
    qb = DIL_QBLOCK
    kb = qb + 2 * side
    ks_ref[:, 0:side, :] = kp_ref[...]
    ks_ref[:, side:side + chunk, :] = kc_ref[...]
    ks_ref[:, side + chunk:, :] = kn_ref[...]
    vs_ref[:, 0:side, :] = vp_ref[...]
    vs_ref[:, side:side + chunk, :] = vc_ref[...]
    vs_ref[:, side + chunk:, :] = vn_ref[...]
    plane_rows = geom.n_tok // dil
    t0 = ((c * chunk) % plane_rows) * dil
    seq_start, seq_len = geom.seq_span(t0)
    row0 = (t0 - seq_start) // dil
    cls_rows = seq_len // dil
    jq = lax.broadcasted_iota(jnp.int32, (qb, kb), 0)
    jk = lax.broadcasted_iota(jnp.int32, (qb, kb), 1) - side
    near_bias = jnp.where(jnp.abs(jk - jq) <= side, 0.0, NEG)
    before_bias = jnp.where(jk < 0, NEG, 0.0)
    after_bias = jnp.where(jk >= qb, NEG, 0.0)
    n_trip = min(DIL_PROBLEMS_PER_TRIP, chunk // qb)
    trips_per_head = chunk // (qb * n_trip)
    assert chunk % (qb * n_trip) == 0

    def body(it, carry):
        h = it // trips_per_head
        ii = it % trips_per_head
        probs = []
        for u in range(n_trip):
            i = ii * n_trip + u
            qoff = pl.multiple_of(i * qb, qb)
            p0 = row0 + i * qb
            bias = near_bias + jnp.where(p0 == 0, before_bias, 0.0)
            bias = bias + jnp.where(p0 + qb == cls_rows, after_bias, 0.0)
            q = q_ref[h, pl.ds(qoff, qb), :]
            k = ks_ref[h, pl.ds(qoff, kb), :]
            s = lax.dot_general(q, k, (((1,), (1,)), ((), ())), preferred_element_type=F32)
            probs.append((qoff, s + bias))
        soft = []
        for qoff, s in probs:
            m = jnp.max(s, axis=-1, keepdims=True)
            p = jnp.exp(s - m)
            soft.append((qoff, p.astype(BF16), m, jnp.sum(p, axis=-1, keepdims=True)))
        for qoff, p, m, l in soft:
            v = vs_ref[h, pl.ds(qoff, kb), :]
            o = jnp.dot(p, v, preferred_element_type=F32) / l
            o_ref[h, pl.ds(qoff, qb), :] = o.astype(o_ref.dtype)
            lse_ref[h, pl.ds(qoff, qb), :] = jnp.broadcast_to(m + jnp.log(l), (qb, hd))
        return carry

    lax.fori_loop(0, n_h * trips_per_head, body, 0)


def dilated_attention(qkv, dil, geom):
    _, n_tok, hd = qkv.shape
    n_h = DIL_HEADS
    chunk = min(DIL_CHUNK, min(geom.s_p, geom.s_d) // dil)
    assert chunk % DIL_QBLOCK == 0
    per = chunk // DIL_SIDE
    n_side = n_tok // DIL_SIDE

    def cur(hb):
        return pl.BlockSpec((n_h, chunk, hd), lambda c: (hb, c, 0))

    def prev(hb):
        return pl.BlockSpec((n_h, DIL_SIDE, hd), lambda c: (hb, jnp.maximum(c * per - 1, 0), 0))

    def nxt(hb):
        return pl.BlockSpec((n_h, DIL_SIDE, hd),
                            lambda c: (hb, jnp.minimum((c + 1) * per, n_side - 1), 0))

    return pl.pallas_call(
        functools.partial(_dil_kernel, geom=geom, dil=dil),
        grid=(n_tok // chunk,),
        in_specs=[cur(0), prev(1), cur(1), nxt(1), prev(2), cur(2), nxt(2)],
        out_specs=[cur(0), cur(0)],
        out_shape=[jax.ShapeDtypeStruct((n_h, n_tok, hd), BF16),
                   jax.ShapeDtypeStruct((n_h, n_tok, hd), F32)],
        scratch_shapes=[pltpu.VMEM((n_h, chunk + 2 * DIL_SIDE, hd), BF16),
                        pltpu.VMEM((n_h, chunk + 2 * DIL_SIDE, hd), BF16)],
        compiler_params=_params(("parallel",), VMEM_LIMIT),
        name="dilated_attention",
    )(qkv, qkv, qkv, qkv, qkv, qkv, qkv)


def _to_natural(src_ref, dst_ref):
    n_h, dil, n, _ = src_ref.shape
    for h in range(0, n_h, 2):
        for r in range(dil):
            pair = jnp.concatenate([src_ref[h, r].astype(F32), src_ref[h + 1, r].astype(F32)],
                                   axis=-1)
            if dil == 1:
                dst_ref[h // 2] = pair
            else:
                dst_ref[h // 2, pl.ds(r, n, stride=dil), :] = pair


def _outproj_ab_kernel(x_ref, mod_ref, oa_ref, od0_ref, od1_ref, od2_ref,
                       l0_ref, l1_ref, l2_ref, w_ref, o_ref, *scratch):
    so = scratch[:N_DIL]
    sl = scratch[N_DIL:]
    for src, dst in zip((od0_ref, od1_ref, od2_ref, l0_ref, l1_ref, l2_ref), so + sl):
        _to_natural(src, dst)
    n_h = oa_ref.shape[0]
    slabs = [jnp.concatenate([oa_ref[h], oa_ref[h + 1]], axis=-1) for h in range(0, n_h, 2)]
    for s in range(n_h // 2):
        l0, l1, l2 = sl[0][s], sl[1][s], sl[2][s]
        m = jnp.maximum(jnp.maximum(l0, l1), l2)
        e0, e1, e2 = jnp.exp(l0 - m), jnp.exp(l1 - m), jnp.exp(l2 - m)
        ob = (e0 * so[0][s] + e1 * so[1][s] + e2 * so[2][s]) / (e0 + e1 + e2)
        slabs.append(ob.astype(BF16))
    cat = jnp.concatenate(slabs, axis=-1)
    res = jnp.dot(cat, w_ref[...], preferred_element_type=F32)
    o_ref[...] = x_ref[...] + mod_ref[2:3, :] * res


def outproj_ab(x, mod, oa, od, lse, w_bf16, layer, geom):
    n_tok, d = x.shape
    n_h, _, hd = oa.shape
    tm = TOKEN_TILE // 2

    def cls(a):
        dil = a.shape[1]
        return pl.BlockSpec((n_h, dil, tm // dil, hd), lambda i: (0, 0, i, 0))

    return pl.pallas_call(
        _outproj_ab_kernel,
        grid=(n_tok // tm,),
        in_specs=[
            pl.BlockSpec((tm, d), lambda i: (i, 0)),
            pl.BlockSpec((None, None, 6, d), lambda i: (layer, geom.seq_of_token(i * tm), 0, 0)),
            pl.BlockSpec((n_h, tm, hd), lambda i: (0, i, 0)),
            *[cls(a) for a in od], *[cls(a) for a in lse],
            pl.BlockSpec(w_bf16.shape, lambda i: (0, 0)),
        ],
        out_specs=pl.BlockSpec((tm, d), lambda i: (i, 0)),
        out_shape=jax.ShapeDtypeStruct(x.shape, F32),
        scratch_shapes=[pltpu.VMEM((n_h // 2, tm, LANES), F32) for _ in range(2 * N_DIL)],
        compiler_params=_params(("parallel",), VMEM_LIMIT),
        name="outproj_ab",
    )(x, mod, oa, *od, *lse, w_bf16)


def _outproj_kernel(x_ref, mod_ref, o_in_ref, w_ref, o_ref):
    res = jnp.dot(o_in_ref[...], w_ref[...], preferred_element_type=F32)
    o_ref[...] = x_ref[...] + mod_ref[2:3, :] * res


def outproj(x, mod, o_in, w_bf16, layer, geom):
    n_tok, d = x.shape
    tm = TOKEN_TILE
    return pl.pallas_call(
        _outproj_kernel,
        grid=(n_tok // tm,),
        in_specs=[
            pl.BlockSpec((tm, d), lambda i: (i, 0)),
            pl.BlockSpec((None, None, 6, d), lambda i: (layer, geom.seq_of_token(i * tm), 0, 0)),
            pl.BlockSpec((tm, o_in.shape[1]), lambda i: (i, 0)),
            pl.BlockSpec(w_bf16.shape, lambda i: (0, 0)),
        ],
        out_specs=pl.BlockSpec((tm, d), lambda i: (i, 0)),
        out_shape=jax.ShapeDtypeStruct(x.shape, F32),
        compiler_params=_params(("parallel",), VMEM_LIMIT),
        name="outproj",
    )(x, mod, o_in, w_bf16)


def _flash_write_out(acc_ref, o_ref, n_g, tq):
    hd = HEAD_DIM
    acc = acc_ref[...]
    o = acc[:hd] / acc[hd:hd + 1]
    pad = jnp.zeros((LANES - hd, tq), F32)
    for g in range(n_g):
        t = jnp.concatenate([o[:, g * tq:(g + 1) * tq], pad], axis=0).T
        o_ref[:, g * hd:(g + 1) * hd] = t[:, :hd].astype(o_ref.dtype)


def _flash_bounded_kernel(q_ref, k_ref, vt_ref, kmx_ref, _o_in_ref, o_ref, acc_ref, *, unroll):
    n_g, tq, kw = q_ref.shape
    n_kt, tk, _ = k_ref.shape
    acc_ref[...] = jnp.zeros(acc_ref.shape, F32)
    kmx = kmx_ref[...]
    n_col = (n_g * tq) // MXU_TILE

    n_kc = tk // MXU_TILE

    def qk(j, n, kc):
        rows = slice(kc * MXU_TILE, (kc + 1) * MXU_TILE)
        q_n = q_ref[...].reshape(n_g * tq, kw)[n * MXU_TILE:(n + 1) * MXU_TILE]
        return lax.dot_general(k_ref[j, rows, :] + kmx, q_n, (((1,), (1,)), ((), ())),
                               preferred_element_type=F32)

    def body(jj, carry):
        units = [(unroll * jj + u, n, kc) for u in range(unroll) for n in range(n_col)
                 for kc in range(n_kc)]
        pending = [qk(*unit) for unit in units[:FLASH_LOOKAHEAD]]
        for idx, (j, n, kc) in enumerate(units):
            if idx + FLASH_LOOKAHEAD < len(units):
                pending.append(qk(*units[idx + FLASH_LOOKAHEAD]))
            cols = slice(n * MXU_TILE, (n + 1) * MXU_TILE)
            rows = slice(kc * MXU_TILE, (kc + 1) * MXU_TILE)
            p = jnp.exp2(pending.pop(0)).astype(BF16)
            acc_ref[:, cols] += jnp.dot(vt_ref[j, :, rows], p, preferred_element_type=F32)
        return carry

    lax.fori_loop(0, n_kt // unroll, body, 0)
    _flash_write_out(acc_ref, o_ref, n_g, tq)


def _flash_kernel(q_ref, k_ref, vt_ref, _o_in_ref, o_ref, sa_ref, sb_ref, mxa_ref, mxb_ref,
                  m_ref, acc_ref, *, unroll):
    n_g, tq, hd = q_ref.shape
    n_kt = k_ref.shape[0]
    m_ref[...] = jnp.full(m_ref.shape, -jnp.inf, F32)
    acc_ref[...] = jnp.zeros(acc_ref.shape, F32)
    n_col = (n_g * tq) // MXU_TILE

    def scores(j, s_ref, mx_ref, n):
        cols = slice(n * MXU_TILE, (n + 1) * MXU_TILE)
        q_n = q_ref[...].reshape(n_g * tq, hd)[n * MXU_TILE:(n + 1) * MXU_TILE]
        s = lax.dot_general(k_ref[j], q_n, (((1,), (1,)), ((), ())),
                            preferred_element_type=F32)
        s_ref[:, cols] = s
        mx_ref[:, cols] = jnp.max(s, axis=0, keepdims=True)

    def consume(j, s_ref, mx_ref, n):
        cols = slice(n * MXU_TILE, (n + 1) * MXU_TILE)
        m_old = m_ref[:, cols]
        m_new = jnp.maximum(m_old, mx_ref[:, cols])
        alpha = jnp.exp2(m_old - m_new)
        m_ref[:, cols] = m_new
        tk = s_ref.shape[0]
        acc = alpha * acc_ref[:, cols]
        for kc in range(tk // MXU_TILE):
            rows = slice(kc * MXU_TILE, (kc + 1) * MXU_TILE)
            p = jnp.exp2(s_ref[rows, cols] - m_new).astype(BF16)
            acc = acc + jnp.dot(vt_ref[j, :, rows], p, preferred_element_type=F32)
        acc_ref[:, cols] = acc

    for n in range(n_col):
        scores(0, sa_ref, mxa_ref, n)

    def body(jj, carry):
        j = unroll * jj
        for u in range(0, unroll, 2):
            for n in range(n_col):
                scores(j + u + 1, sb_ref, mxb_ref, n)
                consume(j + u, sa_ref, mxa_ref, n)
            for n in range(n_col):
                scores(jnp.minimum(j + u + 2, n_kt - 1), sa_ref, mxa_ref, n)
                consume(j + u + 1, sb_ref, mxb_ref, n)
        return carry

    lax.fori_loop(0, n_kt // unroll, body, 0)
    _flash_write_out(acc_ref, o_ref, n_g, tq)


def flash_attention(qk, vt_tiles, o_buf, tok0, n_b, s_n, seq_idx0, kmx=None):
    n_all, n_tok, kw = qk.shape
    n_kv, _, hv, tk = vt_tiles.shape
    n_g = C_Q_HEADS // n_kv
    hd = HEAD_DIM
    tq = FLASH_TQ
    nq = s_n // tq
    kt_per = s_n // tk
    unroll = min(FLASH_UNROLL if kmx is None else FLASH_BOUNDED_UNROLL, kt_per)
    assert kt_per % unroll == 0 and unroll % 2 == 0 and tok0 % s_n == 0
    wq = n_g * tq
    k_view = qk.reshape(n_all, n_tok // tk, tk, kw)
    seq0 = tok0 // s_n
    in_specs = [
        pl.BlockSpec((n_g, tq, kw), lambda b, h, i: (h, (tok0 + b * s_n) // tq + i, 0)),
        pl.BlockSpec((None, kt_per, tk, kw), lambda b, h, i: (C_K0 + h, seq0 + b, 0, 0)),
        pl.BlockSpec((None, kt_per, hv, tk), lambda b, h, i: (h, seq0 + b, 0, 0)),
    ]
    args = [qk, k_view, vt_tiles]
    if kmx is None:
        body = functools.partial(_flash_kernel, unroll=unroll)
        scratch = [pltpu.VMEM((tk, wq), F32), pltpu.VMEM((tk, wq), F32),
                   pltpu.VMEM((1, wq), F32), pltpu.VMEM((1, wq), F32),
                   pltpu.VMEM((1, wq), F32), pltpu.VMEM((hv, wq), F32)]
    else:
        body = functools.partial(_flash_bounded_kernel, unroll=unroll)
        scratch = [pltpu.VMEM((hv, wq), F32)]
        in_specs.append(pl.BlockSpec((None, None, 1, kw), lambda b, h, i: (h, seq_idx0 + b, 0, 0)))
        args.append(kmx)
    in_specs.append(pl.BlockSpec(memory_space=pl.ANY))
    args.append(o_buf)
    return pl.pallas_call(
        body,
        grid=(n_b, n_kv, nq),
        in_specs=in_specs,
        out_specs=pl.BlockSpec((tq, n_g * hd), lambda b, h, i: ((tok0 + b * s_n) // tq + i, h)),
        out_shape=jax.ShapeDtypeStruct(o_buf.shape, o_buf.dtype),
        input_output_aliases={len(args) - 1: 0},
        scratch_shapes=scratch,
        compiler_params=_params(("parallel", "parallel", "arbitrary"), VMEM_LIMIT),
        name="flash_attention",
    )(*args)


def _router_kernel(x_ref, mod_ref, g_ref, wr_ref, br_ref, h_ref, idx_ref, gate_ref, cnt_ref,
                   run_ref):
    @pl.when(pl.program_id(0) == 0)
    def _():
        run_ref[...] = jnp.zeros(run_ref.shape, F32)

    h = _modulated_norm(x_ref[...], g_ref[...], mod_ref[3:4, :], mod_ref[4:5, :])
    h_ref[...] = h.astype(BF16)
    logits = lax.dot_general(wr_ref[...], h, (((1,), (1,)), ((), ())), precision=HIGHEST,
                             preferred_element_type=F32) + br_ref[...]
    n_e, tm = logits.shape
    iota = lax.broadcasted_iota(jnp.int32, (n_e, tm), 0)
    vals, ids = [], []
    cur = logits
    for _ in range(TOP_K):
        m = jnp.max(cur, axis=0, keepdims=True)
        am = jnp.min(jnp.where(cur == m, iota, n_e), axis=0, keepdims=True)
        vals.append(m)
        ids.append(am)
        cur = jnp.where(iota == am, -jnp.inf, cur)
    es = [jnp.exp(v - vals[0]) for v in vals]
    den = es[0] + es[1] + es[2] + es[3]
    chosen = jnp.zeros((n_e, tm), F32)
    for kk in range(TOP_K):
        chosen = chosen + jnp.where(iota == ids[kk], 1.0, 0.0)
    earlier = lax.broadcasted_iota(jnp.int32, (tm, tm), 0)
    token = lax.broadcasted_iota(jnp.int32, (tm, tm), 1)
    tri = jnp.where(earlier < token, 1.0, 0.0).astype(BF16)
    before = jnp.dot(chosen.astype(BF16), tri, preferred_element_type=F32) + run_ref[...]
    run_ref[...] = run_ref[...] + jnp.sum(chosen, axis=1, keepdims=True)
    cnt_ref[...] = run_ref[...]
    row = lax.broadcasted_iota(jnp.int32, idx_ref.shape, 0)
    idx_out = jnp.zeros(idx_ref.shape, jnp.int32)
    gate_out = jnp.zeros(gate_ref.shape, F32)
    for kk in range(TOP_K):
        rank = jnp.sum(jnp.where(iota == ids[kk], before, 0.0), axis=0, keepdims=True)
        idx_out = jnp.where(row == kk, ids[kk], idx_out)
        idx_out = jnp.where(row == TOP_K + kk, rank.astype(jnp.int32), idx_out)
        gate_out = jnp.where(row == kk, es[kk] / den, gate_out)
    idx_ref[...] = idx_out
    gate_ref[...] = gate_out


def router(x, mod, g_norm, w_router, b_router, layer, geom, tile0, n_tiles):
    d = x.shape[1]
    tm = MOE_TOKEN_TILE
    n_tok = n_tiles * tm
    return pl.pallas_call(
        _router_kernel,
        grid=(n_tiles,),
        in_specs=[
            pl.BlockSpec((tm, d), lambda i: (tile0 + i, 0)),
            pl.BlockSpec((None, None, 6, d),
                         lambda i: (layer, geom.seq_of_token((tile0 + i) * tm), 0, 0)),
            pl.BlockSpec((None, 1, d), lambda i: (layer, 0, 0)),
            pl.BlockSpec((None, N_EXPERTS, d), lambda i: (layer, 0, 0)),
            pl.BlockSpec((None, N_EXPERTS, 1), lambda i: (layer, 0, 0)),
        ],
        out_specs=[
            pl.BlockSpec((tm, d), lambda i: (i, 0)),
            pl.BlockSpec((2 * TOP_K, tm), lambda i: (0, i)),
            pl.BlockSpec((2 * TOP_K, tm), lambda i: (0, i)),
            pl.BlockSpec((N_EXPERTS, 1), lambda i: (0, 0)),
        ],
        out_shape=[
            jax.ShapeDtypeStruct((n_tok, d), BF16),
            jax.ShapeDtypeStruct((2 * TOP_K, n_tok), jnp.int32),
            jax.ShapeDtypeStruct((2 * TOP_K, n_tok), F32),
            jax.ShapeDtypeStruct((N_EXPERTS, 1), F32),
        ],
        scratch_shapes=[pltpu.VMEM((N_EXPERTS, 1), F32)],
        compiler_params=_params(("arbitrary",), VMEM_LIMIT),
        name="router",
    )(x, mod, g_norm.reshape(-1, 1, d), w_router.transpose(0, 2, 1),
      b_router.reshape(-1, N_EXPERTS, 1))


def _moe_kernel(be_ref, nu_ref, x_ref, wgu_ref, bgu_ref, wd_ref, bd_ref, o_ref,
                wgu_s, wd_s):
    i = pl.program_id(0)
    e = be_ref[i]
    e_prev = be_ref[jnp.maximum(i - 1, 0)]

    @pl.when((i == 0) | (e != e_prev))
    def _():
        wgu_s[...] = wgu_ref[...].astype(BF16)
        wd_s[...] = wd_ref[...].astype(BF16)

    @pl.when(i < nu_ref[0])
    def _():
        d_ff = wd_s.shape[0]
        gu = jnp.dot(x_ref[...], wgu_s[...], preferred_element_type=F32) + bgu_ref[...]
        glu = jnp.minimum(gu[:, :d_ff], SWIGLU_LIMIT)
        lin = jnp.clip(gu[:, d_ff:], -SWIGLU_LIMIT, SWIGLU_LIMIT)
        act = glu / (1.0 + jnp.exp(-SWIGLU_ALPHA * glu)) * (lin + 1.0)
        y = jnp.dot(act.astype(BF16), wd_s[...], preferred_element_type=F32) + bd_ref[...]
        o_ref[...] = y.astype(o_ref.dtype)

    @pl.when(i >= nu_ref[0])
    def _():
        o_ref[...] = jnp.zeros(o_ref.shape, o_ref.dtype)


def moe_experts(xs, blk_e, n_used, w_gu, b_gu, w_down, b_down, layer):
    n_rows, d = xs.shape
    d_ff = w_down.shape[2]
    nblk = n_rows // MOE_BLOCK
    grid_spec = pltpu.PrefetchScalarGridSpec(
        num_scalar_prefetch=2,
        grid=(nblk,),
        in_specs=[
            pl.BlockSpec((MOE_BLOCK, d), lambda i, be, nu: (jnp.minimum(i, nu[0] - 1), 0)),
            pl.BlockSpec((None, None, d, 2 * d_ff), lambda i, be, nu: (layer, be[i], 0, 0)),
            pl.BlockSpec((None, None, 1, 2 * d_ff), lambda i, be, nu: (layer, be[i], 0, 0)),
            pl.BlockSpec((None, None, d_ff, d), lambda i, be, nu: (layer, be[i], 0, 0)),
            pl.BlockSpec((None, None, 1, d), lambda i, be, nu: (layer, be[i], 0, 0)),
        ],
        out_specs=pl.BlockSpec((MOE_BLOCK, d), lambda i, be, nu: (i, 0)),
        scratch_shapes=[pltpu.VMEM((d, 2 * d_ff), BF16), pltpu.VMEM((d_ff, d), BF16)],
    )
    depth, n_e = w_gu.shape[:2]
    return pl.pallas_call(
        _moe_kernel,
        grid_spec=grid_spec,
        out_shape=jax.ShapeDtypeStruct((n_rows, d), BF16),
        compiler_params=_params(("arbitrary",), VMEM_LIMIT),
        name="moe_experts",
    )(blk_e, n_used, xs, w_gu, b_gu.reshape(depth, n_e, 1, 2 * d_ff), w_down,
      b_down.reshape(depth, n_e, 1, d))


def _combine_kernel(x_ref, mod_ref, y_ref, gate_ref, o_ref):
    g = gate_ref[...]
    acc = g[:, 0:1] * y_ref[0].astype(F32)
    for kk in range(1, TOP_K):
        acc = acc + g[:, kk:kk + 1] * y_ref[kk].astype(F32)
    o_ref[...] = x_ref[...] + mod_ref[5:6, :] * acc


def moe_combine(x, mod, yk, gates, layer, geom, tile0, n_tiles):
    d = x.shape[1]
    tm = MOE_TOKEN_TILE
    return pl.pallas_call(
        _combine_kernel,
        grid=(n_tiles,),
        in_specs=[
            pl.BlockSpec((tm, d), lambda i: (tile0 + i, 0)),
            pl.BlockSpec((None, None, 6, d),
                         lambda i: (layer, geom.seq_of_token((tile0 + i) * tm), 0, 0)),
            pl.BlockSpec((TOP_K, tm, d), lambda i: (0, i, 0)),
            pl.BlockSpec((tm, 2 * TOP_K), lambda i: (i, 0)),
        ],
        out_specs=pl.BlockSpec((tm, d), lambda i: (tile0 + i, 0)),
        out_shape=jax.ShapeDtypeStruct(x.shape, F32),
        input_output_aliases={0: 0},
        compiler_params=_params(("parallel",), VMEM_LIMIT),
        name="moe_combine",
    )(x, mod, yk, gates)


def moe_layer(x, mod, g_norm, w_router, b_router, w_gu, b_gu, w_down, b_down, layer, geom):
    n_tiles = x.shape[0] // MOE_TOKEN_TILE
    assert n_tiles % MOE_PARTS == 0
    per = n_tiles // MOE_PARTS
    routed = [_moe_dispatch(x, mod, g_norm, w_router, b_router, w_gu, b_gu, w_down, b_down, layer,
                            geom, part * per, per) for part in range(MOE_PARTS)]
    for part, (yk, gates8) in enumerate(routed):
        x = moe_combine(x, mod, yk, gates8, layer, geom, part * per, per)
    return x


def _moe_dispatch(x, mod, g_norm, w_router, b_router, w_gu, b_gu, w_down, b_down, layer, geom,
                  tile0, n_tiles):
    d = x.shape[1]
    n_tok = n_tiles * MOE_TOKEN_TILE
    h, idx8, gates8, counts = router(x, mod, g_norm, w_router, b_router, layer, geom, tile0,
                                     n_tiles)
    idx = idx8[:TOP_K].T
    rank = idx8[TOP_K:].T
    gates8 = gates8.T
    counts = counts[:, 0].astype(jnp.int32)
    pcounts = ((counts + MOE_BLOCK - 1) // MOE_BLOCK) * MOE_BLOCK
    pend = jnp.cumsum(pcounts)
    pstart = pend - pcounts
    experts = jnp.arange(N_EXPERTS, dtype=jnp.int32)
    dest = rank + jnp.sum(jnp.where(idx[:, :, None] == experts, pstart, 0), axis=-1)
    nblk = (n_tok * TOP_K) // MOE_BLOCK + N_EXPERTS
    n_rows = nblk * MOE_BLOCK
    tok_ids = jnp.broadcast_to(jnp.arange(n_tok, dtype=jnp.int32)[:, None], dest.shape)
    tok = (jnp.arange(n_rows, dtype=jnp.int32) % n_tok).at[dest.reshape(-1)].set(
        tok_ids.reshape(-1), unique_indices=True)
    n_used = (pend[-1] // MOE_BLOCK).astype(jnp.int32).reshape(1)
    blk_start = jnp.arange(nblk, dtype=jnp.int32) * MOE_BLOCK
    blk_last = jnp.minimum(blk_start, pend[-1] - 1)
    blk_e = jnp.sum(pend[None, :] <= blk_last[:, None], axis=-1, dtype=jnp.int32)
    blk_e = jnp.minimum(blk_e, N_EXPERTS - 1)
    xs = jnp.take(h, tok, axis=0, mode="clip")
    yb = moe_experts(xs, blk_e, n_used, w_gu, b_gu, w_down, b_down, layer)
    yk = jnp.take(yb, dest.T.reshape(-1), axis=0, mode="clip").reshape(TOP_K, n_tok, d)
    return yk, gates8


def _per_token(geom, tables):
    return [jnp.concatenate([t[:s] for _, n_b, s in geom.groups() for _ in range(n_b)])
            for t in tables]


def _qk_gains(g_q, g_k, n_q, n_k, q_scale):
    hd = g_q.shape[-1]
    return jnp.concatenate([jnp.broadcast_to(g_q * q_scale, (n_q, hd)),
                            jnp.broadcast_to(g_k, (n_k, hd))])


def _dilated_group_weight(w_in, gi):
    blk = DIL_HEADS * HEAD_DIM
    base = 3 * NA_HEADS * HEAD_DIM
    return jnp.concatenate([w_in[:, base + (a * N_DIL + gi) * blk:base + (a * N_DIL + gi + 1) * blk]
                            for a in range(3)], axis=1)


def mixer_ab_layer(x, mod, g_norm, w_in, w_out, g_qn_a, g_kn_a, rpb, g_qn_b, g_kn_b, layer, geom):
    n_tok, d = x.shape
    hd = HEAD_DIM
    w = w_in.astype(BF16)
    n_h = NA_HEADS
    qkv = norm_proj(x, mod, g_norm, w[:, :3 * n_h * hd], layer, geom, dil=1)
    qkv = qk_prep(qkv.reshape(3 * n_h, n_tok, hd), _qk_gains(g_qn_a, g_kn_a, n_h, n_h, Q_SCALE),
                  hb=n_h, n_blocks=2)
    oa = neighborhood_attention(qkv, _na_bias_table(rpb), geom)
    cos, sin = _per_token(geom, _rope_angles(jnp.arange(max(geom.s_p, geom.s_d)), hd))
    perm = _rotate_half_matrix(hd, 1)
    od, lse = [], []
    for gi, (window, dil) in enumerate(DIL_PATTERNS):
        assert (window // 2) // dil == DIL_SIDE
        n_h = DIL_HEADS

        def class_major(t):
            return t.reshape(n_tok // dil, dil, hd).transpose(1, 0, 2).reshape(n_tok, hd)

        qkv = norm_proj(x, mod, g_norm, _dilated_group_weight(w, gi), layer, geom, dil=dil)
        qkv = qk_prep(qkv.reshape(3 * n_h, n_tok, hd),
                      _qk_gains(g_qn_b[gi], g_kn_b[gi], n_h, n_h, Q_SCALE), hb=n_h, n_blocks=2,
                      tables=(class_major(cos), class_major(sin), perm))
        o_g, lse_g = dilated_attention(qkv, dil, geom)
        od.append(o_g.reshape(n_h, dil, n_tok // dil, hd))
        lse.append(lse_g.reshape(n_h, dil, n_tok // dil, hd))
    return outproj_ab(x, mod, oa, od, lse, w_out.astype(BF16), layer, geom)


def mixer_c_layer(x, mod, g_norm, w_in, w_out, g_qn, g_kn, layer, geom):
    n_tok, d = x.shape
    hd = HEAD_DIM
    n_heads = C_Q_HEADS + 2 * C_KV_HEADS
    qkv = norm_proj(x, mod, g_norm, w_in.astype(BF16), layer, geom, dil=1).reshape(n_heads, n_tok, hd)
    pos = jnp.arange(max(geom.s_p, geom.s_d))
    half = hd // 2
    cr, sr = _rope_angles(pos // GRID_W, half)
    cc, sc = _rope_angles(pos % GRID_W, half)
    cos, sin = _per_token(geom, (jnp.concatenate([cr, cc], axis=-1),
                                 jnp.concatenate([sr, sc], axis=-1)))
    tables = (cos, sin, _rotate_half_matrix(half, 2))
    side = jnp.zeros((C_Q_HEADS + C_KV_HEADS, hd), F32).at[:C_Q_HEADS, 0].set(-FLASH_BOUND_MARGIN)
    qk = qk_prep(qkv, _qk_gains(g_qn, g_kn, C_Q_HEADS, C_KV_HEADS, Q_SCALE * LOG2_E),
                 hb=C_KV_HEADS, n_blocks=(C_Q_HEADS + C_KV_HEADS) // C_KV_HEADS, tables=tables,
                 side=side)
    tk = FLASH_TK
    vt_tiles = qkv[C_V0:].reshape(C_KV_HEADS, n_tok // tk, tk, hd).transpose(0, 1, 3, 2)
    ones = jnp.ones((C_KV_HEADS, n_tok // tk, BF16_SUBLANES, tk), BF16)
    vt_tiles = jnp.concatenate([vt_tiles, ones], axis=2)

    def per_seq_max(a):
        return jnp.concatenate([a[:, t0:t0 + n_b * s_n].reshape(a.shape[0], n_b, s_n).max(-1)
                                for t0, n_b, s_n in geom.groups()], axis=1)

    n_g = C_Q_HEADS // C_KV_HEADS
    k_f32 = qk[C_K0:, :, :hd].astype(F32)
    k_max = per_seq_max(jnp.sqrt(jnp.sum(k_f32 * k_f32, axis=-1))).astype(BF16)
    q_max = per_seq_max(-qk[:C_K0, :, hd].astype(F32))
    bound = jnp.max(q_max.reshape(C_KV_HEADS, n_g, -1) * k_max.astype(F32)[:, None, :])
    kmx = jnp.zeros((C_KV_HEADS, geom.n_seq, 1, 2 * hd), BF16).at[:, :, 0, hd].set(k_max)
    o_buf = jnp.zeros((n_tok, C_Q_HEADS * hd), BF16)

    def attend(kmx_or_none):
        o = o_buf
        seq_idx0 = 0
        for tok0, n_b, s_n in geom.groups():
            o = flash_attention(qk, vt_tiles, o, tok0, n_b, s_n, seq_idx0, kmx_or_none)
            seq_idx0 += n_b
        return o

    o = lax.cond(bound <= FLASH_SAFE_BOUND, lambda: attend(kmx), lambda: attend(None))
    return outproj(x, mod, o, w_out.astype(BF16), layer, geom)


def kernel(x_prompt, x_sample, c_prompt, c_sample, w_ada, b_ada, g_norm_mix, g_norm_ffn, w_in_ab, w_out_ab, g_qn_a, g_kn_a, rpb_a, g_qn_b, g_kn_b, w_in_c, w_out_c, g_qn_c, g_kn_c, w_router, b_router, w_gate_up, b_gate_up, w_down, b_down):
    b_p, s_p, d = x_prompt.shape
    b_d, s_d, _ = x_sample.shape
    geom = Geom(b_p, s_p, b_d, s_d)
    depth = w_ada.shape[0]
    x = jnp.concatenate([x_prompt.reshape(-1, d), x_sample.reshape(-1, d)], axis=0)
    mod = ada_modulation(jnp.concatenate([c_prompt, c_sample], axis=0), w_ada, b_ada)
    for layer in range(depth):
        i = layer // 2
        if layer % 2 == 0:
            x = mixer_ab_layer(x, mod, g_norm_mix, w_in_ab[i], w_out_ab[i], g_qn_a[i], g_kn_a[i],
                               rpb_a[i], g_qn_b[i], g_kn_b[i], layer, geom)
        else:
            x = mixer_c_layer(x, mod, g_norm_mix, w_in_c[i], w_out_c[i], g_qn_c[i], g_kn_c[i],
                              layer, geom)
        x = moe_layer(x, mod, g_norm_ffn, w_router, b_router, w_gate_up, b_gate_up, w_down,
                      b_down, layer, geom)
    y_prompt = x[:geom.n_p].reshape(b_p, s_p, d)
    y_sample = x[geom.n_p:].reshape(b_d, s_d, d)
    return (y_prompt, y_sample)
```

```python
import functools
import math
from typing import NamedTuple

import jax
import jax.numpy as jnp
from jax import lax
from jax.experimental import pallas as pl
from jax.experimental.pallas import tpu as pltpu

F32 = jnp.float32
BF16 = jnp.bfloat16
HIGHEST = lax.Precision.HIGHEST

GRID_W = 64
HEAD_DIM = 64
ROPE_THETA = 10000.0
EPS = 1e-6
NEG = -1e30
NA_HEADS = 8
NA_KH = 8
NA_KW = 16
DIL_HEADS = 8
DIL_PATTERNS = ((128, 1), (512, 4), (2048, 16))
N_DIL = len(DIL_PATTERNS)
C_Q_HEADS = 16
C_KV_HEADS = 4
N_EXPERTS = 32
TOP_K = 4
SWIGLU_LIMIT = 7.0
SWIGLU_ALPHA = 1.702
Q_SCALE = HEAD_DIM ** -0.5
LOG2_E = math.log2(math.e)
BF16_SUBLANES = 16
LANES = 128
MXU_TILE = 256

V7X_VMEM_BYTES = 64 * 1024 * 1024
VMEM_LIMIT = V7X_VMEM_BYTES - 8 * 1024 * 1024

TOKEN_TILE = 1024
PREP_TILE = 2048
NA_BLOCK = 8 * GRID_W
NA_ROWS_PER_TRIP = 4
DIL_QBLOCK = 128
DIL_SIDE = 64
DIL_CHUNK = 1024
DIL_PROBLEMS_PER_TRIP = 4
FLASH_TQ = 256
FLASH_TK = 512
FLASH_UNROLL = 8
FLASH_BOUNDED_UNROLL = 16
FLASH_LOOKAHEAD = 6
FLASH_SAFE_BOUND = 60.0
FLASH_BOUND_MARGIN = 1.0 + 2.0 ** -6
MOE_BLOCK = 512
MOE_TOKEN_TILE = 512
MOE_PARTS = 1

NA_Q0, NA_K0, NA_V0 = 0, NA_HEADS, 2 * NA_HEADS
C_K0 = C_Q_HEADS
C_V0 = C_Q_HEADS + C_KV_HEADS


class Geom(NamedTuple):
    b_p: int
    s_p: int
    b_d: int
    s_d: int

    @property
    def n_p(self):
        return self.b_p * self.s_p

    @property
    def n_tok(self):
        return self.n_p + self.b_d * self.s_d

    @property
    def n_seq(self):
        return self.b_p + self.b_d

    def groups(self):
        return ((0, self.b_p, self.s_p), (self.n_p, self.b_d, self.s_d))

    def seq_of_token(self, t0):
        return jnp.where(t0 < self.n_p, t0 // self.s_p,
                         self.b_p + (t0 - self.n_p) // self.s_d)

    def seq_span(self, t0):
        in_p = t0 < self.n_p
        start = jnp.where(in_p, (t0 // self.s_p) * self.s_p,
                          self.n_p + ((t0 - self.n_p) // self.s_d) * self.s_d)
        return start, jnp.where(in_p, self.s_p, self.s_d)


def _params(semantics, vmem=None):
    return pltpu.CompilerParams(dimension_semantics=semantics,
                                vmem_limit_bytes=vmem)


def _ada_kernel(c_ref, w_ref, b_ref, o_ref):
    c = c_ref[...]
    a = c / (1.0 + jnp.exp(-c))
    o_ref[...] = jnp.dot(a, w_ref[...], precision=HIGHEST,
                         preferred_element_type=F32) + b_ref[...]


def ada_modulation(c_all, w_ada, b_ada):
    depth, d, d6 = w_ada.shape
    n_seq = c_all.shape[0]
    rows = 8
    c_pad = jnp.zeros((rows, d), F32).at[:n_seq].set(c_all)
    out = pl.pallas_call(
        _ada_kernel,
        grid=(depth, d6 // d),
        in_specs=[
            pl.BlockSpec((rows, d), lambda l, j: (0, 0)),
            pl.BlockSpec((None, d, d), lambda l, j: (l, 0, j)),
            pl.BlockSpec((None, 1, d), lambda l, j: (l, 0, j)),
        ],
        out_specs=pl.BlockSpec((None, rows, d), lambda l, j: (l, 0, j)),
        out_shape=jax.ShapeDtypeStruct((depth, rows, d6), F32),
        compiler_params=_params(("arbitrary", "arbitrary")),
        name="ada_modulation",
    )(c_pad, w_ada, b_ada.reshape(depth, 1, d6))
    return out[:, :n_seq].reshape(depth, n_seq, 6, d)


def _modulated_norm(x, g, shift, scale):
    r = lax.rsqrt(jnp.mean(x * x, axis=-1, keepdims=True) + EPS)
    return (x * r * g) * (1.0 + scale) + shift


def _norm_proj_kernel(x_ref, mod_ref, g_ref, w_ref, o_ref, *scratch, dil):
    h = _modulated_norm(x_ref[...], g_ref[...], mod_ref[0:1, :], mod_ref[1:2, :])
    acc = jnp.dot(h.astype(BF16), w_ref[...], preferred_element_type=F32)
    tm = x_ref.shape[0]
    n_h, _, _, hd = o_ref.shape
    if dil == 1:
        for c in range(n_h):
            o_ref[c, 0] = acc[:, c * hd:(c + 1) * hd].astype(o_ref.dtype)
        return
    acc_ref, = scratch
    for s in range(n_h // 2):
        acc_ref[s] = acc[:, s * LANES:(s + 1) * LANES]
    for s in range(n_h // 2):
        for r in range(dil):
            pair = acc_ref[s, pl.ds(r, tm // dil, stride=dil), :].astype(o_ref.dtype)
            o_ref[2 * s, r] = pair[:, :hd]
            o_ref[2 * s + 1, r] = pair[:, hd:]


def norm_proj(x, mod, g_norm, w_bf16, layer, geom, dil):
    n_tok, d = x.shape
    n_out = w_bf16.shape[1]
    tm = TOKEN_TILE
    hd = HEAD_DIM
    n_h = n_out // hd
    scratch = [] if dil == 1 else [pltpu.VMEM((n_h // 2, tm, LANES), F32)]
    return pl.pallas_call(
        functools.partial(_norm_proj_kernel, dil=dil),
        grid=(n_tok // tm,),
        in_specs=[
            pl.BlockSpec((tm, d), lambda i: (i, 0)),
            pl.BlockSpec((None, None, 6, d), lambda i: (layer, geom.seq_of_token(i * tm), 0, 0)),
            pl.BlockSpec((None, 1, d), lambda i: (layer, 0, 0)),
            pl.BlockSpec((d, n_out), lambda i: (0, 0)),
        ],
        out_specs=pl.BlockSpec((n_h, dil, tm // dil, hd), lambda i: (0, 0, i, 0)),
        out_shape=jax.ShapeDtypeStruct((n_h, dil, n_tok // dil, hd), BF16),
        scratch_shapes=scratch,
        compiler_params=_params(("parallel",), VMEM_LIMIT),
        name="norm_proj",
    )(x, mod, g_norm.reshape(-1, 1, d), w_bf16)


def _prep_kernel(x_ref, g_ref, *rest, rope, aug):
    rest = list(rest)
    hb, ts, hd = x_ref.shape
    x = x_ref[...].astype(F32)
    r = lax.rsqrt(jnp.mean(x * x, axis=-1, keepdims=True) + EPS)
    y = x * r * g_ref[...]
    if rope:
        cos_ref, sin_ref, p_ref = rest[:3]
        rest = rest[3:]
        yr = jnp.dot(y.reshape(hb * ts, hd).astype(BF16), p_ref[...],
                     preferred_element_type=F32).reshape(hb, ts, hd)
        y = y * cos_ref[...] + yr * sin_ref[...]
    if aug:
        side_ref, o_ref, nmax_ref = rest
        norm = jnp.sqrt(jnp.sum(y * y, axis=-1, keepdims=True))
        y = jnp.concatenate([y, norm * side_ref[...]], axis=-1)
        nmax_ref[...] = jnp.broadcast_to(jnp.max(norm, axis=1, keepdims=True), nmax_ref.shape)
    else:
        o_ref, = rest
    o_ref[...] = y.astype(o_ref.dtype)


def qk_prep(x, gains, hb, n_blocks, tables=None, side=None):
    n_h, n_tok, hd = x.shape
    ts = PREP_TILE
    rope = tables is not None
    aug = side is not None
    n_used = n_blocks * hb
    blk = pl.BlockSpec((hb, ts, hd), lambda i, s: (i, s, 0))
    per_head = pl.BlockSpec((hb, 1, hd), lambda i, s: (i, 0, 0))
    in_specs = [blk, per_head]
    args = [x, gains.reshape(n_used, 1, hd).astype(F32)]
    if rope:
        tab = pl.BlockSpec((ts, hd), lambda i, s: (s, 0))
        in_specs += [tab, tab, pl.BlockSpec((hd, hd), lambda i, s: (0, 0))]
        args += list(tables)
    if aug:
        in_specs.append(per_head)
        args.append(side.reshape(n_used, 1, hd).astype(F32))
        out_specs = [pl.BlockSpec((hb, ts, 2 * hd), lambda i, s: (i, s, 0)),
                     pl.BlockSpec((hb, None, 1, 2 * hd), lambda i, s: (i, s, 0, 0))]
        out_shape = [jax.ShapeDtypeStruct((n_used, n_tok, 2 * hd), BF16),
                     jax.ShapeDtypeStruct((n_used, n_tok // ts, 1, 2 * hd), F32)]
    else:
        out_specs = blk
        out_shape = jax.ShapeDtypeStruct(x.shape, BF16)
    return pl.pallas_call(
        functools.partial(_prep_kernel, rope=rope, aug=aug),
        grid=(n_blocks, n_tok // ts),
        in_specs=in_specs,
        out_specs=out_specs,
        out_shape=out_shape,
        input_output_aliases={} if aug else {0: 0},
        compiler_params=_params(("parallel", "parallel")),
        name="qk_prep",
    )(*args)


def _rope_angles(pos, dim):
    inv = ROPE_THETA ** (-jnp.arange(0, dim, 2, dtype=F32) / dim)
    ang = pos.astype(F32)[:, None] * inv[None, :]
    ang = jnp.concatenate([ang, ang], axis=-1)
    return jnp.cos(ang), jnp.sin(ang)


def _rotate_half_matrix(dim, n_blocks):
    half = dim // 2
    i = jnp.arange(dim * n_blocks)
    blk, w = i // dim, i % dim
    src = blk * dim + jnp.where(w < half, w + half, w - half)
    sign = jnp.where(w < half, -1.0, 1.0)
    p = jnp.zeros((dim * n_blocks, dim * n_blocks), F32).at[src, i].set(sign)
    return p.astype(BF16)


def _na_bias_table(rpb):
    n_h = rpb.shape[0]
    qc = jnp.arange(GRID_W)[:, None]
    kc = jnp.arange(GRID_W)[None, :]
    c0 = jnp.clip(qc - NA_KW // 2, 0, GRID_W - NA_KW)
    ok = (kc >= c0) & (kc < c0 + NA_KW)
    n_dr = rpb.shape[1]
    period = 2 * GRID_W
    ext = jnp.concatenate([rpb[..., NA_KW - 1:],
                           jnp.zeros((n_h, n_dr, period - (2 * NA_KW - 1)), rpb.dtype),
                           rpb[..., :NA_KW - 1]], axis=-1)
    toep = jnp.tile(ext, (1, 1, GRID_W))[..., :GRID_W * (period - 1)]
    toep = toep.reshape(n_h, n_dr, GRID_W, period - 1)[..., :GRID_W]
    toep = jnp.where(ok[None, None], toep, NEG)
    bias = jnp.stack([toep[:, NA_KH - 1 - d0:2 * NA_KH - 1 - d0] for d0 in range(NA_KH)])
    bias = bias.transpose(0, 1, 3, 2, 4).reshape(NA_KH, n_h, GRID_W, NA_KH * GRID_W)
    return bias.astype(BF16)


def _na_kernel(q_ref, kp_ref, kc_ref, kn_ref, vp_ref, vc_ref, vn_ref, tbl_ref,
               o_ref, ks_ref, vs_ref, *, geom):
    blk = NA_BLOCK
    j = pl.program_id(0)
    seq_start, seq_len = geom.seq_span(j * blk)
    rows = seq_len // GRID_W
    jl = j - seq_start // blk
    ks_ref[:, 0:blk, :] = kp_ref[...]
    ks_ref[:, blk:2 * blk, :] = kc_ref[...]
    ks_ref[:, 2 * blk:3 * blk, :] = kn_ref[...]
    vs_ref[:, 0:blk, :] = vp_ref[...]
    vs_ref[:, blk:2 * blk, :] = vc_ref[...]
    vs_ref[:, 2 * blk:3 * blk, :] = vn_ref[...]
    n_keys = NA_KH * GRID_W

    def body(ii, carry):
        probs = []
        for u in range(NA_ROWS_PER_TRIP):
            i = ii * NA_ROWS_PER_TRIP + u
            r = jl * NA_KH + i
            r0 = jnp.clip(r - NA_KH // 2, 0, rows - NA_KH)
            off = pl.multiple_of((r0 - (jl - 1) * NA_KH) * GRID_W, GRID_W)
            d0 = r - r0
            qoff = pl.multiple_of(i * GRID_W, GRID_W)
            for h in range(NA_HEADS):
                q = q_ref[h, pl.ds(qoff, GRID_W), :]
                k = ks_ref[h, pl.ds(off, n_keys), :]
                s = lax.dot_general(q, k, (((1,), (1,)), ((), ())), preferred_element_type=F32)
                probs.append((h, off, qoff, s + tbl_ref[d0, h].astype(F32)))
        soft = []
        for h, off, qoff, s in probs:
            m = jnp.max(s, axis=-1, keepdims=True)
            p = jnp.exp(s - m)
            soft.append((h, off, qoff, p.astype(BF16), jnp.sum(p, axis=-1, keepdims=True)))
        for h, off, qoff, p, l in soft:
            v = vs_ref[h, pl.ds(off, n_keys), :]
            o = jnp.dot(p, v, preferred_element_type=F32) / l
            o_ref[h, pl.ds(qoff, GRID_W), :] = o.astype(o_ref.dtype)
        return carry

    lax.fori_loop(0, NA_KH // NA_ROWS_PER_TRIP, body, 0)


def neighborhood_attention(qkv, tbl, geom):
    _, n_tok, hd = qkv.shape
    n_h = NA_HEADS
    blk = NA_BLOCK
    nb = n_tok // blk

    def spec(head0, shift):
        hb = head0 // n_h
        return pl.BlockSpec((n_h, blk, hd), lambda j: (hb, jnp.clip(j + shift, 0, nb - 1), 0))

    return pl.pallas_call(
        functools.partial(_na_kernel, geom=geom),
        grid=(nb,),
        in_specs=[spec(NA_Q0, 0), spec(NA_K0, -1), spec(NA_K0, 0), spec(NA_K0, 1),
                  spec(NA_V0, -1), spec(NA_V0, 0), spec(NA_V0, 1),
                  pl.BlockSpec(tbl.shape, lambda j: (0, 0, 0, 0))],
        out_specs=pl.BlockSpec((n_h, blk, hd), lambda j: (0, j, 0)),
        out_shape=jax.ShapeDtypeStruct((n_h, n_tok, hd), BF16),
        scratch_shapes=[pltpu.VMEM((n_h, 3 * blk, hd), BF16),
                        pltpu.VMEM((n_h, 3 * blk, hd), BF16)],
        compiler_params=_params(("parallel",), VMEM_LIMIT),
        name="neighborhood_attention",
    )(qkv, qkv, qkv, qkv, qkv, qkv, qkv, tbl)


def _dil_kernel(q_ref, kp_ref, kc_ref, kn_ref, vp_ref, vc_ref, vn_ref,
                o_ref, lse_ref, ks_ref, vs_ref, *, geom, dil):
    c = pl.program_id(0)
    n_h, chunk, hd = q_ref.shape
    side = DIL_SIDE
    qb = DIL_QBLOCK
    kb = qb + 2 * side
    ks_ref[:, 0:side, :] = kp_ref[...]
    ks_ref[:, side:side + chunk, :] = kc_ref[...]
    ks_ref[:, side + chunk:, :] = kn_ref[...]
    vs_ref[:, 0:side, :] = vp_ref[...]
    vs_ref[:, side:side + chunk, :] = vc_ref[...]
    vs_ref[:, side + chunk:, :] = vn_ref[...]
    plane_rows = geom.n_tok // dil
    t0 = ((c * chunk) % plane_rows) * dil
    seq_start, seq_len = geom.seq_span(t0)
    row0 = (t0 - seq_start) // dil
    cls_rows = seq_len // dil
    jq = lax.broadcasted_iota(jnp.int32, (qb, kb), 0)
    jk = lax.broadcasted_iota(jnp.int32, (qb, kb), 1) - side
    near_bias = jnp.where(jnp.abs(jk - jq) <= side, 0.0, NEG)
    before_bias = jnp.where(jk < 0, NEG, 0.0)
    after_bias = jnp.where(jk >= qb, NEG, 0.0)
    n_trip = min(DIL_PROBLEMS_PER_TRIP, chunk // qb)
    trips_per_head = chunk // (qb * n_trip)
    assert chunk % (qb * n_trip) == 0

    def body(it, carry):
        h = it // trips_per_head
        ii = it % trips_per_head
        probs = []
        for u in range(n_trip):
            i = ii * n_trip + u
            qoff = pl.multiple_of(i * qb, qb)
            p0 = row0 + i * qb
            bias = near_bias + jnp.where(p0 == 0, before_bias, 0.0)
            bias = bias + jnp.where(p0 + qb == cls_rows, after_bias, 0.0)
            q = q_ref[h, pl.ds(qoff, qb), :]
            k = ks_ref[h, pl.ds(qoff, kb), :]
            s = lax.dot_general(q, k, (((1,), (1,)), ((), ())), preferred_element_type=F32)
            probs.append((qoff, s + bias))
        soft = []
        for qoff, s in probs:
            m = jnp.max(s, axis=-1, keepdims=True)
            p = jnp.exp(s - m)
            soft.append((qoff, p.astype(BF16), m, jnp.sum(p, axis=-1, keepdims=True)))
        for qoff, p, m, l in soft:
            v = vs_ref[h, pl.ds(qoff, kb), :]
            o = jnp.dot(p, v, preferred_element_type=F32) / l
            o_ref[h, pl.ds(qoff, qb), :] = o.astype(o_ref.dtype)
            lse_ref[h, pl.ds(qoff, qb), :] = jnp.broadcast_to(m + jnp.log(l), (qb, hd))
        return carry

    lax.fori_loop(0, n_h * trips_per_head, body, 0)


def dilated_attention(qkv, dil, geom):
    _, n_tok, hd = qkv.shape
    n_h = DIL_HEADS
    chunk = min(DIL_CHUNK, min(geom.s_p, geom.s_d) // dil)
    assert chunk % DIL_QBLOCK == 0
    per = chunk // DIL_SIDE
    n_side = n_tok // DIL_SIDE

    def cur(hb):
        return pl.BlockSpec((n_h, chunk, hd), lambda c: (hb, c, 0))

    def prev(hb):
        return pl.BlockSpec((n_h, DIL_SIDE, hd), lambda c: (hb, jnp.maximum(c * per - 1, 0), 0))

    def nxt(hb):
        return pl.BlockSpec((n_h, DIL_SIDE, hd),
                            lambda c: (hb, jnp.minimum((c + 1) * per, n_side - 1), 0))

    return pl.pallas_call(
        functools.partial(_dil_kernel, geom=geom, dil=dil),
        grid=(n_tok // chunk,),
        in_specs=[cur(0), prev(1), cur(1), nxt(1), prev(2), cur(2), nxt(2)],
        out_specs=[cur(0), cur(0)],
        out_shape=[jax.ShapeDtypeStruct((n_h, n_tok, hd), BF16),
                   jax.ShapeDtypeStruct((n_h, n_tok, hd), F32)],
        scratch_shapes=[pltpu.VMEM((n_h, chunk + 2 * DIL_SIDE, hd), BF16),
                        pltpu.VMEM((n_h, chunk + 2 * DIL_SIDE, hd), BF16)],
        compiler_params=_params(("parallel",), VMEM_LIMIT),
        name="dilated_attention",
    )(qkv, qkv, qkv, qkv, qkv, qkv, qkv)


def _to_natural(src_ref, dst_ref):
    n_h, dil, n, _ = src_ref.shape
    for h in range(0, n_h, 2):
        for r in range(dil):
            pair = jnp.concatenate([src_ref[h, r].astype(F32), src_ref[h + 1, r].astype(F32)],
                                   axis=-1)
            if dil == 1:
                dst_ref[h // 2] = pair
            else:
                dst_ref[h // 2, pl.ds(r, n, stride=dil), :] = pair


def _outproj_ab_kernel(x_ref, mod_ref, oa_ref, od0_ref, od1_ref, od2_ref,
                       l0_ref, l1_ref, l2_ref, w_ref, o_ref, *scratch):
    so = scratch[:N_DIL]
    sl = scratch[N_DIL:]
    for src, dst in zip((od0_ref, od1_ref, od2_ref, l0_ref, l1_ref, l2_ref), so + sl):
        _to_natural(src, dst)
    n_h = oa_ref.shape[0]
    slabs = [jnp.concatenate([oa_ref[h], oa_ref[h + 1]], axis=-1) for h in range(0, n_h, 2)]
    for s in range(n_h // 2):
        l0, l1, l2 = sl[0][s], sl[1][s], sl[2][s]
        m = jnp.maximum(jnp.maximum(l0, l1), l2)
        e0, e1, e2 = jnp.exp(l0 - m), jnp.exp(l1 - m), jnp.exp(l2 - m)
        ob = (e0 * so[0][s] + e1 * so[1][s] + e2 * so[2][s]) / (e0 + e1 + e2)
        slabs.append(ob.astype(BF16))
    cat = jnp.concatenate(slabs, axis=-1)
    res = jnp.dot(cat, w_ref[...], preferred_element_type=F32)
    o_ref[...] = x_ref[...] + mod_ref[2:3, :] * res


def outproj_ab(x, mod, oa, od, lse, w_bf16, layer, geom):
    n_tok, d = x.shape
    n_h, _, hd = oa.shape
    tm = TOKEN_TILE // 2

    def cls(a):
        dil = a.shape[1]
        return pl.BlockSpec((n_h, dil, tm // dil, hd), lambda i: (0, 0, i, 0))

    return pl.pallas_call(
        _outproj_ab_kernel,
        grid=(n_tok // tm,),
        in_specs=[
            pl.BlockSpec((tm, d), lambda i: (i, 0)),
            pl.BlockSpec((None, None, 6, d), lambda i: (layer, geom.seq_of_token(i * tm), 0, 0)),
            pl.BlockSpec((n_h, tm, hd), lambda i: (0, i, 0)),
            *[cls(a) for a in od], *[cls(a) for a in lse],
            pl.BlockSpec(w_bf16.shape, lambda i: (0, 0)),
        ],
        out_specs=pl.BlockSpec((tm, d), lambda i: (i, 0)),
        out_shape=jax.ShapeDtypeStruct(x.shape, F32),
        scratch_shapes=[pltpu.VMEM((n_h // 2, tm, LANES), F32) for _ in range(2 * N_DIL)],
        compiler_params=_params(("parallel",), VMEM_LIMIT),
        name="outproj_ab",
    )(x, mod, oa, *od, *lse, w_bf16)


def _outproj_kernel(x_ref, mod_ref, o_in_ref, w_ref, o_ref):
    res = jnp.dot(o_in_ref[...], w_ref[...], preferred_element_type=F32)
    o_ref[...] = x_ref[...] + mod_ref[2:3, :] * res


def outproj(x, mod, o_in, w_bf16, layer, geom):
    n_tok, d = x.shape
    tm = TOKEN_TILE
    return pl.pallas_call(
        _outproj_kernel,
        grid=(n_tok // tm,),
        in_specs=[
            pl.BlockSpec((tm, d), lambda i: (i, 0)),
            pl.BlockSpec((None, None, 6, d), lambda i: (layer, geom.seq_of_token(i * tm), 0, 0)),
            pl.BlockSpec((tm, o_in.shape[1]), lambda i: (i, 0)),
            pl.BlockSpec(w_bf16.shape, lambda i: (0, 0)),
        ],
        out_specs=pl.BlockSpec((tm, d), lambda i: (i, 0)),
        out_shape=jax.ShapeDtypeStruct(x.shape, F32),
        compiler_params=_params(("parallel",), VMEM_LIMIT),
        name="outproj",
    )(x, mod, o_in, w_bf16)


def _flash_write_out(acc_ref, o_ref, n_g, tq):
    hd = HEAD_DIM
    acc = acc_ref[...]
    o = acc[:hd] / acc[hd:hd + 1]
    pad = jnp.zeros((LANES - hd, tq), F32)
    for g in range(n_g):
        t = jnp.concatenate([o[:, g * tq:(g + 1) * tq], pad], axis=0).T
        o_ref[:, g * hd:(g + 1) * hd] = t[:, :hd].astype(o_ref.dtype)


def _flash_bounded_kernel(q_ref, k_ref, vt_ref, kmx_ref, _o_in_ref, o_ref, acc_ref, *, unroll):
    n_g, tq, kw = q_ref.shape
    n_kt, tk, _ = k_ref.shape
    acc_ref[...] = jnp.zeros(acc_ref.shape, F32)
    kmx = kmx_ref[...]
    n_col = (n_g * tq) // MXU_TILE

    n_kc = tk // MXU_TILE

    def qk(j, n, kc):
        rows = slice(kc * MXU_TILE, (kc + 1) * MXU_TILE)
        q_n = q_ref[...].reshape(n_g * tq, kw)[n * MXU_TILE:(n + 1) * MXU_TILE]
        return lax.dot_general(k_ref[j, rows, :] + kmx, q_n, (((1,), (1,)), ((), ())),
                               preferred_element_type=F32)

    def body(jj, carry):
        units = [(unroll * jj + u, n, kc) for u in range(unroll) for n in range(n_col)
                 for kc in range(n_kc)]
        pending = [qk(*unit) for unit in units[:FLASH_LOOKAHEAD]]
        for idx, (j, n, kc) in enumerate(units):
            if idx + FLASH_LOOKAHEAD < len(units):
                pending.append(qk(*units[idx + FLASH_LOOKAHEAD]))
            cols = slice(n * MXU_TILE, (n + 1) * MXU_TILE)
            rows = slice(kc * MXU_TILE, (kc + 1) * MXU_TILE)
            p = jnp.exp2(pending.pop(0)).astype(BF16)
            acc_ref[:, cols] += jnp.dot(vt_ref[j, :, rows], p, preferred_element_type=F32)
        return carry

    lax.fori_loop(0, n_kt // unroll, body, 0)
    _flash_write_out(acc_ref, o_ref, n_g, tq)


def _flash_kernel(q_ref, k_ref, vt_ref, _o_in_ref, o_ref, sa_ref, sb_ref, mxa_ref, mxb_ref,
                  m_ref, acc_ref, *, unroll):
    n_g, tq, hd = q_ref.shape
    n_kt = k_ref.shape[0]
    m_ref[...] = jnp.full(m_ref.shape, -jnp.inf, F32)
    acc_ref[...] = jnp.zeros(acc_ref.shape, F32)
    n_col = (n_g * tq) // MXU_TILE

    def scores(j, s_ref, mx_ref, n):
        cols = slice(n * MXU_TILE, (n + 1) * MXU_TILE)
        q_n = q_ref[...].reshape(n_g * tq, hd)[n * MXU_TILE:(n + 1) * MXU_TILE]
        s = lax.dot_general(k_ref[j], q_n, (((1,), (1,)), ((), ())),
                            preferred_element_type=F32)
        s_ref[:, cols] = s
        mx_ref[:, cols] = jnp.max(s, axis=0, keepdims=True)

    def consume(j, s_ref, mx_ref, n):
        cols = slice(n * MXU_TILE, (n + 1) * MXU_TILE)
        m_old = m_ref[:, cols]
        m_new = jnp.maximum(m_old, mx_ref[:, cols])
        alpha = jnp.exp2(m_old - m_new)
        m_ref[:, cols] = m_new
        tk = s_ref.shape[0]
        acc = alpha * acc_ref[:, cols]
        for kc in range(tk // MXU_TILE):
            rows = slice(kc * MXU_TILE, (kc + 1) * MXU_TILE)
            p = jnp.exp2(s_ref[rows, cols] - m_new).astype(BF16)
            acc = acc + jnp.dot(vt_ref[j, :, rows], p, preferred_element_type=F32)
        acc_ref[:, cols] = acc

    for n in range(n_col):
        scores(0, sa_ref, mxa_ref, n)

    def body(jj, carry):
        j = unroll * jj
        for u in range(0, unroll, 2):
            for n in range(n_col):
                scores(j + u + 1, sb_ref, mxb_ref, n)
                consume(j + u, sa_ref, mxa_ref, n)
            for n in range(n_col):
                scores(jnp.minimum(j + u + 2, n_kt - 1), sa_ref, mxa_ref, n)
                consume(j + u + 1, sb_ref, mxb_ref, n)
        return carry

    lax.fori_loop(0, n_kt // unroll, body, 0)
    _flash_write_out(acc_ref, o_ref, n_g, tq)


def flash_attention(qk, vt_tiles, o_buf, tok0, n_b, s_n, seq_idx0, kmx=None):
    n_all, n_tok, kw = qk.shape
    n_kv, _, hv, tk = vt_tiles.shape
    n_g = C_Q_HEADS // n_kv
    hd = HEAD_DIM
    tq = FLASH_TQ
    nq = s_n // tq
    kt_per = s_n // tk
    unroll = min(FLASH_UNROLL if kmx is None else FLASH_BOUNDED_UNROLL, kt_per)
    assert kt_per % unroll == 0 and unroll % 2 == 0 and tok0 % s_n == 0
    wq = n_g * tq
    k_view = qk.reshape(n_all, n_tok // tk, tk, kw)
    seq0 = tok0 // s_n
    in_specs = [
        pl.BlockSpec((n_g, tq, kw), lambda b, h, i: (h, (tok0 + b * s_n) // tq + i, 0)),
        pl.BlockSpec((None, kt_per, tk, kw), lambda b, h, i: (C_K0 + h, seq0 + b, 0, 0)),
        pl.BlockSpec((None, kt_per, hv, tk), lambda b, h, i: (h, seq0 + b, 0, 0)),
    ]
    args = [qk, k_view, vt_tiles]
    if kmx is None:
        body = functools.partial(_flash_kernel, unroll=unroll)
        scratch = [pltpu.VMEM((tk, wq), F32), pltpu.VMEM((tk, wq), F32),
                   pltpu.VMEM((1, wq), F32), pltpu.VMEM((1, wq), F32),
                   pltpu.VMEM((1, wq), F32), pltpu.VMEM((hv, wq), F32)]
    else:
        body = functools.partial(_flash_bounded_kernel, unroll=unroll)
        scratch = [pltpu.VMEM((hv, wq), F32)]
        in_specs.append(pl.BlockSpec((None, None, 1, kw), lambda b, h, i: (h, seq_idx0 + b, 0, 0)))
        args.append(kmx)
    in_specs.append(pl.BlockSpec(memory_space=pl.ANY))
    args.append(o_buf)
    return pl.pallas_call(
        body,
        grid=(n_b, n_kv, nq),
        in_specs=in_specs,
        out_specs=pl.BlockSpec((tq, n_g * hd), lambda b, h, i: ((tok0 + b * s_n) // tq + i, h)),
        out_shape=jax.ShapeDtypeStruct(o_buf.shape, o_buf.dtype),
        input_output_aliases={len(args) - 1: 0},
        scratch_shapes=scratch,
        compiler_params=_params(("parallel", "parallel", "arbitrary"), VMEM_LIMIT),
        name="flash_attention",
    )(*args)


def _router_kernel(x_ref, mod_ref, g_ref, wr_ref, br_ref, h_ref, idx_ref, gate_ref, cnt_ref,
                   run_ref):
    @pl.when(pl.program_id(0) == 0)
    def _():
        run_ref[...] = jnp.zeros(run_ref.shape, F32)

    h = _modulated_norm(x_ref[...], g_ref[...], mod_ref[3:4, :], mod_ref[4:5, :])
    h_ref[...] = h.astype(BF16)
    logits = lax.dot_general(wr_ref[...], h, (((1,), (1,)), ((), ())), precision=HIGHEST,
                             preferred_element_type=F32) + br_ref[...]
    n_e, tm = logits.shape
    iota = lax.broadcasted_iota(jnp.int32, (n_e, tm), 0)
    vals, ids = [], []
    cur = logits
    for _ in range(TOP_K):
        m = jnp.max(cur, axis=0, keepdims=True)
        am = jnp.min(jnp.where(cur == m, iota, n_e), axis=0, keepdims=True)
        vals.append(m)
        ids.append(am)
        cur = jnp.where(iota == am, -jnp.inf, cur)
    es = [jnp.exp(v - vals[0]) for v in vals]
    den = es[0] + es[1] + es[2] + es[3]
    chosen = jnp.zeros((n_e, tm), F32)
    for kk in range(TOP_K):
        chosen = chosen + jnp.where(iota == ids[kk], 1.0, 0.0)
    earlier = lax.broadcasted_iota(jnp.int32, (tm, tm), 0)
    token = lax.broadcasted_iota(jnp.int32, (tm, tm), 1)
    tri = jnp.where(earlier < token, 1.0, 0.0).astype(BF16)
    before = jnp.dot(chosen.astype(BF16), tri, preferred_element_type=F32) + run_ref[...]
    run_ref[...] = run_ref[...] + jnp.sum(chosen, axis=1, keepdims=True)
    cnt_ref[...] = run_ref[...]
    row = lax.broadcasted_iota(jnp.int32, idx_ref.shape, 0)
    idx_out = jnp.zeros(idx_ref.shape, jnp.int32)
    gate_out = jnp.zeros(gate_ref.shape, F32)
    for kk in range(TOP_K):
        rank = jnp.sum(jnp.where(iota == ids[kk], before, 0.0), axis=0, keepdims=True)
        idx_out = jnp.where(row == kk, ids[kk], idx_out)
        idx_out = jnp.where(row == TOP_K + kk, rank.astype(jnp.int32), idx_out)
        gate_out = jnp.where(row == kk, es[kk] / den, gate_out)
    idx_ref[...] = idx_out
    gate_ref[...] = gate_out


def router(x, mod, g_norm, w_router, b_router, layer, geom, tile0, n_tiles):
    d = x.shape[1]
    tm = MOE_TOKEN_TILE
    n_tok = n_tiles * tm
    return pl.pallas_call(
        _router_kernel,
        grid=(n_tiles,),
        in_specs=[
            pl.BlockSpec((tm, d), lambda i: (tile0 + i, 0)),
            pl.BlockSpec((None, None, 6, d),
                         lambda i: (layer, geom.seq_of_token((tile0 + i) * tm), 0, 0)),
            pl.BlockSpec((None, 1, d), lambda i: (layer, 0, 0)),
            pl.BlockSpec((None, N_EXPERTS, d), lambda i: (layer, 0, 0)),
            pl.BlockSpec((None, N_EXPERTS, 1), lambda i: (layer, 0, 0)),
        ],
        out_specs=[
            pl.BlockSpec((tm, d), lambda i: (i, 0)),
            pl.BlockSpec((2 * TOP_K, tm), lambda i: (0, i)),
            pl.BlockSpec((2 * TOP_K, tm), lambda i: (0, i)),
            pl.BlockSpec((N_EXPERTS, 1), lambda i: (0, 0)),
        ],
        out_shape=[
            jax.ShapeDtypeStruct((n_tok, d), BF16),
            jax.ShapeDtypeStruct((2 * TOP_K, n_tok), jnp.int32),
            jax.ShapeDtypeStruct((2 * TOP_K, n_tok), F32),
            jax.ShapeDtypeStruct((N_EXPERTS, 1), F32),
        ],
        scratch_shapes=[pltpu.VMEM((N_EXPERTS, 1), F32)],
        compiler_params=_params(("arbitrary",), VMEM_LIMIT),
        name="router",
    )(x, mod, g_norm.reshape(-1, 1, d), w_router.transpose(0, 2, 1),
      b_router.reshape(-1, N_EXPERTS, 1))


def _moe_kernel(be_ref, nu_ref, x_ref, wgu_ref, bgu_ref, wd_ref, bd_ref, o_ref,
                wgu_s, wd_s):
    i = pl.program_id(0)
    e = be_ref[i]
    e_prev = be_ref[jnp.maximum(i - 1, 0)]

    @pl.when((i == 0) | (e != e_prev))
    def _():
        wgu_s[...] = wgu_ref[...].astype(BF16)
        wd_s[...] = wd_ref[...].astype(BF16)

    @pl.when(i < nu_ref[0])
    def _():
        d_ff = wd_s.shape[0]
        gu = jnp.dot(x_ref[...], wgu_s[...], preferred_element_type=F32) + bgu_ref[...]
        glu = jnp.minimum(gu[:, :d_ff], SWIGLU_LIMIT)
        lin = jnp.clip(gu[:, d_ff:], -SWIGLU_LIMIT, SWIGLU_LIMIT)
        act = glu / (1.0 + jnp.exp(-SWIGLU_ALPHA * glu)) * (lin + 1.0)
        y = jnp.dot(act.astype(BF16), wd_s[...], preferred_element_type=F32) + bd_ref[...]
        o_ref[...] = y.astype(o_ref.dtype)

    @pl.when(i >= nu_ref[0])
    def _():
        o_ref[...] = jnp.zeros(o_ref.shape, o_ref.dtype)


def moe_experts(xs, blk_e, n_used, w_gu, b_gu, w_down, b_down, layer):
    n_rows, d = xs.shape
    d_ff = w_down.shape[2]
    nblk = n_rows // MOE_BLOCK
    grid_spec = pltpu.PrefetchScalarGridSpec(
        num_scalar_prefetch=2,
        grid=(nblk,),
        in_specs=[
            pl.BlockSpec((MOE_BLOCK, d), lambda i, be, nu: (jnp.minimum(i, nu[0] - 1), 0)),
            pl.BlockSpec((None, None, d, 2 * d_ff), lambda i, be, nu: (layer, be[i], 0, 0)),
            pl.BlockSpec((None, None, 1, 2 * d_ff), lambda i, be, nu: (layer, be[i], 0, 0)),
            pl.BlockSpec((None, None, d_ff, d), lambda i, be, nu: (layer, be[i], 0, 0)),
            pl.BlockSpec((None, None, 1, d), lambda i, be, nu: (layer, be[i], 0, 0)),
        ],
        out_specs=pl.BlockSpec((MOE_BLOCK, d), lambda i, be, nu: (i, 0)),
        scratch_shapes=[pltpu.VMEM((d, 2 * d_ff), BF16), pltpu.VMEM((d_ff, d), BF16)],
    )
    depth, n_e = w_gu.shape[:2]
    return pl.pallas_call(
        _moe_kernel,
        grid_spec=grid_spec,
        out_shape=jax.ShapeDtypeStruct((n_rows, d), BF16),
        compiler_params=_params(("arbitrary",), VMEM_LIMIT),
        name="moe_experts",
    )(blk_e, n_used, xs, w_gu, b_gu.reshape(depth, n_e, 1, 2 * d_ff), w_down,
      b_down.reshape(depth, n_e, 1, d))


def _combine_kernel(x_ref, mod_ref, y_ref, gate_ref, o_ref):
    g = gate_ref[...]
    acc = g[:, 0:1] * y_ref[0].astype(F32)
    for kk in range(1, TOP_K):
        acc = acc + g[:, kk:kk + 1] * y_ref[kk].astype(F32)
    o_ref[...] = x_ref[...] + mod_ref[5:6, :] * acc


def moe_combine(x, mod, yk, gates, layer, geom, tile0, n_tiles):
    d = x.shape[1]
    tm = MOE_TOKEN_TILE
    return pl.pallas_call(
        _combine_kernel,
        grid=(n_tiles,),
        in_specs=[
            pl.BlockSpec((tm, d), lambda i: (tile0 + i, 0)),
            pl.BlockSpec((None, None, 6, d),
                         lambda i: (layer, geom.seq_of_token((tile0 + i) * tm), 0, 0)),
            pl.BlockSpec((TOP_K, tm, d), lambda i: (0, i, 0)),
            pl.BlockSpec((tm, 2 * TOP_K), lambda i: (i, 0)),
        ],
        out_specs=pl.BlockSpec((tm, d), lambda i: (tile0 + i, 0)),
        out_shape=jax.ShapeDtypeStruct(x.shape, F32),
        input_output_aliases={0: 0},
        compiler_params=_params(("parallel",), VMEM_LIMIT),
        name="moe_combine",
    )(x, mod, yk, gates)


def moe_layer(x, mod, g_norm, w_router, b_router, w_gu, b_gu, w_down, b_down, layer, geom):
    n_tiles = x.shape[0] // MOE_TOKEN_TILE
    assert n_tiles % MOE_PARTS == 0
    per = n_tiles // MOE_PARTS
    routed = [_moe_dispatch(x, mod, g_norm, w_router, b_router, w_gu, b_gu, w_down, b_down, layer,
                            geom, part * per, per) for part in range(MOE_PARTS)]
    for part, (yk, gates8) in enumerate(routed):
        x = moe_combine(x, mod, yk, gates8, layer, geom, part * per, per)
    return x


def _moe_dispatch(x, mod, g_norm, w_router, b_router, w_gu, b_gu, w_down, b_down, layer, geom,
                  tile0, n_tiles):
    d = x.shape[1]
    n_tok = n_tiles * MOE_TOKEN_TILE
    h, idx8, gates8, counts = router(x, mod, g_norm, w_router, b_router, layer, geom, tile0,
                                     n_tiles)
    idx = idx8[:TOP_K].T
    rank = idx8[TOP_K:].T
    gates8 = gates8.T
    counts = counts[:, 0].astype(jnp.int32)
    pcounts = ((counts + MOE_BLOCK - 1) // MOE_BLOCK) * MOE_BLOCK
    pend = jnp.cumsum(pcounts)
    pstart = pend - pcounts
    experts = jnp.arange(N_EXPERTS, dtype=jnp.int32)
    dest = rank + jnp.sum(jnp.where(idx[:, :, None] == experts, pstart, 0), axis=-1)
    nblk = (n_tok * TOP_K) // MOE_BLOCK + N_EXPERTS
    n_rows = nblk * MOE_BLOCK
    tok_ids = jnp.broadcast_to(jnp.arange(n_tok, dtype=jnp.int32)[:, None], dest.shape)
    tok = (jnp.arange(n_rows, dtype=jnp.int32) % n_tok).at[dest.reshape(-1)].set(
        tok_ids.reshape(-1), unique_indices=True)
    n_used = (pend[-1] // MOE_BLOCK).astype(jnp.int32).reshape(1)
    blk_start = jnp.arange(nblk, dtype=jnp.int32) * MOE_BLOCK
    blk_last = jnp.minimum(blk_start, pend[-1] - 1)
    blk_e = jnp.sum(pend[None, :] <= blk_last[:, None], axis=-1, dtype=jnp.int32)
    blk_e = jnp.minimum(blk_e, N_EXPERTS - 1)
    xs = jnp.take(h, tok, axis=0, mode="clip")
    yb = moe_experts(xs, blk_e, n_used, w_gu, b_gu, w_down, b_down, layer)
    yk = jnp.take(yb, dest.T.reshape(-1), axis=0, mode="clip").reshape(TOP_K, n_tok, d)
    return yk, gates8


def _per_token(geom, tables):
    return [jnp.concatenate([t[:s] for _, n_b, s in geom.groups() for _ in range(n_b)])
            for t in tables]


def _qk_gains(g_q, g_k, n_q, n_k, q_scale):
    hd = g_q.shape[-1]
    return jnp.concatenate([jnp.broadcast_to(g_q * q_scale, (n_q, hd)),
                            jnp.broadcast_to(g_k, (n_k, hd))])


def _dilated_group_weight(w_in, gi):
    blk = DIL_HEADS * HEAD_DIM
    base = 3 * NA_HEADS * HEAD_DIM
    return jnp.concatenate([w_in[:, base + (a * N_DIL + gi) * blk:base + (a * N_DIL + gi + 1) * blk]
                            for a in range(3)], axis=1)


def mixer_ab_layer(x, mod, g_norm, w_in, w_out, g_qn_a, g_kn_a, rpb, g_qn_b, g_kn_b, layer, geom):
    n_tok, d = x.shape
    hd = HEAD_DIM
    w = w_in.astype(BF16)
    n_h = NA_HEADS
    qkv = norm_proj(x, mod, g_norm, w[:, :3 * n_h * hd], layer, geom, dil=1)
    qkv = qk_prep(qkv.reshape(3 * n_h, n_tok, hd), _qk_gains(g_qn_a, g_kn_a, n_h, n_h, Q_SCALE),
                  hb=n_h, n_blocks=2)
    oa = neighborhood_attention(qkv, _na_bias_table(rpb), geom)
    cos, sin = _per_token(geom, _rope_angles(jnp.arange(max(geom.s_p, geom.s_d)), hd))
    perm = _rotate_half_matrix(hd, 1)
    od, lse = [], []
    for gi, (window, dil) in enumerate(DIL_PATTERNS):
        assert (window // 2) // dil == DIL_SIDE
        n_h = DIL_HEADS

        def class_major(t):
            return t.reshape(n_tok // dil, dil, hd).transpose(1, 0, 2).reshape(n_tok, hd)

        qkv = norm_proj(x, mod, g_norm, _dilated_group_weight(w, gi), layer, geom, dil=dil)
        qkv = qk_prep(qkv.reshape(3 * n_h, n_tok, hd),
                      _qk_gains(g_qn_b[gi], g_kn_b[gi], n_h, n_h, Q_SCALE), hb=n_h, n_blocks=2,
                      tables=(class_major(cos), class_major(sin), perm))
        o_g, lse_g = dilated_attention(qkv, dil, geom)
        od.append(o_g.reshape(n_h, dil, n_tok // dil, hd))
        lse.append(lse_g.reshape(n_h, dil, n_tok // dil, hd))
    return outproj_ab(x, mod, oa, od, lse, w_out.astype(BF16), layer, geom)


def mixer_c_layer(x, mod, g_norm, w_in, w_out, g_qn, g_kn, layer, geom):
    n_tok, d = x.shape
    hd = HEAD_DIM
    n_heads = C_Q_HEADS + 2 * C_KV_HEADS
    qkv = norm_proj(x, mod, g_norm, w_in.astype(BF16), layer, geom, dil=1).reshape(n_heads, n_tok, hd)
    pos = jnp.arange(max(geom.s_p, geom.s_d))
    half = hd // 2
    cr, sr = _rope_angles(pos // GRID_W, half)
    cc, sc = _rope_angles(pos % GRID_W, half)
    cos, sin = _per_token(geom, (jnp.concatenate([cr, cc], axis=-1),
                                 jnp.concatenate([sr, sc], axis=-1)))
    tables = (cos, sin, _rotate_half_matrix(half, 2))
    side = jnp.zeros((C_Q_HEADS + C_KV_HEADS, hd), F32).at[:C_Q_HEADS, 0].set(-FLASH_BOUND_MARGIN)
    qk, nmax = qk_prep(qkv, _qk_gains(g_qn, g_kn, C_Q_HEADS, C_KV_HEADS, Q_SCALE * LOG2_E),
                       hb=C_KV_HEADS, n_blocks=(C_Q_HEADS + C_KV_HEADS) // C_KV_HEADS,
                       tables=tables, side=side)
    tk = FLASH_TK
    vt_tiles = qkv[C_V0:].reshape(C_KV_HEADS, n_tok // tk, tk, hd).transpose(0, 1, 3, 2)
    ones = jnp.ones((C_KV_HEADS, n_tok // tk, BF16_SUBLANES, tk), BF16)
    vt_tiles = jnp.concatenate([vt_tiles, ones], axis=2)

    ts = PREP_TILE
    tile_max = nmax[:, :, 0, 0]
    seq_max = jnp.concatenate(
        [tile_max[:, t0 // ts:(t0 + n_b * s_n) // ts].reshape(-1, n_b, s_n // ts).max(-1)
         for t0, n_b, s_n in geom.groups()], axis=1)
    n_g = C_Q_HEADS // C_KV_HEADS
    k_max = (seq_max[C_K0:] * FLASH_BOUND_MARGIN).astype(BF16)
    q_max = seq_max[:C_K0] * FLASH_BOUND_MARGIN
    bound = jnp.max(q_max.reshape(C_KV_HEADS, n_g, -1) * k_max.astype(F32)[:, None, :])
    kmx = jnp.zeros((C_KV_HEADS, geom.n_seq, 1, 2 * hd), BF16).at[:, :, 0, hd].set(k_max)
    o_buf = jnp.zeros((n_tok, C_Q_HEADS * hd), BF16)

    def attend(kmx_or_none):
        o = o_buf
        seq_idx0 = 0
        for tok0, n_b, s_n in geom.groups():
            o = flash_attention(qk, vt_tiles, o, tok0, n_b, s_n, seq_idx0, kmx_or_none)
            seq_idx0 += n_b
        return o

    o = lax.cond(bound <= FLASH_SAFE_BOUND, lambda: attend(kmx), lambda: attend(None))
    return outproj(x, mod, o, w_out.astype(BF16), layer, geom)


def kernel(x_prompt, x_sample, c_prompt, c_sample, w_ada, b_ada, g_norm_mix, g_norm_ffn, w_in_ab, w_out_ab, g_qn_a, g_kn_a, rpb_a, g_qn_b, g_kn_b, w_in_c, w_out_c, g_qn_c, g_kn_c, w_router, b_router, w_gate_up, b_gate_up, w_down, b_down):
    b_p, s_p, d = x_prompt.shape
    b_d, s_d, _ = x_sample.shape
    geom = Geom(b_p, s_p, b_d, s_d)
    depth = w_ada.shape[0]
    x = jnp.concatenate([x_prompt.reshape(-1, d), x_sample.reshape(-1, d)], axis=0)
    mod = ada_modulation(jnp.concatenate([c_prompt, c_sample], axis=0), w_ada, b_ada)
    for layer in range(depth):
        i = layer // 2
        if layer % 2 == 0:
            x = mixer_ab_layer(x, mod, g_norm_mix, w_in_ab[i], w_out_ab[i], g_qn_a[i], g_kn_a[i],
                               rpb_a[i], g_qn_b[i], g_kn_b[i], layer, geom)
        else:
            x = mixer_c_layer(x, mod, g_norm_mix, w_in_c[i], w_out_c[i], g_qn_c[i], g_kn_c[i],
                              layer, geom)
        x = moe_layer(x, mod, g_norm_ffn, w_router, b_router, w_gate_up, b_gate_up, w_down,
                      b_down, layer, geom)
    y_prompt = x[:geom.n_p].reshape(b_p, s_p, d)
    y_sample = x[geom.n_p:].reshape(b_d, s_d, d)
    return (y_prompt, y_sample)
```

```python
import functools
import math
from typing import NamedTuple

import jax
import jax.numpy as jnp
from jax import lax
from jax.experimental import pallas as pl
from jax.experimental.pallas import tpu as pltpu

F32 = jnp.float32
BF16 = jnp.bfloat16
HIGHEST = lax.Precision.HIGHEST

GRID_W = 64
HEAD_DIM = 64
ROPE_THETA = 10000.0
EPS = 1e-6
NEG = -1e30
NA_HEADS = 8
NA_KH = 8
NA_KW = 16
DIL_HEADS = 8
DIL_PATTERNS = ((128, 1), (512, 4), (2048, 16))
N_DIL = len(DIL_PATTERNS)
C_Q_HEADS = 16
C_KV_HEADS = 4
N_EXPERTS = 32
TOP_K = 4
SWIGLU_LIMIT = 7.0
SWIGLU_ALPHA = 1.702
Q_SCALE = HEAD_DIM ** -0.5
LOG2_E = math.log2(math.e)
BF16_SUBLANES = 16
LANES = 128
MXU_TILE = 256

V7X_VMEM_BYTES = 64 * 1024 * 1024
VMEM_LIMIT = V7X_VMEM_BYTES - 8 * 1024 * 1024

TOKEN_TILE = 1024
PREP_TILE = 2048
NA_BLOCK = 8 * GRID_W
NA_ROWS_PER_TRIP = 4
DIL_QBLOCK = 128
DIL_SIDE = 64
DIL_CHUNK = 1024
DIL_PROBLEMS_PER_TRIP = 4
FLASH_TQ = 256
FLASH_TK = 512
FLASH_UNROLL = 8
FLASH_BOUNDED_UNROLL = 16
FLASH_LOOKAHEAD = 6
FLASH_SAFE_BOUND = 60.0
FLASH_BOUND_MARGIN = 1.0 + 2.0 ** -6
MOE_BLOCK = 512
MOE_TOKEN_TILE = 512
MOE_PARTS = 1

NA_Q0, NA_K0, NA_V0 = 0, NA_HEADS, 2 * NA_HEADS
C_K0 = C_Q_HEADS
C_V0 = C_Q_HEADS + C_KV_HEADS


class Geom(NamedTuple):
    b_p: int
    s_p: int
    b_d: int
    s_d: int

    @property
    def n_p(self):
        return self.b_p * self.s_p

    @property
    def n_tok(self):
        return self.n_p + self.b_d * self.s_d

    @property
    def n_seq(self):
        return self.b_p + self.b_d

    def groups(self):
        return ((0, self.b_p, self.s_p), (self.n_p, self.b_d, self.s_d))

    def seq_of_token(self, t0):
        return jnp.where(t0 < self.n_p, t0 // self.s_p,
                         self.b_p + (t0 - self.n_p) // self.s_d)

    def seq_span(self, t0):
        in_p = t0 < self.n_p
        start = jnp.where(in_p, (t0 // self.s_p) * self.s_p,
                          self.n_p + ((t0 - self.n_p) // self.s_d) * self.s_d)
        return start, jnp.where(in_p, self.s_p, self.s_d)


def _params(semantics, vmem=None):
    return pltpu.CompilerParams(dimension_semantics=semantics,
                                vmem_limit_bytes=vmem)


def _ada_kernel(c_ref, w_ref, b_ref, o_ref):
    c = c_ref[...]
    a = c / (1.0 + jnp.exp(-c))
    o_ref[...] = jnp.dot(a, w_ref[...], precision=HIGHEST,
                         preferred_element_type=F32) + b_ref[...]


def ada_modulation(c_all, w_ada, b_ada):
    depth, d, d6 = w_ada.shape
    n_seq = c_all.shape[0]
    rows = 8
    c_pad = jnp.zeros((rows, d), F32).at[:n_seq].set(c_all)
    out = pl.pallas_call(
        _ada_kernel,
        grid=(depth, d6 // d),
        in_specs=[
            pl.BlockSpec((rows, d), lambda l, j: (0, 0)),
            pl.BlockSpec((None, d, d), lambda l, j: (l, 0, j)),
            pl.BlockSpec((None, 1, d), lambda l, j: (l, 0, j)),
        ],
        out_specs=pl.BlockSpec((None, rows, d), lambda l, j: (l, 0, j)),
        out_shape=jax.ShapeDtypeStruct((depth, rows, d6), F32),
        compiler_params=_params(("arbitrary", "arbitrary")),
        name="ada_modulation",
    )(c_pad, w_ada, b_ada.reshape(depth, 1, d6))
    return out[:, :n_seq].reshape(depth, n_seq, 6, d)


def _modulated_norm(x, g, shift, scale):
    r = lax.rsqrt(jnp.mean(x * x, axis=-1, keepdims=True) + EPS)
    return (x * r * g) * (1.0 + scale) + shift


def _norm_proj_kernel(x_ref, mod_ref, g_ref, w_ref, o_ref, *scratch, dil):
    h = _modulated_norm(x_ref[...], g_ref[...], mod_ref[0:1, :], mod_ref[1:2, :])
    acc = jnp.dot(h.astype(BF16), w_ref[...], preferred_element_type=F32)
    tm = x_ref.shape[0]
    n_h, _, _, hd = o_ref.shape
    if dil == 1:
        for c in range(n_h):
            o_ref[c, 0] = acc[:, c * hd:(c + 1) * hd].astype(o_ref.dtype)
        return
    acc_ref, = scratch
    for s in range(n_h // 2):
        acc_ref[s] = acc[:, s * LANES:(s + 1) * LANES]
    for s in range(n_h // 2):
        for r in range(dil):
            pair = acc_ref[s, pl.ds(r, tm // dil, stride=dil), :].astype(o_ref.dtype)
            o_ref[2 * s, r] = pair[:, :hd]
            o_ref[2 * s + 1, r] = pair[:, hd:]


def norm_proj(x, mod, g_norm, w_bf16, layer, geom, dil):
    n_tok, d = x.shape
    n_out = w_bf16.shape[1]
    tm = TOKEN_TILE
    hd = HEAD_DIM
    n_h = n_out // hd
    scratch = [] if dil == 1 else [pltpu.VMEM((n_h // 2, tm, LANES), F32)]
    return pl.pallas_call(
        functools.partial(_norm_proj_kernel, dil=dil),
        grid=(n_tok // tm,),
        in_specs=[
            pl.BlockSpec((tm, d), lambda i: (i, 0)),
            pl.BlockSpec((None, None, 6, d), lambda i: (layer, geom.seq_of_token(i * tm), 0, 0)),
            pl.BlockSpec((None, 1, d), lambda i: (layer, 0, 0)),
            pl.BlockSpec((d, n_out), lambda i: (0, 0)),
        ],
        out_specs=pl.BlockSpec((n_h, dil, tm // dil, hd), lambda i: (0, 0, i, 0)),
        out_shape=jax.ShapeDtypeStruct((n_h, dil, n_tok // dil, hd), BF16),
        scratch_shapes=scratch,
        compiler_params=_params(("parallel",), VMEM_LIMIT),
        name="norm_proj",
    )(x, mod, g_norm.reshape(-1, 1, d), w_bf16)


def _prep_kernel(x_ref, g_ref, *rest, rope, aug):
    rest = list(rest)
    hb, ts, hd = x_ref.shape
    x = x_ref[...].astype(F32)
    r = lax.rsqrt(jnp.mean(x * x, axis=-1, keepdims=True) + EPS)
    y = x * r * g_ref[...]
    if rope:
        cos_ref, sin_ref, p_ref = rest[:3]
        rest = rest[3:]
        yr = jnp.dot(y.reshape(hb * ts, hd).astype(BF16), p_ref[...],
                     preferred_element_type=F32).reshape(hb, ts, hd)
        y = y * cos_ref[...] + yr * sin_ref[...]
    if aug:
        side_ref, o_ref, nmax_ref = rest
        norm = jnp.sqrt(jnp.sum(y * y, axis=-1, keepdims=True))
        y = jnp.concatenate([y, norm * side_ref[...]], axis=-1)
        nmax_ref[...] = jnp.broadcast_to(jnp.max(norm, axis=1, keepdims=True), nmax_ref.shape)
    else:
        o_ref, = rest
    o_ref[...] = y.astype(o_ref.dtype)


def qk_prep(x, gains, hb, n_blocks, tables=None, side=None):
    n_h, n_tok, hd = x.shape
    ts = PREP_TILE
    rope = tables is not None
    aug = side is not None
    n_used = n_blocks * hb
    blk = pl.BlockSpec((hb, ts, hd), lambda i, s: (i, s, 0))
    per_head = pl.BlockSpec((hb, 1, hd), lambda i, s: (i, 0, 0))
    in_specs = [blk, per_head]
    args = [x, gains.reshape(n_used, 1, hd).astype(F32)]
    if rope:
        tab = pl.BlockSpec((ts, hd), lambda i, s: (s, 0))
        in_specs += [tab, tab, pl.BlockSpec((hd, hd), lambda i, s: (0, 0))]
        args += list(tables)
    if aug:
        in_specs.append(per_head)
        args.append(side.reshape(n_used, 1, hd).astype(F32))
        out_specs = [pl.BlockSpec((hb, ts, 2 * hd), lambda i, s: (i, s, 0)),
                     pl.BlockSpec((hb, None, 1, 2 * hd), lambda i, s: (i, s, 0, 0))]
        out_shape = [jax.ShapeDtypeStruct((n_used, n_tok, 2 * hd), BF16),
                     jax.ShapeDtypeStruct((n_used, n_tok // ts, 1, 2 * hd), F32)]
    else:
        out_specs = blk
        out_shape = jax.ShapeDtypeStruct(x.shape, BF16)
    return pl.pallas_call(
        functools.partial(_prep_kernel, rope=rope, aug=aug),
        grid=(n_blocks, n_tok // ts),
        in_specs=in_specs,
        out_specs=out_specs,
        out_shape=out_shape,
        input_output_aliases={} if aug else {0: 0},
        compiler_params=_params(("parallel", "parallel")),
        name="qk_prep",
    )(*args)


def _rope_angles(pos, dim):
    inv = ROPE_THETA ** (-jnp.arange(0, dim, 2, dtype=F32) / dim)
    ang = pos.astype(F32)[:, None] * inv[None, :]
    ang = jnp.concatenate([ang, ang], axis=-1)
    return jnp.cos(ang), jnp.sin(ang)


def _rotate_half_matrix(dim, n_blocks):
    half = dim // 2
    i = jnp.arange(dim * n_blocks)
    blk, w = i // dim, i % dim
    src = blk * dim + jnp.where(w < half, w + half, w - half)
    sign = jnp.where(w < half, -1.0, 1.0)
    p = jnp.zeros((dim * n_blocks, dim * n_blocks), F32).at[src, i].set(sign)
    return p.astype(BF16)


def _na_bias_table(rpb):
    n_h = rpb.shape[0]
    qc = jnp.arange(GRID_W)[:, None]
    kc = jnp.arange(GRID_W)[None, :]
    c0 = jnp.clip(qc - NA_KW // 2, 0, GRID_W - NA_KW)
    ok = (kc >= c0) & (kc < c0 + NA_KW)
    n_dr = rpb.shape[1]
    period = 2 * GRID_W
    ext = jnp.concatenate([rpb[..., NA_KW - 1:],
                           jnp.zeros((n_h, n_dr, period - (2 * NA_KW - 1)), rpb.dtype),
                           rpb[..., :NA_KW - 1]], axis=-1)
    toep = jnp.tile(ext, (1, 1, GRID_W))[..., :GRID_W * (period - 1)]
    toep = toep.reshape(n_h, n_dr, GRID_W, period - 1)[..., :GRID_W]
    toep = jnp.where(ok[None, None], toep, NEG)
    bias = jnp.stack([toep[:, NA_KH - 1 - d0:2 * NA_KH - 1 - d0] for d0 in range(NA_KH)])
    bias = bias.transpose(0, 1, 3, 2, 4).reshape(NA_KH, n_h, GRID_W, NA_KH * GRID_W)
    return bias.astype(BF16)


def _na_kernel(q_ref, kp_ref, kc_ref, kn_ref, vp_ref, vc_ref, vn_ref, tbl_ref,
               o_ref, ks_ref, vs_ref, *, geom):
    blk = NA_BLOCK
    j = pl.program_id(0)
    seq_start, seq_len = geom.seq_span(j * blk)
    rows = seq_len // GRID_W
    jl = j - seq_start // blk
    ks_ref[:, 0:blk, :] = kp_ref[...]
    ks_ref[:, blk:2 * blk, :] = kc_ref[...]
    ks_ref[:, 2 * blk:3 * blk, :] = kn_ref[...]
    vs_ref[:, 0:blk, :] = vp_ref[...]
    vs_ref[:, blk:2 * blk, :] = vc_ref[...]
    vs_ref[:, 2 * blk:3 * blk, :] = vn_ref[...]
    n_keys = NA_KH * GRID_W

    def body(ii, carry):
        probs = []
        for u in range(NA_ROWS_PER_TRIP):
            i = ii * NA_ROWS_PER_TRIP + u
            r = jl * NA_KH + i
            r0 = jnp.clip(r - NA_KH // 2, 0, rows - NA_KH)
            off = pl.multiple_of((r0 - (jl - 1) * NA_KH) * GRID_W, GRID_W)
            d0 = r - r0
            qoff = pl.multiple_of(i * GRID_W, GRID_W)
            for h in range(NA_HEADS):
                q = q_ref[h, pl.ds(qoff, GRID_W), :]
                k = ks_ref[h, pl.ds(off, n_keys), :]
                s = lax.dot_general(q, k, (((1,), (1,)), ((), ())), preferred_element_type=F32)
                probs.append((h, off, qoff, s + tbl_ref[d0, h].astype(F32)))
        soft = []
        for h, off, qoff, s in probs:
            m = jnp.max(s, axis=-1, keepdims=True)
            p = jnp.exp(s - m)
            soft.append((h, off, qoff, p.astype(BF16), jnp.sum(p, axis=-1, keepdims=True)))
        for h, off, qoff, p, l in soft:
            v = vs_ref[h, pl.ds(off, n_keys), :]
            o = jnp.dot(p, v, preferred_element_type=F32) / l
            o_ref[h, pl.ds(qoff, GRID_W), :] = o.astype(o_ref.dtype)
        return carry

    lax.fori_loop(0, NA_KH // NA_ROWS_PER_TRIP, body, 0)


def neighborhood_attention(qkv, tbl, geom):
    _, n_tok, hd = qkv.shape
    n_h = NA_HEADS
    blk = NA_BLOCK
    nb = n_tok // blk

    def spec(head0, shift):
        hb = head0 // n_h
        return pl.BlockSpec((n_h, blk, hd), lambda j: (hb, jnp.clip(j + shift, 0, nb - 1), 0))

    return pl.pallas_call(
        functools.partial(_na_kernel, geom=geom),
        grid=(nb,),
        in_specs=[spec(NA_Q0, 0), spec(NA_K0, -1), spec(NA_K0, 0), spec(NA_K0, 1),
                  spec(NA_V0, -1), spec(NA_V0, 0), spec(NA_V0, 1),
                  pl.BlockSpec(tbl.shape, lambda j: (0, 0, 0, 0))],
        out_specs=pl.BlockSpec((n_h, blk, hd), lambda j: (0, j, 0)),
        out_shape=jax.ShapeDtypeStruct((n_h, n_tok, hd), BF16),
        scratch_shapes=[pltpu.VMEM((n_h, 3 * blk, hd), BF16),
                        pltpu.VMEM((n_h, 3 * blk, hd), BF16)],
        compiler_params=_params(("parallel",), VMEM_LIMIT),
        name="neighborhood_attention",
    )(qkv, qkv, qkv, qkv, qkv, qkv, qkv, tbl)


def _dil_kernel(q_ref, kp_ref, kc_ref, kn_ref, vp_ref, vc_ref, vn_ref,
                o_ref, lse_ref, ks_ref, vs_ref, *, geom, dil):
    c = pl.program_id(0)
    n_h, chunk, hd = q_ref.shape
    side = DIL_SIDE
    qb = DIL_QBLOCK
    kb = qb + 2 * side
    ks_ref[:, 0:side, :] = kp_ref[...]
    ks_ref[:, side:side + chunk, :] = kc_ref[...]
    ks_ref[:, side + chunk:, :] = kn_ref[...]
    vs_ref[:, 0:side, :] = vp_ref[...]
    vs_ref[:, side:side + chunk, :] = vc_ref[...]
    vs_ref[:, side + chunk:, :] = vn_ref[...]
    plane_rows = geom.n_tok // dil
    t0 = ((c * chunk) % plane_rows) * dil
    seq_start, seq_len = geom.seq_span(t0)
    row0 = (t0 - seq_start) // dil
    cls_rows = seq_len // dil
    jq = lax.broadcasted_iota(jnp.int32, (qb, kb), 0)
    jk = lax.broadcasted_iota(jnp.int32, (qb, kb), 1) - side
    near_bias = jnp.where(jnp.abs(jk - jq) <= side, 0.0, NEG)
    before_bias = jnp.where(jk < 0, NEG, 0.0)
    after_bias = jnp.where(jk >= qb, NEG, 0.0)
    n_trip = min(DIL_PROBLEMS_PER_TRIP, chunk // qb)
    trips_per_head = chunk // (qb * n_trip)
    assert chunk % (qb * n_trip) == 0

    def body(it, carry):
        h = it // trips_per_head
        ii = it % trips_per_head
        probs = []
        for u in range(n_trip):
            i = ii * n_trip + u
            qoff = pl.multiple_of(i * qb, qb)
            p0 = row0 + i * qb
            bias = near_bias + jnp.where(p0 == 0, before_bias, 0.0)
            bias = bias + jnp.where(p0 + qb == cls_rows, after_bias, 0.0)
            q = q_ref[h, pl.ds(qoff, qb), :]
            k = ks_ref[h, pl.ds(qoff, kb), :]
            s = lax.dot_general(q, k, (((1,), (1,)), ((), ())), preferred_element_type=F32)
            probs.append((qoff, s + bias))
        soft = []
        for qoff, s in probs:
            m = jnp.max(s, axis=-1, keepdims=True)
            p = jnp.exp(s - m)
            soft.append((qoff, p.astype(BF16), m, jnp.sum(p, axis=-1, keepdims=True)))
        for qoff, p, m, l in soft:
            v = vs_ref[h, pl.ds(qoff, kb), :]
            o = jnp.dot(p, v, preferred_element_type=F32) / l
            o_ref[h, pl.ds(qoff, qb), :] = o.astype(o_ref.dtype)
            lse_ref[h, pl.ds(qoff, qb), :] = jnp.broadcast_to(m + jnp.log(l), (qb, hd))
        return carry

    lax.fori_loop(0, n_h * trips_per_head, body, 0)


def dilated_attention(qkv, dil, geom):
    _, n_tok, hd = qkv.shape
    n_h = DIL_HEADS
    chunk = min(DIL_CHUNK, min(geom.s_p, geom.s_d) // dil)
    assert chunk % DIL_QBLOCK == 0
    per = chunk // DIL_SIDE
    n_side = n_tok // DIL_SIDE

    def cur(hb):
        return pl.BlockSpec((n_h, chunk, hd), lambda c: (hb, c, 0))

    def prev(hb):
        return pl.BlockSpec((n_h, DIL_SIDE, hd), lambda c: (hb, jnp.maximum(c * per - 1, 0), 0))

    def nxt(hb):
        return pl.BlockSpec((n_h, DIL_SIDE, hd),
                            lambda c: (hb, jnp.minimum((c + 1) * per, n_side - 1), 0))

    return pl.pallas_call(
        functools.partial(_dil_kernel, geom=geom, dil=dil),
        grid=(n_tok // chunk,),
        in_specs=[cur(0), prev(1), cur(1), nxt(1), prev(2), cur(2), nxt(2)],
        out_specs=[cur(0), cur(0)],
        out_shape=[jax.ShapeDtypeStruct((n_h, n_tok, hd), BF16),
                   jax.ShapeDtypeStruct((n_h, n_tok, hd), F32)],
        scratch_shapes=[pltpu.VMEM((n_h, chunk + 2 * DIL_SIDE, hd), BF16),
                        pltpu.VMEM((n_h, chunk + 2 * DIL_SIDE, hd), BF16)],
        compiler_params=_params(("parallel",), VMEM_LIMIT),
        name="dilated_attention",
    )(qkv, qkv, qkv, qkv, qkv, qkv, qkv)


def _to_natural(src_ref, dst_ref):
    n_h, dil, n, _ = src_ref.shape
    for h in range(0, n_h, 2):
        for r in range(dil):
            pair = jnp.concatenate([src_ref[h, r].astype(F32), src_ref[h + 1, r].astype(F32)],
                                   axis=-1)
            if dil == 1:
                dst_ref[h // 2] = pair
            else:
                dst_ref[h // 2, pl.ds(r, n, stride=dil), :] = pair


def _outproj_ab_kernel(x_ref, mod_ref, oa_ref, od0_ref, od1_ref, od2_ref,
                       l0_ref, l1_ref, l2_ref, w_ref, o_ref, *scratch):
    so = scratch[:N_DIL]
    sl = scratch[N_DIL:]
    for src, dst in zip((od0_ref, od1_ref, od2_ref, l0_ref, l1_ref, l2_ref), so + sl):
        _to_natural(src, dst)
    n_h = oa_ref.shape[0]
    slabs = [jnp.concatenate([oa_ref[h], oa_ref[h + 1]], axis=-1) for h in range(0, n_h, 2)]
    for s in range(n_h // 2):
        l0, l1, l2 = sl[0][s], sl[1][s], sl[2][s]
        m = jnp.maximum(jnp.maximum(l0, l1), l2)
        e0, e1, e2 = jnp.exp(l0 - m), jnp.exp(l1 - m), jnp.exp(l2 - m)
        ob = (e0 * so[0][s] + e1 * so[1][s] + e2 * so[2][s]) / (e0 + e1 + e2)
        slabs.append(ob.astype(BF16))
    cat = jnp.concatenate(slabs, axis=-1)
    res = jnp.dot(cat, w_ref[...], preferred_element_type=F32)
    o_ref[...] = x_ref[...] + mod_ref[2:3, :] * res


def outproj_ab(x, mod, oa, od, lse, w_bf16, layer, geom):
    n_tok, d = x.shape
    n_h, _, hd = oa.shape
    tm = TOKEN_TILE // 2

    def cls(a):
        dil = a.shape[1]
        return pl.BlockSpec((n_h, dil, tm // dil, hd), lambda i: (0, 0, i, 0))

    return pl.pallas_call(
        _outproj_ab_kernel,
        grid=(n_tok // tm,),
        in_specs=[
            pl.BlockSpec((tm, d), lambda i: (i, 0)),
            pl.BlockSpec((None, None, 6, d), lambda i: (layer, geom.seq_of_token(i * tm), 0, 0)),
            pl.BlockSpec((n_h, tm, hd), lambda i: (0, i, 0)),
            *[cls(a) for a in od], *[cls(a) for a in lse],
            pl.BlockSpec(w_bf16.shape, lambda i: (0, 0)),
        ],
        out_specs=pl.BlockSpec((tm, d), lambda i: (i, 0)),
        out_shape=jax.ShapeDtypeStruct(x.shape, F32),
        scratch_shapes=[pltpu.VMEM((n_h // 2, tm, LANES), F32) for _ in range(2 * N_DIL)],
        compiler_params=_params(("parallel",), VMEM_LIMIT),
        name="outproj_ab",
    )(x, mod, oa, *od, *lse, w_bf16)


def _outproj_kernel(x_ref, mod_ref, o_in_ref, w_ref, o_ref):
    res = jnp.dot(o_in_ref[...], w_ref[...], preferred_element_type=F32)
    o_ref[...] = x_ref[...] + mod_ref[2:3, :] * res


def outproj(x, mod, o_in, w_bf16, layer, geom):
    n_tok, d = x.shape
    tm = TOKEN_TILE
    return pl.pallas_call(
        _outproj_kernel,
        grid=(n_tok // tm,),
        in_specs=[
            pl.BlockSpec((tm, d), lambda i: (i, 0)),
            pl.BlockSpec((None, None, 6, d), lambda i: (layer, geom.seq_of_token(i * tm), 0, 0)),
            pl.BlockSpec((tm, o_in.shape[1]), lambda i: (i, 0)),
            pl.BlockSpec(w_bf16.shape, lambda i: (0, 0)),
        ],
        out_specs=pl.BlockSpec((tm, d), lambda i: (i, 0)),
        out_shape=jax.ShapeDtypeStruct(x.shape, F32),
        compiler_params=_params(("parallel",), VMEM_LIMIT),
        name="outproj",
    )(x, mod, o_in, w_bf16)


def _flash_write_out(acc_ref, o_ref, n_g, tq):
    hd = HEAD_DIM
    acc = acc_ref[...]
    o = acc[:hd] / acc[hd:hd + 1]
    pad = jnp.zeros((LANES - hd, tq), F32)
    for g in range(n_g):
        t = jnp.concatenate([o[:, g * tq:(g + 1) * tq], pad], axis=0).T
        o_ref[:, g * hd:(g + 1) * hd] = t[:, :hd].astype(o_ref.dtype)


def _flash_bounded_kernel(q_ref, k_ref, vt_ref, kmx_ref, _o_in_ref, o_ref, acc_ref, *, unroll):
    n_g, tq, kw = q_ref.shape
    n_kt, tk, _ = k_ref.shape
    acc_ref[...] = jnp.zeros(acc_ref.shape, F32)
    kmx = kmx_ref[...]
    n_col = (n_g * tq) // MXU_TILE

    n_kc = tk // MXU_TILE

    def qk(j, n, kc):
        rows = slice(kc * MXU_TILE, (kc + 1) * MXU_TILE)
        q_n = q_ref[...].reshape(n_g * tq, kw)[n * MXU_TILE:(n + 1) * MXU_TILE]
        return lax.dot_general(k_ref[j, rows, :] + kmx, q_n, (((1,), (1,)), ((), ())),
                               preferred_element_type=F32)

    def body(jj, carry):
        units = [(unroll * jj + u, n, kc) for u in range(unroll) for n in range(n_col)
                 for kc in range(n_kc)]
        pending = [qk(*unit) for unit in units[:FLASH_LOOKAHEAD]]
        for idx, (j, n, kc) in enumerate(units):
            if idx + FLASH_LOOKAHEAD < len(units):
                pending.append(qk(*units[idx + FLASH_LOOKAHEAD]))
            cols = slice(n * MXU_TILE, (n + 1) * MXU_TILE)
            rows = slice(kc * MXU_TILE, (kc + 1) * MXU_TILE)
            p = jnp.exp2(pending.pop(0)).astype(BF16)
            acc_ref[:, cols] += jnp.dot(vt_ref[j, :, rows], p, preferred_element_type=F32)
        return carry

    lax.fori_loop(0, n_kt // unroll, body, 0)
    _flash_write_out(acc_ref, o_ref, n_g, tq)


def _flash_kernel(q_ref, k_ref, vt_ref, _o_in_ref, o_ref, sa_ref, sb_ref, mxa_ref, mxb_ref,
                  m_ref, acc_ref, *, unroll):
    n_g, tq, hd = q_ref.shape
    n_kt = k_ref.shape[0]
    m_ref[...] = jnp.full(m_ref.shape, -jnp.inf, F32)
    acc_ref[...] = jnp.zeros(acc_ref.shape, F32)
    n_col = (n_g * tq) // MXU_TILE

    def scores(j, s_ref, mx_ref, n):
        cols = slice(n * MXU_TILE, (n + 1) * MXU_TILE)
        q_n = q_ref[...].reshape(n_g * tq, hd)[n * MXU_TILE:(n + 1) * MXU_TILE]
        s = lax.dot_general(k_ref[j], q_n, (((1,), (1,)), ((), ())),
                            preferred_element_type=F32)
        s_ref[:, cols] = s
        mx_ref[:, cols] = jnp.max(s, axis=0, keepdims=True)

    def consume(j, s_ref, mx_ref, n):
        cols = slice(n * MXU_TILE, (n + 1) * MXU_TILE)
        m_old = m_ref[:, cols]
        m_new = jnp.maximum(m_old, mx_ref[:, cols])
        alpha = jnp.exp2(m_old - m_new)
        m_ref[:, cols] = m_new
        tk = s_ref.shape[0]
        acc = alpha * acc_ref[:, cols]
        for kc in range(tk // MXU_TILE):
            rows = slice(kc * MXU_TILE, (kc + 1) * MXU_TILE)
            p = jnp.exp2(s_ref[rows, cols] - m_new).astype(BF16)
            acc = acc + jnp.dot(vt_ref[j, :, rows], p, preferred_element_type=F32)
        acc_ref[:, cols] = acc

    for n in range(n_col):
        scores(0, sa_ref, mxa_ref, n)

    def body(jj, carry):
        j = unroll * jj
        for u in range(0, unroll, 2):
            for n in range(n_col):
                scores(j + u + 1, sb_ref, mxb_ref, n)
                consume(j + u, sa_ref, mxa_ref, n)
            for n in range(n_col):
                scores(jnp.minimum(j + u + 2, n_kt - 1), sa_ref, mxa_ref, n)
                consume(j + u + 1, sb_ref, mxb_ref, n)
        return carry

    lax.fori_loop(0, n_kt // unroll, body, 0)
    _flash_write_out(acc_ref, o_ref, n_g, tq)


def flash_attention(qk, vt_tiles, o_buf, tok0, n_b, s_n, seq_idx0, kmx=None):
    n_all, n_tok, kw = qk.shape
    n_kv, _, hv, tk = vt_tiles.shape
    n_g = C_Q_HEADS // n_kv
    hd = HEAD_DIM
    tq = FLASH_TQ
    nq = s_n // tq
    kt_per = s_n // tk
    unroll = min(FLASH_UNROLL if kmx is None else FLASH_BOUNDED_UNROLL, kt_per)
    assert kt_per % unroll == 0 and unroll % 2 == 0 and tok0 % s_n == 0
    wq = n_g * tq
    k_view = qk.reshape(n_all, n_tok // tk, tk, kw)
    seq0 = tok0 // s_n
    in_specs = [
        pl.BlockSpec((n_g, tq, kw), lambda b, h, i: (h, (tok0 + b * s_n) // tq + i, 0)),
        pl.BlockSpec((None, kt_per, tk, kw), lambda b, h, i: (C_K0 + h, seq0 + b, 0, 0)),
        pl.BlockSpec((None, kt_per, hv, tk), lambda b, h, i: (h, seq0 + b, 0, 0)),
    ]
    args = [qk, k_view, vt_tiles]
    if kmx is None:
        body = functools.partial(_flash_kernel, unroll=unroll)
        scratch = [pltpu.VMEM((tk, wq), F32), pltpu.VMEM((tk, wq), F32),
                   pltpu.VMEM((1, wq), F32), pltpu.VMEM((1, wq), F32),
                   pltpu.VMEM((1, wq), F32), pltpu.VMEM((hv, wq), F32)]
    else:
        body = functools.partial(_flash_bounded_kernel, unroll=unroll)
        scratch = [pltpu.VMEM((hv, wq), F32)]
        in_specs.append(pl.BlockSpec((None, None, 1, kw), lambda b, h, i: (h, seq_idx0 + b, 0, 0)))
        args.append(kmx)
    in_specs.append(pl.BlockSpec(memory_space=pl.ANY))
    args.append(o_buf)
    return pl.pallas_call(
        body,
        grid=(n_b, n_kv, nq),
        in_specs=in_specs,
        out_specs=pl.BlockSpec((tq, n_g * hd), lambda b, h, i: ((tok0 + b * s_n) // tq + i, h)),
        out_shape=jax.ShapeDtypeStruct(o_buf.shape, o_buf.dtype),
        input_output_aliases={len(args) - 1: 0},
        scratch_shapes=scratch,
        compiler_params=_params(("parallel", "parallel", "arbitrary"), VMEM_LIMIT),
        name="flash_attention",
    )(*args)


def _router_kernel(x_ref, mod_ref, g_ref, wr_ref, br_ref, h_ref, idx_ref, gate_ref, cnt_ref,
                   run_ref):
    @pl.when(pl.program_id(0) == 0)
    def _():
        run_ref[...] = jnp.zeros(run_ref.shape, F32)

    h = _modulated_norm(x_ref[...], g_ref[...], mod_ref[3:4, :], mod_ref[4:5, :])
    h_ref[...] = h.astype(BF16)
    logits = lax.dot_general(wr_ref[...], h, (((1,), (1,)), ((), ())), precision=HIGHEST,
                             preferred_element_type=F32) + br_ref[...]
    n_e, tm = logits.shape
    iota = lax.broadcasted_iota(jnp.int32, (n_e, tm), 0)
    vals, ids = [], []
    cur = logits
    for _ in range(TOP_K):
        m = jnp.max(cur, axis=0, keepdims=True)
        am = jnp.min(jnp.where(cur == m, iota, n_e), axis=0, keepdims=True)
        vals.append(m)
        ids.append(am)
        cur = jnp.where(iota == am, -jnp.inf, cur)
    es = [jnp.exp(v - vals[0]) for v in vals]
    den = es[0] + es[1] + es[2] + es[3]
    chosen = jnp.zeros((n_e, tm), F32)
    for kk in range(TOP_K):
        chosen = chosen + jnp.where(iota == ids[kk], 1.0, 0.0)
    earlier = lax.broadcasted_iota(jnp.int32, (tm, tm), 0)
    token = lax.broadcasted_iota(jnp.int32, (tm, tm), 1)
    tri = jnp.where(earlier < token, 1.0, 0.0).astype(BF16)
    before = jnp.dot(chosen.astype(BF16), tri, preferred_element_type=F32) + run_ref[...]
    run_ref[...] = run_ref[...] + jnp.sum(chosen, axis=1, keepdims=True)
    cnt_ref[...] = run_ref[...]
    row = lax.broadcasted_iota(jnp.int32, idx_ref.shape, 0)
    idx_out = jnp.zeros(idx_ref.shape, jnp.int32)
    gate_out = jnp.zeros(gate_ref.shape, F32)
    for kk in range(TOP_K):
        rank = jnp.sum(jnp.where(iota == ids[kk], before, 0.0), axis=0, keepdims=True)
        idx_out = jnp.where(row == kk, ids[kk], idx_out)
        idx_out = jnp.where(row == TOP_K + kk, rank.astype(jnp.int32), idx_out)
        gate_out = jnp.where(row == kk, es[kk] / den, gate_out)
    idx_ref[...] = idx_out
    gate_ref[...] = gate_out


def router(x, mod, g_norm, w_router, b_router, layer, geom, tile0, n_tiles):
    d = x.shape[1]
    tm = MOE_TOKEN_TILE
    n_tok = n_tiles * tm
    return pl.pallas_call(
        _router_kernel,
        grid=(n_tiles,),
        in_specs=[
            pl.BlockSpec((tm, d), lambda i: (tile0 + i, 0)),
            pl.BlockSpec((None, None, 6, d),
                         lambda i: (layer, geom.seq_of_token((tile0 + i) * tm), 0, 0)),
            pl.BlockSpec((None, 1, d), lambda i: (layer, 0, 0)),
            pl.BlockSpec((None, N_EXPERTS, d), lambda i: (layer, 0, 0)),
            pl.BlockSpec((None, N_EXPERTS, 1), lambda i: (layer, 0, 0)),
        ],
        out_specs=[
            pl.BlockSpec((tm, d), lambda i: (i, 0)),
            pl.BlockSpec((2 * TOP_K, tm), lambda i: (0, i)),
            pl.BlockSpec((2 * TOP_K, tm), lambda i: (0, i)),
            pl.BlockSpec((N_EXPERTS, 1), lambda i: (0, 0)),
        ],
        out_shape=[
            jax.ShapeDtypeStruct((n_tok, d), BF16),
            jax.ShapeDtypeStruct((2 * TOP_K, n_tok), jnp.int32),
            jax.ShapeDtypeStruct((2 * TOP_K, n_tok), F32),
            jax.ShapeDtypeStruct((N_EXPERTS, 1), F32),
        ],
        scratch_shapes=[pltpu.VMEM((N_EXPERTS, 1), F32)],
        compiler_params=_params(("arbitrary",), VMEM_LIMIT),
        name="router",
    )(x, mod, g_norm.reshape(-1, 1, d), w_router.transpose(0, 2, 1),
      b_router.reshape(-1, N_EXPERTS, 1))


def _moe_kernel(be_ref, nu_ref, x_ref, wgu_ref, bgu_ref, wd_ref, bd_ref, o_ref,
                wgu_s, wd_s):
    i = pl.program_id(0)
    e = be_ref[i]
    e_prev = be_ref[jnp.maximum(i - 1, 0)]

    @pl.when((i == 0) | (e != e_prev))
    def _():
        wgu_s[...] = wgu_ref[...].astype(BF16)
        wd_s[...] = wd_ref[...].astype(BF16)

    @pl.when(i < nu_ref[0])
    def _():
        d_ff = wd_s.shape[0]
        gu = jnp.dot(x_ref[...], wgu_s[...], preferred_element_type=F32) + bgu_ref[...]
        glu = jnp.minimum(gu[:, :d_ff], SWIGLU_LIMIT)
        lin = jnp.clip(gu[:, d_ff:], -SWIGLU_LIMIT, SWIGLU_LIMIT)
        act = glu / (1.0 + jnp.exp(-SWIGLU_ALPHA * glu)) * (lin + 1.0)
        y = jnp.dot(act.astype(BF16), wd_s[...], preferred_element_type=F32) + bd_ref[...]
        o_ref[...] = y.astype(o_ref.dtype)

    @pl.when(i >= nu_ref[0])
    def _():
        o_ref[...] = jnp.zeros(o_ref.shape, o_ref.dtype)


def moe_experts(xs, blk_e, n_used, w_gu, b_gu, w_down, b_down, layer):
    n_rows, d = xs.shape
    d_ff = w_down.shape[2]
    nblk = n_rows // MOE_BLOCK
    grid_spec = pltpu.PrefetchScalarGridSpec(
        num_scalar_prefetch=2,
        grid=(nblk,),
        in_specs=[
            pl.BlockSpec((MOE_BLOCK, d), lambda i, be, nu: (jnp.minimum(i, nu[0] - 1), 0)),
            pl.BlockSpec((None, None, d, 2 * d_ff), lambda i, be, nu: (layer, be[i], 0, 0)),
            pl.BlockSpec((None, None, 1, 2 * d_ff), lambda i, be, nu: (layer, be[i], 0, 0)),
            pl.BlockSpec((None, None, d_ff, d), lambda i, be, nu: (layer, be[i], 0, 0)),
            pl.BlockSpec((None, None, 1, d), lambda i, be, nu: (layer, be[i], 0, 0)),
        ],
        out_specs=pl.BlockSpec((MOE_BLOCK, d), lambda i, be, nu: (i, 0)),
        scratch_shapes=[pltpu.VMEM((d, 2 * d_ff), BF16), pltpu.VMEM((d_ff, d), BF16)],
    )
    depth, n_e = w_gu.shape[:2]
    return pl.pallas_call(
        _moe_kernel,
        grid_spec=grid_spec,
        out_shape=jax.ShapeDtypeStruct((n_rows, d), BF16),
        compiler_params=_params(("arbitrary",), VMEM_LIMIT),
        name="moe_experts",
    )(blk_e, n_used, xs, w_gu, b_gu.reshape(depth, n_e, 1, 2 * d_ff), w_down,
      b_down.reshape(depth, n_e, 1, d))


def _combine_kernel(x_ref, mod_ref, y_ref, gate_ref, o_ref):
    g = gate_ref[...]
    acc = g[:, 0:1] * y_ref[0].astype(F32)
    for kk in range(1, TOP_K):
        acc = acc + g[:, kk:kk + 1] * y_ref[kk].astype(F32)
    o_ref[...] = x_ref[...] + mod_ref[5:6, :] * acc


def moe_combine(x, mod, yk, gates, layer, geom, tile0, n_tiles):
    d = x.shape[1]
    tm = MOE_TOKEN_TILE
    return pl.pallas_call(
        _combine_kernel,
        grid=(n_tiles,),
        in_specs=[
            pl.BlockSpec((tm, d), lambda i: (tile0 + i, 0)),
            pl.BlockSpec((None, None, 6, d),
                         lambda i: (layer, geom.seq_of_token((tile0 + i) * tm), 0, 0)),
            pl.BlockSpec((TOP_K, tm, d), lambda i: (0, i, 0)),
            pl.BlockSpec((tm, 2 * TOP_K), lambda i: (i, 0)),
        ],
        out_specs=pl.BlockSpec((tm, d), lambda i: (tile0 + i, 0)),
        out_shape=jax.ShapeDtypeStruct(x.shape, F32),
        input_output_aliases={0: 0},
        compiler_params=_params(("parallel",), VMEM_LIMIT),
        name="moe_combine",
    )(x, mod, yk, gates)


def moe_layer(x, mod, g_norm, w_router, b_router, w_gu, b_gu, w_down, b_down, layer, geom):
    n_tiles = x.shape[0] // MOE_TOKEN_TILE
    assert n_tiles % MOE_PARTS == 0
    per = n_tiles // MOE_PARTS
    routed = [_moe_dispatch(x, mod, g_norm, w_router, b_router, w_gu, b_gu, w_down, b_down, layer,
                            geom, part * per, per) for part in range(MOE_PARTS)]
    for part, (yk, gates8) in enumerate(routed):
        x = moe_combine(x, mod, yk, gates8, layer, geom, part * per, per)
    return x


def _moe_dispatch(x, mod, g_norm, w_router, b_router, w_gu, b_gu, w_down, b_down, layer, geom,
                  tile0, n_tiles):
    d = x.shape[1]
    n_tok = n_tiles * MOE_TOKEN_TILE
    h, idx8, gates8, counts = router(x, mod, g_norm, w_router, b_router, layer, geom, tile0,
                                     n_tiles)
    idx = idx8[:TOP_K].T
    rank = idx8[TOP_K:].T
    gates8 = gates8.T
    counts = counts[:, 0].astype(jnp.int32)
    pcounts = ((counts + MOE_BLOCK - 1) // MOE_BLOCK) * MOE_BLOCK
    pend = jnp.cumsum(pcounts)
    pstart = pend - pcounts
    experts = jnp.arange(N_EXPERTS, dtype=jnp.int32)
    dest = rank + jnp.sum(jnp.where(idx[:, :, None] == experts, pstart, 0), axis=-1)
    nblk = (n_tok * TOP_K) // MOE_BLOCK + N_EXPERTS
    n_rows = nblk * MOE_BLOCK
    tok_ids = jnp.broadcast_to(jnp.arange(n_tok, dtype=jnp.int32)[:, None], dest.shape)
    tok = (jnp.arange(n_rows, dtype=jnp.int32) % n_tok).at[dest.reshape(-1)].set(
        tok_ids.reshape(-1), unique_indices=True, mode="promise_in_bounds")
    n_used = (pend[-1] // MOE_BLOCK).astype(jnp.int32).reshape(1)
    blk_start = jnp.arange(nblk, dtype=jnp.int32) * MOE_BLOCK
    blk_last = jnp.minimum(blk_start, pend[-1] - 1)
    blk_e = jnp.sum(pend[None, :] <= blk_last[:, None], axis=-1, dtype=jnp.int32)
    blk_e = jnp.minimum(blk_e, N_EXPERTS - 1)
    xs = jnp.take(h, tok, axis=0, mode="clip")
    yb = moe_experts(xs, blk_e, n_used, w_gu, b_gu, w_down, b_down, layer)
    yk = jnp.take(yb, dest.T.reshape(-1), axis=0, mode="clip").reshape(TOP_K, n_tok, d)
    return yk, gates8


def _per_token(geom, tables):
    return [jnp.concatenate([t[:s] for _, n_b, s in geom.groups() for _ in range(n_b)])
            for t in tables]


def _qk_gains(g_q, g_k, n_q, n_k, q_scale):
    hd = g_q.shape[-1]
    return jnp.concatenate([jnp.broadcast_to(g_q * q_scale, (n_q, hd)),
                            jnp.broadcast_to(g_k, (n_k, hd))])


def _dilated_group_weight(w_in, gi):
    blk = DIL_HEADS * HEAD_DIM
    base = 3 * NA_HEADS * HEAD_DIM
    return jnp.concatenate([w_in[:, base + (a * N_DIL + gi) * blk:base + (a * N_DIL + gi + 1) * blk]
                            for a in range(3)], axis=1)


def mixer_ab_layer(x, mod, g_norm, w_in, w_out, g_qn_a, g_kn_a, rpb, g_qn_b, g_kn_b, layer, geom):
    n_tok, d = x.shape
    hd = HEAD_DIM
    w = w_in.astype(BF16)
    n_h = NA_HEADS
    qkv = norm_proj(x, mod, g_norm, w[:, :3 * n_h * hd], layer, geom, dil=1)
    qkv = qk_prep(qkv.reshape(3 * n_h, n_tok, hd), _qk_gains(g_qn_a, g_kn_a, n_h, n_h, Q_SCALE),
                  hb=n_h, n_blocks=2)
    oa = neighborhood_attention(qkv, _na_bias_table(rpb), geom)
    cos, sin = _per_token(geom, _rope_angles(jnp.arange(max(geom.s_p, geom.s_d)), hd))
    perm = _rotate_half_matrix(hd, 1)
    od, lse = [], []
    for gi, (window, dil) in enumerate(DIL_PATTERNS):
        assert (window // 2) // dil == DIL_SIDE
        n_h = DIL_HEADS

        def class_major(t):
            return t.reshape(n_tok // dil, dil, hd).transpose(1, 0, 2).reshape(n_tok, hd)

        qkv = norm_proj(x, mod, g_norm, _dilated_group_weight(w, gi), layer, geom, dil=dil)
        qkv = qk_prep(qkv.reshape(3 * n_h, n_tok, hd),
                      _qk_gains(g_qn_b[gi], g_kn_b[gi], n_h, n_h, Q_SCALE), hb=n_h, n_blocks=2,
                      tables=(class_major(cos), class_major(sin), perm))
        o_g, lse_g = dilated_attention(qkv, dil, geom)
        od.append(o_g.reshape(n_h, dil, n_tok // dil, hd))
        lse.append(lse_g.reshape(n_h, dil, n_tok // dil, hd))
    return outproj_ab(x, mod, oa, od, lse, w_out.astype(BF16), layer, geom)


def mixer_c_layer(x, mod, g_norm, w_in, w_out, g_qn, g_kn, layer, geom):
    n_tok, d = x.shape
    hd = HEAD_DIM
    n_heads = C_Q_HEADS + 2 * C_KV_HEADS
    qkv = norm_proj(x, mod, g_norm, w_in.astype(BF16), layer, geom, dil=1).reshape(n_heads, n_tok, hd)
    pos = jnp.arange(max(geom.s_p, geom.s_d))
    half = hd // 2
    cr, sr = _rope_angles(pos // GRID_W, half)
    cc, sc = _rope_angles(pos % GRID_W, half)
    cos, sin = _per_token(geom, (jnp.concatenate([cr, cc], axis=-1),
                                 jnp.concatenate([sr, sc], axis=-1)))
    tables = (cos, sin, _rotate_half_matrix(half, 2))
    side = jnp.zeros((C_Q_HEADS + C_KV_HEADS, hd), F32).at[:C_Q_HEADS, 0].set(-FLASH_BOUND_MARGIN)
    qk, nmax = qk_prep(qkv, _qk_gains(g_qn, g_kn, C_Q_HEADS, C_KV_HEADS, Q_SCALE * LOG2_E),
                       hb=C_KV_HEADS, n_blocks=(C_Q_HEADS + C_KV_HEADS) // C_KV_HEADS,
                       tables=tables, side=side)
    tk = FLASH_TK
    vt_tiles = qkv[C_V0:].reshape(C_KV_HEADS, n_tok // tk, tk, hd).transpose(0, 1, 3, 2)
    ones = jnp.ones((C_KV_HEADS, n_tok // tk, BF16_SUBLANES, tk), BF16)
    vt_tiles = jnp.concatenate([vt_tiles, ones], axis=2)

    ts = PREP_TILE
    tile_max = nmax[:, :, 0, 0]
    seq_max = jnp.concatenate(
        [tile_max[:, t0 // ts:(t0 + n_b * s_n) // ts].reshape(-1, n_b, s_n // ts).max(-1)
         for t0, n_b, s_n in geom.groups()], axis=1)
    n_g = C_Q_HEADS // C_KV_HEADS
    k_max = (seq_max[C_K0:] * FLASH_BOUND_MARGIN).astype(BF16)
    q_max = seq_max[:C_K0] * FLASH_BOUND_MARGIN
    bound = jnp.max(q_max.reshape(C_KV_HEADS, n_g, -1) * k_max.astype(F32)[:, None, :])
    kmx = jnp.zeros((C_KV_HEADS, geom.n_seq, 1, 2 * hd), BF16).at[:, :, 0, hd].set(k_max)
    o_buf = jnp.zeros((n_tok, C_Q_HEADS * hd), BF16)

    def attend(kmx_or_none):
        o = o_buf
        seq_idx0 = 0
        for tok0, n_b, s_n in geom.groups():
            o = flash_attention(qk, vt_tiles, o, tok0, n_b, s_n, seq_idx0, kmx_or_none)
            seq_idx0 += n_b
        return o

    o = lax.cond(bound <= FLASH_SAFE_BOUND, lambda: attend(kmx), lambda: attend(None))
    return outproj(x, mod, o, w_out.astype(BF16), layer, geom)


def kernel(x_prompt, x_sample, c_prompt, c_sample, w_ada, b_ada, g_norm_mix, g_norm_ffn, w_in_ab, w_out_ab, g_qn_a, g_kn_a, rpb_a, g_qn_b, g_kn_b, w_in_c, w_out_c, g_qn_c, g_kn_c, w_router, b_router, w_gate_up, b_gate_up, w_down, b_down):
    b_p, s_p, d = x_prompt.shape
    b_d, s_d, _ = x_sample.shape
    geom = Geom(b_p, s_p, b_d, s_d)
    depth = w_ada.shape[0]
    x = jnp.concatenate([x_prompt.reshape(-1, d), x_sample.reshape(-1, d)], axis=0)
    mod = ada_modulation(jnp.concatenate([c_prompt, c_sample], axis=0), w_ada, b_ada)
    for layer in range(depth):
        i = layer // 2
        if layer % 2 == 0:
            x = mixer_ab_layer(x, mod, g_norm_mix, w_in_ab[i], w_out_ab[i], g_qn_a[i], g_kn_a[i],
                               rpb_a[i], g_qn_b[i], g_kn_b[i], layer, geom)
        else:
            x = mixer_c_layer(x, mod, g_norm_mix, w_in_c[i], w_out_c[i], g_qn_c[i], g_kn_c[i],
                              layer, geom)
        x = moe_layer(x, mod, g_norm_ffn, w_router, b_router, w_gate_up, b_gate_up, w_down,
                      b_down, layer, geom)
    y_prompt = x[:geom.n_p].reshape(b_p, s_p, d)
    y_sample = x[geom.n_p:].reshape(b_d, s_d, d)
    return (y_prompt, y_sample)
```

```python
import functools
import math
from typing import NamedTuple

import jax
import jax.numpy as jnp
from jax import lax
from jax.experimental import pallas as pl
from jax.experimental.pallas import tpu as pltpu

F32 = jnp.float32
BF16 = jnp.bfloat16
HIGHEST = lax.Precision.HIGHEST

GRID_W = 64
HEAD_DIM = 64
ROPE_THETA = 10000.0
EPS = 1e-6
NEG = -1e30
NA_HEADS = 8
NA_KH = 8
NA_KW = 16
DIL_HEADS = 8
DIL_PATTERNS = ((128, 1), (512, 4), (2048, 16))
N_DIL = len(DIL_PATTERNS)
C_Q_HEADS = 16
C_KV_HEADS = 4
N_EXPERTS = 32
TOP_K = 4
SWIGLU_LIMIT = 7.0
SWIGLU_ALPHA = 1.702
Q_SCALE = HEAD_DIM ** -0.5
LOG2_E = math.log2(math.e)
BF16_SUBLANES = 16
LANES = 128
MXU_TILE = 256

V7X_VMEM_BYTES = 64 * 1024 * 1024
VMEM_LIMIT = V7X_VMEM_BYTES - 8 * 1024 * 1024

TOKEN_TILE = 1024
PREP_TILE = 2048
NA_BLOCK = 8 * GRID_W
NA_ROWS_PER_TRIP = 4
DIL_QBLOCK = 128
DIL_SIDE = 64
DIL_CHUNK = 1024
DIL_PROBLEMS_PER_TRIP = 8
FLASH_TQ = 256
FLASH_TK = 512
FLASH_UNROLL = 8
FLASH_BOUNDED_UNROLL = 16
FLASH_LOOKAHEAD = 6
FLASH_SAFE_BOUND = 60.0
FLASH_BOUND_MARGIN = 1.0 + 2.0 ** -6
MOE_BLOCK = 512
MOE_TOKEN_TILE = 512
MOE_PARTS = 1

NA_Q0, NA_K0, NA_V0 = 0, NA_HEADS, 2 * NA_HEADS
C_K0 = C_Q_HEADS
C_V0 = C_Q_HEADS + C_KV_HEADS


class Geom(NamedTuple):
    b_p: int
    s_p: int
    b_d: int
    s_d: int

    @property
    def n_p(self):
        return self.b_p * self.s_p

    @property
    def n_tok(self):
        return self.n_p + self.b_d * self.s_d

    @property
    def n_seq(self):
        return self.b_p + self.b_d

    def groups(self):
        return ((0, self.b_p, self.s_p), (self.n_p, self.b_d, self.s_d))

    def seq_of_token(self, t0):
        return jnp.where(t0 < self.n_p, t0 // self.s_p,
                         self.b_p + (t0 - self.n_p) // self.s_d)

    def seq_span(self, t0):
        in_p = t0 < self.n_p
        start = jnp.where(in_p, (t0 // self.s_p) * self.s_p,
                          self.n_p + ((t0 - self.n_p) // self.s_d) * self.s_d)
        return start, jnp.where(in_p, self.s_p, self.s_d)


def _params(semantics, vmem=None):
    return pltpu.CompilerParams(dimension_semantics=semantics,
                                vmem_limit_bytes=vmem)


def _ada_kernel(c_ref, w_ref, b_ref, o_ref):
    c = c_ref[...]
    a = c / (1.0 + jnp.exp(-c))
    o_ref[...] = jnp.dot(a, w_ref[...], precision=HIGHEST,
                         preferred_element_type=F32) + b_ref[...]


def ada_modulation(c_all, w_ada, b_ada):
    depth, d, d6 = w_ada.shape
    n_seq = c_all.shape[0]
    rows = 8
    c_pad = jnp.zeros((rows, d), F32).at[:n_seq].set(c_all)
    out = pl.pallas_call(
        _ada_kernel,
        grid=(depth, d6 // d),
        in_specs=[
            pl.BlockSpec((rows, d), lambda l, j: (0, 0)),
            pl.BlockSpec((None, d, d), lambda l, j: (l, 0, j)),
            pl.BlockSpec((None, 1, d), lambda l, j: (l, 0, j)),
        ],
        out_specs=pl.BlockSpec((None, rows, d), lambda l, j: (l, 0, j)),
        out_shape=jax.ShapeDtypeStruct((depth, rows, d6), F32),
        compiler_params=_params(("arbitrary", "arbitrary")),
        name="ada_modulation",
    )(c_pad, w_ada, b_ada.reshape(depth, 1, d6))
    return out[:, :n_seq].reshape(depth, n_seq, 6, d)


def _modulated_norm(x, g, shift, scale):
    r = lax.rsqrt(jnp.mean(x * x, axis=-1, keepdims=True) + EPS)
    return (x * r * g) * (1.0 + scale) + shift


def _norm_proj_kernel(x_ref, mod_ref, g_ref, w_ref, o_ref, *scratch, dil):
    h = _modulated_norm(x_ref[...], g_ref[...], mod_ref[0:1, :], mod_ref[1:2, :])
    acc = jnp.dot(h.astype(BF16), w_ref[...], preferred_element_type=F32)
    tm = x_ref.shape[0]
    n_h, _, _, hd = o_ref.shape
    if dil == 1:
        for c in range(n_h):
            o_ref[c, 0] = acc[:, c * hd:(c + 1) * hd].astype(o_ref.dtype)
        return
    acc_ref, = scratch
    for s in range(n_h // 2):
        acc_ref[s] = acc[:, s * LANES:(s + 1) * LANES]
    for s in range(n_h // 2):
        for r in range(dil):
            pair = acc_ref[s, pl.ds(r, tm // dil, stride=dil), :].astype(o_ref.dtype)
            o_ref[2 * s, r] = pair[:, :hd]
            o_ref[2 * s + 1, r] = pair[:, hd:]


def norm_proj(x, mod, g_norm, w_bf16, layer, geom, dil):
    n_tok, d = x.shape
    n_out = w_bf16.shape[1]
    tm = TOKEN_TILE
    hd = HEAD_DIM
    n_h = n_out // hd
    scratch = [] if dil == 1 else [pltpu.VMEM((n_h // 2, tm, LANES), F32)]
    return pl.pallas_call(
        functools.partial(_norm_proj_kernel, dil=dil),
        grid=(n_tok // tm,),
        in_specs=[
            pl.BlockSpec((tm, d), lambda i: (i, 0)),
            pl.BlockSpec((None, None, 6, d), lambda i: (layer, geom.seq_of_token(i * tm), 0, 0)),
            pl.BlockSpec((None, 1, d), lambda i: (layer, 0, 0)),
            pl.BlockSpec((d, n_out), lambda i: (0, 0)),
        ],
        out_specs=pl.BlockSpec((n_h, dil, tm // dil, hd), lambda i: (0, 0, i, 0)),
        out_shape=jax.ShapeDtypeStruct((n_h, dil, n_tok // dil, hd), BF16),
        scratch_shapes=scratch,
        compiler_params=_params(("parallel",), VMEM_LIMIT),
        name="norm_proj",
    )(x, mod, g_norm.reshape(-1, 1, d), w_bf16)


def _prep_kernel(x_ref, g_ref, *rest, rope, aug):
    rest = list(rest)
    hb, ts, hd = x_ref.shape
    x = x_ref[...].astype(F32)
    r = lax.rsqrt(jnp.mean(x * x, axis=-1, keepdims=True) + EPS)
    y = x * r * g_ref[...]
    if rope:
        cos_ref, sin_ref, p_ref = rest[:3]
        rest = rest[3:]
        yr = jnp.dot(y.reshape(hb * ts, hd).astype(BF16), p_ref[...],
                     preferred_element_type=F32).reshape(hb, ts, hd)
        y = y * cos_ref[...] + yr * sin_ref[...]
    if aug:
        side_ref, o_ref, nmax_ref = rest
        norm = jnp.sqrt(jnp.sum(y * y, axis=-1, keepdims=True))
        y = jnp.concatenate([y, norm * side_ref[...]], axis=-1)
        nmax_ref[...] = jnp.broadcast_to(jnp.max(norm, axis=1, keepdims=True), nmax_ref.shape)
    else:
        o_ref, = rest
    o_ref[...] = y.astype(o_ref.dtype)


def qk_prep(x, gains, hb, n_blocks, tables=None, side=None):
    n_h, n_tok, hd = x.shape
    ts = PREP_TILE
    rope = tables is not None
    aug = side is not None
    n_used = n_blocks * hb
    blk = pl.BlockSpec((hb, ts, hd), lambda i, s: (i, s, 0))
    per_head = pl.BlockSpec((hb, 1, hd), lambda i, s: (i, 0, 0))
    in_specs = [blk, per_head]
    args = [x, gains.reshape(n_used, 1, hd).astype(F32)]
    if rope:
        tab = pl.BlockSpec((ts, hd), lambda i, s: (s, 0))
        in_specs += [tab, tab, pl.BlockSpec((hd, hd), lambda i, s: (0, 0))]
        args += list(tables)
    if aug:
        in_specs.append(per_head)
        args.append(side.reshape(n_used, 1, hd).astype(F32))
        out_specs = [pl.BlockSpec((hb, ts, 2 * hd), lambda i, s: (i, s, 0)),
                     pl.BlockSpec((hb, None, 1, 2 * hd), lambda i, s: (i, s, 0, 0))]
        out_shape = [jax.ShapeDtypeStruct((n_used, n_tok, 2 * hd), BF16),
                     jax.ShapeDtypeStruct((n_used, n_tok // ts, 1, 2 * hd), F32)]
    else:
        out_specs = blk
        out_shape = jax.ShapeDtypeStruct(x.shape, BF16)
    return pl.pallas_call(
        functools.partial(_prep_kernel, rope=rope, aug=aug),
        grid=(n_blocks, n_tok // ts),
        in_specs=in_specs,
        out_specs=out_specs,
        out_shape=out_shape,
        input_output_aliases={} if aug else {0: 0},
        compiler_params=_params(("parallel", "parallel")),
        name="qk_prep",
    )(*args)


def _rope_angles(pos, dim):
    inv = ROPE_THETA ** (-jnp.arange(0, dim, 2, dtype=F32) / dim)
    ang = pos.astype(F32)[:, None] * inv[None, :]
    ang = jnp.concatenate([ang, ang], axis=-1)
    return jnp.cos(ang), jnp.sin(ang)


def _rotate_half_matrix(dim, n_blocks):
    half = dim // 2
    i = jnp.arange(dim * n_blocks)
    blk, w = i // dim, i % dim
    src = blk * dim + jnp.where(w < half, w + half, w - half)
    sign = jnp.where(w < half, -1.0, 1.0)
    p = jnp.zeros((dim * n_blocks, dim * n_blocks), F32).at[src, i].set(sign)
    return p.astype(BF16)


def _na_bias_table(rpb):
    n_h = rpb.shape[0]
    qc = jnp.arange(GRID_W)[:, None]
    kc = jnp.arange(GRID_W)[None, :]
    c0 = jnp.clip(qc - NA_KW // 2, 0, GRID_W - NA_KW)
    ok = (kc >= c0) & (kc < c0 + NA_KW)
    n_dr = rpb.shape[1]
    period = 2 * GRID_W
    ext = jnp.concatenate([rpb[..., NA_KW - 1:],
                           jnp.zeros((n_h, n_dr, period - (2 * NA_KW - 1)), rpb.dtype),
                           rpb[..., :NA_KW - 1]], axis=-1)
    toep = jnp.tile(ext, (1, 1, GRID_W))[..., :GRID_W * (period - 1)]
    toep = toep.reshape(n_h, n_dr, GRID_W, period - 1)[..., :GRID_W]
    toep = jnp.where(ok[None, None], toep, NEG)
    bias = jnp.stack([toep[:, NA_KH - 1 - d0:2 * NA_KH - 1 - d0] for d0 in range(NA_KH)])
    bias = bias.transpose(0, 1, 3, 2, 4).reshape(NA_KH, n_h, GRID_W, NA_KH * GRID_W)
    return bias.astype(BF16)


def _na_kernel(q_ref, kp_ref, kc_ref, kn_ref, vp_ref, vc_ref, vn_ref, tbl_ref,
               o_ref, ks_ref, vs_ref, *, geom):
    blk = NA_BLOCK
    j = pl.program_id(0)
    seq_start, seq_len = geom.seq_span(j * blk)
    rows = seq_len // GRID_W
    jl = j - seq_start // blk
    ks_ref[:, 0:blk, :] = kp_ref[...]
    ks_ref[:, blk:2 * blk, :] = kc_ref[...]
    ks_ref[:, 2 * blk:3 * blk, :] = kn_ref[...]
    vs_ref[:, 0:blk, :] = vp_ref[...]
    vs_ref[:, blk:2 * blk, :] = vc_ref[...]
    vs_ref[:, 2 * blk:3 * blk, :] = vn_ref[...]
    n_keys = NA_KH * GRID_W

    def body(ii, carry):
        probs = []
        for u in range(NA_ROWS_PER_TRIP):
            i = ii * NA_ROWS_PER_TRIP + u
            r = jl * NA_KH + i
            r0 = jnp.clip(r - NA_KH // 2, 0, rows - NA_KH)
            off = pl.multiple_of((r0 - (jl - 1) * NA_KH) * GRID_W, GRID_W)
            d0 = r - r0
            qoff = pl.multiple_of(i * GRID_W, GRID_W)
            for h in range(NA_HEADS):
                q = q_ref[h, pl.ds(qoff, GRID_W), :]
                k = ks_ref[h, pl.ds(off, n_keys), :]
                s = lax.dot_general(q, k, (((1,), (1,)), ((), ())), preferred_element_type=F32)
                probs.append((h, off, qoff, s + tbl_ref[d0, h].astype(F32)))
        soft = []
        for h, off, qoff, s in probs:
            m = jnp.max(s, axis=-1, keepdims=True)
            p = jnp.exp(s - m)
            soft.append((h, off, qoff, p.astype(BF16), jnp.sum(p, axis=-1, keepdims=True)))
        for h, off, qoff, p, l in soft:
            v = vs_ref[h, pl.ds(off, n_keys), :]
            o = jnp.dot(p, v, preferred_element_type=F32) / l
            o_ref[h, pl.ds(qoff, GRID_W), :] = o.astype(o_ref.dtype)
        return carry

    lax.fori_loop(0, NA_KH // NA_ROWS_PER_TRIP, body, 0)


def neighborhood_attention(qkv, tbl, geom):
    _, n_tok, hd = qkv.shape
    n_h = NA_HEADS
    blk = NA_BLOCK
    nb = n_tok // blk

    def spec(head0, shift):
        hb = head0 // n_h
        return pl.BlockSpec((n_h, blk, hd), lambda j: (hb, jnp.clip(j + shift, 0, nb - 1), 0))

    return pl.pallas_call(
        functools.partial(_na_kernel, geom=geom),
        grid=(nb,),
        in_specs=[spec(NA_Q0, 0), spec(NA_K0, -1), spec(NA_K0, 0), spec(NA_K0, 1),
                  spec(NA_V0, -1), spec(NA_V0, 0), spec(NA_V0, 1),
                  pl.BlockSpec(tbl.shape, lambda j: (0, 0, 0, 0))],
        out_specs=pl.BlockSpec((n_h, blk, hd), lambda j: (0, j, 0)),
        out_shape=jax.ShapeDtypeStruct((n_h, n_tok, hd), BF16),
        scratch_shapes=[pltpu.VMEM((n_h, 3 * blk, hd), BF16),
                        pltpu.VMEM((n_h, 3 * blk, hd), BF16)],
        compiler_params=_params(("parallel",), VMEM_LIMIT),
        name="neighborhood_attention",
    )(qkv, qkv, qkv, qkv, qkv, qkv, qkv, tbl)


def _dil_kernel(q_ref, kp_ref, kc_ref, kn_ref, vp_ref, vc_ref, vn_ref,
                o_ref, lse_ref, ks_ref, vs_ref, *, geom, dil):
    c = pl.program_id(0)
    n_h, chunk, hd = q_ref.shape
    side = DIL_SIDE
    qb = DIL_QBLOCK
    kb = qb + 2 * side
    ks_ref[:, 0:side, :] = kp_ref[...]
    ks_ref[:, side:side + chunk, :] = kc_ref[...]
    ks_ref[:, side + chunk:, :] = kn_ref[...]
    vs_ref[:, 0:side, :] = vp_ref[...]
    vs_ref[:, side:side + chunk, :] = vc_ref[...]
    vs_ref[:, side + chunk:, :] = vn_ref[...]
    plane_rows = geom.n_tok // dil
    t0 = ((c * chunk) % plane_rows) * dil
    seq_start, seq_len = geom.seq_span(t0)
    row0 = (t0 - seq_start) // dil
    cls_rows = seq_len // dil
    jq = lax.broadcasted_iota(jnp.int32, (qb, kb), 0)
    jk = lax.broadcasted_iota(jnp.int32, (qb, kb), 1) - side
    near_bias = jnp.where(jnp.abs(jk - jq) <= side, 0.0, NEG)
    before_bias = jnp.where(jk < 0, NEG, 0.0)
    after_bias = jnp.where(jk >= qb, NEG, 0.0)
    n_trip = min(DIL_PROBLEMS_PER_TRIP, chunk // qb)
    trips_per_head = chunk // (qb * n_trip)
    assert chunk % (qb * n_trip) == 0

    def body(it, carry):
        h = it // trips_per_head
        ii = it % trips_per_head
        probs = []
        for u in range(n_trip):
            i = ii * n_trip + u
            qoff = pl.multiple_of(i * qb, qb)
            p0 = row0 + i * qb
            bias = near_bias + jnp.where(p0 == 0, before_bias, 0.0)
            bias = bias + jnp.where(p0 + qb == cls_rows, after_bias, 0.0)
            q = q_ref[h, pl.ds(qoff, qb), :]
            k = ks_ref[h, pl.ds(qoff, kb), :]
            s = lax.dot_general(q, k, (((1,), (1,)), ((), ())), preferred_element_type=F32)
            probs.append((qoff, s + bias))
        soft = []
        for qoff, s in probs:
            m = jnp.max(s, axis=-1, keepdims=True)
            p = jnp.exp(s - m)
            soft.append((qoff, p.astype(BF16), m, jnp.sum(p, axis=-1, keepdims=True)))
        for qoff, p, m, l in soft:
            v = vs_ref[h, pl.ds(qoff, kb), :]
            o = jnp.dot(p, v, preferred_element_type=F32) / l
            o_ref[h, pl.ds(qoff, qb), :] = o.astype(o_ref.dtype)
            lse_ref[h, pl.ds(qoff, qb), :] = jnp.broadcast_to(m + jnp.log(l), (qb, hd))
        return carry

    lax.fori_loop(0, n_h * trips_per_head, body, 0)


def dilated_attention(qkv, dil, geom):
    _, n_tok, hd = qkv.shape
    n_h = DIL_HEADS
    chunk = min(DIL_CHUNK, min(geom.s_p, geom.s_d) // dil)
    assert chunk % DIL_QBLOCK == 0
    per = chunk // DIL_SIDE
    n_side = n_tok // DIL_SIDE

    def cur(hb):
        return pl.BlockSpec((n_h, chunk, hd), lambda c: (hb, c, 0))

    def prev(hb):
        return pl.BlockSpec((n_h, DIL_SIDE, hd), lambda c: (hb, jnp.maximum(c * per - 1, 0), 0))

    def nxt(hb):
        return pl.BlockSpec((n_h, DIL_SIDE, hd),
                            lambda c: (hb, jnp.minimum((c + 1) * per, n_side - 1), 0))

    return pl.pallas_call(
        functools.partial(_dil_kernel, geom=geom, dil=dil),
        grid=(n_tok // chunk,),
        in_specs=[cur(0), prev(1), cur(1), nxt(1), prev(2), cur(2), nxt(2)],
        out_specs=[cur(0), cur(0)],
        out_shape=[jax.ShapeDtypeStruct((n_h, n_tok, hd), BF16),
                   jax.ShapeDtypeStruct((n_h, n_tok, hd), F32)],
        scratch_shapes=[pltpu.VMEM((n_h, chunk + 2 * DIL_SIDE, hd), BF16),
                        pltpu.VMEM((n_h, chunk + 2 * DIL_SIDE, hd), BF16)],
        compiler_params=_params(("parallel",), VMEM_LIMIT),
        name="dilated_attention",
    )(qkv, qkv, qkv, qkv, qkv, qkv, qkv)


def _to_natural(src_ref, dst_ref):
    n_h, dil, n, _ = src_ref.shape
    for h in range(0, n_h, 2):
        for r in range(dil):
            pair = jnp.concatenate([src_ref[h, r].astype(F32), src_ref[h + 1, r].astype(F32)],
                                   axis=-1)
            if dil == 1:
                dst_ref[h // 2] = pair
            else:
                dst_ref[h // 2, pl.ds(r, n, stride=dil), :] = pair


def _outproj_ab_kernel(x_ref, mod_ref, oa_ref, od0_ref, od1_ref, od2_ref,
                       l0_ref, l1_ref, l2_ref, w_ref, o_ref, *scratch):
    so = scratch[:N_DIL]
    sl = scratch[N_DIL:]
    for src, dst in zip((od0_ref, od1_ref, od2_ref, l0_ref, l1_ref, l2_ref), so + sl):
        _to_natural(src, dst)
    n_h = oa_ref.shape[0]
    slabs = [jnp.concatenate([oa_ref[h], oa_ref[h + 1]], axis=-1) for h in range(0, n_h, 2)]
    for s in range(n_h // 2):
        l0, l1, l2 = sl[0][s], sl[1][s], sl[2][s]
        m = jnp.maximum(jnp.maximum(l0, l1), l2)
        e0, e1, e2 = jnp.exp(l0 - m), jnp.exp(l1 - m), jnp.exp(l2 - m)
        ob = (e0 * so[0][s] + e1 * so[1][s] + e2 * so[2][s]) / (e0 + e1 + e2)
        slabs.append(ob.astype(BF16))
    cat = jnp.concatenate(slabs, axis=-1)
    res = jnp.dot(cat, w_ref[...], preferred_element_type=F32)
    o_ref[...] = x_ref[...] + mod_ref[2:3, :] * res


def outproj_ab(x, mod, oa, od, lse, w_bf16, layer, geom):
    n_tok, d = x.shape
    n_h, _, hd = oa.shape
    tm = TOKEN_TILE // 2

    def cls(a):
        dil = a.shape[1]
        return pl.BlockSpec((n_h, dil, tm // dil, hd), lambda i: (0, 0, i, 0))

    return pl.pallas_call(
        _outproj_ab_kernel,
        grid=(n_tok // tm,),
        in_specs=[
            pl.BlockSpec((tm, d), lambda i: (i, 0)),
            pl.BlockSpec((None, None, 6, d), lambda i: (layer, geom.seq_of_token(i * tm), 0, 0)),
            pl.BlockSpec((n_h, tm, hd), lambda i: (0, i, 0)),
            *[cls(a) for a in od], *[cls(a) for a in lse],
            pl.BlockSpec(w_bf16.shape, lambda i: (0, 0)),
        ],
        out_specs=pl.BlockSpec((tm, d), lambda i: (i, 0)),
        out_shape=jax.ShapeDtypeStruct(x.shape, F32),
        scratch_shapes=[pltpu.VMEM((n_h // 2, tm, LANES), F32) for _ in range(2 * N_DIL)],
        compiler_params=_params(("parallel",), VMEM_LIMIT),
        name="outproj_ab",
    )(x, mod, oa, *od, *lse, w_bf16)


def _outproj_kernel(x_ref, mod_ref, o_in_ref, w_ref, o_ref):
    res = jnp.dot(o_in_ref[...], w_ref[...], preferred_element_type=F32)
    o_ref[...] = x_ref[...] + mod_ref[2:3, :] * res


def outproj(x, mod, o_in, w_bf16, layer, geom):
    n_tok, d = x.shape
    tm = TOKEN_TILE
    return pl.pallas_call(
        _outproj_kernel,
        grid=(n_tok // tm,),
        in_specs=[
            pl.BlockSpec((tm, d), lambda i: (i, 0)),
            pl.BlockSpec((None, None, 6, d), lambda i: (layer, geom.seq_of_token(i * tm), 0, 0)),
            pl.BlockSpec((tm, o_in.shape[1]), lambda i: (i, 0)),
            pl.BlockSpec(w_bf16.shape, lambda i: (0, 0)),
        ],
        out_specs=pl.BlockSpec((tm, d), lambda i: (i, 0)),
        out_shape=jax.ShapeDtypeStruct(x.shape, F32),
        compiler_params=_params(("parallel",), VMEM_LIMIT),
        name="outproj",
    )(x, mod, o_in, w_bf16)


def _flash_write_out(acc_ref, o_ref, n_g, tq):
    hd = HEAD_DIM
    acc = acc_ref[...]
    o = acc[:hd] / acc[hd:hd + 1]
    pad = jnp.zeros((LANES - hd, tq), F32)
    for g in range(n_g):
        t = jnp.concatenate([o[:, g * tq:(g + 1) * tq], pad], axis=0).T
        o_ref[:, g * hd:(g + 1) * hd] = t[:, :hd].astype(o_ref.dtype)


def _flash_bounded_kernel(q_ref, k_ref, vt_ref, kmx_ref, _o_in_ref, o_ref, acc_ref, *, unroll):
    n_g, tq, kw = q_ref.shape
    n_kt, tk, _ = k_ref.shape
    acc_ref[...] = jnp.zeros(acc_ref.shape, F32)
    kmx = kmx_ref[...]
    n_col = (n_g * tq) // MXU_TILE

    n_kc = tk // MXU_TILE

    def qk(j, n, kc):
        rows = slice(kc * MXU_TILE, (kc + 1) * MXU_TILE)
        q_n = q_ref[...].reshape(n_g * tq, kw)[n * MXU_TILE:(n + 1) * MXU_TILE]
        return lax.dot_general(k_ref[j, rows, :] + kmx, q_n, (((1,), (1,)), ((), ())),
                               preferred_element_type=F32)

    def body(jj, carry):
        units = [(unroll * jj + u, n, kc) for u in range(unroll) for n in range(n_col)
                 for kc in range(n_kc)]
        pending = [qk(*unit) for unit in units[:FLASH_LOOKAHEAD]]
        for idx, (j, n, kc) in enumerate(units):
            if idx + FLASH_LOOKAHEAD < len(units):
                pending.append(qk(*units[idx + FLASH_LOOKAHEAD]))
            cols = slice(n * MXU_TILE, (n + 1) * MXU_TILE)
            rows = slice(kc * MXU_TILE, (kc + 1) * MXU_TILE)
            p = jnp.exp2(pending.pop(0)).astype(BF16)
            acc_ref[:, cols] += jnp.dot(vt_ref[j, :, rows], p, preferred_element_type=F32)
        return carry

    lax.fori_loop(0, n_kt // unroll, body, 0)
    _flash_write_out(acc_ref, o_ref, n_g, tq)


def _flash_kernel(q_ref, k_ref, vt_ref, _o_in_ref, o_ref, sa_ref, sb_ref, mxa_ref, mxb_ref,
                  m_ref, acc_ref, *, unroll):
    n_g, tq, hd = q_ref.shape
    n_kt = k_ref.shape[0]
    m_ref[...] = jnp.full(m_ref.shape, -jnp.inf, F32)
    acc_ref[...] = jnp.zeros(acc_ref.shape, F32)
    n_col = (n_g * tq) // MXU_TILE

    def scores(j, s_ref, mx_ref, n):
        cols = slice(n * MXU_TILE, (n + 1) * MXU_TILE)
        q_n = q_ref[...].reshape(n_g * tq, hd)[n * MXU_TILE:(n + 1) * MXU_TILE]
        s = lax.dot_general(k_ref[j], q_n, (((1,), (1,)), ((), ())),
                            preferred_element_type=F32)
        s_ref[:, cols] = s
        mx_ref[:, cols] = jnp.max(s, axis=0, keepdims=True)

    def consume(j, s_ref, mx_ref, n):
        cols = slice(n * MXU_TILE, (n + 1) * MXU_TILE)
        m_old = m_ref[:, cols]
        m_new = jnp.maximum(m_old, mx_ref[:, cols])
        alpha = jnp.exp2(m_old - m_new)
        m_ref[:, cols] = m_new
        tk = s_ref.shape[0]
        acc = alpha * acc_ref[:, cols]
        for kc in range(tk // MXU_TILE):
            rows = slice(kc * MXU_TILE, (kc + 1) * MXU_TILE)
            p = jnp.exp2(s_ref[rows, cols] - m_new).astype(BF16)
            acc = acc + jnp.dot(vt_ref[j, :, rows], p, preferred_element_type=F32)
        acc_ref[:, cols] = acc

    for n in range(n_col):
        scores(0, sa_ref, mxa_ref, n)

    def body(jj, carry):
        j = unroll * jj
        for u in range(0, unroll, 2):
            for n in range(n_col):
                scores(j + u + 1, sb_ref, mxb_ref, n)
                consume(j + u, sa_ref, mxa_ref, n)
            for n in range(n_col):
                scores(jnp.minimum(j + u + 2, n_kt - 1), sa_ref, mxa_ref, n)
                consume(j + u + 1, sb_ref, mxb_ref, n)
        return carry

    lax.fori_loop(0, n_kt // unroll, body, 0)
    _flash_write_out(acc_ref, o_ref, n_g, tq)


def flash_attention(qk, vt_tiles, o_buf, tok0, n_b, s_n, seq_idx0, kmx=None):
    n_all, n_tok, kw = qk.shape
    n_kv, _, hv, tk = vt_tiles.shape
    n_g = C_Q_HEADS // n_kv
    hd = HEAD_DIM
    tq = FLASH_TQ
    nq = s_n // tq
    kt_per = s_n // tk
    unroll = min(FLASH_UNROLL if kmx is None else FLASH_BOUNDED_UNROLL, kt_per)
    assert kt_per % unroll == 0 and unroll % 2 == 0 and tok0 % s_n == 0
    wq = n_g * tq
    k_view = qk.reshape(n_all, n_tok // tk, tk, kw)
    seq0 = tok0 // s_n
    in_specs = [
        pl.BlockSpec((n_g, tq, kw), lambda b, h, i: (h, (tok0 + b * s_n) // tq + i, 0)),
        pl.BlockSpec((None, kt_per, tk, kw), lambda b, h, i: (C_K0 + h, seq0 + b, 0, 0)),
        pl.BlockSpec((None, kt_per, hv, tk), lambda b, h, i: (h, seq0 + b, 0, 0)),
    ]
    args = [qk, k_view, vt_tiles]
    if kmx is None:
        body = functools.partial(_flash_kernel, unroll=unroll)
        scratch = [pltpu.VMEM((tk, wq), F32), pltpu.VMEM((tk, wq), F32),
                   pltpu.VMEM((1, wq), F32), pltpu.VMEM((1, wq), F32),
                   pltpu.VMEM((1, wq), F32), pltpu.VMEM((hv, wq), F32)]
    else:
        body = functools.partial(_flash_bounded_kernel, unroll=unroll)
        scratch = [pltpu.VMEM((hv, wq), F32)]
        in_specs.append(pl.BlockSpec((None, None, 1, kw), lambda b, h, i: (h, seq_idx0 + b, 0, 0)))
        args.append(kmx)
    in_specs.append(pl.BlockSpec(memory_space=pl.ANY))
    args.append(o_buf)
    return pl.pallas_call(
        body,
        grid=(n_b, n_kv, nq),
        in_specs=in_specs,
        out_specs=pl.BlockSpec((tq, n_g * hd), lambda b, h, i: ((tok0 + b * s_n) // tq + i, h)),
        out_shape=jax.ShapeDtypeStruct(o_buf.shape, o_buf.dtype),
        input_output_aliases={len(args) - 1: 0},
        scratch_shapes=scratch,
        compiler_params=_params(("parallel", "parallel", "arbitrary"), VMEM_LIMIT),
        name="flash_attention",
    )(*args)


def _router_kernel(x_ref, mod_ref, g_ref, wr_ref, br_ref, h_ref, idx_ref, gate_ref, cnt_ref,
                   run_ref):
    @pl.when(pl.program_id(0) == 0)
    def _():
        run_ref[...] = jnp.zeros(run_ref.shape, F32)

    h = _modulated_norm(x_ref[...], g_ref[...], mod_ref[3:4, :], mod_ref[4:5, :])
    h_ref[...] = h.astype(BF16)
    logits = lax.dot_general(wr_ref[...], h, (((1,), (1,)), ((), ())), precision=HIGHEST,
                             preferred_element_type=F32) + br_ref[...]
    n_e, tm = logits.shape
    iota = lax.broadcasted_iota(jnp.int32, (n_e, tm), 0)
    vals, ids = [], []
    cur = logits
    for _ in range(TOP_K):
        m = jnp.max(cur, axis=0, keepdims=True)
        am = jnp.min(jnp.where(cur == m, iota, n_e), axis=0, keepdims=True)
        vals.append(m)
        ids.append(am)
        cur = jnp.where(iota == am, -jnp.inf, cur)
    es = [jnp.exp(v - vals[0]) for v in vals]
    den = es[0] + es[1] + es[2] + es[3]
    chosen = jnp.zeros((n_e, tm), F32)
    for kk in range(TOP_K):
        chosen = chosen + jnp.where(iota == ids[kk], 1.0, 0.0)
    earlier = lax.broadcasted_iota(jnp.int32, (tm, tm), 0)
    token = lax.broadcasted_iota(jnp.int32, (tm, tm), 1)
    tri = jnp.where(earlier < token, 1.0, 0.0).astype(BF16)
    before = jnp.dot(chosen.astype(BF16), tri, preferred_element_type=F32) + run_ref[...]
    run_ref[...] = run_ref[...] + jnp.sum(chosen, axis=1, keepdims=True)
    cnt_ref[...] = run_ref[...]
    row = lax.broadcasted_iota(jnp.int32, idx_ref.shape, 0)
    idx_out = jnp.zeros(idx_ref.shape, jnp.int32)
    gate_out = jnp.zeros(gate_ref.shape, F32)
    for kk in range(TOP_K):
        rank = jnp.sum(jnp.where(iota == ids[kk], before, 0.0), axis=0, keepdims=True)
        idx_out = jnp.where(row == kk, ids[kk], idx_out)
        idx_out = jnp.where(row == TOP_K + kk, rank.astype(jnp.int32), idx_out)
        gate_out = jnp.where(row == kk, es[kk] / den, gate_out)
    idx_ref[...] = idx_out
    gate_ref[...] = gate_out


def router(x, mod, g_norm, w_router, b_router, layer, geom, tile0, n_tiles):
    d = x.shape[1]
    tm = MOE_TOKEN_TILE
    n_tok = n_tiles * tm
    return pl.pallas_call(
        _router_kernel,
        grid=(n_tiles,),
        in_specs=[
            pl.BlockSpec((tm, d), lambda i: (tile0 + i, 0)),
            pl.BlockSpec((None, None, 6, d),
                         lambda i: (layer, geom.seq_of_token((tile0 + i) * tm), 0, 0)),
            pl.BlockSpec((None, 1, d), lambda i: (layer, 0, 0)),
            pl.BlockSpec((None, N_EXPERTS, d), lambda i: (layer, 0, 0)),
            pl.BlockSpec((None, N_EXPERTS, 1), lambda i: (layer, 0, 0)),
        ],
        out_specs=[
            pl.BlockSpec((tm, d), lambda i: (i, 0)),
            pl.BlockSpec((2 * TOP_K, tm), lambda i: (0, i)),
            pl.BlockSpec((2 * TOP_K, tm), lambda i: (0, i)),
            pl.BlockSpec((N_EXPERTS, 1), lambda i: (0, 0)),
        ],
        out_shape=[
            jax.ShapeDtypeStruct((n_tok, d), BF16),
            jax.ShapeDtypeStruct((2 * TOP_K, n_tok), jnp.int32),
            jax.ShapeDtypeStruct((2 * TOP_K, n_tok), F32),
            jax.ShapeDtypeStruct((N_EXPERTS, 1), F32),
        ],
        scratch_shapes=[pltpu.VMEM((N_EXPERTS, 1), F32)],
        compiler_params=_params(("arbitrary",), VMEM_LIMIT),
        name="router",
    )(x, mod, g_norm.reshape(-1, 1, d), w_router.transpose(0, 2, 1),
      b_router.reshape(-1, N_EXPERTS, 1))


def _moe_kernel(be_ref, nu_ref, x_ref, wgu_ref, bgu_ref, wd_ref, bd_ref, o_ref,
                wgu_s, wd_s):
    i = pl.program_id(0)
    e = be_ref[i]
    e_prev = be_ref[jnp.maximum(i - 1, 0)]

    @pl.when((i == 0) | (e != e_prev))
    def _():
        wgu_s[...] = wgu_ref[...].astype(BF16)
        wd_s[...] = wd_ref[...].astype(BF16)

    @pl.when(i < nu_ref[0])
    def _():
        d_ff = wd_s.shape[0]
        gu = jnp.dot(x_ref[...], wgu_s[...], preferred_element_type=F32) + bgu_ref[...]
        glu = jnp.minimum(gu[:, :d_ff], SWIGLU_LIMIT)
        lin = jnp.clip(gu[:, d_ff:], -SWIGLU_LIMIT, SWIGLU_LIMIT)
        act = glu / (1.0 + jnp.exp(-SWIGLU_ALPHA * glu)) * (lin + 1.0)
        y = jnp.dot(act.astype(BF16), wd_s[...], preferred_element_type=F32) + bd_ref[...]
        o_ref[...] = y.astype(o_ref.dtype)

    @pl.when(i >= nu_ref[0])
    def _():
        o_ref[...] = jnp.zeros(o_ref.shape, o_ref.dtype)


def moe_experts(xs, blk_e, n_used, w_gu, b_gu, w_down, b_down, layer):
    n_rows, d = xs.shape
    d_ff = w_down.shape[2]
    nblk = n_rows // MOE_BLOCK
    grid_spec = pltpu.PrefetchScalarGridSpec(
        num_scalar_prefetch=2,
        grid=(nblk,),
        in_specs=[
            pl.BlockSpec((MOE_BLOCK, d), lambda i, be, nu: (jnp.minimum(i, nu[0] - 1), 0)),
            pl.BlockSpec((None, None, d, 2 * d_ff), lambda i, be, nu: (layer, be[i], 0, 0)),
            pl.BlockSpec((None, None, 1, 2 * d_ff), lambda i, be, nu: (layer, be[i], 0, 0)),
            pl.BlockSpec((None, None, d_ff, d), lambda i, be, nu: (layer, be[i], 0, 0)),
            pl.BlockSpec((None, None, 1, d), lambda i, be, nu: (layer, be[i], 0, 0)),
        ],
        out_specs=pl.BlockSpec((MOE_BLOCK, d), lambda i, be, nu: (i, 0)),
        scratch_shapes=[pltpu.VMEM((d, 2 * d_ff), BF16), pltpu.VMEM((d_ff, d), BF16)],
    )
    depth, n_e = w_gu.shape[:2]
    return pl.pallas_call(
        _moe_kernel,
        grid_spec=grid_spec,
        out_shape=jax.ShapeDtypeStruct((n_rows, d), BF16),
        compiler_params=_params(("arbitrary",), VMEM_LIMIT),
        name="moe_experts",
    )(blk_e, n_used, xs, w_gu, b_gu.reshape(depth, n_e, 1, 2 * d_ff), w_down,
      b_down.reshape(depth, n_e, 1, d))


def _combine_kernel(x_ref, mod_ref, y_ref, gate_ref, o_ref):
    g = gate_ref[...]
    acc = g[:, 0:1] * y_ref[0].astype(F32)
    for kk in range(1, TOP_K):
        acc = acc + g[:, kk:kk + 1] * y_ref[kk].astype(F32)
    o_ref[...] = x_ref[...] + mod_ref[5:6, :] * acc


def moe_combine(x, mod, yk, gates, layer, geom, tile0, n_tiles):
    d = x.shape[1]
    tm = MOE_TOKEN_TILE
    return pl.pallas_call(
        _combine_kernel,
        grid=(n_tiles,),
        in_specs=[
            pl.BlockSpec((tm, d), lambda i: (tile0 + i, 0)),
            pl.BlockSpec((None, None, 6, d),
                         lambda i: (layer, geom.seq_of_token((tile0 + i) * tm), 0, 0)),
            pl.BlockSpec((TOP_K, tm, d), lambda i: (0, i, 0)),
            pl.BlockSpec((tm, 2 * TOP_K), lambda i: (i, 0)),
        ],
        out_specs=pl.BlockSpec((tm, d), lambda i: (tile0 + i, 0)),
        out_shape=jax.ShapeDtypeStruct(x.shape, F32),
        input_output_aliases={0: 0},
        compiler_params=_params(("parallel",), VMEM_LIMIT),
        name="moe_combine",
    )(x, mod, yk, gates)


def moe_layer(x, mod, g_norm, w_router, b_router, w_gu, b_gu, w_down, b_down, layer, geom):
    n_tiles = x.shape[0] // MOE_TOKEN_TILE
    assert n_tiles % MOE_PARTS == 0
    per = n_tiles // MOE_PARTS
    routed = [_moe_dispatch(x, mod, g_norm, w_router, b_router, w_gu, b_gu, w_down, b_down, layer,
                            geom, part * per, per) for part in range(MOE_PARTS)]
    for part, (yk, gates8) in enumerate(routed):
        x = moe_combine(x, mod, yk, gates8, layer, geom, part * per, per)
    return x


def _moe_dispatch(x, mod, g_norm, w_router, b_router, w_gu, b_gu, w_down, b_down, layer, geom,
                  tile0, n_tiles):
    d = x.shape[1]
    n_tok = n_tiles * MOE_TOKEN_TILE
    h, idx8, gates8, counts = router(x, mod, g_norm, w_router, b_router, layer, geom, tile0,
                                     n_tiles)
    idx = idx8[:TOP_K].T
    rank = idx8[TOP_K:].T
    gates8 = gates8.T
    counts = counts[:, 0].astype(jnp.int32)
    pcounts = ((counts + MOE_BLOCK - 1) // MOE_BLOCK) * MOE_BLOCK
    pend = jnp.cumsum(pcounts)
    pstart = pend - pcounts
    experts = jnp.arange(N_EXPERTS, dtype=jnp.int32)
    dest = rank + jnp.sum(jnp.where(idx[:, :, None] == experts, pstart, 0), axis=-1)
    nblk = (n_tok * TOP_K) // MOE_BLOCK + N_EXPERTS
    n_rows = nblk * MOE_BLOCK
    tok_ids = jnp.broadcast_to(jnp.arange(n_tok, dtype=jnp.int32)[:, None], dest.shape)
    tok = (jnp.arange(n_rows, dtype=jnp.int32) % n_tok).at[dest.reshape(-1)].set(
        tok_ids.reshape(-1), unique_indices=True, mode="promise_in_bounds")
    n_used = (pend[-1] // MOE_BLOCK).astype(jnp.int32).reshape(1)
    blk_start = jnp.arange(nblk, dtype=jnp.int32) * MOE_BLOCK
    blk_last = jnp.minimum(blk_start, pend[-1] - 1)
    blk_e = jnp.sum(pend[None, :] <= blk_last[:, None], axis=-1, dtype=jnp.int32)
    blk_e = jnp.minimum(blk_e, N_EXPERTS - 1)
    xs = jnp.take(h, tok, axis=0, mode="clip")
    yb = moe_experts(xs, blk_e, n_used, w_gu, b_gu, w_down, b_down, layer)
    yk = jnp.take(yb, dest.T.reshape(-1), axis=0, mode="clip").reshape(TOP_K, n_tok, d)
    return yk, gates8


def _per_token(geom, tables):
    return [jnp.concatenate([t[:s] for _, n_b, s in geom.groups() for _ in range(n_b)])
            for t in tables]


def _qk_gains(g_q, g_k, n_q, n_k, q_scale):
    hd = g_q.shape[-1]
    return jnp.concatenate([jnp.broadcast_to(g_q * q_scale, (n_q, hd)),
                            jnp.broadcast_to(g_k, (n_k, hd))])


def _dilated_group_weight(w_in, gi):
    blk = DIL_HEADS * HEAD_DIM
    base = 3 * NA_HEADS * HEAD_DIM
    return jnp.concatenate([w_in[:, base + (a * N_DIL + gi) * blk:base + (a * N_DIL + gi + 1) * blk]
                            for a in range(3)], axis=1)


def mixer_ab_layer(x, mod, g_norm, w_in, w_out, g_qn_a, g_kn_a, rpb, g_qn_b, g_kn_b, layer, geom):
    n_tok, d = x.shape
    hd = HEAD_DIM
    w = w_in.astype(BF16)
    n_h = NA_HEADS
    qkv = norm_proj(x, mod, g_norm, w[:, :3 * n_h * hd], layer, geom, dil=1)
    qkv = qk_prep(qkv.reshape(3 * n_h, n_tok, hd), _qk_gains(g_qn_a, g_kn_a, n_h, n_h, Q_SCALE),
                  hb=n_h, n_blocks=2)
    oa = neighborhood_attention(qkv, _na_bias_table(rpb), geom)
    cos, sin = _per_token(geom, _rope_angles(jnp.arange(max(geom.s_p, geom.s_d)), hd))
    perm = _rotate_half_matrix(hd, 1)
    od, lse = [], []
    for gi, (window, dil) in enumerate(DIL_PATTERNS):
        assert (window // 2) // dil == DIL_SIDE
        n_h = DIL_HEADS

        def class_major(t):
            return t.reshape(n_tok // dil, dil, hd).transpose(1, 0, 2).reshape(n_tok, hd)

        qkv = norm_proj(x, mod, g_norm, _dilated_group_weight(w, gi), layer, geom, dil=dil)
        qkv = qk_prep(qkv.reshape(3 * n_h, n_tok, hd),
                      _qk_gains(g_qn_b[gi], g_kn_b[gi], n_h, n_h, Q_SCALE), hb=n_h, n_blocks=2,
                      tables=(class_major(cos), class_major(sin), perm))
        o_g, lse_g = dilated_attention(qkv, dil, geom)
        od.append(o_g.reshape(n_h, dil, n_tok // dil, hd))
        lse.append(lse_g.reshape(n_h, dil, n_tok // dil, hd))
    return outproj_ab(x, mod, oa, od, lse, w_out.astype(BF16), layer, geom)


def mixer_c_layer(x, mod, g_norm, w_in, w_out, g_qn, g_kn, layer, geom):
    n_tok, d = x.shape
    hd = HEAD_DIM
    n_heads = C_Q_HEADS + 2 * C_KV_HEADS
    qkv = norm_proj(x, mod, g_norm, w_in.astype(BF16), layer, geom, dil=1).reshape(n_heads, n_tok, hd)
    pos = jnp.arange(max(geom.s_p, geom.s_d))
    half = hd // 2
    cr, sr = _rope_angles(pos // GRID_W, half)
    cc, sc = _rope_angles(pos % GRID_W, half)
    cos, sin = _per_token(geom, (jnp.concatenate([cr, cc], axis=-1),
                                 jnp.concatenate([sr, sc], axis=-1)))
    tables = (cos, sin, _rotate_half_matrix(half, 2))
    side = jnp.zeros((C_Q_HEADS + C_KV_HEADS, hd), F32).at[:C_Q_HEADS, 0].set(-FLASH_BOUND_MARGIN)
    qk, nmax = qk_prep(qkv, _qk_gains(g_qn, g_kn, C_Q_HEADS, C_KV_HEADS, Q_SCALE * LOG2_E),
                       hb=C_KV_HEADS, n_blocks=(C_Q_HEADS + C_KV_HEADS) // C_KV_HEADS,
                       tables=tables, side=side)
    tk = FLASH_TK
    vt_tiles = qkv[C_V0:].reshape(C_KV_HEADS, n_tok // tk, tk, hd).transpose(0, 1, 3, 2)
    ones = jnp.ones((C_KV_HEADS, n_tok // tk, BF16_SUBLANES, tk), BF16)
    vt_tiles = jnp.concatenate([vt_tiles, ones], axis=2)

    ts = PREP_TILE
    tile_max = nmax[:, :, 0, 0]
    seq_max = jnp.concatenate(
        [tile_max[:, t0 // ts:(t0 + n_b * s_n) // ts].reshape(-1, n_b, s_n // ts).max(-1)
         for t0, n_b, s_n in geom.groups()], axis=1)
    n_g = C_Q_HEADS // C_KV_HEADS
    k_max = (seq_max[C_K0:] * FLASH_BOUND_MARGIN).astype(BF16)
    q_max = seq_max[:C_K0] * FLASH_BOUND_MARGIN
    bound = jnp.max(q_max.reshape(C_KV_HEADS, n_g, -1) * k_max.astype(F32)[:, None, :])
    kmx = jnp.zeros((C_KV_HEADS, geom.n_seq, 1, 2 * hd), BF16).at[:, :, 0, hd].set(k_max)
    o_buf = jnp.zeros((n_tok, C_Q_HEADS * hd), BF16)

    def attend(kmx_or_none):
        o = o_buf
        seq_idx0 = 0
        for tok0, n_b, s_n in geom.groups():
            o = flash_attention(qk, vt_tiles, o, tok0, n_b, s_n, seq_idx0, kmx_or_none)
            seq_idx0 += n_b
        return o

    o = lax.cond(bound <= FLASH_SAFE_BOUND, lambda: attend(kmx), lambda: attend(None))
    return outproj(x, mod, o, w_out.astype(BF16), layer, geom)


def kernel(x_prompt, x_sample, c_prompt, c_sample, w_ada, b_ada, g_norm_mix, g_norm_ffn, w_in_ab, w_out_ab, g_qn_a, g_kn_a, rpb_a, g_qn_b, g_kn_b, w_in_c, w_out_c, g_qn_c, g_kn_c, w_router, b_router, w_gate_up, b_gate_up, w_down, b_down):
    b_p, s_p, d = x_prompt.shape
    b_d, s_d, _ = x_sample.shape
    geom = Geom(b_p, s_p, b_d, s_d)
    depth = w_ada.shape[0]
    x = jnp.concatenate([x_prompt.reshape(-1, d), x_sample.reshape(-1, d)], axis=0)
    mod = ada_modulation(jnp.concatenate([c_prompt, c_sample], axis=0), w_ada, b_ada)
    for layer in range(depth):
        i = layer // 2
        if layer % 2 == 0:
            x = mixer_ab_layer(x, mod, g_norm_mix, w_in_ab[i], w_out_ab[i], g_qn_a[i], g_kn_a[i],
                               rpb_a[i], g_qn_b[i], g_kn_b[i], layer, geom)
        else:
            x = mixer_c_layer(x, mod, g_norm_mix, w_in_c[i], w_out_c[i], g_qn_c[i], g_kn_c[i],
                              layer, geom)
        x = moe_layer(x, mod, g_norm_ffn, w_router, b_router, w_gate_up, b_gate_up, w_down,
                      b_down, layer, geom)
    y_prompt = x[:geom.n_p].reshape(b_p, s_p, d)
    y_sample = x[geom.n_p:].reshape(b_d, s_d, d)
    return (y_prompt, y_sample)
```

```python
import functools
import math
from typing import NamedTuple

import jax
import jax.numpy as jnp
from jax import lax
from jax.experimental import pallas as pl
from jax.experimental.pallas import tpu as pltpu

F32 = jnp.float32
BF16 = jnp.bfloat16
HIGHEST = lax.Precision.HIGHEST

GRID_W = 64
HEAD_DIM = 64
ROPE_THETA = 10000.0
EPS = 1e-6
NEG = -1e30
NA_HEADS = 8
NA_KH = 8
NA_KW = 16
DIL_HEADS = 8
DIL_PATTERNS = ((128, 1), (512, 4), (2048, 16))
N_DIL = len(DIL_PATTERNS)
C_Q_HEADS = 16
C_KV_HEADS = 4
N_EXPERTS = 32
TOP_K = 4
SWIGLU_LIMIT = 7.0
SWIGLU_ALPHA = 1.702
Q_SCALE = HEAD_DIM ** -0.5
LOG2_E = math.log2(math.e)
BF16_SUBLANES = 16
LANES = 128
MXU_TILE = 256

V7X_VMEM_BYTES = 64 * 1024 * 1024
VMEM_LIMIT = V7X_VMEM_BYTES - 8 * 1024 * 1024

TOKEN_TILE = 1024
PREP_TILE = 2048
NA_BLOCK = 8 * GRID_W
NA_ROWS_PER_TRIP = 4
DIL_QBLOCK = 128
DIL_SIDE = 64
DIL_CHUNK = 1024
DIL_PROBLEMS_PER_TRIP = 8
FLASH_TQ = 256
FLASH_TK = 512
FLASH_UNROLL = 8
FLASH_BOUNDED_UNROLL = 16
FLASH_LOOKAHEAD = 6
FLASH_SAFE_BOUND = 60.0
FLASH_BOUND_MARGIN = 1.0 + 2.0 ** -6
MOE_BLOCK = 512
MOE_TOKEN_TILE = 512
MOE_PARTS = 1

NA_Q0, NA_K0, NA_V0 = 0, NA_HEADS, 2 * NA_HEADS
C_K0 = C_Q_HEADS
C_V0 = C_Q_HEADS + C_KV_HEADS


class Geom(NamedTuple):
    b_p: int
    s_p: int
    b_d: int
    s_d: int

    @property
    def n_p(self):
        return self.b_p * self.s_p

    @property
    def n_tok(self):
        return self.n_p + self.b_d * self.s_d

    @property
    def n_seq(self):
        return self.b_p + self.b_d

    def groups(self):
        return ((0, self.b_p, self.s_p), (self.n_p, self.b_d, self.s_d))

    def seq_of_token(self, t0):
        return jnp.where(t0 < self.n_p, t0 // self.s_p,
                         self.b_p + (t0 - self.n_p) // self.s_d)

    def seq_span(self, t0):
        in_p = t0 < self.n_p
        start = jnp.where(in_p, (t0 // self.s_p) * self.s_p,
                          self.n_p + ((t0 - self.n_p) // self.s_d) * self.s_d)
        return start, jnp.where(in_p, self.s_p, self.s_d)


def _params(semantics, vmem=None):
    return pltpu.CompilerParams(dimension_semantics=semantics,
                                vmem_limit_bytes=vmem)


def _ada_kernel(c_ref, w_ref, b_ref, o_ref):
    c = c_ref[...]
    a = c / (1.0 + jnp.exp(-c))
    o_ref[...] = jnp.dot(a, w_ref[...], precision=HIGHEST,
                         preferred_element_type=F32) + b_ref[...]


def ada_modulation(c_all, w_ada, b_ada):
    depth, d, d6 = w_ada.shape
    n_seq = c_all.shape[0]
    rows = 8
    c_pad = jnp.zeros((rows, d), F32).at[:n_seq].set(c_all)
    out = pl.pallas_call(
        _ada_kernel,
        grid=(depth, d6 // d),
        in_specs=[
            pl.BlockSpec((rows, d), lambda l, j: (0, 0)),
            pl.BlockSpec((None, d, d), lambda l, j: (l, 0, j)),
            pl.BlockSpec((None, 1, d), lambda l, j: (l, 0, j)),
        ],
        out_specs=pl.BlockSpec((None, rows, d), lambda l, j: (l, 0, j)),
        out_shape=jax.ShapeDtypeStruct((depth, rows, d6), F32),
        compiler_params=_params(("arbitrary", "arbitrary")),
        name="ada_modulation",
    )(c_pad, w_ada, b_ada.reshape(depth, 1, d6))
    return out[:, :n_seq].reshape(depth, n_seq, 6, d)


def _modulated_norm(x, g, shift, scale):
    r = lax.rsqrt(jnp.mean(x * x, axis=-1, keepdims=True) + EPS)
    return (x * r * g) * (1.0 + scale) + shift


def _norm_proj_kernel(x_ref, mod_ref, g_ref, w_ref, o_ref, *scratch, dil):
    h = _modulated_norm(x_ref[...], g_ref[...], mod_ref[0:1, :], mod_ref[1:2, :])
    acc = jnp.dot(h.astype(BF16), w_ref[...], preferred_element_type=F32)
    tm = x_ref.shape[0]
    n_h, _, _, hd = o_ref.shape
    if dil == 1:
        for c in range(n_h):
            o_ref[c, 0] = acc[:, c * hd:(c + 1) * hd].astype(o_ref.dtype)
        return
    acc_ref, = scratch
    for s in range(n_h // 2):
        acc_ref[s] = acc[:, s * LANES:(s + 1) * LANES]
    for s in range(n_h // 2):
        for r in range(dil):
            pair = acc_ref[s, pl.ds(r, tm // dil, stride=dil), :].astype(o_ref.dtype)
            o_ref[2 * s, r] = pair[:, :hd]
            o_ref[2 * s + 1, r] = pair[:, hd:]


def norm_proj(x, mod, g_norm, w_bf16, layer, geom, dil):
    n_tok, d = x.shape
    n_out = w_bf16.shape[1]
    tm = TOKEN_TILE
    hd = HEAD_DIM
    n_h = n_out // hd
    scratch = [] if dil == 1 else [pltpu.VMEM((n_h // 2, tm, LANES), F32)]
    return pl.pallas_call(
        functools.partial(_norm_proj_kernel, dil=dil),
        grid=(n_tok // tm,),
        in_specs=[
            pl.BlockSpec((tm, d), lambda i: (i, 0)),
            pl.BlockSpec((None, None, 6, d), lambda i: (layer, geom.seq_of_token(i * tm), 0, 0)),
            pl.BlockSpec((None, 1, d), lambda i: (layer, 0, 0)),
            pl.BlockSpec((d, n_out), lambda i: (0, 0)),
        ],
        out_specs=pl.BlockSpec((n_h, dil, tm // dil, hd), lambda i: (0, 0, i, 0)),
        out_shape=jax.ShapeDtypeStruct((n_h, dil, n_tok // dil, hd), BF16),
        scratch_shapes=scratch,
        compiler_params=_params(("parallel",), VMEM_LIMIT),
        name="norm_proj",
    )(x, mod, g_norm.reshape(-1, 1, d), w_bf16)


def _prep_kernel(x_ref, g_ref, *rest, rope, aug):
    rest = list(rest)
    hb, ts, hd = x_ref.shape
    x = x_ref[...].astype(F32)
    r = lax.rsqrt(jnp.mean(x * x, axis=-1, keepdims=True) + EPS)
    y = x * r * g_ref[...]
    if rope:
        cos_ref, sin_ref, p_ref = rest[:3]
        rest = rest[3:]
        yr = jnp.dot(y.reshape(hb * ts, hd).astype(BF16), p_ref[...],
                     preferred_element_type=F32).reshape(hb, ts, hd)
        y = y * cos_ref[...] + yr * sin_ref[...]
    if aug:
        side_ref, o_ref, nmax_ref = rest
        norm = jnp.sqrt(jnp.sum(y * y, axis=-1, keepdims=True))
        y = jnp.concatenate([y, norm * side_ref[...]], axis=-1)
        nmax_ref[...] = jnp.broadcast_to(jnp.max(norm, axis=1, keepdims=True), nmax_ref.shape)
    else:
        o_ref, = rest
    o_ref[...] = y.astype(o_ref.dtype)


def qk_prep(x, gains, hb, n_blocks, tables=None, side=None):
    n_h, n_tok, hd = x.shape
    ts = PREP_TILE
    rope = tables is not None
    aug = side is not None
    n_used = n_blocks * hb
    blk = pl.BlockSpec((hb, ts, hd), lambda i, s: (i, s, 0))
    per_head = pl.BlockSpec((hb, 1, hd), lambda i, s: (i, 0, 0))
    in_specs = [blk, per_head]
    args = [x, gains.reshape(n_used, 1, hd).astype(F32)]
    if rope:
        tab = pl.BlockSpec((ts, hd), lambda i, s: (s, 0))
        in_specs += [tab, tab, pl.BlockSpec((hd, hd), lambda i, s: (0, 0))]
        args += list(tables)
    if aug:
        in_specs.append(per_head)
        args.append(side.reshape(n_used, 1, hd).astype(F32))
        out_specs = [pl.BlockSpec((hb, ts, 2 * hd), lambda i, s: (i, s, 0)),
                     pl.BlockSpec((hb, None, 1, 2 * hd), lambda i, s: (i, s, 0, 0))]
        out_shape = [jax.ShapeDtypeStruct((n_used, n_tok, 2 * hd), BF16),
                     jax.ShapeDtypeStruct((n_used, n_tok // ts, 1, 2 * hd), F32)]
    else:
        out_specs = blk
        out_shape = jax.ShapeDtypeStruct(x.shape, BF16)
    return pl.pallas_call(
        functools.partial(_prep_kernel, rope=rope, aug=aug),
        grid=(n_blocks, n_tok // ts),
        in_specs=in_specs,
        out_specs=out_specs,
        out_shape=out_shape,
        input_output_aliases={} if aug else {0: 0},
        compiler_params=_params(("parallel", "parallel")),
        name="qk_prep",
    )(*args)


def _rope_angles(pos, dim):
    inv = ROPE_THETA ** (-jnp.arange(0, dim, 2, dtype=F32) / dim)
    ang = pos.astype(F32)[:, None] * inv[None, :]
    ang = jnp.concatenate([ang, ang], axis=-1)
    return jnp.cos(ang), jnp.sin(ang)


def _rotate_half_matrix(dim, n_blocks):
    half = dim // 2
    i = jnp.arange(dim * n_blocks)
    blk, w = i // dim, i % dim
    src = blk * dim + jnp.where(w < half, w + half, w - half)
    sign = jnp.where(w < half, -1.0, 1.0)
    p = jnp.zeros((dim * n_blocks, dim * n_blocks), F32).at[src, i].set(sign)
    return p.astype(BF16)


def _na_bias_table(rpb):
    n_h = rpb.shape[0]
    qc = jnp.arange(GRID_W)[:, None]
    kc = jnp.arange(GRID_W)[None, :]
    c0 = jnp.clip(qc - NA_KW // 2, 0, GRID_W - NA_KW)
    ok = (kc >= c0) & (kc < c0 + NA_KW)
    n_dr = rpb.shape[1]
    period = 2 * GRID_W
    ext = jnp.concatenate([rpb[..., NA_KW - 1:],
                           jnp.zeros((n_h, n_dr, period - (2 * NA_KW - 1)), rpb.dtype),
                           rpb[..., :NA_KW - 1]], axis=-1)
    toep = jnp.tile(ext, (1, 1, GRID_W))[..., :GRID_W * (period - 1)]
    toep = toep.reshape(n_h, n_dr, GRID_W, period - 1)[..., :GRID_W]
    toep = jnp.where(ok[None, None], toep, NEG)
    bias = jnp.stack([toep[:, NA_KH - 1 - d0:2 * NA_KH - 1 - d0] for d0 in range(NA_KH)])
    bias = bias.transpose(0, 1, 3, 2, 4).reshape(NA_KH, n_h, GRID_W, NA_KH * GRID_W)
    return bias.astype(BF16)


def _na_kernel(q_ref, kp_ref, kc_ref, kn_ref, vp_ref, vc_ref, vn_ref, tbl_ref,
               o_ref, ks_ref, vs_ref, *, geom):
    blk = NA_BLOCK
    j = pl.program_id(0)
    seq_start, seq_len = geom.seq_span(j * blk)
    rows = seq_len // GRID_W
    jl = j - seq_start // blk
    ks_ref[:, 0:blk, :] = kp_ref[...]
    ks_ref[:, blk:2 * blk, :] = kc_ref[...]
    ks_ref[:, 2 * blk:3 * blk, :] = kn_ref[...]
    vs_ref[:, 0:blk, :] = vp_ref[...]
    vs_ref[:, blk:2 * blk, :] = vc_ref[...]
    vs_ref[:, 2 * blk:3 * blk, :] = vn_ref[...]
    n_keys = NA_KH * GRID_W

    def body(ii, carry):
        probs = []
        for u in range(NA_ROWS_PER_TRIP):
            i = ii * NA_ROWS_PER_TRIP + u
            r = jl * NA_KH + i
            r0 = jnp.clip(r - NA_KH // 2, 0, rows - NA_KH)
            off = pl.multiple_of((r0 - (jl - 1) * NA_KH) * GRID_W, GRID_W)
            d0 = r - r0
            qoff = pl.multiple_of(i * GRID_W, GRID_W)
            for h in range(NA_HEADS):
                q = q_ref[h, pl.ds(qoff, GRID_W), :]
                k = ks_ref[h, pl.ds(off, n_keys), :]
                s = lax.dot_general(q, k, (((1,), (1,)), ((), ())), preferred_element_type=F32)
                probs.append((h, off, qoff, s + tbl_ref[d0, h].astype(F32)))
        soft = []
        for h, off, qoff, s in probs:
            m = jnp.max(s, axis=-1, keepdims=True)
            p = jnp.exp(s - m)
            soft.append((h, off, qoff, p.astype(BF16), jnp.sum(p, axis=-1, keepdims=True)))
        for h, off, qoff, p, l in soft:
            v = vs_ref[h, pl.ds(off, n_keys), :]
            o = jnp.dot(p, v, preferred_element_type=F32) / l
            o_ref[h, pl.ds(qoff, GRID_W), :] = o.astype(o_ref.dtype)
        return carry

    lax.fori_loop(0, NA_KH // NA_ROWS_PER_TRIP, body, 0)


def neighborhood_attention(qkv, tbl, geom):
    _, n_tok, hd = qkv.shape
    n_h = NA_HEADS
    blk = NA_BLOCK
    nb = n_tok // blk

    def spec(head0, shift):
        hb = head0 // n_h
        return pl.BlockSpec((n_h, blk, hd), lambda j: (hb, jnp.clip(j + shift, 0, nb - 1), 0))

    return pl.pallas_call(
        functools.partial(_na_kernel, geom=geom),
        grid=(nb,),
        in_specs=[spec(NA_Q0, 0), spec(NA_K0, -1), spec(NA_K0, 0), spec(NA_K0, 1),
                  spec(NA_V0, -1), spec(NA_V0, 0), spec(NA_V0, 1),
                  pl.BlockSpec(tbl.shape, lambda j: (0, 0, 0, 0))],
        out_specs=pl.BlockSpec((n_h, blk, hd), lambda j: (0, j, 0)),
        out_shape=jax.ShapeDtypeStruct((n_h, n_tok, hd), BF16),
        scratch_shapes=[pltpu.VMEM((n_h, 3 * blk, hd), BF16),
                        pltpu.VMEM((n_h, 3 * blk, hd), BF16)],
        compiler_params=_params(("parallel",), VMEM_LIMIT),
        name="neighborhood_attention",
    )(qkv, qkv, qkv, qkv, qkv, qkv, qkv, tbl)


def _dil_kernel(q_ref, kp_ref, kc_ref, kn_ref, vp_ref, vc_ref, vn_ref,
                o_ref, lse_ref, ks_ref, vs_ref, *, geom, dil):
    c = pl.program_id(0)
    n_h, chunk, hd = q_ref.shape
    side = DIL_SIDE
    qb = DIL_QBLOCK
    kb = qb + 2 * side
    ks_ref[:, 0:side, :] = kp_ref[...]
    ks_ref[:, side:side + chunk, :] = kc_ref[...]
    ks_ref[:, side + chunk:, :] = kn_ref[...]
    vs_ref[:, 0:side, :] = vp_ref[...]
    vs_ref[:, side:side + chunk, :] = vc_ref[...]
    vs_ref[:, side + chunk:, :] = vn_ref[...]
    plane_rows = geom.n_tok // dil
    t0 = ((c * chunk) % plane_rows) * dil
    seq_start, seq_len = geom.seq_span(t0)
    row0 = (t0 - seq_start) // dil
    cls_rows = seq_len // dil
    jq = lax.broadcasted_iota(jnp.int32, (qb, kb), 0)
    jk = lax.broadcasted_iota(jnp.int32, (qb, kb), 1) - side
    near_bias = jnp.where(jnp.abs(jk - jq) <= side, 0.0, NEG)
    before_bias = jnp.where(jk < 0, NEG, 0.0)
    after_bias = jnp.where(jk >= qb, NEG, 0.0)
    blocks = chunk // qb
    n_trip = min(DIL_PROBLEMS_PER_TRIP, blocks)
    trips_per_head = blocks // n_trip
    heads_per_trip = max(1, DIL_PROBLEMS_PER_TRIP // blocks) if trips_per_head == 1 else 1
    assert blocks % n_trip == 0 and n_h % heads_per_trip == 0

    def body(it, carry):
        ii = it % trips_per_head
        probs = []
        for hh in range(heads_per_trip):
            h = (it // trips_per_head) * heads_per_trip + hh
            for u in range(n_trip):
                i = ii * n_trip + u
                qoff = pl.multiple_of(i * qb, qb)
                p0 = row0 + i * qb
                bias = near_bias + jnp.where(p0 == 0, before_bias, 0.0)
                bias = bias + jnp.where(p0 + qb == cls_rows, after_bias, 0.0)
                q = q_ref[h, pl.ds(qoff, qb), :]
                k = ks_ref[h, pl.ds(qoff, kb), :]
                s = lax.dot_general(q, k, (((1,), (1,)), ((), ())), preferred_element_type=F32)
                probs.append((h, qoff, s + bias))
        soft = []
        for h, qoff, s in probs:
            m = jnp.max(s, axis=-1, keepdims=True)
            p = jnp.exp(s - m)
            soft.append((h, qoff, p.astype(BF16), m, jnp.sum(p, axis=-1, keepdims=True)))
        for h, qoff, p, m, l in soft:
            v = vs_ref[h, pl.ds(qoff, kb), :]
            o = jnp.dot(p, v, preferred_element_type=F32) / l
            o_ref[h, pl.ds(qoff, qb), :] = o.astype(o_ref.dtype)
            lse_ref[h, pl.ds(qoff, qb), :] = jnp.broadcast_to(m + jnp.log(l), (qb, hd))
        return carry

    lax.fori_loop(0, n_h * trips_per_head // heads_per_trip, body, 0)


def dilated_attention(qkv, dil, geom):
    _, n_tok, hd = qkv.shape
    n_h = DIL_HEADS
    chunk = min(DIL_CHUNK, min(geom.s_p, geom.s_d) // dil)
    assert chunk % DIL_QBLOCK == 0
    per = chunk // DIL_SIDE
    n_side = n_tok // DIL_SIDE

    def cur(hb):
        return pl.BlockSpec((n_h, chunk, hd), lambda c: (hb, c, 0))

    def prev(hb):
        return pl.BlockSpec((n_h, DIL_SIDE, hd), lambda c: (hb, jnp.maximum(c * per - 1, 0), 0))

    def nxt(hb):
        return pl.BlockSpec((n_h, DIL_SIDE, hd),
                            lambda c: (hb, jnp.minimum((c + 1) * per, n_side - 1), 0))

    return pl.pallas_call(
        functools.partial(_dil_kernel, geom=geom, dil=dil),
        grid=(n_tok // chunk,),
        in_specs=[cur(0), prev(1), cur(1), nxt(1), prev(2), cur(2), nxt(2)],
        out_specs=[cur(0), cur(0)],
        out_shape=[jax.ShapeDtypeStruct((n_h, n_tok, hd), BF16),
                   jax.ShapeDtypeStruct((n_h, n_tok, hd), F32)],
        scratch_shapes=[pltpu.VMEM((n_h, chunk + 2 * DIL_SIDE, hd), BF16),
                        pltpu.VMEM((n_h, chunk + 2 * DIL_SIDE, hd), BF16)],
        compiler_params=_params(("parallel",), VMEM_LIMIT),
        name="dilated_attention",
    )(qkv, qkv, qkv, qkv, qkv, qkv, qkv)


def _to_natural(src_ref, dst_ref):
    n_h, dil, n, _ = src_ref.shape
    for h in range(0, n_h, 2):
        for r in range(dil):
            pair = jnp.concatenate([src_ref[h, r].astype(F32), src_ref[h + 1, r].astype(F32)],
                                   axis=-1)
            if dil == 1:
                dst_ref[h // 2] = pair
            else:
                dst_ref[h // 2, pl.ds(r, n, stride=dil), :] = pair


def _outproj_ab_kernel(x_ref, mod_ref, oa_ref, od0_ref, od1_ref, od2_ref,
                       l0_ref, l1_ref, l2_ref, w_ref, o_ref, *scratch):
    so = scratch[:N_DIL]
    sl = scratch[N_DIL:]
    for src, dst in zip((od0_ref, od1_ref, od2_ref, l0_ref, l1_ref, l2_ref), so + sl):
        _to_natural(src, dst)
    n_h = oa_ref.shape[0]
    slabs = [jnp.concatenate([oa_ref[h], oa_ref[h + 1]], axis=-1) for h in range(0, n_h, 2)]
    for s in range(n_h // 2):
        l0, l1, l2 = sl[0][s], sl[1][s], sl[2][s]
        m = jnp.maximum(jnp.maximum(l0, l1), l2)
        e0, e1, e2 = jnp.exp(l0 - m), jnp.exp(l1 - m), jnp.exp(l2 - m)
        ob = (e0 * so[0][s] + e1 * so[1][s] + e2 * so[2][s]) / (e0 + e1 + e2)
        slabs.append(ob.astype(BF16))
    cat = jnp.concatenate(slabs, axis=-1)
    res = jnp.dot(cat, w_ref[...], preferred_element_type=F32)
    o_ref[...] = x_ref[...] + mod_ref[2:3, :] * res


def outproj_ab(x, mod, oa, od, lse, w_bf16, layer, geom):
    n_tok, d = x.shape
    n_h, _, hd = oa.shape
    tm = TOKEN_TILE // 2

    def cls(a):
        dil = a.shape[1]
        return pl.BlockSpec((n_h, dil, tm // dil, hd), lambda i: (0, 0, i, 0))

    return pl.pallas_call(
        _outproj_ab_kernel,
        grid=(n_tok // tm,),
        in_specs=[
            pl.BlockSpec((tm, d), lambda i: (i, 0)),
            pl.BlockSpec((None, None, 6, d), lambda i: (layer, geom.seq_of_token(i * tm), 0, 0)),
            pl.BlockSpec((n_h, tm, hd), lambda i: (0, i, 0)),
            *[cls(a) for a in od], *[cls(a) for a in lse],
            pl.BlockSpec(w_bf16.shape, lambda i: (0, 0)),
        ],
        out_specs=pl.BlockSpec((tm, d), lambda i: (i, 0)),
        out_shape=jax.ShapeDtypeStruct(x.shape, F32),
        scratch_shapes=[pltpu.VMEM((n_h // 2, tm, LANES), F32) for _ in range(2 * N_DIL)],
        compiler_params=_params(("parallel",), VMEM_LIMIT),
        name="outproj_ab",
    )(x, mod, oa, *od, *lse, w_bf16)


def _outproj_kernel(x_ref, mod_ref, o_in_ref, w_ref, o_ref):
    res = jnp.dot(o_in_ref[...], w_ref[...], preferred_element_type=F32)
    o_ref[...] = x_ref[...] + mod_ref[2:3, :] * res


def outproj(x, mod, o_in, w_bf16, layer, geom):
    n_tok, d = x.shape
    tm = TOKEN_TILE
    return pl.pallas_call(
        _outproj_kernel,
        grid=(n_tok // tm,),
        in_specs=[
            pl.BlockSpec((tm, d), lambda i: (i, 0)),
            pl.BlockSpec((None, None, 6, d), lambda i: (layer, geom.seq_of_token(i * tm), 0, 0)),
            pl.BlockSpec((tm, o_in.shape[1]), lambda i: (i, 0)),
            pl.BlockSpec(w_bf16.shape, lambda i: (0, 0)),
        ],
        out_specs=pl.BlockSpec((tm, d), lambda i: (i, 0)),
        out_shape=jax.ShapeDtypeStruct(x.shape, F32),
        compiler_params=_params(("parallel",), VMEM_LIMIT),
        name="outproj",
    )(x, mod, o_in, w_bf16)


def _flash_write_out(acc_ref, o_ref, n_g, tq):
    hd = HEAD_DIM
    acc = acc_ref[...]
    o = acc[:hd] / acc[hd:hd + 1]
    pad = jnp.zeros((LANES - hd, tq), F32)
    for g in range(n_g):
        t = jnp.concatenate([o[:, g * tq:(g + 1) * tq], pad], axis=0).T
        o_ref[:, g * hd:(g + 1) * hd] = t[:, :hd].astype(o_ref.dtype)


def _flash_bounded_kernel(q_ref, k_ref, vt_ref, kmx_ref, _o_in_ref, o_ref, acc_ref, *, unroll):
    n_g, tq, kw = q_ref.shape
    n_kt, tk, _ = k_ref.shape
    acc_ref[...] = jnp.zeros(acc_ref.shape, F32)
    kmx = kmx_ref[...]
    n_col = (n_g * tq) // MXU_TILE

    n_kc = tk // MXU_TILE

    def qk(j, n, kc):
        rows = slice(kc * MXU_TILE, (kc + 1) * MXU_TILE)
        q_n = q_ref[...].reshape(n_g * tq, kw)[n * MXU_TILE:(n + 1) * MXU_TILE]
        return lax.dot_general(k_ref[j, rows, :] + kmx, q_n, (((1,), (1,)), ((), ())),
                               preferred_element_type=F32)

    def body(jj, carry):
        units = [(unroll * jj + u, n, kc) for u in range(unroll) for n in range(n_col)
                 for kc in range(n_kc)]
        pending = [qk(*unit) for unit in units[:FLASH_LOOKAHEAD]]
        for idx, (j, n, kc) in enumerate(units):
            if idx + FLASH_LOOKAHEAD < len(units):
                pending.append(qk(*units[idx + FLASH_LOOKAHEAD]))
            cols = slice(n * MXU_TILE, (n + 1) * MXU_TILE)
            rows = slice(kc * MXU_TILE, (kc + 1) * MXU_TILE)
            p = jnp.exp2(pending.pop(0)).astype(BF16)
            acc_ref[:, cols] += jnp.dot(vt_ref[j, :, rows], p, preferred_element_type=F32)
        return carry

    lax.fori_loop(0, n_kt // unroll, body, 0)
    _flash_write_out(acc_ref, o_ref, n_g, tq)


def _flash_kernel(q_ref, k_ref, vt_ref, _o_in_ref, o_ref, sa_ref, sb_ref, mxa_ref, mxb_ref,
                  m_ref, acc_ref, *, unroll):
    n_g, tq, hd = q_ref.shape
    n_kt = k_ref.shape[0]
    m_ref[...] = jnp.full(m_ref.shape, -jnp.inf, F32)
    acc_ref[...] = jnp.zeros(acc_ref.shape, F32)
    n_col = (n_g * tq) // MXU_TILE

    def scores(j, s_ref, mx_ref, n):
        cols = slice(n * MXU_TILE, (n + 1) * MXU_TILE)
        q_n = q_ref[...].reshape(n_g * tq, hd)[n * MXU_TILE:(n + 1) * MXU_TILE]
        s = lax.dot_general(k_ref[j], q_n, (((1,), (1,)), ((), ())),
                            preferred_element_type=F32)
        s_ref[:, cols] = s
        mx_ref[:, cols] = jnp.max(s, axis=0, keepdims=True)

    def consume(j, s_ref, mx_ref, n):
        cols = slice(n * MXU_TILE, (n + 1) * MXU_TILE)
        m_old = m_ref[:, cols]
        m_new = jnp.maximum(m_old, mx_ref[:, cols])
        alpha = jnp.exp2(m_old - m_new)
        m_ref[:, cols] = m_new
        tk = s_ref.shape[0]
        acc = alpha * acc_ref[:, cols]
        for kc in range(tk // MXU_TILE):
            rows = slice(kc * MXU_TILE, (kc + 1) * MXU_TILE)
            p = jnp.exp2(s_ref[rows, cols] - m_new).astype(BF16)
            acc = acc + jnp.dot(vt_ref[j, :, rows], p, preferred_element_type=F32)
        acc_ref[:, cols] = acc

    for n in range(n_col):
        scores(0, sa_ref, mxa_ref, n)

    def body(jj, carry):
        j = unroll * jj
        for u in range(0, unroll, 2):
            for n in range(n_col):
                scores(j + u + 1, sb_ref, mxb_ref, n)
                consume(j + u, sa_ref, mxa_ref, n)
            for n in range(n_col):
                scores(jnp.minimum(j + u + 2, n_kt - 1), sa_ref, mxa_ref, n)
                consume(j + u + 1, sb_ref, mxb_ref, n)
        return carry

    lax.fori_loop(0, n_kt // unroll, body, 0)
    _flash_write_out(acc_ref, o_ref, n_g, tq)


def flash_attention(qk, vt_tiles, o_buf, tok0, n_b, s_n, seq_idx0, kmx=None):
    n_all, n_tok, kw = qk.shape
    n_kv, _, hv, tk = vt_tiles.shape
    n_g = C_Q_HEADS // n_kv
    hd = HEAD_DIM
    tq = FLASH_TQ
    nq = s_n // tq
    kt_per = s_n // tk
    unroll = min(FLASH_UNROLL if kmx is None else FLASH_BOUNDED_UNROLL, kt_per)
    assert kt_per % unroll == 0 and unroll % 2 == 0 and tok0 % s_n == 0
    wq = n_g * tq
    k_view = qk.reshape(n_all, n_tok // tk, tk, kw)
    seq0 = tok0 // s_n
    in_specs = [
        pl.BlockSpec((n_g, tq, kw), lambda b, h, i: (h, (tok0 + b * s_n) // tq + i, 0)),
        pl.BlockSpec((None, kt_per, tk, kw), lambda b, h, i: (C_K0 + h, seq0 + b, 0, 0)),
        pl.BlockSpec((None, kt_per, hv, tk), lambda b, h, i: (h, seq0 + b, 0, 0)),
    ]
    args = [qk, k_view, vt_tiles]
    if kmx is None:
        body = functools.partial(_flash_kernel, unroll=unroll)
        scratch = [pltpu.VMEM((tk, wq), F32), pltpu.VMEM((tk, wq), F32),
                   pltpu.VMEM((1, wq), F32), pltpu.VMEM((1, wq), F32),
                   pltpu.VMEM((1, wq), F32), pltpu.VMEM((hv, wq), F32)]
    else:
        body = functools.partial(_flash_bounded_kernel, unroll=unroll)
        scratch = [pltpu.VMEM((hv, wq), F32)]
        in_specs.append(pl.BlockSpec((None, None, 1, kw), lambda b, h, i: (h, seq_idx0 + b, 0, 0)))
        args.append(kmx)
    in_specs.append(pl.BlockSpec(memory_space=pl.ANY))
    args.append(o_buf)
    return pl.pallas_call(
        body,
        grid=(n_b, n_kv, nq),
        in_specs=in_specs,
        out_specs=pl.BlockSpec((tq, n_g * hd), lambda b, h, i: ((tok0 + b * s_n) // tq + i, h)),
        out_shape=jax.ShapeDtypeStruct(o_buf.shape, o_buf.dtype),
        input_output_aliases={len(args) - 1: 0},
        scratch_shapes=scratch,
        compiler_params=_params(("parallel", "parallel", "arbitrary"), VMEM_LIMIT),
        name="flash_attention",
    )(*args)


def _router_kernel(x_ref, mod_ref, g_ref, wr_ref, br_ref, h_ref, idx_ref, gate_ref, cnt_ref,
                   run_ref):
    @pl.when(pl.program_id(0) == 0)
    def _():
        run_ref[...] = jnp.zeros(run_ref.shape, F32)

    h = _modulated_norm(x_ref[...], g_ref[...], mod_ref[3:4, :], mod_ref[4:5, :])
    h_ref[...] = h.astype(BF16)
    logits = lax.dot_general(wr_ref[...], h, (((1,), (1,)), ((), ())), precision=HIGHEST,
                             preferred_element_type=F32) + br_ref[...]
    n_e, tm = logits.shape
    iota = lax.broadcasted_iota(jnp.int32, (n_e, tm), 0)
    vals, ids = [], []
    cur = logits
    for _ in range(TOP_K):
        m = jnp.max(cur, axis=0, keepdims=True)
        am = jnp.min(jnp.where(cur == m, iota, n_e), axis=0, keepdims=True)
        vals.append(m)
        ids.append(am)
        cur = jnp.where(iota == am, -jnp.inf, cur)
    es = [jnp.exp(v - vals[0]) for v in vals]
    den = es[0] + es[1] + es[2] + es[3]
    chosen = jnp.zeros((n_e, tm), F32)
    for kk in range(TOP_K):
        chosen = chosen + jnp.where(iota == ids[kk], 1.0, 0.0)
    earlier = lax.broadcasted_iota(jnp.int32, (tm, tm), 0)
    token = lax.broadcasted_iota(jnp.int32, (tm, tm), 1)
    tri = jnp.where(earlier < token, 1.0, 0.0).astype(BF16)
    before = jnp.dot(chosen.astype(BF16), tri, preferred_element_type=F32) + run_ref[...]
    run_ref[...] = run_ref[...] + jnp.sum(chosen, axis=1, keepdims=True)
    cnt_ref[...] = run_ref[...]
    row = lax.broadcasted_iota(jnp.int32, idx_ref.shape, 0)
    idx_out = jnp.zeros(idx_ref.shape, jnp.int32)
    gate_out = jnp.zeros(gate_ref.shape, F32)
    for kk in range(TOP_K):
        rank = jnp.sum(jnp.where(iota == ids[kk], before, 0.0), axis=0, keepdims=True)
        idx_out = jnp.where(row == kk, ids[kk], idx_out)
        idx_out = jnp.where(row == TOP_K + kk, rank.astype(jnp.int32), idx_out)
        gate_out = jnp.where(row == kk, es[kk] / den, gate_out)
    idx_ref[...] = idx_out
    gate_ref[...] = gate_out


def router(x, mod, g_norm, w_router, b_router, layer, geom, tile0, n_tiles):
    d = x.shape[1]
    tm = MOE_TOKEN_TILE
    n_tok = n_tiles * tm
    return pl.pallas_call(
        _router_kernel,
        grid=(n_tiles,),
        in_specs=[
            pl.BlockSpec((tm, d), lambda i: (tile0 + i, 0)),
            pl.BlockSpec((None, None, 6, d),
                         lambda i: (layer, geom.seq_of_token((tile0 + i) * tm), 0, 0)),
            pl.BlockSpec((None, 1, d), lambda i: (layer, 0, 0)),
            pl.BlockSpec((None, N_EXPERTS, d), lambda i: (layer, 0, 0)),
            pl.BlockSpec((None, N_EXPERTS, 1), lambda i: (layer, 0, 0)),
        ],
        out_specs=[
            pl.BlockSpec((tm, d), lambda i: (i, 0)),
            pl.BlockSpec((2 * TOP_K, tm), lambda i: (0, i)),
            pl.BlockSpec((2 * TOP_K, tm), lambda i: (0, i)),
            pl.BlockSpec((N_EXPERTS, 1), lambda i: (0, 0)),
        ],
        out_shape=[
            jax.ShapeDtypeStruct((n_tok, d), BF16),
            jax.ShapeDtypeStruct((2 * TOP_K, n_tok), jnp.int32),
            jax.ShapeDtypeStruct((2 * TOP_K, n_tok), F32),
            jax.ShapeDtypeStruct((N_EXPERTS, 1), F32),
        ],
        scratch_shapes=[pltpu.VMEM((N_EXPERTS, 1), F32)],
        compiler_params=_params(("arbitrary",), VMEM_LIMIT),
        name="router",
    )(x, mod, g_norm.reshape(-1, 1, d), w_router.transpose(0, 2, 1),
      b_router.reshape(-1, N_EXPERTS, 1))


def _moe_kernel(be_ref, nu_ref, x_ref, wgu_ref, bgu_ref, wd_ref, bd_ref, o_ref,
                wgu_s, wd_s):
    i = pl.program_id(0)
    e = be_ref[i]
    e_prev = be_ref[jnp.maximum(i - 1, 0)]

    @pl.when((i == 0) | (e != e_prev))
    def _():
        wgu_s[...] = wgu_ref[...].astype(BF16)
        wd_s[...] = wd_ref[...].astype(BF16)

    @pl.when(i < nu_ref[0])
    def _():
        d_ff = wd_s.shape[0]
        gu = jnp.dot(x_ref[...], wgu_s[...], preferred_element_type=F32) + bgu_ref[...]
        glu = jnp.minimum(gu[:, :d_ff], SWIGLU_LIMIT)
        lin = jnp.clip(gu[:, d_ff:], -SWIGLU_LIMIT, SWIGLU_LIMIT)
        act = glu / (1.0 + jnp.exp(-SWIGLU_ALPHA * glu)) * (lin + 1.0)
        y = jnp.dot(act.astype(BF16), wd_s[...], preferred_element_type=F32) + bd_ref[...]
        o_ref[...] = y.astype(o_ref.dtype)

    @pl.when(i >= nu_ref[0])
    def _():
        o_ref[...] = jnp.zeros(o_ref.shape, o_ref.dtype)


def moe_experts(xs, blk_e, n_used, w_gu, b_gu, w_down, b_down, layer):
    n_rows, d = xs.shape
    d_ff = w_down.shape[2]
    nblk = n_rows // MOE_BLOCK
    grid_spec = pltpu.PrefetchScalarGridSpec(
        num_scalar_prefetch=2,
        grid=(nblk,),
        in_specs=[
            pl.BlockSpec((MOE_BLOCK, d), lambda i, be, nu: (jnp.minimum(i, nu[0] - 1), 0)),
            pl.BlockSpec((None, None, d, 2 * d_ff), lambda i, be, nu: (layer, be[i], 0, 0)),
            pl.BlockSpec((None, None, 1, 2 * d_ff), lambda i, be, nu: (layer, be[i], 0, 0)),
            pl.BlockSpec((None, None, d_ff, d), lambda i, be, nu: (layer, be[i], 0, 0)),
            pl.BlockSpec((None, None, 1, d), lambda i, be, nu: (layer, be[i], 0, 0)),
        ],
        out_specs=pl.BlockSpec((MOE_BLOCK, d), lambda i, be, nu: (i, 0)),
        scratch_shapes=[pltpu.VMEM((d, 2 * d_ff), BF16), pltpu.VMEM((d_ff, d), BF16)],
    )
    depth, n_e = w_gu.shape[:2]
    return pl.pallas_call(
        _moe_kernel,
        grid_spec=grid_spec,
        out_shape=jax.ShapeDtypeStruct((n_rows, d), BF16),
        compiler_params=_params(("arbitrary",), VMEM_LIMIT),
        name="moe_experts",
    )(blk_e, n_used, xs, w_gu, b_gu.reshape(depth, n_e, 1, 2 * d_ff), w_down,
      b_down.reshape(depth, n_e, 1, d))


def _combine_kernel(x_ref, mod_ref, y_ref, gate_ref, o_ref):
    g = gate_ref[...]
    acc = g[:, 0:1] * y_ref[0].astype(F32)
    for kk in range(1, TOP_K):
        acc = acc + g[:, kk:kk + 1] * y_ref[kk].astype(F32)
    o_ref[...] = x_ref[...] + mod_ref[5:6, :] * acc


def moe_combine(x, mod, yk, gates, layer, geom, tile0, n_tiles):
    d = x.shape[1]
    tm = MOE_TOKEN_TILE
    return pl.pallas_call(
        _combine_kernel,
        grid=(n_tiles,),
        in_specs=[
            pl.BlockSpec((tm, d), lambda i: (tile0 + i, 0)),
            pl.BlockSpec((None, None, 6, d),
                         lambda i: (layer, geom.seq_of_token((tile0 + i) * tm), 0, 0)),
            pl.BlockSpec((TOP_K, tm, d), lambda i: (0, i, 0)),
            pl.BlockSpec((tm, 2 * TOP_K), lambda i: (i, 0)),
        ],
        out_specs=pl.BlockSpec((tm, d), lambda i: (tile0 + i, 0)),
        out_shape=jax.ShapeDtypeStruct(x.shape, F32),
        input_output_aliases={0: 0},
        compiler_params=_params(("parallel",), VMEM_LIMIT),
        name="moe_combine",
    )(x, mod, yk, gates)


def moe_layer(x, mod, g_norm, w_router, b_router, w_gu, b_gu, w_down, b_down, layer, geom):
    n_tiles = x.shape[0] // MOE_TOKEN_TILE
    assert n_tiles % MOE_PARTS == 0
    per = n_tiles // MOE_PARTS
    routed = [_moe_dispatch(x, mod, g_norm, w_router, b_router, w_gu, b_gu, w_down, b_down, layer,
                            geom, part * per, per) for part in range(MOE_PARTS)]
    for part, (yk, gates8) in enumerate(routed):
        x = moe_combine(x, mod, yk, gates8, layer, geom, part * per, per)
    return x


def _moe_dispatch(x, mod, g_norm, w_router, b_router, w_gu, b_gu, w_down, b_down, layer, geom,
                  tile0, n_tiles):
    d = x.shape[1]
    n_tok = n_tiles * MOE_TOKEN_TILE
    h, idx8, gates8, counts = router(x, mod, g_norm, w_router, b_router, layer, geom, tile0,
                                     n_tiles)
    idx = idx8[:TOP_K].T
    rank = idx8[TOP_K:].T
    gates8 = gates8.T
    counts = counts[:, 0].astype(jnp.int32)
    pcounts = ((counts + MOE_BLOCK - 1) // MOE_BLOCK) * MOE_BLOCK
    pend = jnp.cumsum(pcounts)
    pstart = pend - pcounts
    experts = jnp.arange(N_EXPERTS, dtype=jnp.int32)
    dest = rank + jnp.sum(jnp.where(idx[:, :, None] == experts, pstart, 0), axis=-1)
    nblk = (n_tok * TOP_K) // MOE_BLOCK + N_EXPERTS
    n_rows = nblk * MOE_BLOCK
    tok_ids = jnp.broadcast_to(jnp.arange(n_tok, dtype=jnp.int32)[:, None], dest.shape)
    tok = (jnp.arange(n_rows, dtype=jnp.int32) % n_tok).at[dest.reshape(-1)].set(
        tok_ids.reshape(-1), unique_indices=True, mode="promise_in_bounds")
    n_used = (pend[-1] // MOE_BLOCK).astype(jnp.int32).reshape(1)
    blk_start = jnp.arange(nblk, dtype=jnp.int32) * MOE_BLOCK
    blk_last = jnp.minimum(blk_start, pend[-1] - 1)
    blk_e = jnp.sum(pend[None, :] <= blk_last[:, None], axis=-1, dtype=jnp.int32)
    blk_e = jnp.minimum(blk_e, N_EXPERTS - 1)
    xs = jnp.take(h, tok, axis=0, mode="clip")
    yb = moe_experts(xs, blk_e, n_used, w_gu, b_gu, w_down, b_down, layer)
    yk = jnp.take(yb, dest.T.reshape(-1), axis=0, mode="clip").reshape(TOP_K, n_tok, d)
    return yk, gates8


def _per_token(geom, tables):
    return [jnp.concatenate([t[:s] for _, n_b, s in geom.groups() for _ in range(n_b)])
            for t in tables]


def _qk_gains(g_q, g_k, n_q, n_k, q_scale):
    hd = g_q.shape[-1]
    return jnp.concatenate([jnp.broadcast_to(g_q * q_scale, (n_q, hd)),
                            jnp.broadcast_to(g_k, (n_k, hd))])


def _dilated_group_weight(w_in, gi):
    blk = DIL_HEADS * HEAD_DIM
    base = 3 * NA_HEADS * HEAD_DIM
    return jnp.concatenate([w_in[:, base + (a * N_DIL + gi) * blk:base + (a * N_DIL + gi + 1) * blk]
                            for a in range(3)], axis=1)


def mixer_ab_layer(x, mod, g_norm, w_in, w_out, g_qn_a, g_kn_a, rpb, g_qn_b, g_kn_b, layer, geom):
    n_tok, d = x.shape
    hd = HEAD_DIM
    w = w_in.astype(BF16)
    n_h = NA_HEADS
    qkv = norm_proj(x, mod, g_norm, w[:, :3 * n_h * hd], layer, geom, dil=1)
    qkv = qk_prep(qkv.reshape(3 * n_h, n_tok, hd), _qk_gains(g_qn_a, g_kn_a, n_h, n_h, Q_SCALE),
                  hb=n_h, n_blocks=2)
    oa = neighborhood_attention(qkv, _na_bias_table(rpb), geom)
    cos, sin = _per_token(geom, _rope_angles(jnp.arange(max(geom.s_p, geom.s_d)), hd))
    perm = _rotate_half_matrix(hd, 1)
    od, lse = [], []
    for gi, (window, dil) in enumerate(DIL_PATTERNS):
        assert (window // 2) // dil == DIL_SIDE
        n_h = DIL_HEADS

        def class_major(t):
            return t.reshape(n_tok // dil, dil, hd).transpose(1, 0, 2).reshape(n_tok, hd)

        qkv = norm_proj(x, mod, g_norm, _dilated_group_weight(w, gi), layer, geom, dil=dil)
        qkv = qk_prep(qkv.reshape(3 * n_h, n_tok, hd),
                      _qk_gains(g_qn_b[gi], g_kn_b[gi], n_h, n_h, Q_SCALE), hb=n_h, n_blocks=2,
                      tables=(class_major(cos), class_major(sin), perm))
        o_g, lse_g = dilated_attention(qkv, dil, geom)
        od.append(o_g.reshape(n_h, dil, n_tok // dil, hd))
        lse.append(lse_g.reshape(n_h, dil, n_tok // dil, hd))
    return outproj_ab(x, mod, oa, od, lse, w_out.astype(BF16), layer, geom)


def mixer_c_layer(x, mod, g_norm, w_in, w_out, g_qn, g_kn, layer, geom):
    n_tok, d = x.shape
    hd = HEAD_DIM
    n_heads = C_Q_HEADS + 2 * C_KV_HEADS
    qkv = norm_proj(x, mod, g_norm, w_in.astype(BF16), layer, geom, dil=1).reshape(n_heads, n_tok, hd)
    pos = jnp.arange(max(geom.s_p, geom.s_d))
    half = hd // 2
    cr, sr = _rope_angles(pos // GRID_W, half)
    cc, sc = _rope_angles(pos % GRID_W, half)
    cos, sin = _per_token(geom, (jnp.concatenate([cr, cc], axis=-1),
                                 jnp.concatenate([sr, sc], axis=-1)))
    tables = (cos, sin, _rotate_half_matrix(half, 2))
    side = jnp.zeros((C_Q_HEADS + C_KV_HEADS, hd), F32).at[:C_Q_HEADS, 0].set(-FLASH_BOUND_MARGIN)
    qk, nmax = qk_prep(qkv, _qk_gains(g_qn, g_kn, C_Q_HEADS, C_KV_HEADS, Q_SCALE * LOG2_E),
                       hb=C_KV_HEADS, n_blocks=(C_Q_HEADS + C_KV_HEADS) // C_KV_HEADS,
                       tables=tables, side=side)
    tk = FLASH_TK
    vt_tiles = qkv[C_V0:].reshape(C_KV_HEADS, n_tok // tk, tk, hd).transpose(0, 1, 3, 2)
    ones = jnp.ones((C_KV_HEADS, n_tok // tk, BF16_SUBLANES, tk), BF16)
    vt_tiles = jnp.concatenate([vt_tiles, ones], axis=2)

    ts = PREP_TILE
    tile_max = nmax[:, :, 0, 0]
    seq_max = jnp.concatenate(
        [tile_max[:, t0 // ts:(t0 + n_b * s_n) // ts].reshape(-1, n_b, s_n // ts).max(-1)
         for t0, n_b, s_n in geom.groups()], axis=1)
    n_g = C_Q_HEADS // C_KV_HEADS
    k_max = (seq_max[C_K0:] * FLASH_BOUND_MARGIN).astype(BF16)
    q_max = seq_max[:C_K0] * FLASH_BOUND_MARGIN
    bound = jnp.max(q_max.reshape(C_KV_HEADS, n_g, -1) * k_max.astype(F32)[:, None, :])
    kmx = jnp.zeros((C_KV_HEADS, geom.n_seq, 1, 2 * hd), BF16).at[:, :, 0, hd].set(k_max)
    o_buf = jnp.zeros((n_tok, C_Q_HEADS * hd), BF16)

    def attend(kmx_or_none):
        o = o_buf
        seq_idx0 = 0
        for tok0, n_b, s_n in geom.groups():
            o = flash_attention(qk, vt_tiles, o, tok0, n_b, s_n, seq_idx0, kmx_or_none)
            seq_idx0 += n_b
        return o

    o = lax.cond(bound <= FLASH_SAFE_BOUND, lambda: attend(kmx), lambda: attend(None))
    return outproj(x, mod, o, w_out.astype(BF16), layer, geom)


def kernel(x_prompt, x_sample, c_prompt, c_sample, w_ada, b_ada, g_norm_mix, g_norm_ffn, w_in_ab, w_out_ab, g_qn_a, g_kn_a, rpb_a, g_qn_b, g_kn_b, w_in_c, w_out_c, g_qn_c, g_kn_c, w_router, b_router, w_gate_up, b_gate_up, w_down, b_down):
    b_p, s_p, d = x_prompt.shape
    b_d, s_d, _ = x_sample.shape
    geom = Geom(b_p, s_p, b_d, s_d)
    depth = w_ada.shape[0]
    x = jnp.concatenate([x_prompt.reshape(-1, d), x_sample.reshape(-1, d)], axis=0)
    mod = ada_modulation(jnp.concatenate([c_prompt, c_sample], axis=0), w_ada, b_ada)
    for layer in range(depth):
        i = layer // 2
        if layer % 2 == 0:
            x = mixer_ab_layer(x, mod, g_norm_mix, w_in_ab[i], w_out_ab[i], g_qn_a[i], g_kn_a[i],
                               rpb_a[i], g_qn_b[i], g_kn_b[i], layer, geom)
        else:
            x = mixer_c_layer(x, mod, g_norm_mix, w_in_c[i], w_out_c[i], g_qn_c[i], g_kn_c[i],
                              layer, geom)
        x = moe_layer(x, mod, g_norm_ffn, w_router, b_router, w_gate_up, b_gate_up, w_down,
                      b_down, layer, geom)
    y_prompt = x[:geom.n_p].reshape(b_p, s_p, d)
    y_sample = x[geom.n_p:].reshape(b_d, s_d, d)
    return (y_prompt, y_sample)
```
